```python
import functools
import jax, jax.numpy as jnp
from jax import lax
import numpy as np

D_MODEL = 1024
BATCH = 4
SEQ = 4096
DEPTH = 2
DEC_BATCH = 32
DEC_SEQ = 1
PAST_LEN = 8192
PAGE_SIZE = 128

N_EVEN = (DEPTH + 1) // 2
N_ODD = DEPTH // 2
N_PAGES = PAST_LEN // PAGE_SIZE
N_POOL_PAGES = (DEC_BATCH * N_PAGES * 5) // 4

LRU_W = D_MODEL // 2
LRU_BLOCKS = 8
LRU_BW = LRU_W // LRU_BLOCKS
LRU_C = 8.0
LRU_CONV = 4
FOX_H = 8
FOX_HD = 64
FOX_W = FOX_H * FOX_HD
Q_BLOCK = 128
IN_E = 2 * LRU_W + 3 * FOX_W + FOX_H
MIX_E = LRU_W + FOX_W

CONV_W = D_MODEL // 2
CONV_K = 31
POOL_W = D_MODEL // 2
POOL_WINDOWS = (2, 4, 8, 16)
POOL_G = len(POOL_WINDOWS)
POOL_GW = POOL_W // POOL_G
POOL_BUF = max(POOL_WINDOWS) - 1
IN_O = 2 * CONV_W + POOL_W
MIX_O = CONV_W + POOL_W

MEM_LEN = 256
MEM_H = 4
MEM_HD = 128
MEM_W = MEM_H * MEM_HD

D_FF = 2816
N_EXPERTS = 8
TOP_K = 2
D_FF_E = 3584

N_NORMS = 7
EPS = 1e-6
NEG = -1e30

kernel_name = 'hybrid_rglru_fox_conformer_pool_decode_step'

F32 = jnp.float32


def rmsnorm(x, g):
    x32 = x.astype(F32)
    y = x32 * lax.rsqrt(jnp.mean(x32 * x32, axis=-1, keepdims=True) + EPS)
    return (y * g.astype(F32)).astype(x.dtype)


def layernorm(x, g, b):
    x32 = x.astype(F32)
    mu = jnp.mean(x32, axis=-1, keepdims=True)
    xc = x32 - mu
    y = xc * lax.rsqrt(jnp.mean(xc * xc, axis=-1, keepdims=True) + EPS)
    return (y * g.astype(F32) + b.astype(F32)).astype(x.dtype)


def causal_dwconv(prefix, u, w, b):
    z = jnp.concatenate([prefix.astype(u.dtype), u], axis=1)
    out = lax.conv_general_dilated(z, w[:, None, :].astype(u.dtype), (1,), 'VALID',
                                   dimension_numbers=('NWC', 'WIO', 'NWC'),
                                   feature_group_count=u.shape[-1])
    return out + b.astype(u.dtype), z[:, z.shape[1] - (w.shape[0] - 1):]


def _lin_combine(c1, c2):
    a1, b1 = c1
    a2, b2 = c2
    return a1 * a2, a2 * b1 + b2


def rglru(u, h0, w_a, b_a, w_i, b_i, lam):
    B, T, W = u.shape
    u32 = u.astype(F32)
    ub = u32.reshape(B, T, LRU_BLOCKS, LRU_BW)
    r = jax.nn.sigmoid(jnp.einsum('btnc,ncd->btnd', ub, w_a.astype(F32)).reshape(B, T, W) + b_a.astype(F32))
    ig = jax.nn.sigmoid(jnp.einsum('btnc,ncd->btnd', ub, w_i.astype(F32)).reshape(B, T, W) + b_i.astype(F32))
    log_a = -LRU_C * r * jax.nn.softplus(-lam.astype(F32))
    a = jnp.exp(log_a)
    bx = jnp.sqrt(-jnp.expm1(2.0 * log_a)) * (ig * u32)
    A, Bs = lax.associative_scan(_lin_combine, (a, bx), axis=1)
    h = A * h0.astype(F32)[:, None, :] + Bs
    return h, h[:, -1]


def fox_attend(q, k, v, cq, ck, qpos, kpos):
    s = jnp.einsum('bqhd,bkhd->bhqk', q, k.astype(q.dtype), preferred_element_type=F32) * (FOX_HD ** -0.5)
    s = s + jnp.swapaxes(cq, 1, 2)[:, :, :, None] - jnp.swapaxes(ck, 1, 2)[:, :, None, :]
    s = jnp.where(kpos[None, :] <= qpos[:, None], s, NEG)
    p = jax.nn.softmax(s, axis=-1)
    return jnp.einsum('bhqk,bkhd->bqhd', p.astype(v.dtype), v)


def fox_attend_prompt(q, k, v, logf):
    B, S, H, hd = q.shape
    c = jnp.cumsum(logf, axis=1)
    nb = S // Q_BLOCK
    qb = q.reshape(B, nb, Q_BLOCK, H, hd).swapaxes(0, 1)
    cb = c.reshape(B, nb, Q_BLOCK, H).swapaxes(0, 1)
    pb = jnp.arange(S).reshape(nb, Q_BLOCK)
    kpos = jnp.arange(S)

    def one_block(blk):
        qi, ci, pi = blk
        return fox_attend(qi, k, v, ci, c, pi, kpos)

    o = lax.map(one_block, (qb, cb, pb))
    return o.swapaxes(0, 1).reshape(B, S, H, hd)


def fox_attend_sample(q, k, v, logf, k_past, v_past, logf_past):
    past = k_past.shape[1]
    T = q.shape[1]
    k_all = jnp.concatenate([k_past.astype(k.dtype), k], axis=1)
    v_all = jnp.concatenate([v_past.astype(v.dtype), v], axis=1)
    c_all = jnp.cumsum(jnp.concatenate([logf_past.astype(F32), logf], axis=1), axis=1)
    qpos = past + jnp.arange(T)
    kpos = jnp.arange(past + T)
    return fox_attend(q, k_all, v_all, c_all[:, past:], c_all, qpos, kpos)


def pool_mix(prefix, u, pos0, w_pg, scale):
    B, T, P = u.shape
    L = prefix.shape[1]
    z = jnp.concatenate([prefix.astype(u.dtype), u], axis=1)
    c = jnp.concatenate([jnp.zeros((B, 1, P), F32), jnp.cumsum(z.astype(F32), axis=1)], axis=1)
    end = c[:, L + 1:]
    pos = pos0 + jnp.arange(T)
    outs = []
    for g, w in enumerate(POOL_WINDOWS):
        cs = slice(g * POOL_GW, (g + 1) * POOL_GW)
        win = end[:, :, cs] - c[:, L + 1 - w:L + 1 - w + T, cs]
        cnt = jnp.minimum(pos + 1, w).astype(F32)[None, :, None]
        d = win / cnt - u[:, :, cs].astype(F32)
        outs.append(jnp.einsum('btc,cd->btd', d, w_pg[g].astype(F32)))
    y = jnp.concatenate(outs, axis=-1) * scale.astype(F32)
    return y.astype(u.dtype), z[:, z.shape[1] - L:]


def even_mixer(h, lru_h0, lru_prefix, attend, w_in, b_f, conv_w, conv_b, wa, ba, wi, bi, lam, w_out):
    B, T, _ = h.shape
    z = h @ w_in
    xl, gate, q, k, v, fl = jnp.split(z, [LRU_W, 2 * LRU_W, 2 * LRU_W + FOX_W,
                                          2 * LRU_W + 2 * FOX_W, 2 * LRU_W + 3 * FOX_W], axis=-1)
    xc, new_prefix = causal_dwconv(lru_prefix, xl, conv_w, conv_b)
    hs, h_last = rglru(xc, lru_h0, wa, ba, wi, bi, lam)
    lru_out = jax.nn.gelu(gate) * hs.astype(gate.dtype)
    logf = jax.nn.log_sigmoid(fl.astype(F32) + b_f.astype(F32))
    q = q.reshape(B, T, FOX_H, FOX_HD)
    k = k.reshape(B, T, FOX_H, FOX_HD)
    v = v.reshape(B, T, FOX_H, FOX_HD)
    att = attend(q, k, v, logf).reshape(B, T, FOX_W)
    y = jnp.concatenate([lru_out, att.astype(lru_out.dtype)], axis=-1) @ w_out
    return y, (k, v, logf, h_last, new_prefix)


def odd_mixer(h, conv_prefix, pool_prefix, pos0, w_in, dw_w, dw_b, ln_g, ln_b, pool_w, pool_scale, w_out):
    z = h @ w_in
    a, gt, up = jnp.split(z, [CONV_W, 2 * CONV_W], axis=-1)
    glu = a * jax.nn.sigmoid(gt)
    cv, new_conv = causal_dwconv(conv_prefix, glu, dw_w, dw_b)
    cv = jax.nn.silu(layernorm(cv, ln_g, ln_b))
    pl, new_pool = pool_mix(pool_prefix, up, pos0, pool_w, pool_scale)
    y = jnp.concatenate([cv, pl.astype(cv.dtype)], axis=-1) @ w_out
    return y, (new_conv, new_pool)


def mem_kv(mem, g, w_k, w_v):
    B, M, _ = mem.shape
    m = rmsnorm(mem, g)
    return (m @ w_k).reshape(B, M, MEM_H, MEM_HD), (m @ w_v).reshape(B, M, MEM_H, MEM_HD)


def cross_attn(x, mk, mv, w_q, w_o):
    B, T, _ = x.shape
    q = (x @ w_q).reshape(B, T, MEM_H, MEM_HD)
    s = jnp.einsum('bqhd,bkhd->bhqk', q, mk.astype(q.dtype), preferred_element_type=F32) * (MEM_HD ** -0.5)
    p = jax.nn.softmax(s, axis=-1)
    o = jnp.einsum('bhqk,bkhd->bqhd', p.astype(q.dtype), mv.astype(q.dtype))
    return o.reshape(B, T, MEM_W) @ w_o


def swiglu(x, w1, w3, w2):
    return (jax.nn.silu(x @ w1) * (x @ w3)) @ w2


def moe(x, w_r, b_r, w1, w3, w2):
    B, T, D = x.shape
    xf = x.reshape(B * T, D)
    logits = (xf @ w_r).astype(F32) + b_r.astype(F32)
    top_v, top_i = lax.top_k(logits, TOP_K)
    gates = jax.nn.softmax(top_v, axis=-1)
    dense_g = jnp.sum(jax.nn.one_hot(top_i, N_EXPERTS, dtype=F32) * gates[..., None], axis=1)
    out = jnp.zeros((B * T, D), F32)
    for e in range(N_EXPERTS):
        out = out + dense_g[:, e:e + 1] * swiglu(xf, w1[e], w3[e], w2[e]).astype(F32)
    return out.astype(x.dtype).reshape(B, T, D)


def layer_tail(x, mk, mv, g, w_q, w_o, ffn):
    x = x + rmsnorm(cross_attn(rmsnorm(x, g[2]), mk, mv, w_q, w_o), g[3])
    x = x + rmsnorm(ffn(rmsnorm(x, g[4])), g[5])
    return x


def setup_inputs(seed: int = 0) -> dict:
    key = jax.random.key(seed)
    ks = iter(jax.random.split(key, 64))

    def nrm(shape, scale=1.0):
        return jax.random.normal(next(ks), shape, F32) * scale

    D = D_MODEL
    n_pool = (DEC_BATCH * (PAST_LEN // PAGE_SIZE) * 5) // 4
    n_pages = PAST_LEN // PAGE_SIZE
    perm = jax.random.permutation(next(ks), n_pool)[:DEC_BATCH * n_pages]
    page_table = perm.reshape(DEC_BATCH, n_pages).astype(jnp.int32)
    u = jax.random.uniform(next(ks), (N_EVEN, LRU_W), F32, 0.9, 0.999)
    a0 = u ** (1.0 / LRU_C)
    lru_lam = jnp.log(a0) - jnp.log1p(-a0)
    return {
        'x_prompt': nrm((BATCH, SEQ, D)),
        'x_sample': nrm((DEC_BATCH, DEC_SEQ, D)),
        'cache_fox_k': nrm((N_EVEN, n_pool, PAGE_SIZE, FOX_H, FOX_HD)),
        'cache_fox_v': nrm((N_EVEN, n_pool, PAGE_SIZE, FOX_H, FOX_HD)),
        'cache_fox_logf': jax.nn.log_sigmoid(4.0 + nrm((N_EVEN, n_pool, PAGE_SIZE, FOX_H))),
        'state_lru_h': nrm((N_EVEN, DEC_BATCH, LRU_W), 0.5),
        'state_lru_conv': nrm((N_EVEN, DEC_BATCH, LRU_CONV - 1, LRU_W)),
        'state_conv_buf': nrm((N_ODD, DEC_BATCH, CONV_K - 1, CONV_W), 0.5),
        'state_pool_buf': nrm((N_ODD, DEC_BATCH, POOL_BUF, POOL_W)),
        'cache_mem_k': nrm((DEPTH, DEC_BATCH, MEM_LEN, MEM_H, MEM_HD)),
        'cache_mem_v': nrm((DEPTH, DEC_BATCH, MEM_LEN, MEM_H, MEM_HD)),
        'page_table': page_table,
        'mem_prompt': nrm((BATCH, MEM_LEN, D)),
        'norm_g': 1.0 + nrm((DEPTH, N_NORMS, D), 0.05),
        'w_xq': nrm((DEPTH, D, MEM_W), D ** -0.5),
        'w_xk': nrm((DEPTH, D, MEM_W), D ** -0.5),
        'w_xv': nrm((DEPTH, D, MEM_W), D ** -0.5),
        'w_xo': nrm((DEPTH, MEM_W, D), MEM_W ** -0.5),
        'w_in_e': nrm((N_EVEN, D, IN_E), D ** -0.5),
        'b_f': 4.0 + nrm((N_EVEN, FOX_H), 0.5),
        'lru_conv_w': nrm((N_EVEN, LRU_CONV, LRU_W), LRU_CONV ** -0.5),
        'lru_conv_b': nrm((N_EVEN, LRU_W), 0.02),
        'lru_wa': nrm((N_EVEN, LRU_BLOCKS, LRU_BW, LRU_BW), LRU_BW ** -0.5),
        'lru_ba': nrm((N_EVEN, LRU_W), 0.1),
        'lru_wi': nrm((N_EVEN, LRU_BLOCKS, LRU_BW, LRU_BW), LRU_BW ** -0.5),
        'lru_bi': nrm((N_EVEN, LRU_W), 0.1),
        'lru_lam': lru_lam,
        'w_out_e': nrm((N_EVEN, MIX_E, D), MIX_E ** -0.5),
        'w_ff1': nrm((N_EVEN, D, D_FF), D ** -0.5),
        'w_ff3': nrm((N_EVEN, D, D_FF), D ** -0.5),
        'w_ff2': nrm((N_EVEN, D_FF, D), D_FF ** -0.5),
        'w_in_o': nrm((N_ODD, D, IN_O), D ** -0.5),
        'cc_dw_w': nrm((N_ODD, CONV_K, CONV_W), CONV_K ** -0.5),
        'cc_dw_b': nrm((N_ODD, CONV_W), 0.02),
        'cc_ln_g': 1.0 + nrm((N_ODD, CONV_W), 0.05),
        'cc_ln_b': nrm((N_ODD, CONV_W), 0.02),
        'pool_w': nrm((N_ODD, POOL_G, POOL_GW, POOL_GW), POOL_GW ** -0.5),
        'pool_scale': 1.0 + nrm((N_ODD, POOL_W), 0.1),
        'w_out_o': nrm((N_ODD, MIX_O, D), MIX_O ** -0.5),
        'w_router': nrm((N_ODD, D, N_EXPERTS), D ** -0.5),
        'b_router': nrm((N_ODD, N_EXPERTS), 0.01),
        'w_e1': nrm((N_ODD, N_EXPERTS, D, D_FF_E), D ** -0.5),
        'w_e3': nrm((N_ODD, N_EXPERTS, D, D_FF_E), D ** -0.5),
        'w_e2': nrm((N_ODD, N_EXPERTS, D_FF_E, D), D_FF_E ** -0.5),
    }


def reference(x_prompt, x_sample, cache_fox_k, cache_fox_v, cache_fox_logf, state_lru_h, state_lru_conv,
              state_conv_buf, state_pool_buf, cache_mem_k, cache_mem_v, page_table, mem_prompt,
              norm_g, w_xq, w_xk, w_xv, w_xo,
              w_in_e, b_f, lru_conv_w, lru_conv_b, lru_wa, lru_ba, lru_wi, lru_bi, lru_lam, w_out_e,
              w_ff1, w_ff3, w_ff2,
              w_in_o, cc_dw_w, cc_dw_b, cc_ln_g, cc_ln_b, pool_w, pool_scale, w_out_o,
              w_router, b_router, w_e1, w_e3, w_e2):
    bp = x_prompt.shape[0]
    bs = x_sample.shape[0]
    past_len = page_table.shape[1] * cache_fox_k.shape[2]
    xp, xs = x_prompt, x_sample
    fk_p, fv_p, fl_p, lh_p, lc_p, cb_p, pb_p, mk_pl, mv_pl = [], [], [], [], [], [], [], [], []
    fk_s, fv_s, fl_s, lh_s, lc_s, cb_s, pb_s = [], [], [], [], [], [], []
    for l in range(DEPTH):
        g = norm_g[l]
        mk_p, mv_p = mem_kv(mem_prompt, g[6], w_xk[l], w_xv[l])
        mk_pl.append(mk_p)
        mv_pl.append(mv_p)
        if l % 2 == 0:
            i = l // 2
            ew = (w_in_e[i], b_f[i], lru_conv_w[i], lru_conv_b[i], lru_wa[i], lru_ba[i],
                  lru_wi[i], lru_bi[i], lru_lam[i], w_out_e[i])
            m, (k, v, lf, hl, cp) = even_mixer(rmsnorm(xp, g[0]), jnp.zeros((bp, LRU_W), xp.dtype),
                                               jnp.zeros((bp, LRU_CONV - 1, LRU_W), xp.dtype),
                                               fox_attend_prompt, *ew)
            xp = xp + rmsnorm(m, g[1])
            fk_p.append(k); fv_p.append(v); fl_p.append(lf); lh_p.append(hl); lc_p.append(cp)
            k_past = jnp.take(cache_fox_k[i], page_table, axis=0).reshape(bs, past_len, FOX_H, FOX_HD)
            v_past = jnp.take(cache_fox_v[i], page_table, axis=0).reshape(bs, past_len, FOX_H, FOX_HD)
            lf_past = jnp.take(cache_fox_logf[i], page_table, axis=0).reshape(bs, past_len, FOX_H)
            attend_s = functools.partial(fox_attend_sample, k_past=k_past, v_past=v_past, logf_past=lf_past)
            m, (k, v, lf, hl, cp) = even_mixer(rmsnorm(xs, g[0]), state_lru_h[i], state_lru_conv[i],
                                               attend_s, *ew)
            xs = xs + rmsnorm(m, g[1])
            fk_s.append(k); fv_s.append(v); fl_s.append(lf); lh_s.append(hl); lc_s.append(cp)
            ffn = functools.partial(swiglu, w1=w_ff1[i], w3=w_ff3[i], w2=w_ff2[i])
        else:
            j = l // 2
            ow = (w_in_o[j], cc_dw_w[j], cc_dw_b[j], cc_ln_g[j], cc_ln_b[j], pool_w[j], pool_scale[j], w_out_o[j])
            m, (cb, pbuf) = odd_mixer(rmsnorm(xp, g[0]), jnp.zeros((bp, CONV_K - 1, CONV_W), xp.dtype),
                                      jnp.zeros((bp, POOL_BUF, POOL_W), xp.dtype), 0, *ow)
            xp = xp + rmsnorm(m, g[1])
            cb_p.append(cb); pb_p.append(pbuf)
            m, (cb, pbuf) = odd_mixer(rmsnorm(xs, g[0]), state_conv_buf[j], state_pool_buf[j], past_len, *ow)
            xs = xs + rmsnorm(m, g[1])
            cb_s.append(cb); pb_s.append(pbuf)
            ffn = functools.partial(moe, w_r=w_router[j], b_r=b_router[j], w1=w_e1[j], w3=w_e3[j], w2=w_e2[j])
        xp = layer_tail(xp, mk_p, mv_p, g, w_xq[l], w_xo[l], ffn)
        xs = layer_tail(xs, cache_mem_k[l], cache_mem_v[l], g, w_xq[l], w_xo[l], ffn)
    fox_k_p = jnp.stack(fk_p); fox_v_p = jnp.stack(fv_p); fox_logf_p = jnp.stack(fl_p)
    lru_h_p = jnp.stack(lh_p); lru_conv_p = jnp.stack(lc_p)
    conv_buf_p = jnp.stack(cb_p); pool_buf_p = jnp.stack(pb_p)
    mem_k_p = jnp.stack(mk_pl); mem_v_p = jnp.stack(mv_pl)
    fox_k_s = jnp.stack(fk_s); fox_v_s = jnp.stack(fv_s); fox_logf_s = jnp.stack(fl_s)
    lru_h_s = jnp.stack(lh_s); lru_conv_s = jnp.stack(lc_s)
    conv_buf_s = jnp.stack(cb_s); pool_buf_s = jnp.stack(pb_s)
    return (xp, xs, fox_k_p, fox_v_p, fox_logf_p, lru_h_p, lru_conv_p, conv_buf_p, pool_buf_p, mem_k_p, mem_v_p,
            fox_k_s, fox_v_s, fox_logf_s, lru_h_s, lru_conv_s, conv_buf_s, pool_buf_s)
```

```python
import functools

import jax
import jax.numpy as jnp
from jax import lax
from jax.experimental import pallas as pl
from jax.experimental.pallas import tpu as pltpu

F32 = jnp.float32
BF16 = jnp.bfloat16

D_MODEL = 1024
LRU_W = 512
LRU_CONV = 4
LRU_C = 8.0
FOX_H = 8
FOX_HD = 64
FOX_W = FOX_H * FOX_HD
CONV_W = 512
CONV_K = 31
POOL_W = 512
POOL_WINDOWS = (2, 4, 8, 16)
POOL_GW = POOL_W // len(POOL_WINDOWS)
POOL_BUF = max(POOL_WINDOWS) - 1
MEM_H = 4
MEM_HD = 128
MEM_W = MEM_H * MEM_HD
N_EXPERTS = 8
EPS = 1e-6
NEG = -1e30

LANES = 128
ROW_TILE = 512
SEQ_TILE = 512
SCAN_TILE = 256
PAGES_PER_STEP = 8
MOE_TILE = 512
MOE_FF_TILE = 512
FFN_FF_TILE = 1408
VMEM_LIMIT = 56 * 1024 * 1024


def _cp(*sem):
    return pltpu.CompilerParams(dimension_semantics=sem, vmem_limit_bytes=VMEM_LIMIT)


def _rms(x, g):
    return x * lax.rsqrt(jnp.mean(x * x, axis=-1, keepdims=True) + EPS) * g


def _sigmoid(x):
    return 1.0 / (1.0 + jnp.exp(-x))


def _softplus(x):
    return jnp.maximum(x, 0.0) + jnp.log1p(jnp.exp(-jnp.abs(x)))


def _gelu_tanh(x):
    return 0.5 * x * (1.0 + jnp.tanh(0.7978845608028654 * (x + 0.044715 * (x * x * x))))


def _dot(a, b):
    return jnp.dot(a, b, preferred_element_type=F32)


def _dot_nt(a, b):
    return lax.dot_general(a, b, (((1,), (1,)), ((), ())), preferred_element_type=F32)


def _norm_matmul_kernel(x_ref, g_ref, w_ref, *rest, has_t):
    hn = _rms(x_ref[...], g_ref[...]).astype(BF16)
    if has_t:
        wt_ref, o_ref, ot_ref = rest
        ot_ref[...] = _dot_nt(wt_ref[...], hn)
    else:
        (o_ref,) = rest
    o_ref[...] = _dot(hn, w_ref[...])


def _norm_matmul(x, g, w, *, tm, wt=None, name):
    m, d = x.shape
    n = w.shape[1]
    in_specs = [pl.BlockSpec((tm, d), lambda i: (i, 0)),
                pl.BlockSpec((1, d), lambda i: (0, 0)),
                pl.BlockSpec((d, n), lambda i: (0, 0))]
    out_shape = [jax.ShapeDtypeStruct((m, n), F32)]
    out_specs = [pl.BlockSpec((tm, n), lambda i: (i, 0))]
    args = [x, g, w]
    if wt is not None:
        in_specs.append(pl.BlockSpec(wt.shape, lambda i: (0, 0)))
        out_shape.append(jax.ShapeDtypeStruct((wt.shape[0], m), F32))
        out_specs.append(pl.BlockSpec((wt.shape[0], tm), lambda i: (0, i)))
        args.append(wt)
    res = pl.pallas_call(
        functools.partial(_norm_matmul_kernel, has_t=wt is not None),
        grid=(m // tm,), in_specs=in_specs, out_specs=out_specs, out_shape=out_shape,
        compiler_params=_cp("parallel"), name=name)(*args)
    return res if wt is not None else res[0]


def _matmul_norm_res_kernel(*refs, widths):
    n_a = len(widths)
    a_refs = refs[:n_a]
    w_ref, g_ref, r_ref, o_ref = refs[n_a:]
    y = None
    off = 0
    for a_ref, k in zip(a_refs, widths):
        part = _dot(a_ref[...].astype(BF16), w_ref[off:off + k, :])
        y = part if y is None else y + part
        off += k
    o_ref[...] = r_ref[...] + _rms(y, g_ref[...])


def _matmul_norm_res(a_list, w, g, resid, *, tm, name):
    m, d = resid.shape
    widths = tuple(a.shape[1] for a in a_list)
    in_specs = [pl.BlockSpec((tm, k), lambda i: (i, 0)) for k in widths]
    in_specs += [pl.BlockSpec(w.shape, lambda i: (0, 0)),
                 pl.BlockSpec((1, d), lambda i: (0, 0)),
                 pl.BlockSpec((tm, d), lambda i: (i, 0))]
    return pl.pallas_call(
        functools.partial(_matmul_norm_res_kernel, widths=widths),
        grid=(m // tm,), in_specs=in_specs,
        out_specs=pl.BlockSpec((tm, d), lambda i: (i, 0)),
        out_shape=jax.ShapeDtypeStruct((m, d), F32),
        compiler_params=_cp("parallel"), name=name)(*a_list, w, g, resid)


def _lru_gates(xc, wa_ref, ba_ref, wi_ref, bi_ref, lam_ref):
    xb = xc.astype(BF16)
    ra, ia = [], []
    for c in range(LRU_W // LANES):
        xs = xb[:, c * LANES:(c + 1) * LANES]
        ra.append(_dot(xs, wa_ref[c]))
        ia.append(_dot(xs, wi_ref[c]))
    r = _sigmoid(jnp.concatenate(ra, axis=1) + ba_ref[...])
    ig = _sigmoid(jnp.concatenate(ia, axis=1) + bi_ref[...])
    log_a = -LRU_C * r * _softplus(-lam_ref[...])
    a = jnp.exp(log_a)
    bx = jnp.sqrt(-jnp.tanh(log_a) * (a * a + 1.0)) * (ig * xc)
    return a, bx


def _lru_prompt_kernel(xl_ref, gate_ref, cw_ref, cb_ref, wa_ref, ba_ref, wi_ref, bi_ref, lam_ref,
                       out_ref, hlast_ref, ext_ref, hc_ref, *, tc, n_t):
    t = pl.program_id(1)

    @pl.when(t == 0)
    def _():
        ext_ref[0:8, :] = jnp.zeros((8, LRU_W), F32)
        hc_ref[...] = jnp.zeros((1, LRU_W), F32)

    xl = xl_ref[...]
    ext_ref[8:8 + tc, :] = xl
    xc = cb_ref[...] + cw_ref[LRU_CONV - 1:LRU_CONV, :] * xl
    for j in range(1, LRU_CONV):
        xc = xc + cw_ref[LRU_CONV - 1 - j:LRU_CONV - j, :] * ext_ref[pl.ds(8 - j, tc), :]
    ext_ref[0:8, :] = ext_ref[tc:tc + 8, :]

    a, b = _lru_gates(xc, wa_ref, ba_ref, wi_ref, bi_ref, lam_ref)
    row = lax.broadcasted_iota(jnp.int32, (tc, 1), 0)
    d = 1
    while d < tc:
        keep = row >= d
        a_sh = jnp.where(keep, pltpu.roll(a, d, 0), 1.0)
        b_sh = jnp.where(keep, pltpu.roll(b, d, 0), 0.0)
        b = a * b_sh + b
        a = a * a_sh
        d *= 2
    h = a * hc_ref[...] + b
    hc_ref[...] = h[tc - 1:tc, :]
    out_ref[...] = (_gelu_tanh(gate_ref[...]) * h).astype(BF16)

    @pl.when(t == n_t - 1)
    def _():
        hlast_ref[...] = h[tc - 1:tc, :]


def _lru_prompt(z, n_b, seq, cw, cb, wa, ba, wi, bi, lam):
    tc = SCAN_TILE
    n_t = seq // tc
    vec = lambda: pl.BlockSpec((1, LRU_W), lambda b, t: (0, 0))
    bd = lambda: pl.BlockSpec((LRU_W // LANES, LANES, LANES), lambda b, t: (0, 0, 0))
    return pl.pallas_call(
        functools.partial(_lru_prompt_kernel, tc=tc, n_t=n_t),
        grid=(n_b, n_t),
        in_specs=[pl.BlockSpec((tc, LRU_W), lambda b, t: (b * n_t + t, 0)),
                  pl.BlockSpec((tc, LRU_W), lambda b, t: (b * n_t + t, 1)),
                  pl.BlockSpec((LRU_CONV, LRU_W), lambda b, t: (0, 0)),
                  vec(), bd(), vec(), bd(), vec(), vec()],
        out_specs=[pl.BlockSpec((tc, LRU_W), lambda b, t: (b * n_t + t, 0)),
                   pl.BlockSpec((None, 1, LRU_W), lambda b, t: (b, 0, 0))],
        out_shape=[jax.ShapeDtypeStruct((n_b * seq, LRU_W), BF16),
                   jax.ShapeDtypeStruct((n_b, 1, LRU_W), F32)],
        scratch_shapes=[pltpu.VMEM((tc + 8, LRU_W), F32), pltpu.VMEM((1, LRU_W), F32)],
        compiler_params=_cp("parallel", "arbitrary"), name="lru_prompt")(
            z, z, cw, cb, wa, ba, wi, bi, lam)


def _lru_sample_kernel(z_ref, pre_ref, h0_ref, cw_ref, cb_ref, wa_ref, ba_ref, wi_ref, bi_ref, lam_ref,
                       out_ref, h_ref):
    xl = z_ref[:, 0:LRU_W]
    gate = z_ref[:, LRU_W:2 * LRU_W]
    xc = cb_ref[...] + cw_ref[LRU_CONV - 1:LRU_CONV, :] * xl
    for k in range(LRU_CONV - 1):
        xc = xc + cw_ref[k:k + 1, :] * pre_ref[k]
    a, bx = _lru_gates(xc, wa_ref, ba_ref, wi_ref, bi_ref, lam_ref)
    h = a * h0_ref[...] + bx
    h_ref[...] = h
    out_ref[...] = (_gelu_tanh(gate) * h).astype(BF16)


def _lru_sample(z, prefix_t, h0, cw, cb, wa, ba, wi, bi, lam):
    n_b = z.shape[0]
    return pl.pallas_call(
        _lru_sample_kernel,
        out_shape=[jax.ShapeDtypeStruct((n_b, LRU_W), BF16), jax.ShapeDtypeStruct((n_b, LRU_W), F32)],
        compiler_params=pltpu.CompilerParams(vmem_limit_bytes=VMEM_LIMIT),
        name="lru_sample")(z, prefix_t, h0, cw, cb, wa, ba, wi, bi, lam)


def _log_sigmoid(x):
    return jnp.minimum(x, 0.0) - jnp.log1p(jnp.exp(-jnp.abs(x)))


def _lane_cumsum(x):
    n = x.shape[1]
    lane = lax.broadcasted_iota(jnp.int32, x.shape, 1)
    d = 1
    while d < n:
        x = x + jnp.where(lane >= d, pltpu.roll(x, d, 1), 0.0)
        d *= 2
    return x


def _fox_prep_kernel(fl_ref, bf_ref, lf_ref, c_ref):
    lf = _log_sigmoid(fl_ref[0:FOX_H, :] + bf_ref[...])
    lf_ref[...] = lf
    c_ref[...] = _lane_cumsum(lf)


def _fox_prep(fl_t, bf, n_b, seq):
    return pl.pallas_call(
        _fox_prep_kernel, grid=(n_b,),
        in_specs=[pl.BlockSpec((fl_t.shape[0], seq), lambda b: (0, b)),
                  pl.BlockSpec((FOX_H, 1), lambda b: (0, 0))],
        out_specs=[pl.BlockSpec((FOX_H, seq), lambda b: (0, b)),
                   pl.BlockSpec((FOX_H, seq), lambda b: (0, b))],
        out_shape=[jax.ShapeDtypeStruct((FOX_H, n_b * seq), F32)] * 2,
        compiler_params=_cp("parallel"), name="fox_prep")(fl_t, bf)


def _logsig_kernel(fl_ref, bf_ref, lf_ref):
    lf_ref[...] = _log_sigmoid(fl_ref[0:FOX_H, :] + bf_ref[...])


def _logsig(fl_t, bf):
    return pl.pallas_call(
        _logsig_kernel, out_shape=jax.ShapeDtypeStruct((FOX_H, fl_t.shape[1]), F32),
        name="fox_logf_sample")(fl_t, bf)


def _fox_prompt_kernel(q_ref, k_ref, v_ref, cq_ref, ck_ref, o_ref, m_sc, l_sc, acc_sc, *, tb, n_k):
    hp = pl.program_id(1)
    qi = pl.program_id(2)
    ki = pl.program_id(3)

    @pl.when(ki == 0)
    def _():
        m_sc[...] = jnp.full(m_sc.shape, NEG, F32)
        l_sc[...] = jnp.zeros(l_sc.shape, F32)
        acc_sc[...] = jnp.zeros(acc_sc.shape, F32)

    @pl.when(ki <= qi)
    def _():
        q = q_ref[...] * (FOX_HD ** -0.5)
        k = k_ref[...].astype(BF16)
        v = v_ref[...].astype(BF16)
        lane = lax.broadcasted_iota(jnp.int32, (tb, LANES), 1)
        qpos = qi * tb + lax.broadcasted_iota(jnp.int32, (tb, tb), 0)
        kpos = ki * tb + lax.broadcasted_iota(jnp.int32, (tb, tb), 1)
        causal = kpos <= qpos
        for h in range(2):
            in_head = (lane < FOX_HD) if h == 0 else (lane >= FOX_HD)
            qh = jnp.where(in_head, q, 0.0).astype(BF16)
            s = _dot_nt(qh, k)
            c_q = cq_ref[pl.ds(2 * hp + h, 1), :]
            c_k = ck_ref[pl.ds(2 * hp + h, 1), :]
            s = jnp.where(causal, s + (c_q[:, 0:1] - c_k), NEG)
            m_prev = m_sc[h]
            m_new = jnp.maximum(m_prev, jnp.max(s, axis=1, keepdims=True))
            alpha = jnp.exp(m_prev - m_new)
            p = jnp.exp(s - m_new)
            l_sc[h] = alpha * l_sc[h] + jnp.sum(p, axis=1, keepdims=True)
            acc_sc[h] = alpha * acc_sc[h] + _dot(p.astype(BF16), v)
            m_sc[h] = m_new

    @pl.when(ki == n_k - 1)
    def _():
        lane = lax.broadcasted_iota(jnp.int32, (tb, LANES), 1)
        o0 = acc_sc[0] / l_sc[0]
        o1 = acc_sc[1] / l_sc[1]
        o_ref[...] = jnp.where(lane < FOX_HD, o0, o1).astype(BF16)


def _fox_prompt(z, c_t, n_b, seq):
    tb = SEQ_TILE
    n_q = seq // tb
    q_blk = (2 * LRU_W) // LANES
    k_blk = q_blk + FOX_W // LANES
    v_blk = k_blk + FOX_W // LANES
    return pl.pallas_call(
        functools.partial(_fox_prompt_kernel, tb=tb, n_k=n_q),
        grid=(n_b, FOX_H // 2, n_q, n_q),
        in_specs=[
            pl.BlockSpec((tb, LANES), lambda b, hp, qi, ki: (b * n_q + qi, q_blk + hp)),
            pl.BlockSpec((tb, LANES), lambda b, hp, qi, ki: (b * n_q + jnp.minimum(ki, qi), k_blk + hp)),
            pl.BlockSpec((tb, LANES), lambda b, hp, qi, ki: (b * n_q + jnp.minimum(ki, qi), v_blk + hp)),
            pl.BlockSpec((FOX_H, tb), lambda b, hp, qi, ki: (0, b * n_q + qi)),
            pl.BlockSpec((FOX_H, tb), lambda b, hp, qi, ki: (0, b * n_q + jnp.minimum(ki, qi))),
        ],
        out_specs=pl.BlockSpec((tb, LANES), lambda b, hp, qi, ki: (b * n_q + qi, hp)),
        out_shape=jax.ShapeDtypeStruct((n_b * seq, FOX_W), BF16),
        scratch_shapes=[pltpu.VMEM((2, tb, 1), F32), pltpu.VMEM((2, tb, 1), F32),
                        pltpu.VMEM((2, tb, LANES), F32)],
        compiler_params=_cp("parallel", "parallel", "parallel", "arbitrary"),
        name="fox_prompt")(z, z, z, c_t, c_t)


def _fox_sample_kernel(pt_ref, q_ref, kn_ref, vn_ref, lfn_ref, *rest, n_pg, n_g):
    k_refs = rest[0:n_pg]
    v_refs = rest[n_pg:2 * n_pg]
    lf_refs = rest[2 * n_pg:3 * n_pg]
    o_ref, m_sc, l_sc, acc_sc, cc_sc = rest[3 * n_pg:]
    g = pl.program_id(1)

    @pl.when(g == 0)
    def _():
        m_sc[...] = jnp.full(m_sc.shape, NEG, F32)
        l_sc[...] = jnp.zeros(l_sc.shape, F32)
        acc_sc[...] = jnp.zeros(acc_sc.shape, F32)
        cc_sc[...] = jnp.zeros(cc_sc.shape, F32)

    q = q_ref[...] * (FOX_HD ** -0.5)
    row = lax.broadcasted_iota(jnp.int32, (2 * FOX_H, FOX_W), 0)
    col = lax.broadcasted_iota(jnp.int32, (2 * FOX_H, FOX_W), 1)
    head_cols = jnp.right_shift(col, FOX_HD.bit_length() - 1) == row
    q_rows = jnp.where(head_cols, q, 0.0)
    q_rows_b = q_rows.astype(BF16)

    s_parts = [_dot_nt(q_rows_b, k_refs[j][...].astype(BF16))[0:FOX_H] for j in range(n_pg)]
    s = jnp.concatenate(s_parts, axis=1)
    lf = jnp.concatenate([lf_refs[j][...] for j in range(n_pg)], axis=1)
    c = _lane_cumsum(lf) + cc_sc[...]
    cc_sc[...] = c[:, c.shape[1] - 1:]
    s = s - c
    m_prev = m_sc[...]
    m_new = jnp.maximum(m_prev, jnp.max(s, axis=1, keepdims=True))
    alpha = jnp.exp(m_prev - m_new)
    p = jnp.exp(s - m_new)
    l_sc[...] = alpha * l_sc[...] + jnp.sum(p, axis=1, keepdims=True)
    p16 = jnp.concatenate([p, jnp.zeros_like(p)], axis=0).astype(BF16)
    page = k_refs[0].shape[0]
    pv = None
    for j in range(n_pg):
        part = _dot(p16[:, j * page:(j + 1) * page], v_refs[j][...].astype(BF16))[0:FOX_H]
        pv = part if pv is None else pv + part
    acc_sc[...] = alpha * acc_sc[...] + pv
    m_sc[...] = m_new

    @pl.when(g == n_g - 1)
    def _():
        s_n = jnp.sum(q_rows[0:FOX_H] * kn_ref[...], axis=1, keepdims=True)
        s_n = s_n - (cc_sc[...] + lfn_ref[...])
        m_p = m_sc[...]
        m_n = jnp.maximum(m_p, s_n)
        al = jnp.exp(m_p - m_n)
        p_n = jnp.exp(s_n - m_n)
        l_n = al * l_sc[...] + p_n
        acc = al * acc_sc[...] + p_n * vn_ref[...]
        o = jnp.where(head_cols[0:FOX_H], acc / l_n, 0.0)
        o_ref[...] = jnp.sum(o, axis=0, keepdims=True).astype(BF16)


def _fox_sample(page_table, q, k_new, v_new, lf_new, cache_k, cache_v, cache_lf_t):
    n_b, n_pages = page_table.shape
    n_pg = PAGES_PER_STEP
    n_g = n_pages // n_pg
    page = cache_k.shape[1]
    row_spec = lambda: pl.BlockSpec((None, 1, FOX_W), lambda b, g, pt: (b, 0, 0))

    def page_spec(shape, j):
        return pl.BlockSpec((None,) + shape,
                            lambda b, g, pt, j=j: (pt[b * n_pages + g * n_pg + j], 0, 0))

    in_specs = [row_spec(), row_spec(), row_spec(),
                pl.BlockSpec((None, FOX_H, 1), lambda b, g, pt: (b, 0, 0))]
    in_specs += [page_spec((page, FOX_W), j) for j in range(n_pg)]
    in_specs += [page_spec((page, FOX_W), j) for j in range(n_pg)]
    in_specs += [page_spec((FOX_H, page), j) for j in range(n_pg)]
    grid_spec = pltpu.PrefetchScalarGridSpec(
        num_scalar_prefetch=1, grid=(n_b, n_g), in_specs=in_specs,
        out_specs=pl.BlockSpec((None, 1, FOX_W), lambda b, g, pt: (b, 0, 0)),
        scratch_shapes=[pltpu.VMEM((FOX_H, 1), F32), pltpu.VMEM((FOX_H, 1), F32),
                        pltpu.VMEM((FOX_H, FOX_W), F32), pltpu.VMEM((FOX_H, 1), F32)])
    return pl.pallas_call(
        functools.partial(_fox_sample_kernel, n_pg=n_pg, n_g=n_g),
        grid_spec=grid_spec,
        out_shape=jax.ShapeDtypeStruct((n_b, 1, FOX_W), BF16),
        compiler_params=_cp("parallel", "arbitrary"), name="fox_sample")(
            page_table.reshape(-1), q, k_new, v_new, lf_new,
            *([cache_k] * n_pg), *([cache_v] * n_pg), *([cache_lf_t] * n_pg))


def _xattn_prompt_kernel(q_ref, mk_ref, mv_ref, o_ref):
    outs = []
    for h in range(MEM_H):
        cs = slice(h * MEM_HD, (h + 1) * MEM_HD)
        qh = q_ref[:, cs].astype(BF16)
        s = _dot_nt(qh, mk_ref[:, cs].astype(BF16)) * (MEM_HD ** -0.5)
        p = jnp.exp(s - jnp.max(s, axis=1, keepdims=True))
        l = jnp.sum(p, axis=1, keepdims=True)
        outs.append(_dot(p.astype(BF16), mv_ref[:, cs].astype(BF16)) / l)
    o_ref[...] = jnp.concatenate(outs, axis=1).astype(BF16)


def _xattn_prompt(q, memkv, n_b, seq):
    tb = SEQ_TILE
    n_t = seq // tb
    mem_len = memkv.shape[0] // n_b
    return pl.pallas_call(
        _xattn_prompt_kernel, grid=(n_b, n_t),
        in_specs=[pl.BlockSpec((tb, MEM_W), lambda b, t: (b * n_t + t, 0)),
                  pl.BlockSpec((mem_len, MEM_W), lambda b, t: (b, 0)),
                  pl.BlockSpec((mem_len, MEM_W), lambda b, t: (b, 1))],
        out_specs=pl.BlockSpec((tb, MEM_W), lambda b, t: (b * n_t + t, 0)),
        out_shape=jax.ShapeDtypeStruct((n_b * seq, MEM_W), BF16),
        compiler_params=_cp("parallel", "parallel"), name="xattn_prompt")(q, memkv, memkv)


def _xattn_sample_kernel(q_ref, mk_ref, mv_ref, o_ref):
    q = q_ref[...]
    row = lax.broadcasted_iota(jnp.int32, (16, MEM_W), 0)
    col = lax.broadcasted_iota(jnp.int32, (16, MEM_W), 1)
    head_cols = jnp.right_shift(col, MEM_HD.bit_length() - 1) == row
    q_rows = jnp.where(head_cols, q, 0.0).astype(BF16)
    s = _dot_nt(q_rows, mk_ref[...].astype(BF16)) * (MEM_HD ** -0.5)
    p = jnp.exp(s - jnp.max(s, axis=1, keepdims=True))
    l = jnp.sum(p, axis=1, keepdims=True)
    o = _dot(p.astype(BF16), mv_ref[...].astype(BF16)) / l
    o_ref[...] = jnp.sum(jnp.where(head_cols, o, 0.0), axis=0, keepdims=True).astype(BF16)


def _xattn_sample(q, mk, mv):
    n_b, mem_len, _ = mk.shape
    return pl.pallas_call(
        _xattn_sample_kernel, grid=(n_b,),
        in_specs=[pl.BlockSpec((None, 1, MEM_W), lambda b: (b, 0, 0)),
                  pl.BlockSpec((None, mem_len, MEM_W), lambda b: (b, 0, 0)),
                  pl.BlockSpec((None, mem_len, MEM_W), lambda b: (b, 0, 0))],
        out_specs=pl.BlockSpec((None, 1, MEM_W), lambda b: (b, 0, 0)),
        out_shape=jax.ShapeDtypeStruct((n_b, 1, MEM_W), BF16),
        compiler_params=_cp("parallel"), name="xattn_sample")(q, mk, mv)


def _ffn_kernel(x_ref, g4_ref, g5_ref, w1_ref, w3_ref, w2_ref, o_ref, hn_sc, acc_sc, *, n_f):
    f = pl.program_id(1)

    @pl.when(f == 0)
    def _():
        hn_sc[...] = _rms(x_ref[...], g4_ref[...]).astype(BF16)
        acc_sc[...] = jnp.zeros(acc_sc.shape, F32)

    hn = hn_sc[...]
    h1 = _dot(hn, w1_ref[...])
    h3 = _dot(hn, w3_ref[...])
    hh = (h1 * _sigmoid(h1) * h3).astype(BF16)
    acc_sc[...] += _dot(hh, w2_ref[...])

    @pl.when(f == n_f - 1)
    def _():
        o_ref[...] = x_ref[...] + _rms(acc_sc[...], g5_ref[...])


def _ffn(x, g4, g5, w1, w3, w2, *, tm):
    m, d = x.shape
    ff = w1.shape[1]
    tf = FFN_FF_TILE
    n_f = ff // tf
    return pl.pallas_call(
        functools.partial(_ffn_kernel, n_f=n_f), grid=(m // tm, n_f),
        in_specs=[pl.BlockSpec((tm, d), lambda i, f: (i, 0)),
                  pl.BlockSpec((1, d), lambda i, f: (0, 0)),
                  pl.BlockSpec((1, d), lambda i, f: (0, 0)),
                  pl.BlockSpec((d, tf), lambda i, f: (0, f)),
                  pl.BlockSpec((d, tf), lambda i, f: (0, f)),
                  pl.BlockSpec((tf, d), lambda i, f: (f, 0))],
        out_specs=pl.BlockSpec((tm, d), lambda i, f: (i, 0)),
        out_shape=jax.ShapeDtypeStruct((m, d), F32),
        scratch_shapes=[pltpu.VMEM((tm, d), BF16), pltpu.VMEM((tm, d), F32)],
        compiler_params=_cp("parallel", "arbitrary"), name="ffn")(x, g4, g5, w1, w3, w2)


def _layernorm_silu(x, g, b):
    mu = jnp.mean(x, axis=-1, keepdims=True)
    xc = x - mu
    y = xc * lax.rsqrt(jnp.mean(xc * xc, axis=-1, keepdims=True) + EPS) * g + b
    return y * _sigmoid(y)


def _odd_prompt_kernel(a_ref, gt_ref, up_ref, dww_ref, dwb_ref, lng_ref, lnb_ref, pw_ref, ps_ref,
                       cv_ref, pool_ref, cbuf_ref, pbuf_ref, eg_ref, eu_ref, *, tc, n_t):
    t = pl.program_id(1)
    halo_g, halo_u = 32, 16

    @pl.when(t == 0)
    def _():
        eg_ref[0:halo_g, :] = jnp.zeros((halo_g, CONV_W), F32)
        eu_ref[0:halo_u, :] = jnp.zeros((halo_u, POOL_W), F32)

    glu = a_ref[...] * _sigmoid(gt_ref[...])
    up = up_ref[...]
    eg_ref[halo_g:halo_g + tc, :] = glu
    eu_ref[halo_u:halo_u + tc, :] = up

    acc = dwb_ref[...] + dww_ref[CONV_K - 1:CONV_K, :] * glu
    for k in range(CONV_K - 1):
        acc = acc + dww_ref[k:k + 1, :] * eg_ref[pl.ds(halo_g - (CONV_K - 1) + k, tc), :]
    cv_ref[...] = _layernorm_silu(acc, lng_ref[...], lnb_ref[...]).astype(BF16)

    pos = t * tc + lax.broadcasted_iota(jnp.int32, (tc, 1), 0)
    outs = []
    for gi, w in enumerate(POOL_WINDOWS):
        cs = slice(gi * POOL_GW, (gi + 1) * POOL_GW)
        u_g = up[:, cs]
        win = u_g
        for j in range(1, w):
            win = win + eu_ref[pl.ds(halo_u - j, tc), cs]
        cnt = jnp.minimum(pos + 1, w).astype(F32)
        dlt = win / cnt - u_g
        outs.append(_dot(dlt.astype(BF16), pw_ref[gi]))
    pool_ref[...] = (jnp.concatenate(outs, axis=1) * ps_ref[...]).astype(BF16)

    @pl.when(t == n_t - 1)
    def _():
        cbuf_ref[...] = eg_ref[pl.ds(halo_g + tc - (CONV_K - 1), CONV_K - 1), :]
        pbuf_ref[...] = eu_ref[pl.ds(halo_u + tc - POOL_BUF, POOL_BUF), :]

    eg_ref[0:halo_g, :] = eg_ref[tc:tc + halo_g, :]
    eu_ref[0:halo_u, :] = eu_ref[tc:tc + halo_u, :]


def _odd_prompt(z, n_b, seq, dww, dwb, lng, lnb, pw, ps):
    tc = SEQ_TILE
    n_t = seq // tc
    vec = lambda: pl.BlockSpec((1, CONV_W), lambda b, t: (0, 0))
    return pl.pallas_call(
        functools.partial(_odd_prompt_kernel, tc=tc, n_t=n_t), grid=(n_b, n_t),
        in_specs=[pl.BlockSpec((tc, CONV_W), lambda b, t: (b * n_t + t, 0)),
                  pl.BlockSpec((tc, CONV_W), lambda b, t: (b * n_t + t, 1)),
                  pl.BlockSpec((tc, POOL_W), lambda b, t: (b * n_t + t, 2)),
                  pl.BlockSpec((CONV_K, CONV_W), lambda b, t: (0, 0)),
                  vec(), vec(), vec(),
                  pl.BlockSpec((len(POOL_WINDOWS), POOL_GW, POOL_GW), lambda b, t: (0, 0, 0)),
                  vec()],
        out_specs=[pl.BlockSpec((tc, CONV_W), lambda b, t: (b * n_t + t, 0)),
                   pl.BlockSpec((tc, POOL_W), lambda b, t: (b * n_t + t, 0)),
                   pl.BlockSpec((None, CONV_K - 1, CONV_W), lambda b, t: (b, 0, 0)),
                   pl.BlockSpec((None, POOL_BUF, POOL_W), lambda b, t: (b, 0, 0))],
        out_shape=[jax.ShapeDtypeStruct((n_b * seq, CONV_W), BF16),
                   jax.ShapeDtypeStruct((n_b * seq, POOL_W), BF16),
                   jax.ShapeDtypeStruct((n_b, CONV_K - 1, CONV_W), F32),
                   jax.ShapeDtypeStruct((n_b, POOL_BUF, POOL_W), F32)],
        scratch_shapes=[pltpu.VMEM((tc + 32, CONV_W), F32), pltpu.VMEM((tc + 16, POOL_W), F32)],
        compiler_params=_cp("parallel", "arbitrary"), name="odd_prompt")(
            z, z, z, dww, dwb, lng, lnb, pw, ps)


def _odd_sample_kernel(z_ref, cbuf_ref, pbuf_ref, dww_ref, dwb_ref, lng_ref, lnb_ref, pw_ref, ps_ref,
                       cv_ref, pool_ref, glu_ref, *, pos0):
    glu = z_ref[:, 0:CONV_W] * _sigmoid(z_ref[:, CONV_W:2 * CONV_W])
    up = z_ref[:, 2 * CONV_W:2 * CONV_W + POOL_W]
    glu_ref[...] = glu
    acc = dwb_ref[...] + dww_ref[CONV_K - 1:CONV_K, :] * glu
    for k in range(CONV_K - 1):
        acc = acc + dww_ref[k:k + 1, :] * cbuf_ref[k]
    cv_ref[...] = _layernorm_silu(acc, lng_ref[...], lnb_ref[...]).astype(BF16)
    outs = []
    for gi, w in enumerate(POOL_WINDOWS):
        cs = slice(gi * POOL_GW, (gi + 1) * POOL_GW)
        u_g = up[:, cs]
        win = u_g
        for j in range(1, w):
            win = win + pbuf_ref[POOL_BUF - j][:, cs]
        dlt = win / float(min(pos0 + 1, w)) - u_g
        outs.append(_dot(dlt.astype(BF16), pw_ref[gi]))
    pool_ref[...] = (jnp.concatenate(outs, axis=1) * ps_ref[...]).astype(BF16)


def _odd_sample(z, cbuf_t, pbuf_t, dww, dwb, lng, lnb, pw, ps, pos0):
    n_b = z.shape[0]
    return pl.pallas_call(
        functools.partial(_odd_sample_kernel, pos0=pos0),
        out_shape=[jax.ShapeDtypeStruct((n_b, CONV_W), BF16),
                   jax.ShapeDtypeStruct((n_b, POOL_W), BF16),
                   jax.ShapeDtypeStruct((n_b, CONV_W), F32)],
        compiler_params=pltpu.CompilerParams(vmem_limit_bytes=VMEM_LIMIT),
        name="odd_sample")(z, cbuf_t, pbuf_t, dww, dwb, lng, lnb, pw, ps)


def _router_kernel(x_ref, g_ref, whi_ref, wlo_ref, b_ref, hn_ref, idx_ref, gate_ref):
    hn = _rms(x_ref[...], g_ref[...])
    hb = hn.astype(BF16)
    hn_ref[...] = hb
    hlo = (hn - hb.astype(F32)).astype(BF16)
    logits = _dot(hb, whi_ref[...]) + (_dot(hb, wlo_ref[...]) + _dot(hlo, whi_ref[...])) + b_ref[...]
    lane = lax.broadcasted_iota(jnp.int32, logits.shape, 1)
    lane_f = lane.astype(F32)
    logits = jnp.where(lane < N_EXPERTS, logits, NEG)
    m1 = jnp.max(logits, axis=1, keepdims=True)
    i1 = jnp.min(jnp.where(logits == m1, lane_f, float(LANES)), axis=1, keepdims=True)
    rest = jnp.where(lane_f == i1, NEG, logits)
    m2 = jnp.max(rest, axis=1, keepdims=True)
    i2 = jnp.min(jnp.where(rest == m2, lane_f, float(LANES)), axis=1, keepdims=True)
    e = jnp.exp(m2 - m1)
    g1 = 1.0 / (1.0 + e)
    g2 = e / (1.0 + e)
    idx_ref[...] = jnp.where(lane == 0, i1, jnp.where(lane == 1, i2, 0.0)).astype(jnp.int32)
    gate_ref[...] = jnp.where(lane == 0, g1, jnp.where(lane == 1, g2, 0.0))


def _router(x, g, whi, wlo, b, *, tm):
    m, d = x.shape
    return pl.pallas_call(
        _router_kernel, grid=(m // tm,),
        in_specs=[pl.BlockSpec((tm, d), lambda i: (i, 0)),
                  pl.BlockSpec((1, d), lambda i: (0, 0)),
                  pl.BlockSpec((d, LANES), lambda i: (0, 0)),
                  pl.BlockSpec((d, LANES), lambda i: (0, 0)),
                  pl.BlockSpec((1, LANES), lambda i: (0, 0))],
        out_specs=[pl.BlockSpec((tm, d), lambda i: (i, 0)),
                   pl.BlockSpec((tm, LANES), lambda i: (i, 0)),
                   pl.BlockSpec((tm, LANES), lambda i: (i, 0))],
        out_shape=[jax.ShapeDtypeStruct((m, d), BF16),
                   jax.ShapeDtypeStruct((m, LANES), jnp.int32),
                   jax.ShapeDtypeStruct((m, LANES), F32)],
        compiler_params=_cp("parallel"), name="router")(x, g, whi, wlo, b)


def _gmm_kernel(te_ref, tv_ref, x_ref, gate_ref, w1_ref, w3_ref, w2_ref, o_ref, acc_sc, *, n_f):
    t = pl.program_id(0)
    f = pl.program_id(1)

    @pl.when(f == 0)
    def _():
        acc_sc[...] = jnp.zeros(acc_sc.shape, F32)

    @pl.when(tv_ref[t] != 0)
    def _():
        x = x_ref[...]
        h1 = _dot(x, w1_ref[...])
        h3 = _dot(x, w3_ref[...])
        hh = (h1 * _sigmoid(h1) * h3).astype(BF16)
        acc_sc[...] += _dot(hh, w2_ref[...])

    @pl.when(f == n_f - 1)
    def _():
        o_ref[...] = acc_sc[...] * gate_ref[...]


def _gmm(tile_expert, tile_valid, x_sorted, gate_sorted, w1, w3, w2):
    rows, d = x_sorted.shape
    ff = w1.shape[2]
    tm, tf = MOE_TILE, MOE_FF_TILE
    n_f = ff // tf
    grid_spec = pltpu.PrefetchScalarGridSpec(
        num_scalar_prefetch=2, grid=(rows // tm, n_f),
        in_specs=[pl.BlockSpec((tm, d), lambda t, f, te, tv: (t, 0)),
                  pl.BlockSpec((tm, 1), lambda t, f, te, tv: (t, 0)),
                  pl.BlockSpec((None, d, tf), lambda t, f, te, tv: (te[t], 0, f)),
                  pl.BlockSpec((None, d, tf), lambda t, f, te, tv: (te[t], 0, f)),
                  pl.BlockSpec((None, tf, d), lambda t, f, te, tv: (te[t], f, 0))],
        out_specs=pl.BlockSpec((tm, d), lambda t, f, te, tv: (t, 0)),
        scratch_shapes=[pltpu.VMEM((tm, d), F32)])
    return pl.pallas_call(
        functools.partial(_gmm_kernel, n_f=n_f), grid_spec=grid_spec,
        out_shape=jax.ShapeDtypeStruct((rows, d), F32),
        compiler_params=_cp("parallel", "arbitrary"), name="moe_experts")(
            tile_expert, tile_valid, x_sorted, gate_sorted, w1, w3, w2)


def _combine_kernel(x_ref, y1_ref, y2_ref, g_ref, o_ref):
    o_ref[...] = x_ref[...] + _rms(y1_ref[...] + y2_ref[...], g_ref[...])


def _combine(x, y1, y2, g, *, tm):
    m, d = x.shape
    row = lambda: pl.BlockSpec((tm, d), lambda i: (i, 0))
    return pl.pallas_call(
        _combine_kernel, grid=(m // tm,),
        in_specs=[row(), row(), row(), pl.BlockSpec((1, d), lambda i: (0, 0))],
        out_specs=row(), out_shape=jax.ShapeDtypeStruct((m, d), F32),
        compiler_params=_cp("parallel"), name="moe_combine")(x, y1, y2, g)


def _moe_block(xs, g4, g5, w_r, b_r, w1, w3, w2):
    pad = LANES - N_EXPERTS
    w_r_p = jnp.pad(w_r, ((0, 0), (0, pad)))
    whi = w_r_p.astype(BF16)
    wlo = (w_r_p - whi.astype(F32)).astype(BF16)
    b_p = jnp.pad(b_r, (0, pad)).reshape(1, LANES)
    hn, idx, gate = [], [], []
    for x in xs:
        h, i, gt = _router(x, g4, whi, wlo, b_p, tm=min(ROW_TILE, x.shape[0]))
        hn.append(h)
        idx.append(i[:, :2])
        gate.append(gt[:, :2])
    hn = jnp.concatenate(hn, axis=0)
    e_flat = jnp.concatenate(idx, axis=0).reshape(-1)
    g_flat = jnp.concatenate(gate, axis=0).reshape(-1)
    n_assign = e_flat.shape[0]
    tm = MOE_TILE
    n_tiles = n_assign // tm + N_EXPERTS
    rows = n_tiles * tm

    onehot = (e_flat[:, None] == jnp.arange(N_EXPERTS, dtype=jnp.int32)[None, :]).astype(jnp.int32)
    rank = jnp.sum((jnp.cumsum(onehot, axis=0) - onehot) * onehot, axis=1)
    cnt = jnp.sum(onehot, axis=0)
    tiles_e = (cnt + tm - 1) // tm
    tile_end = jnp.cumsum(tiles_e)
    row_start = (tile_end - tiles_e) * tm
    dest = row_start[e_flat] + rank
    src_tok = jnp.zeros((rows,), jnp.int32).at[dest].set(jnp.arange(n_assign, dtype=jnp.int32) // 2)
    gate_sorted = jnp.zeros((rows,), F32).at[dest].set(g_flat).reshape(rows, 1)
    tile_ids = jnp.arange(n_tiles, dtype=jnp.int32)
    tile_expert = jnp.minimum(jnp.searchsorted(tile_end, tile_ids, side="right"),
                              N_EXPERTS - 1).astype(jnp.int32)
    tile_valid = (tile_ids < tile_end[-1]).astype(jnp.int32)

    x_sorted = jnp.take(hn, src_tok, axis=0)
    y_sorted = _gmm(tile_expert, tile_valid, x_sorted, gate_sorted, w1, w3, w2)

    outs = []
    off = 0
    dest2 = dest.reshape(-1, 2)
    for x in xs:
        m = x.shape[0]
        d1 = dest2[off:off + m, 0]
        d2 = dest2[off:off + m, 1]
        outs.append(_combine(x, jnp.take(y_sorted, d1, axis=0), jnp.take(y_sorted, d2, axis=0), g5,
                             tm=min(ROW_TILE, m)))
        off += m
    return outs


def _block_diag_pairs(w):
    nb, bw, _ = w.shape
    w = w.reshape(nb // 2, 2, bw, bw)
    z = jnp.zeros((nb // 2, bw, bw), w.dtype)
    top = jnp.concatenate([w[:, 0], z], axis=2)
    bot = jnp.concatenate([z, w[:, 1]], axis=2)
    return jnp.concatenate([top, bot], axis=1).astype(BF16)


def kernel(x_prompt, x_sample, cache_fox_k, cache_fox_v, cache_fox_logf, state_lru_h, state_lru_conv, state_conv_buf, state_pool_buf, cache_mem_k, cache_mem_v, page_table, mem_prompt, norm_g, w_xq, w_xk, w_xv, w_xo, w_in_e, b_f, lru_conv_w, lru_conv_b, lru_wa, lru_ba, lru_wi, lru_bi, lru_lam, w_out_e, w_ff1, w_ff3, w_ff2, w_in_o, cc_dw_w, cc_dw_b, cc_ln_g, cc_ln_b, pool_w, pool_scale, w_out_o, w_router, b_router, w_e1, w_e3, w_e2):
    bp, seq, d = x_prompt.shape
    bs = x_sample.shape[0]
    depth = norm_g.shape[0]
    page = cache_fox_k.shape[2]
    past_len = page_table.shape[1] * page
    mem_len = mem_prompt.shape[1]
    tm_p = ROW_TILE

    xp = x_prompt.reshape(bp * seq, d)
    xs = x_sample.reshape(bs, d)
    mem = mem_prompt.reshape(bp * mem_len, d)
    vec = lambda v: v.reshape(1, -1)

    fk_p, fv_p, fl_p, lh_p, lc_p, cb_p, pb_p, mk_pl, mv_pl = [], [], [], [], [], [], [], [], []
    fk_s, fv_s, fl_s, lh_s, lc_s, cb_s, pb_s = [], [], [], [], [], [], []

    for l in range(depth):
        g = [vec(norm_g[l, i]) for i in range(norm_g.shape[1])]
        w_kv = jnp.concatenate([w_xk[l], w_xv[l]], axis=1).astype(BF16)
        memkv = _norm_matmul(mem, g[6], w_kv, tm=tm_p, name="mem_kv")
        mk_pl.append(memkv[:, :MEM_W].reshape(bp, mem_len, MEM_H, MEM_HD))
        mv_pl.append(memkv[:, MEM_W:].reshape(bp, mem_len, MEM_H, MEM_HD))

        if l % 2 == 0:
            i = l // 2
            n_main = 2 * LRU_W + 3 * FOX_W
            w_main = w_in_e[i][:, :n_main].astype(BF16)
            w_fl_t = jnp.pad(w_in_e[i][:, n_main:].T, ((0, 16 - FOX_H), (0, 0))).astype(BF16)
            cw, cb = lru_conv_w[i], vec(lru_conv_b[i])
            wa, wi = _block_diag_pairs(lru_wa[i]), _block_diag_pairs(lru_wi[i])
            ba, bi, lam = vec(lru_ba[i]), vec(lru_bi[i]), vec(lru_lam[i])
            bf = b_f[i].reshape(FOX_H, 1)
            w_out = w_out_e[i].astype(BF16)

            z, fl_t = _norm_matmul(xp, g[0], w_main, tm=tm_p, wt=w_fl_t, name="in_proj_even")
            lf_t, c_t = _fox_prep(fl_t, bf, bp, seq)
            lru_out, h_last = _lru_prompt(z, bp, seq, cw, cb, wa, ba, wi, bi, lam)
            att = _fox_prompt(z, c_t, bp, seq)
            xp = _matmul_norm_res([lru_out, att], w_out, g[1], xp, tm=tm_p, name="out_proj_even")
            z3 = z.reshape(bp, seq, n_main)
            fk_p.append(z3[:, :, 2 * LRU_W + FOX_W:2 * LRU_W + 2 * FOX_W].reshape(bp, seq, FOX_H, FOX_HD))
            fv_p.append(z3[:, :, 2 * LRU_W + 2 * FOX_W:].reshape(bp, seq, FOX_H, FOX_HD))
            fl_p.append(lf_t.T.reshape(bp, seq, FOX_H))
            lh_p.append(h_last.reshape(bp, LRU_W))
            lc_p.append(z3[:, seq - (LRU_CONV - 1):, :LRU_W])

            zs, fls_t = _norm_matmul(xs, g[0], w_main, tm=bs, wt=w_fl_t, name="in_proj_even_s")
            lfs_t = _logsig(fls_t, bf)
            pre_t = jnp.swapaxes(state_lru_conv[i], 0, 1)
            lru_out_s, h_s = _lru_sample(zs, pre_t, state_lru_h[i], cw, cb, wa, ba, wi, bi, lam)
            q_s = zs[:, 2 * LRU_W:2 * LRU_W + FOX_W]
            k_s = zs[:, 2 * LRU_W + FOX_W:2 * LRU_W + 2 * FOX_W]
            v_s = zs[:, 2 * LRU_W + 2 * FOX_W:]
            n_pool = cache_fox_k.shape[1]
            att_s = _fox_sample(page_table, q_s.reshape(bs, 1, FOX_W), k_s.reshape(bs, 1, FOX_W),
                                v_s.reshape(bs, 1, FOX_W), lfs_t.T.reshape(bs, FOX_H, 1),
                                cache_fox_k[i].reshape(n_pool, page, FOX_W),
                                cache_fox_v[i].reshape(n_pool, page, FOX_W),
                                jnp.swapaxes(cache_fox_logf[i], 1, 2))
            xs = _matmul_norm_res([lru_out_s, att_s.reshape(bs, FOX_W)], w_out, g[1], xs, tm=bs,
                                  name="out_proj_even_s")
            fk_s.append(k_s.reshape(bs, 1, FOX_H, FOX_HD))
            fv_s.append(v_s.reshape(bs, 1, FOX_H, FOX_HD))
            fl_s.append(lfs_t.T.reshape(bs, 1, FOX_H))
            lh_s.append(h_s)
            lc_s.append(jnp.concatenate([state_lru_conv[i][:, 1:], zs[:, None, :LRU_W]], axis=1))
        else:
            j = l // 2
            w_in = w_in_o[j].astype(BF16)
            dww, dwb = cc_dw_w[j], vec(cc_dw_b[j])
            lng, lnb = vec(cc_ln_g[j]), vec(cc_ln_b[j])
            pw, ps = pool_w[j].astype(BF16), vec(pool_scale[j])
            w_out = w_out_o[j].astype(BF16)

            z = _norm_matmul(xp, g[0], w_in, tm=tm_p, name="in_proj_odd")
            cv, pool, cbuf, pbuf = _odd_prompt(z, bp, seq, dww, dwb, lng, lnb, pw, ps)
            xp = _matmul_norm_res([cv, pool], w_out, g[1], xp, tm=tm_p, name="out_proj_odd")
            cb_p.append(cbuf)
            pb_p.append(pbuf)

            zs = _norm_matmul(xs, g[0], w_in, tm=bs, name="in_proj_odd_s")
            cv_s, pool_s, glu_s = _odd_sample(zs, jnp.swapaxes(state_conv_buf[j], 0, 1),
                                              jnp.swapaxes(state_pool_buf[j], 0, 1),
                                              dww, dwb, lng, lnb, pw, ps, past_len)
            xs = _matmul_norm_res([cv_s, pool_s], w_out, g[1], xs, tm=bs, name="out_proj_odd_s")
            cb_s.append(jnp.concatenate([state_conv_buf[j][:, 1:], glu_s[:, None, :]], axis=1))
            pb_s.append(jnp.concatenate([state_pool_buf[j][:, 1:], zs[:, None, 2 * CONV_W:]], axis=1))

        wq, wo = w_xq[l].astype(BF16), w_xo[l].astype(BF16)
        q = _norm_matmul(xp, g[2], wq, tm=tm_p, name="xattn_q")
        o = _xattn_prompt(q, memkv, bp, seq)
        xp = _matmul_norm_res([o], wo, g[3], xp, tm=tm_p, name="xattn_o")
        q_s = _norm_matmul(xs, g[2], wq, tm=bs, name="xattn_q_s")
        o_s = _xattn_sample(q_s.reshape(bs, 1, MEM_W), cache_mem_k[l].reshape(bs, mem_len, MEM_W),
                            cache_mem_v[l].reshape(bs, mem_len, MEM_W))
        xs = _matmul_norm_res([o_s.reshape(bs, MEM_W)], wo, g[3], xs, tm=bs, name="xattn_o_s")

        if l % 2 == 0:
            i = l // 2
            w1, w3, w2 = w_ff1[i].astype(BF16), w_ff3[i].astype(BF16), w_ff2[i].astype(BF16)
            xp = _ffn(xp, g[4], g[5], w1, w3, w2, tm=tm_p)
            xs = _ffn(xs, g[4], g[5], w1, w3, w2, tm=bs)
        else:
            j = l // 2
            xp, xs = _moe_block([xp, xs], g[4], g[5], w_router[j], b_router[j],
                                w_e1[j].astype(BF16), w_e3[j].astype(BF16), w_e2[j].astype(BF16))

    return (xp.reshape(bp, seq, d), xs.reshape(bs, 1, d),
            jnp.stack(fk_p), jnp.stack(fv_p), jnp.stack(fl_p), jnp.stack(lh_p), jnp.stack(lc_p),
            jnp.stack(cb_p), jnp.stack(pb_p), jnp.stack(mk_pl), jnp.stack(mv_pl),
            jnp.stack(fk_s), jnp.stack(fv_s), jnp.stack(fl_s), jnp.stack(lh_s), jnp.stack(lc_s),
            jnp.stack(cb_s), jnp.stack(pb_s))
```

```python
import functools

import jax
import jax.numpy as jnp
from jax import lax
from jax.experimental import pallas as pl
from jax.experimental.pallas import tpu as pltpu

F32 = jnp.float32
BF16 = jnp.bfloat16

D_MODEL = 1024
LRU_W = 512
LRU_CONV = 4
LRU_C = 8.0
FOX_H = 8
FOX_HD = 64
FOX_W = FOX_H * FOX_HD
CONV_W = 512
CONV_K = 31
POOL_W = 512
POOL_WINDOWS = (2, 4, 8, 16)
POOL_GW = POOL_W // len(POOL_WINDOWS)
POOL_BUF = max(POOL_WINDOWS) - 1
MEM_H = 4
MEM_HD = 128
MEM_W = MEM_H * MEM_HD
N_EXPERTS = 8
EPS = 1e-6
NEG = -1e30

LANES = 128
ROW_TILE = 512
SEQ_TILE = 512
SCAN_TILE = 256
ATT_ROWS = 32
PAGES_PER_STEP = 8
MOE_TILE = 512
MOE_FF_TILE = 512
FFN_FF_TILE = 1408
VMEM_LIMIT = 56 * 1024 * 1024


def _cp(*sem):
    return pltpu.CompilerParams(dimension_semantics=sem, vmem_limit_bytes=VMEM_LIMIT)


def _rms(x, g):
    return x * lax.rsqrt(jnp.mean(x * x, axis=-1, keepdims=True) + EPS) * g


def _sigmoid(x):
    return 1.0 / (1.0 + jnp.exp(-x))


def _softplus(x):
    return jnp.maximum(x, 0.0) + jnp.log1p(jnp.exp(-jnp.abs(x)))


def _gelu_tanh(x):
    return 0.5 * x * (1.0 + jnp.tanh(0.7978845608028654 * (x + 0.044715 * (x * x * x))))


def _dot(a, b):
    return jnp.dot(a, b, preferred_element_type=F32)


def _dot_nt(a, b):
    return lax.dot_general(a, b, (((1,), (1,)), ((), ())), preferred_element_type=F32)


def _norm_matmul_kernel(x_ref, g_ref, w_ref, *rest, has_t):
    hn = _rms(x_ref[...], g_ref[...]).astype(BF16)
    if has_t:
        wt_ref, o_ref, ot_ref = rest
        ot_ref[...] = _dot_nt(wt_ref[...], hn)
    else:
        (o_ref,) = rest
    o_ref[...] = _dot(hn, w_ref[...])


def _norm_matmul(x, g, w, *, tm, wt=None, name):
    m, d = x.shape
    n = w.shape[1]
    in_specs = [pl.BlockSpec((tm, d), lambda i: (i, 0)),
                pl.BlockSpec((1, d), lambda i: (0, 0)),
                pl.BlockSpec((d, n), lambda i: (0, 0))]
    out_shape = [jax.ShapeDtypeStruct((m, n), F32)]
    out_specs = [pl.BlockSpec((tm, n), lambda i: (i, 0))]
    args = [x, g, w]
    if wt is not None:
        in_specs.append(pl.BlockSpec(wt.shape, lambda i: (0, 0)))
        out_shape.append(jax.ShapeDtypeStruct((wt.shape[0], m), F32))
        out_specs.append(pl.BlockSpec((wt.shape[0], tm), lambda i: (0, i)))
        args.append(wt)
    res = pl.pallas_call(
        functools.partial(_norm_matmul_kernel, has_t=wt is not None),
        grid=(m // tm,), in_specs=in_specs, out_specs=out_specs, out_shape=out_shape,
        compiler_params=_cp("parallel"), name=name)(*args)
    return res if wt is not None else res[0]


def _matmul_norm_res_kernel(*refs, widths):
    n_a = len(widths)
    a_refs = refs[:n_a]
    w_ref, g_ref, r_ref, o_ref = refs[n_a:]
    y = None
    off = 0
    for a_ref, k in zip(a_refs, widths):
        part = _dot(a_ref[...].astype(BF16), w_ref[off:off + k, :])
        y = part if y is None else y + part
        off += k
    o_ref[...] = r_ref[...] + _rms(y, g_ref[...])


def _matmul_norm_res(a_list, w, g, resid, *, tm, name):
    m, d = resid.shape
    widths = tuple(a.shape[1] for a in a_list)
    in_specs = [pl.BlockSpec((tm, k), lambda i: (i, 0)) for k in widths]
    in_specs += [pl.BlockSpec(w.shape, lambda i: (0, 0)),
                 pl.BlockSpec((1, d), lambda i: (0, 0)),
                 pl.BlockSpec((tm, d), lambda i: (i, 0))]
    return pl.pallas_call(
        functools.partial(_matmul_norm_res_kernel, widths=widths),
        grid=(m // tm,), in_specs=in_specs,
        out_specs=pl.BlockSpec((tm, d), lambda i: (i, 0)),
        out_shape=jax.ShapeDtypeStruct((m, d), F32),
        compiler_params=_cp("parallel"), name=name)(*a_list, w, g, resid)


def _lru_gates(xc, wa_ref, ba_ref, wi_ref, bi_ref, lam_ref):
    xb = xc.astype(BF16)
    ra, ia = [], []
    for c in range(LRU_W // LANES):
        xs = xb[:, c * LANES:(c + 1) * LANES]
        ra.append(_dot(xs, wa_ref[c]))
        ia.append(_dot(xs, wi_ref[c]))
    r = _sigmoid(jnp.concatenate(ra, axis=1) + ba_ref[...])
    ig = _sigmoid(jnp.concatenate(ia, axis=1) + bi_ref[...])
    log_a = -LRU_C * r * _softplus(-lam_ref[...])
    a = jnp.exp(log_a)
    bx = jnp.sqrt(-jnp.tanh(log_a) * (a * a + 1.0)) * (ig * xc)
    return a, bx


def _lru_prompt_kernel(xl_ref, gate_ref, cw_ref, cb_ref, wa_ref, ba_ref, wi_ref, bi_ref, lam_ref,
                       out_ref, hlast_ref, ext_ref, hc_ref, *, tc, n_t):
    t = pl.program_id(1)

    @pl.when(t == 0)
    def _():
        ext_ref[0:8, :] = jnp.zeros((8, LRU_W), F32)
        hc_ref[...] = jnp.zeros((1, LRU_W), F32)

    xl = xl_ref[...]
    ext_ref[8:8 + tc, :] = xl
    xc = cb_ref[...] + cw_ref[LRU_CONV - 1:LRU_CONV, :] * xl
    for j in range(1, LRU_CONV):
        xc = xc + cw_ref[LRU_CONV - 1 - j:LRU_CONV - j, :] * ext_ref[pl.ds(8 - j, tc), :]
    ext_ref[0:8, :] = ext_ref[tc:tc + 8, :]

    a, b = _lru_gates(xc, wa_ref, ba_ref, wi_ref, bi_ref, lam_ref)
    row = lax.broadcasted_iota(jnp.int32, (tc, 1), 0)
    d = 1
    while d < tc:
        keep = row >= d
        a_sh = jnp.where(keep, pltpu.roll(a, d, 0), 1.0)
        b_sh = jnp.where(keep, pltpu.roll(b, d, 0), 0.0)
        b = a * b_sh + b
        a = a * a_sh
        d *= 2
    h = a * hc_ref[...] + b
    hc_ref[...] = h[tc - 1:tc, :]
    out_ref[...] = (_gelu_tanh(gate_ref[...]) * h).astype(BF16)

    @pl.when(t == n_t - 1)
    def _():
        hlast_ref[...] = h[tc - 1:tc, :]


def _lru_prompt(z, n_b, seq, cw, cb, wa, ba, wi, bi, lam):
    tc = SCAN_TILE
    n_t = seq // tc
    vec = lambda: pl.BlockSpec((1, LRU_W), lambda b, t: (0, 0))
    bd = lambda: pl.BlockSpec((LRU_W // LANES, LANES, LANES), lambda b, t: (0, 0, 0))
    return pl.pallas_call(
        functools.partial(_lru_prompt_kernel, tc=tc, n_t=n_t),
        grid=(n_b, n_t),
        in_specs=[pl.BlockSpec((tc, LRU_W), lambda b, t: (b * n_t + t, 0)),
                  pl.BlockSpec((tc, LRU_W), lambda b, t: (b * n_t + t, 1)),
                  pl.BlockSpec((LRU_CONV, LRU_W), lambda b, t: (0, 0)),
                  vec(), bd(), vec(), bd(), vec(), vec()],
        out_specs=[pl.BlockSpec((tc, LRU_W), lambda b, t: (b * n_t + t, 0)),
                   pl.BlockSpec((None, 1, LRU_W), lambda b, t: (b, 0, 0))],
        out_shape=[jax.ShapeDtypeStruct((n_b * seq, LRU_W), BF16),
                   jax.ShapeDtypeStruct((n_b, 1, LRU_W), F32)],
        scratch_shapes=[pltpu.VMEM((tc + 8, LRU_W), F32), pltpu.VMEM((1, LRU_W), F32)],
        compiler_params=_cp("parallel", "arbitrary"), name="lru_prompt")(
            z, z, cw, cb, wa, ba, wi, bi, lam)


def _lru_sample_kernel(z_ref, pre_ref, h0_ref, cw_ref, cb_ref, wa_ref, ba_ref, wi_ref, bi_ref, lam_ref,
                       out_ref, h_ref):
    xl = z_ref[:, 0:LRU_W]
    gate = z_ref[:, LRU_W:2 * LRU_W]
    xc = cb_ref[...] + cw_ref[LRU_CONV - 1:LRU_CONV, :] * xl
    for k in range(LRU_CONV - 1):
        xc = xc + cw_ref[k:k + 1, :] * pre_ref[k]
    a, bx = _lru_gates(xc, wa_ref, ba_ref, wi_ref, bi_ref, lam_ref)
    h = a * h0_ref[...] + bx
    h_ref[...] = h
    out_ref[...] = (_gelu_tanh(gate) * h).astype(BF16)


def _lru_sample(z, prefix_t, h0, cw, cb, wa, ba, wi, bi, lam):
    n_b = z.shape[0]
    return pl.pallas_call(
        _lru_sample_kernel,
        out_shape=[jax.ShapeDtypeStruct((n_b, LRU_W), BF16), jax.ShapeDtypeStruct((n_b, LRU_W), F32)],
        compiler_params=pltpu.CompilerParams(vmem_limit_bytes=VMEM_LIMIT),
        name="lru_sample")(z, prefix_t, h0, cw, cb, wa, ba, wi, bi, lam)


def _log_sigmoid(x):
    return jnp.minimum(x, 0.0) - jnp.log1p(jnp.exp(-jnp.abs(x)))


def _lane_cumsum(x):
    n = x.shape[1]
    lane = lax.broadcasted_iota(jnp.int32, x.shape, 1)
    d = 1
    while d < n:
        x = x + jnp.where(lane >= d, pltpu.roll(x, d, 1), 0.0)
        d *= 2
    return x


def _fox_prep_kernel(fl_ref, bf_ref, lf_ref, c_ref):
    lf = _log_sigmoid(fl_ref[0:FOX_H, :] + bf_ref[...])
    lf_ref[...] = lf
    c_ref[...] = _lane_cumsum(lf)


def _fox_prep(fl_t, bf, n_b, seq):
    return pl.pallas_call(
        _fox_prep_kernel, grid=(n_b,),
        in_specs=[pl.BlockSpec((fl_t.shape[0], seq), lambda b: (0, b)),
                  pl.BlockSpec((FOX_H, 1), lambda b: (0, 0))],
        out_specs=[pl.BlockSpec((FOX_H, seq), lambda b: (0, b)),
                   pl.BlockSpec((FOX_H, seq), lambda b: (0, b))],
        out_shape=[jax.ShapeDtypeStruct((FOX_H, n_b * seq), F32)] * 2,
        compiler_params=_cp("parallel"), name="fox_prep")(fl_t, bf)


def _logsig_kernel(fl_ref, bf_ref, lf_ref):
    lf_ref[...] = _log_sigmoid(fl_ref[0:FOX_H, :] + bf_ref[...])


def _logsig(fl_t, bf):
    return pl.pallas_call(
        _logsig_kernel, out_shape=jax.ShapeDtypeStruct((FOX_H, fl_t.shape[1]), F32),
        name="fox_logf_sample")(fl_t, bf)


def _fox_prompt_kernel(qi_ref, ki_ref, q_ref, k_ref, v_ref, cq_ref, ck_ref, o_ref,
                       m_sc, l_sc, acc_sc, s_sc, p_sc, al_sc, *, tb):
    hp = pl.program_id(1)
    t = pl.program_id(2)
    qi = qi_ref[t]
    ki = ki_ref[t]

    @pl.when(ki == 0)
    def _():
        m_sc[...] = jnp.full(m_sc.shape, NEG, F32)
        l_sc[...] = jnp.zeros(l_sc.shape, F32)
        acc_sc[...] = jnp.zeros(acc_sc.shape, F32)

    def step(diagonal):
        q = q_ref[...] * (FOX_HD ** -0.5)
        k = k_ref[...].astype(BF16)
        v = v_ref[...].astype(BF16)
        lane = lax.broadcasted_iota(jnp.int32, (tb, LANES), 1)
        rc = ATT_ROWS
        if diagonal:
            row_i = lax.broadcasted_iota(jnp.int32, (rc, tb), 0)
            col_i = lax.broadcasted_iota(jnp.int32, (rc, tb), 1)
        n_lt = tb // LANES
        for h in range(2):
            in_head = (lane < FOX_HD) if h == 0 else (lane >= FOX_HD)
            s_sc[h] = _dot_nt(jnp.where(in_head, q, 0.0).astype(BF16), k)
            c_q = cq_ref[pl.ds(2 * hp + h, 1), :]
            c_k = ck_ref[pl.ds(2 * hp + h, 1), :]
            bias = c_q[:, 0:1] - c_k
            for c in range(tb // rc):
                rows = slice(c * rc, (c + 1) * rc)
                s = s_sc[h, rows, :] + bias
                if diagonal:
                    s = jnp.where(col_i <= row_i + c * rc, s, NEG)
                s_sc[h, rows, :] = s
                m_prev = m_sc[h, rows, :]
                m_new = jnp.maximum(m_prev, jnp.max(s, axis=1, keepdims=True))
                m_sc[h, rows, :] = m_new
                al_sc[h, rows, :] = jnp.exp(m_prev - m_new)
            for c in range(tb // rc):
                rows = slice(c * rc, (c + 1) * rc)
                m_new = m_sc[h, rows, :]
                p_sum = None
                for j in range(n_lt):
                    cols = slice(j * LANES, (j + 1) * LANES)
                    p = jnp.exp(s_sc[h, rows, cols] - m_new)
                    p_sc[h, rows, cols] = p.astype(BF16)
                    p_sum = p if p_sum is None else p_sum + p
                l_sc[h, rows, :] = al_sc[h, rows, :] * l_sc[h, rows, :] + p_sum
            acc_sc[h] = al_sc[h] * acc_sc[h] + _dot(p_sc[h], v)

    @pl.when(ki < qi)
    def _():
        step(False)

    @pl.when(ki == qi)
    def _():
        step(True)
        lane = lax.broadcasted_iota(jnp.int32, (tb, LANES), 1)
        o0 = acc_sc[0] / jnp.sum(l_sc[0], axis=1, keepdims=True)
        o1 = acc_sc[1] / jnp.sum(l_sc[1], axis=1, keepdims=True)
        o_ref[...] = jnp.where(lane < FOX_HD, o0, o1).astype(BF16)


def _fox_prompt(z, c_t, n_b, seq):
    tb = SEQ_TILE
    n_q = seq // tb
    q_blk = (2 * LRU_W) // LANES
    k_blk = q_blk + FOX_W // LANES
    v_blk = k_blk + FOX_W // LANES
    pairs = [(qi, ki) for qi in range(n_q) for ki in range(qi + 1)]
    qi_list = jnp.asarray([p[0] for p in pairs], jnp.int32)
    ki_list = jnp.asarray([p[1] for p in pairs], jnp.int32)
    grid_spec = pltpu.PrefetchScalarGridSpec(
        num_scalar_prefetch=2, grid=(n_b, FOX_H // 2, len(pairs)),
        in_specs=[
            pl.BlockSpec((tb, LANES), lambda b, hp, t, qi, ki: (b * n_q + qi[t], q_blk + hp)),
            pl.BlockSpec((tb, LANES), lambda b, hp, t, qi, ki: (b * n_q + ki[t], k_blk + hp)),
            pl.BlockSpec((tb, LANES), lambda b, hp, t, qi, ki: (b * n_q + ki[t], v_blk + hp)),
            pl.BlockSpec((FOX_H, tb), lambda b, hp, t, qi, ki: (0, b * n_q + qi[t])),
            pl.BlockSpec((FOX_H, tb), lambda b, hp, t, qi, ki: (0, b * n_q + ki[t])),
        ],
        out_specs=pl.BlockSpec((tb, LANES), lambda b, hp, t, qi, ki: (b * n_q + qi[t], hp)),
        scratch_shapes=[pltpu.VMEM((2, tb, LANES), F32), pltpu.VMEM((2, tb, LANES), F32),
                        pltpu.VMEM((2, tb, LANES), F32), pltpu.VMEM((2, tb, tb), F32),
                        pltpu.VMEM((2, tb, tb), BF16), pltpu.VMEM((2, tb, LANES), F32)])
    return pl.pallas_call(
        functools.partial(_fox_prompt_kernel, tb=tb), grid_spec=grid_spec,
        out_shape=jax.ShapeDtypeStruct((n_b * seq, FOX_W), BF16),
        compiler_params=_cp("parallel", "parallel", "arbitrary"),
        name="fox_prompt")(qi_list, ki_list, z, z, z, c_t, c_t)


def _fox_sample_kernel(pt_ref, q_ref, kn_ref, vn_ref, lfn_ref, *rest, n_pg, n_g):
    k_refs = rest[0:n_pg]
    v_refs = rest[n_pg:2 * n_pg]
    lf_refs = rest[2 * n_pg:3 * n_pg]
    o_ref, m_sc, l_sc, acc_sc, cc_sc = rest[3 * n_pg:]
    g = pl.program_id(1)

    @pl.when(g == 0)
    def _():
        m_sc[...] = jnp.full(m_sc.shape, NEG, F32)
        l_sc[...] = jnp.zeros(l_sc.shape, F32)
        acc_sc[...] = jnp.zeros(acc_sc.shape, F32)
        cc_sc[...] = jnp.zeros(cc_sc.shape, F32)

    q = q_ref[...] * (FOX_HD ** -0.5)
    row = lax.broadcasted_iota(jnp.int32, (2 * FOX_H, FOX_W), 0)
    col = lax.broadcasted_iota(jnp.int32, (2 * FOX_H, FOX_W), 1)
    head_cols = jnp.right_shift(col, FOX_HD.bit_length() - 1) == row
    q_rows = jnp.where(head_cols, q, 0.0)
    q_rows_b = q_rows.astype(BF16)

    s_parts = [_dot_nt(q_rows_b, k_refs[j][...].astype(BF16))[0:FOX_H] for j in range(n_pg)]
    s = jnp.concatenate(s_parts, axis=1)
    lf = jnp.concatenate([lf_refs[j][...] for j in range(n_pg)], axis=1)
    c = _lane_cumsum(lf) + cc_sc[...]
    cc_sc[...] = c[:, c.shape[1] - 1:]
    s = s - c
    m_prev = m_sc[...]
    m_new = jnp.maximum(m_prev, jnp.max(s, axis=1, keepdims=True))
    alpha = jnp.exp(m_prev - m_new)
    p = jnp.exp(s - m_new)
    l_sc[...] = alpha * l_sc[...] + jnp.sum(p, axis=1, keepdims=True)
    p16 = jnp.concatenate([p, jnp.zeros_like(p)], axis=0).astype(BF16)
    page = k_refs[0].shape[0]
    pv = None
    for j in range(n_pg):
        part = _dot(p16[:, j * page:(j + 1) * page], v_refs[j][...].astype(BF16))[0:FOX_H]
        pv = part if pv is None else pv + part
    acc_sc[...] = alpha * acc_sc[...] + pv
    m_sc[...] = m_new

    @pl.when(g == n_g - 1)
    def _():
        s_n = jnp.sum(q_rows[0:FOX_H] * kn_ref[...], axis=1, keepdims=True)
        s_n = s_n - (cc_sc[...] + lfn_ref[...])
        m_p = m_sc[...]
        m_n = jnp.maximum(m_p, s_n)
        al = jnp.exp(m_p - m_n)
        p_n = jnp.exp(s_n - m_n)
        l_n = al * l_sc[...] + p_n
        acc = al * acc_sc[...] + p_n * vn_ref[...]
        o = jnp.where(head_cols[0:FOX_H], acc / l_n, 0.0)
        o_ref[...] = jnp.sum(o, axis=0, keepdims=True).astype(BF16)


def _fox_sample(page_table, q, k_new, v_new, lf_new, cache_k, cache_v, cache_lf_t):
    n_b, n_pages = page_table.shape
    n_pg = PAGES_PER_STEP
    n_g = n_pages // n_pg
    page = cache_k.shape[1]
    row_spec = lambda: pl.BlockSpec((None, 1, FOX_W), lambda b, g, pt: (b, 0, 0))

    def page_spec(shape, j):
        return pl.BlockSpec((None,) + shape,
                            lambda b, g, pt, j=j: (pt[b * n_pages + g * n_pg + j], 0, 0))

    in_specs = [row_spec(), row_spec(), row_spec(),
                pl.BlockSpec((None, FOX_H, 1), lambda b, g, pt: (b, 0, 0))]
    in_specs += [page_spec((page, FOX_W), j) for j in range(n_pg)]
    in_specs += [page_spec((page, FOX_W), j) for j in range(n_pg)]
    in_specs += [page_spec((FOX_H, page), j) for j in range(n_pg)]
    grid_spec = pltpu.PrefetchScalarGridSpec(
        num_scalar_prefetch=1, grid=(n_b, n_g), in_specs=in_specs,
        out_specs=pl.BlockSpec((None, 1, FOX_W), lambda b, g, pt: (b, 0, 0)),
        scratch_shapes=[pltpu.VMEM((FOX_H, 1), F32), pltpu.VMEM((FOX_H, 1), F32),
                        pltpu.VMEM((FOX_H, FOX_W), F32), pltpu.VMEM((FOX_H, 1), F32)])
    return pl.pallas_call(
        functools.partial(_fox_sample_kernel, n_pg=n_pg, n_g=n_g),
        grid_spec=grid_spec,
        out_shape=jax.ShapeDtypeStruct((n_b, 1, FOX_W), BF16),
        compiler_params=_cp("parallel", "arbitrary"), name="fox_sample")(
            page_table.reshape(-1), q, k_new, v_new, lf_new,
            *([cache_k] * n_pg), *([cache_v] * n_pg), *([cache_lf_t] * n_pg))


def _xattn_prompt_kernel(q_ref, mk_ref, mv_ref, o_ref):
    outs = []
    for h in range(MEM_H):
        cs = slice(h * MEM_HD, (h + 1) * MEM_HD)
        qh = q_ref[:, cs].astype(BF16)
        s = _dot_nt(qh, mk_ref[:, cs].astype(BF16)) * (MEM_HD ** -0.5)
        p = jnp.exp(s - jnp.max(s, axis=1, keepdims=True))
        l = jnp.sum(p, axis=1, keepdims=True)
        outs.append(_dot(p.astype(BF16), mv_ref[:, cs].astype(BF16)) / l)
    o_ref[...] = jnp.concatenate(outs, axis=1).astype(BF16)


def _xattn_prompt(q, memkv, n_b, seq):
    tb = SEQ_TILE
    n_t = seq // tb
    mem_len = memkv.shape[0] // n_b
    return pl.pallas_call(
        _xattn_prompt_kernel, grid=(n_b, n_t),
        in_specs=[pl.BlockSpec((tb, MEM_W), lambda b, t: (b * n_t + t, 0)),
                  pl.BlockSpec((mem_len, MEM_W), lambda b, t: (b, 0)),
                  pl.BlockSpec((mem_len, MEM_W), lambda b, t: (b, 1))],
        out_specs=pl.BlockSpec((tb, MEM_W), lambda b, t: (b * n_t + t, 0)),
        out_shape=jax.ShapeDtypeStruct((n_b * seq, MEM_W), BF16),
        compiler_params=_cp("parallel", "parallel"), name="xattn_prompt")(q, memkv, memkv)


def _xattn_sample_kernel(q_ref, mk_ref, mv_ref, o_ref):
    q = q_ref[...]
    row = lax.broadcasted_iota(jnp.int32, (16, MEM_W), 0)
    col = lax.broadcasted_iota(jnp.int32, (16, MEM_W), 1)
    head_cols = jnp.right_shift(col, MEM_HD.bit_length() - 1) == row
    q_rows = jnp.where(head_cols, q, 0.0).astype(BF16)
    s = _dot_nt(q_rows, mk_ref[...].astype(BF16)) * (MEM_HD ** -0.5)
    p = jnp.exp(s - jnp.max(s, axis=1, keepdims=True))
    l = jnp.sum(p, axis=1, keepdims=True)
    o = _dot(p.astype(BF16), mv_ref[...].astype(BF16)) / l
    o_ref[...] = jnp.sum(jnp.where(head_cols, o, 0.0), axis=0, keepdims=True).astype(BF16)


def _xattn_sample(q, mk, mv):
    n_b, mem_len, _ = mk.shape
    return pl.pallas_call(
        _xattn_sample_kernel, grid=(n_b,),
        in_specs=[pl.BlockSpec((None, 1, MEM_W), lambda b: (b, 0, 0)),
                  pl.BlockSpec((None, mem_len, MEM_W), lambda b: (b, 0, 0)),
                  pl.BlockSpec((None, mem_len, MEM_W), lambda b: (b, 0, 0))],
        out_specs=pl.BlockSpec((None, 1, MEM_W), lambda b: (b, 0, 0)),
        out_shape=jax.ShapeDtypeStruct((n_b, 1, MEM_W), BF16),
        compiler_params=_cp("parallel"), name="xattn_sample")(q, mk, mv)


def _ffn_kernel(x_ref, g4_ref, g5_ref, w1_ref, w3_ref, w2_ref, o_ref, hn_sc, acc_sc, *, n_f):
    f = pl.program_id(1)

    @pl.when(f == 0)
    def _():
        hn_sc[...] = _rms(x_ref[...], g4_ref[...]).astype(BF16)
        acc_sc[...] = jnp.zeros(acc_sc.shape, F32)

    hn = hn_sc[...]
    h1 = _dot(hn, w1_ref[...])
    h3 = _dot(hn, w3_ref[...])
    hh = (h1 * _sigmoid(h1) * h3).astype(BF16)
    acc_sc[...] += _dot(hh, w2_ref[...])

    @pl.when(f == n_f - 1)
    def _():
        o_ref[...] = x_ref[...] + _rms(acc_sc[...], g5_ref[...])


def _ffn(x, g4, g5, w1, w3, w2, *, tm):
    m, d = x.shape
    ff = w1.shape[1]
    tf = FFN_FF_TILE
    n_f = ff // tf
    return pl.pallas_call(
        functools.partial(_ffn_kernel, n_f=n_f), grid=(m // tm, n_f),
        in_specs=[pl.BlockSpec((tm, d), lambda i, f: (i, 0)),
                  pl.BlockSpec((1, d), lambda i, f: (0, 0)),
                  pl.BlockSpec((1, d), lambda i, f: (0, 0)),
                  pl.BlockSpec((d, tf), lambda i, f: (0, f)),
                  pl.BlockSpec((d, tf), lambda i, f: (0, f)),
                  pl.BlockSpec((tf, d), lambda i, f: (f, 0))],
        out_specs=pl.BlockSpec((tm, d), lambda i, f: (i, 0)),
        out_shape=jax.ShapeDtypeStruct((m, d), F32),
        scratch_shapes=[pltpu.VMEM((tm, d), BF16), pltpu.VMEM((tm, d), F32)],
        compiler_params=_cp("parallel", "arbitrary"), name="ffn")(x, g4, g5, w1, w3, w2)


def _layernorm_silu(x, g, b):
    mu = jnp.mean(x, axis=-1, keepdims=True)
    xc = x - mu
    y = xc * lax.rsqrt(jnp.mean(xc * xc, axis=-1, keepdims=True) + EPS) * g + b
    return y * _sigmoid(y)


def _odd_prompt_kernel(a_ref, gt_ref, up_ref, dww_ref, dwb_ref, lng_ref, lnb_ref, pw_ref, ps_ref,
                       cv_ref, pool_ref, cbuf_ref, pbuf_ref, eg_ref, eu_ref, *, tc, n_t):
    t = pl.program_id(1)
    halo_g, halo_u = 32, 16

    @pl.when(t == 0)
    def _():
        eg_ref[0:halo_g, :] = jnp.zeros((halo_g, CONV_W), F32)
        eu_ref[0:halo_u, :] = jnp.zeros((halo_u, POOL_W), F32)

    glu = a_ref[...] * _sigmoid(gt_ref[...])
    up = up_ref[...]
    eg_ref[halo_g:halo_g + tc, :] = glu
    eu_ref[halo_u:halo_u + tc, :] = up

    acc = dwb_ref[...] + dww_ref[CONV_K - 1:CONV_K, :] * glu
    for k in range(CONV_K - 1):
        acc = acc + dww_ref[k:k + 1, :] * eg_ref[pl.ds(halo_g - (CONV_K - 1) + k, tc), :]
    cv_ref[...] = _layernorm_silu(acc, lng_ref[...], lnb_ref[...]).astype(BF16)

    pos = t * tc + lax.broadcasted_iota(jnp.int32, (tc, 1), 0)
    outs = []
    for gi, w in enumerate(POOL_WINDOWS):
        cs = slice(gi * POOL_GW, (gi + 1) * POOL_GW)
        u_g = up[:, cs]
        win = u_g
        for j in range(1, w):
            win = win + eu_ref[pl.ds(halo_u - j, tc), cs]
        cnt = jnp.minimum(pos + 1, w).astype(F32)
        dlt = win / cnt - u_g
        outs.append(_dot(dlt.astype(BF16), pw_ref[gi]))
    pool_ref[...] = (jnp.concatenate(outs, axis=1) * ps_ref[...]).astype(BF16)

    @pl.when(t == n_t - 1)
    def _():
        cbuf_ref[...] = eg_ref[pl.ds(halo_g + tc - (CONV_K - 1), CONV_K - 1), :]
        pbuf_ref[...] = eu_ref[pl.ds(halo_u + tc - POOL_BUF, POOL_BUF), :]

    eg_ref[0:halo_g, :] = eg_ref[tc:tc + halo_g, :]
    eu_ref[0:halo_u, :] = eu_ref[tc:tc + halo_u, :]


def _odd_prompt(z, n_b, seq, dww, dwb, lng, lnb, pw, ps):
    tc = SEQ_TILE
    n_t = seq // tc
    vec = lambda: pl.BlockSpec((1, CONV_W), lambda b, t: (0, 0))
    return pl.pallas_call(
        functools.partial(_odd_prompt_kernel, tc=tc, n_t=n_t), grid=(n_b, n_t),
        in_specs=[pl.BlockSpec((tc, CONV_W), lambda b, t: (b * n_t + t, 0)),
                  pl.BlockSpec((tc, CONV_W), lambda b, t: (b * n_t + t, 1)),
                  pl.BlockSpec((tc, POOL_W), lambda b, t: (b * n_t + t, 2)),
                  pl.BlockSpec((CONV_K, CONV_W), lambda b, t: (0, 0)),
                  vec(), vec(), vec(),
                  pl.BlockSpec((len(POOL_WINDOWS), POOL_GW, POOL_GW), lambda b, t: (0, 0, 0)),
                  vec()],
        out_specs=[pl.BlockSpec((tc, CONV_W), lambda b, t: (b * n_t + t, 0)),
                   pl.BlockSpec((tc, POOL_W), lambda b, t: (b * n_t + t, 0)),
                   pl.BlockSpec((None, CONV_K - 1, CONV_W), lambda b, t: (b, 0, 0)),
                   pl.BlockSpec((None, POOL_BUF, POOL_W), lambda b, t: (b, 0, 0))],
        out_shape=[jax.ShapeDtypeStruct((n_b * seq, CONV_W), BF16),
                   jax.ShapeDtypeStruct((n_b * seq, POOL_W), BF16),
                   jax.ShapeDtypeStruct((n_b, CONV_K - 1, CONV_W), F32),
                   jax.ShapeDtypeStruct((n_b, POOL_BUF, POOL_W), F32)],
        scratch_shapes=[pltpu.VMEM((tc + 32, CONV_W), F32), pltpu.VMEM((tc + 16, POOL_W), F32)],
        compiler_params=_cp("parallel", "arbitrary"), name="odd_prompt")(
            z, z, z, dww, dwb, lng, lnb, pw, ps)


def _odd_sample_kernel(z_ref, cbuf_ref, pbuf_ref, dww_ref, dwb_ref, lng_ref, lnb_ref, pw_ref, ps_ref,
                       cv_ref, pool_ref, glu_ref, *, pos0):
    glu = z_ref[:, 0:CONV_W] * _sigmoid(z_ref[:, CONV_W:2 * CONV_W])
    up = z_ref[:, 2 * CONV_W:2 * CONV_W + POOL_W]
    glu_ref[...] = glu
    acc = dwb_ref[...] + dww_ref[CONV_K - 1:CONV_K, :] * glu
    for k in range(CONV_K - 1):
        acc = acc + dww_ref[k:k + 1, :] * cbuf_ref[k]
    cv_ref[...] = _layernorm_silu(acc, lng_ref[...], lnb_ref[...]).astype(BF16)
    outs = []
    for gi, w in enumerate(POOL_WINDOWS):
        cs = slice(gi * POOL_GW, (gi + 1) * POOL_GW)
        u_g = up[:, cs]
        win = u_g
        for j in range(1, w):
            win = win + pbuf_ref[POOL_BUF - j][:, cs]
        dlt = win / float(min(pos0 + 1, w)) - u_g
        outs.append(_dot(dlt.astype(BF16), pw_ref[gi]))
    pool_ref[...] = (jnp.concatenate(outs, axis=1) * ps_ref[...]).astype(BF16)


def _odd_sample(z, cbuf_t, pbuf_t, dww, dwb, lng, lnb, pw, ps, pos0):
    n_b = z.shape[0]
    return pl.pallas_call(
        functools.partial(_odd_sample_kernel, pos0=pos0),
        out_shape=[jax.ShapeDtypeStruct((n_b, CONV_W), BF16),
                   jax.ShapeDtypeStruct((n_b, POOL_W), BF16),
                   jax.ShapeDtypeStruct((n_b, CONV_W), F32)],
        compiler_params=pltpu.CompilerParams(vmem_limit_bytes=VMEM_LIMIT),
        name="odd_sample")(z, cbuf_t, pbuf_t, dww, dwb, lng, lnb, pw, ps)


def _router_kernel(x_ref, g_ref, whi_ref, wlo_ref, b_ref, hn_ref, idx_ref, gate_ref):
    hn = _rms(x_ref[...], g_ref[...])
    hb = hn.astype(BF16)
    hn_ref[...] = hn
    hlo = (hn - hb.astype(F32)).astype(BF16)
    logits = _dot(hb, whi_ref[...]) + (_dot(hb, wlo_ref[...]) + _dot(hlo, whi_ref[...])) + b_ref[...]
    lane = lax.broadcasted_iota(jnp.int32, logits.shape, 1)
    lane_f = lane.astype(F32)
    logits = jnp.where(lane < N_EXPERTS, logits, NEG)
    m1 = jnp.max(logits, axis=1, keepdims=True)
    i1 = jnp.min(jnp.where(logits == m1, lane_f, float(LANES)), axis=1, keepdims=True)
    rest = jnp.where(lane_f == i1, NEG, logits)
    m2 = jnp.max(rest, axis=1, keepdims=True)
    i2 = jnp.min(jnp.where(rest == m2, lane_f, float(LANES)), axis=1, keepdims=True)
    e = jnp.exp(m2 - m1)
    g1 = 1.0 / (1.0 + e)
    g2 = e / (1.0 + e)
    idx_ref[...] = jnp.where(lane == 0, i1, jnp.where(lane == 1, i2, 0.0)).astype(jnp.int32)
    gate_ref[...] = jnp.where(lane == 0, g1, jnp.where(lane == 1, g2, 0.0))


def _router(x, g, whi, wlo, b, *, tm):
    m, d = x.shape
    return pl.pallas_call(
        _router_kernel, grid=(m // tm,),
        in_specs=[pl.BlockSpec((tm, d), lambda i: (i, 0)),
                  pl.BlockSpec((1, d), lambda i: (0, 0)),
                  pl.BlockSpec((d, LANES), lambda i: (0, 0)),
                  pl.BlockSpec((d, LANES), lambda i: (0, 0)),
                  pl.BlockSpec((1, LANES), lambda i: (0, 0))],
        out_specs=[pl.BlockSpec((tm, d), lambda i: (i, 0)),
                   pl.BlockSpec((tm, LANES), lambda i: (i, 0)),
                   pl.BlockSpec((tm, LANES), lambda i: (i, 0))],
        out_shape=[jax.ShapeDtypeStruct((m, d), F32),
                   jax.ShapeDtypeStruct((m, LANES), jnp.int32),
                   jax.ShapeDtypeStruct((m, LANES), F32)],
        compiler_params=_cp("parallel"), name="router")(x, g, whi, wlo, b)


def _gmm_kernel(te_ref, tv_ref, x_ref, gate_ref, w1_ref, w3_ref, w2_ref, o_ref, acc_sc, xb_sc, *, n_f):
    t = pl.program_id(0)
    f = pl.program_id(1)

    @pl.when(f == 0)
    def _():
        acc_sc[...] = jnp.zeros(acc_sc.shape, F32)
        xb_sc[...] = x_ref[...].astype(BF16)

    @pl.when(tv_ref[t] != 0)
    def _():
        x = xb_sc[...]
        h1 = _dot(x, w1_ref[...])
        h3 = _dot(x, w3_ref[...])
        hh = (h1 * _sigmoid(h1) * h3).astype(BF16)
        acc_sc[...] += _dot(hh, w2_ref[...])

    @pl.when(f == n_f - 1)
    def _():
        o_ref[...] = acc_sc[...] * gate_ref[...]


def _gmm(tile_expert, tile_valid, x_sorted, gate_sorted, w1, w3, w2):
    rows, d = x_sorted.shape
    ff = w1.shape[2]
    tm, tf = MOE_TILE, MOE_FF_TILE
    n_f = ff // tf
    grid_spec = pltpu.PrefetchScalarGridSpec(
        num_scalar_prefetch=2, grid=(rows // tm, n_f),
        in_specs=[pl.BlockSpec((tm, d), lambda t, f, te, tv: (t, 0)),
                  pl.BlockSpec((tm, 1), lambda t, f, te, tv: (t, 0)),
                  pl.BlockSpec((None, d, tf), lambda t, f, te, tv: (te[t], 0, f)),
                  pl.BlockSpec((None, d, tf), lambda t, f, te, tv: (te[t], 0, f)),
                  pl.BlockSpec((None, tf, d), lambda t, f, te, tv: (te[t], f, 0))],
        out_specs=pl.BlockSpec((tm, d), lambda t, f, te, tv: (t, 0)),
        scratch_shapes=[pltpu.VMEM((tm, d), F32), pltpu.VMEM((tm, d), BF16)])
    return pl.pallas_call(
        functools.partial(_gmm_kernel, n_f=n_f), grid_spec=grid_spec,
        out_shape=jax.ShapeDtypeStruct((rows, d), F32),
        compiler_params=_cp("parallel", "arbitrary"), name="moe_experts")(
            tile_expert, tile_valid, x_sorted, gate_sorted, w1, w3, w2)


def _combine_kernel(x_ref, y1_ref, y2_ref, g_ref, o_ref):
    o_ref[...] = x_ref[...] + _rms(y1_ref[...] + y2_ref[...], g_ref[...])


def _combine(x, y1, y2, g, *, tm):
    m, d = x.shape
    row = lambda: pl.BlockSpec((tm, d), lambda i: (i, 0))
    return pl.pallas_call(
        _combine_kernel, grid=(m // tm,),
        in_specs=[row(), row(), row(), pl.BlockSpec((1, d), lambda i: (0, 0))],
        out_specs=row(), out_shape=jax.ShapeDtypeStruct((m, d), F32),
        compiler_params=_cp("parallel"), name="moe_combine")(x, y1, y2, g)


def _moe_block(xs, g4, g5, w_r, b_r, w1, w3, w2):
    pad = LANES - N_EXPERTS
    w_r_p = jnp.pad(w_r, ((0, 0), (0, pad)))
    whi = w_r_p.astype(BF16)
    wlo = (w_r_p - whi.astype(F32)).astype(BF16)
    b_p = jnp.pad(b_r, (0, pad)).reshape(1, LANES)
    hn, idx, gate = [], [], []
    for x in xs:
        h, i, gt = _router(x, g4, whi, wlo, b_p, tm=min(ROW_TILE, x.shape[0]))
        hn.append(h)
        idx.append(i[:, :2])
        gate.append(gt[:, :2])
    hn = jnp.concatenate(hn, axis=0)
    e_flat = jnp.concatenate(idx, axis=0).reshape(-1)
    g_flat = jnp.concatenate(gate, axis=0).reshape(-1)
    n_assign = e_flat.shape[0]
    tm = MOE_TILE
    n_tiles = n_assign // tm + N_EXPERTS
    rows = n_tiles * tm

    onehot = (e_flat[:, None] == jnp.arange(N_EXPERTS, dtype=jnp.int32)[None, :]).astype(jnp.int32)
    rank = jnp.sum((jnp.cumsum(onehot, axis=0) - onehot) * onehot, axis=1)
    cnt = jnp.sum(onehot, axis=0)
    tiles_e = (cnt + tm - 1) // tm
    tile_end = jnp.cumsum(tiles_e)
    row_start = (tile_end - tiles_e) * tm
    dest = row_start[e_flat] + rank
    tile_ids = jnp.arange(n_tiles, dtype=jnp.int32)
    tile_expert = jnp.minimum(jnp.sum((tile_end[None, :] <= tile_ids[:, None]).astype(jnp.int32), axis=1),
                              N_EXPERTS - 1)
    tile_valid = (tile_ids < tile_end[-1]).astype(jnp.int32)
    row_assign = jnp.zeros((rows,), jnp.int32).at[dest].set(jnp.arange(n_assign, dtype=jnp.int32))
    row_ids = jnp.arange(rows, dtype=jnp.int32)
    row_expert = jnp.repeat(tile_expert, tm)
    row_valid = (row_ids - row_start[row_expert]) < cnt[row_expert]
    src_tok = row_assign // 2
    gate_sorted = jnp.where(row_valid, g_flat[row_assign], 0.0).reshape(rows, 1)

    x_sorted = jnp.take(hn, src_tok, axis=0)
    y_sorted = _gmm(tile_expert, tile_valid, x_sorted, gate_sorted, w1, w3, w2)

    outs = []
    off = 0
    dest2 = dest.reshape(-1, 2)
    for x in xs:
        m = x.shape[0]
        d1 = dest2[off:off + m, 0]
        d2 = dest2[off:off + m, 1]
        outs.append(_combine(x, jnp.take(y_sorted, d1, axis=0), jnp.take(y_sorted, d2, axis=0), g5,
                             tm=min(ROW_TILE, m)))
        off += m
    return outs


def _block_diag_pairs(w):
    nb, bw, _ = w.shape
    w = w.reshape(nb // 2, 2, bw, bw)
    z = jnp.zeros((nb // 2, bw, bw), w.dtype)
    top = jnp.concatenate([w[:, 0], z], axis=2)
    bot = jnp.concatenate([z, w[:, 1]], axis=2)
    return jnp.concatenate([top, bot], axis=1).astype(BF16)


def kernel(x_prompt, x_sample, cache_fox_k, cache_fox_v, cache_fox_logf, state_lru_h, state_lru_conv, state_conv_buf, state_pool_buf, cache_mem_k, cache_mem_v, page_table, mem_prompt, norm_g, w_xq, w_xk, w_xv, w_xo, w_in_e, b_f, lru_conv_w, lru_conv_b, lru_wa, lru_ba, lru_wi, lru_bi, lru_lam, w_out_e, w_ff1, w_ff3, w_ff2, w_in_o, cc_dw_w, cc_dw_b, cc_ln_g, cc_ln_b, pool_w, pool_scale, w_out_o, w_router, b_router, w_e1, w_e3, w_e2):
    bp, seq, d = x_prompt.shape
    bs = x_sample.shape[0]
    depth = norm_g.shape[0]
    page = cache_fox_k.shape[2]
    past_len = page_table.shape[1] * page
    mem_len = mem_prompt.shape[1]
    tm_p = ROW_TILE

    xp = x_prompt.reshape(bp * seq, d)
    xs = x_sample.reshape(bs, d)
    mem = mem_prompt.reshape(bp * mem_len, d)
    vec = lambda v: v.reshape(1, -1)

    fk_p, fv_p, fl_p, lh_p, lc_p, cb_p, pb_p, mk_pl, mv_pl = [], [], [], [], [], [], [], [], []
    fk_s, fv_s, fl_s, lh_s, lc_s, cb_s, pb_s = [], [], [], [], [], [], []

    for l in range(depth):
        g = [vec(norm_g[l, i]) for i in range(norm_g.shape[1])]
        w_kv = jnp.concatenate([w_xk[l], w_xv[l]], axis=1).astype(BF16)
        memkv = _norm_matmul(mem, g[6], w_kv, tm=tm_p, name="mem_kv")
        mk_pl.append(memkv[:, :MEM_W].reshape(bp, mem_len, MEM_H, MEM_HD))
        mv_pl.append(memkv[:, MEM_W:].reshape(bp, mem_len, MEM_H, MEM_HD))

        if l % 2 == 0:
            i = l // 2
            n_main = 2 * LRU_W + 3 * FOX_W
            w_main = w_in_e[i][:, :n_main].astype(BF16)
            w_fl_t = jnp.pad(w_in_e[i][:, n_main:].T, ((0, 16 - FOX_H), (0, 0))).astype(BF16)
            cw, cb = lru_conv_w[i], vec(lru_conv_b[i])
            wa, wi = _block_diag_pairs(lru_wa[i]), _block_diag_pairs(lru_wi[i])
            ba, bi, lam = vec(lru_ba[i]), vec(lru_bi[i]), vec(lru_lam[i])
            bf = b_f[i].reshape(FOX_H, 1)
            w_out = w_out_e[i].astype(BF16)

            z, fl_t = _norm_matmul(xp, g[0], w_main, tm=tm_p, wt=w_fl_t, name="in_proj_even")
            lf_t, c_t = _fox_prep(fl_t, bf, bp, seq)
            lru_out, h_last = _lru_prompt(z, bp, seq, cw, cb, wa, ba, wi, bi, lam)
            att = _fox_prompt(z, c_t, bp, seq)
            xp = _matmul_norm_res([lru_out, att], w_out, g[1], xp, tm=tm_p, name="out_proj_even")
            z3 = z.reshape(bp, seq, n_main)
            fk_p.append(z3[:, :, 2 * LRU_W + FOX_W:2 * LRU_W + 2 * FOX_W].reshape(bp, seq, FOX_H, FOX_HD))
            fv_p.append(z3[:, :, 2 * LRU_W + 2 * FOX_W:].reshape(bp, seq, FOX_H, FOX_HD))
            fl_p.append(lf_t.T.reshape(bp, seq, FOX_H))
            lh_p.append(h_last.reshape(bp, LRU_W))
            lc_p.append(z3[:, seq - (LRU_CONV - 1):, :LRU_W])

            zs, fls_t = _norm_matmul(xs, g[0], w_main, tm=bs, wt=w_fl_t, name="in_proj_even_s")
            lfs_t = _logsig(fls_t, bf)
            pre_t = jnp.swapaxes(state_lru_conv[i], 0, 1)
            lru_out_s, h_s = _lru_sample(zs, pre_t, state_lru_h[i], cw, cb, wa, ba, wi, bi, lam)
            q_s = zs[:, 2 * LRU_W:2 * LRU_W + FOX_W]
            k_s = zs[:, 2 * LRU_W + FOX_W:2 * LRU_W + 2 * FOX_W]
            v_s = zs[:, 2 * LRU_W + 2 * FOX_W:]
            n_pool = cache_fox_k.shape[1]
            att_s = _fox_sample(page_table, q_s.reshape(bs, 1, FOX_W), k_s.reshape(bs, 1, FOX_W),
                                v_s.reshape(bs, 1, FOX_W), lfs_t.T.reshape(bs, FOX_H, 1),
                                cache_fox_k[i].reshape(n_pool, page, FOX_W),
                                cache_fox_v[i].reshape(n_pool, page, FOX_W),
                                jnp.swapaxes(cache_fox_logf[i], 1, 2))
            xs = _matmul_norm_res([lru_out_s, att_s.reshape(bs, FOX_W)], w_out, g[1], xs, tm=bs,
                                  name="out_proj_even_s")
            fk_s.append(k_s.reshape(bs, 1, FOX_H, FOX_HD))
            fv_s.append(v_s.reshape(bs, 1, FOX_H, FOX_HD))
            fl_s.append(lfs_t.T.reshape(bs, 1, FOX_H))
            lh_s.append(h_s)
            lc_s.append(jnp.concatenate([state_lru_conv[i][:, 1:], zs[:, None, :LRU_W]], axis=1))
        else:
            j = l // 2
            w_in = w_in_o[j].astype(BF16)
            dww, dwb = cc_dw_w[j], vec(cc_dw_b[j])
            lng, lnb = vec(cc_ln_g[j]), vec(cc_ln_b[j])
            pw, ps = pool_w[j].astype(BF16), vec(pool_scale[j])
            w_out = w_out_o[j].astype(BF16)

            z = _norm_matmul(xp, g[0], w_in, tm=tm_p, name="in_proj_odd")
            cv, pool, cbuf, pbuf = _odd_prompt(z, bp, seq, dww, dwb, lng, lnb, pw, ps)
            xp = _matmul_norm_res([cv, pool], w_out, g[1], xp, tm=tm_p, name="out_proj_odd")
            cb_p.append(cbuf)
            pb_p.append(pbuf)

            zs = _norm_matmul(xs, g[0], w_in, tm=bs, name="in_proj_odd_s")
            cv_s, pool_s, glu_s = _odd_sample(zs, jnp.swapaxes(state_conv_buf[j], 0, 1),
                                              jnp.swapaxes(state_pool_buf[j], 0, 1),
                                              dww, dwb, lng, lnb, pw, ps, past_len)
            xs = _matmul_norm_res([cv_s, pool_s], w_out, g[1], xs, tm=bs, name="out_proj_odd_s")
            cb_s.append(jnp.concatenate([state_conv_buf[j][:, 1:], glu_s[:, None, :]], axis=1))
            pb_s.append(jnp.concatenate([state_pool_buf[j][:, 1:], zs[:, None, 2 * CONV_W:]], axis=1))

        wq, wo = w_xq[l].astype(BF16), w_xo[l].astype(BF16)
        q = _norm_matmul(xp, g[2], wq, tm=tm_p, name="xattn_q")
        o = _xattn_prompt(q, memkv, bp, seq)
        xp = _matmul_norm_res([o], wo, g[3], xp, tm=tm_p, name="xattn_o")
        q_s = _norm_matmul(xs, g[2], wq, tm=bs, name="xattn_q_s")
        o_s = _xattn_sample(q_s.reshape(bs, 1, MEM_W), cache_mem_k[l].reshape(bs, mem_len, MEM_W),
                            cache_mem_v[l].reshape(bs, mem_len, MEM_W))
        xs = _matmul_norm_res([o_s.reshape(bs, MEM_W)], wo, g[3], xs, tm=bs, name="xattn_o_s")

        if l % 2 == 0:
            i = l // 2
            w1, w3, w2 = w_ff1[i].astype(BF16), w_ff3[i].astype(BF16), w_ff2[i].astype(BF16)
            xp = _ffn(xp, g[4], g[5], w1, w3, w2, tm=tm_p)
            xs = _ffn(xs, g[4], g[5], w1, w3, w2, tm=bs)
        else:
            j = l // 2
            xp, xs = _moe_block([xp, xs], g[4], g[5], w_router[j], b_router[j],
                                w_e1[j].astype(BF16), w_e3[j].astype(BF16), w_e2[j].astype(BF16))

    return (xp.reshape(bp, seq, d), xs.reshape(bs, 1, d),
            jnp.stack(fk_p), jnp.stack(fv_p), jnp.stack(fl_p), jnp.stack(lh_p), jnp.stack(lc_p),
            jnp.stack(cb_p), jnp.stack(pb_p), jnp.stack(mk_pl), jnp.stack(mv_pl),
            jnp.stack(fk_s), jnp.stack(fv_s), jnp.stack(fl_s), jnp.stack(lh_s), jnp.stack(lc_s),
            jnp.stack(cb_s), jnp.stack(pb_s))
```

```python
import functools

import jax
import jax.numpy as jnp
from jax import lax
from jax.experimental import pallas as pl
from jax.experimental.pallas import tpu as pltpu

F32 = jnp.float32
BF16 = jnp.bfloat16

D_MODEL = 1024
LRU_W = 512
LRU_CONV = 4
LRU_C = 8.0
FOX_H = 8
FOX_HD = 64
FOX_W = FOX_H * FOX_HD
CONV_W = 512
CONV_K = 31
POOL_W = 512
POOL_WINDOWS = (2, 4, 8, 16)
POOL_GW = POOL_W // len(POOL_WINDOWS)
POOL_BUF = max(POOL_WINDOWS) - 1
MEM_H = 4
MEM_HD = 128
MEM_W = MEM_H * MEM_HD
N_EXPERTS = 8
EPS = 1e-6
NEG = -1e30

LANES = 128
ROW_TILE = 512
SEQ_TILE = 512
SCAN_TILE = 256
ATT_ROWS = 32
PAGES_PER_STEP = 8
MOE_TILE = 512
MOE_FF_TILE = 512
FFN_FF_TILE = 1408
VMEM_LIMIT = 56 * 1024 * 1024


def _cp(*sem):
    return pltpu.CompilerParams(dimension_semantics=sem, vmem_limit_bytes=VMEM_LIMIT)


def _rms(x, g):
    return x * lax.rsqrt(jnp.mean(x * x, axis=-1, keepdims=True) + EPS) * g


def _sigmoid(x):
    return 1.0 / (1.0 + jnp.exp(-x))


def _softplus(x):
    return jnp.maximum(x, 0.0) + jnp.log1p(jnp.exp(-jnp.abs(x)))


def _gelu_tanh(x):
    return 0.5 * x * (1.0 + jnp.tanh(0.7978845608028654 * (x + 0.044715 * (x * x * x))))


def _dot(a, b):
    return jnp.dot(a, b, preferred_element_type=F32)


def _dot_nt(a, b):
    return lax.dot_general(a, b, (((1,), (1,)), ((), ())), preferred_element_type=F32)


def _norm_matmul_kernel(x_ref, g_ref, w_ref, *rest, has_t):
    hn = _rms(x_ref[...], g_ref[...]).astype(BF16)
    if has_t:
        wt_ref, o_ref, ot_ref = rest
        ot_ref[...] = _dot_nt(wt_ref[...], hn)
    else:
        (o_ref,) = rest
    o_ref[...] = _dot(hn, w_ref[...])


def _norm_matmul(x, g, w, *, tm, wt=None, name):
    m, d = x.shape
    n = w.shape[1]
    in_specs = [pl.BlockSpec((tm, d), lambda i: (i, 0)),
                pl.BlockSpec((1, d), lambda i: (0, 0)),
                pl.BlockSpec((d, n), lambda i: (0, 0))]
    out_shape = [jax.ShapeDtypeStruct((m, n), F32)]
    out_specs = [pl.BlockSpec((tm, n), lambda i: (i, 0))]
    args = [x, g, w]
    if wt is not None:
        in_specs.append(pl.BlockSpec(wt.shape, lambda i: (0, 0)))
        out_shape.append(jax.ShapeDtypeStruct((wt.shape[0], m), F32))
        out_specs.append(pl.BlockSpec((wt.shape[0], tm), lambda i: (0, i)))
        args.append(wt)
    res = pl.pallas_call(
        functools.partial(_norm_matmul_kernel, has_t=wt is not None),
        grid=(m // tm,), in_specs=in_specs, out_specs=out_specs, out_shape=out_shape,
        compiler_params=_cp("parallel"), name=name)(*args)
    return res if wt is not None else res[0]


def _matmul_norm_res_kernel(*refs, widths):
    n_a = len(widths)
    a_refs = refs[:n_a]
    w_ref, g_ref, r_ref, o_ref = refs[n_a:]
    y = None
    off = 0
    for a_ref, k in zip(a_refs, widths):
        part = _dot(a_ref[...].astype(BF16), w_ref[off:off + k, :])
        y = part if y is None else y + part
        off += k
    o_ref[...] = r_ref[...] + _rms(y, g_ref[...])


def _matmul_norm_res(a_list, w, g, resid, *, tm, name):
    m, d = resid.shape
    widths = tuple(a.shape[1] for a in a_list)
    in_specs = [pl.BlockSpec((tm, k), lambda i: (i, 0)) for k in widths]
    in_specs += [pl.BlockSpec(w.shape, lambda i: (0, 0)),
                 pl.BlockSpec((1, d), lambda i: (0, 0)),
                 pl.BlockSpec((tm, d), lambda i: (i, 0))]
    return pl.pallas_call(
        functools.partial(_matmul_norm_res_kernel, widths=widths),
        grid=(m // tm,), in_specs=in_specs,
        out_specs=pl.BlockSpec((tm, d), lambda i: (i, 0)),
        out_shape=jax.ShapeDtypeStruct((m, d), F32),
        compiler_params=_cp("parallel"), name=name)(*a_list, w, g, resid)


def _lru_gates(xc, wa_ref, ba_ref, wi_ref, bi_ref, lam_ref):
    xb = xc.astype(BF16)
    ra, ia = [], []
    for c in range(LRU_W // LANES):
        xs = xb[:, c * LANES:(c + 1) * LANES]
        ra.append(_dot(xs, wa_ref[c]))
        ia.append(_dot(xs, wi_ref[c]))
    r = _sigmoid(jnp.concatenate(ra, axis=1) + ba_ref[...])
    ig = _sigmoid(jnp.concatenate(ia, axis=1) + bi_ref[...])
    log_a = -LRU_C * r * _softplus(-lam_ref[...])
    a = jnp.exp(log_a)
    bx = jnp.sqrt(-jnp.tanh(log_a) * (a * a + 1.0)) * (ig * xc)
    return a, bx


def _lru_prompt_kernel(xl_ref, gate_ref, cw_ref, cb_ref, wa_ref, ba_ref, wi_ref, bi_ref, lam_ref,
                       out_ref, hlast_ref, ext_ref, hc_ref, *, tc, n_t):
    t = pl.program_id(1)

    @pl.when(t == 0)
    def _():
        ext_ref[0:8, :] = jnp.zeros((8, LRU_W), F32)
        hc_ref[...] = jnp.zeros((1, LRU_W), F32)

    xl = xl_ref[...]
    ext_ref[8:8 + tc, :] = xl
    xc = cb_ref[...] + cw_ref[LRU_CONV - 1:LRU_CONV, :] * xl
    for j in range(1, LRU_CONV):
        xc = xc + cw_ref[LRU_CONV - 1 - j:LRU_CONV - j, :] * ext_ref[pl.ds(8 - j, tc), :]
    ext_ref[0:8, :] = ext_ref[tc:tc + 8, :]

    a, b = _lru_gates(xc, wa_ref, ba_ref, wi_ref, bi_ref, lam_ref)
    row = lax.broadcasted_iota(jnp.int32, (tc, 1), 0)
    d = 1
    while d < tc:
        keep = row >= d
        a_sh = jnp.where(keep, pltpu.roll(a, d, 0), 1.0)
        b_sh = jnp.where(keep, pltpu.roll(b, d, 0), 0.0)
        b = a * b_sh + b
        a = a * a_sh
        d *= 2
    h = a * hc_ref[...] + b
    hc_ref[...] = h[tc - 1:tc, :]
    out_ref[...] = (_gelu_tanh(gate_ref[...]) * h).astype(BF16)

    @pl.when(t == n_t - 1)
    def _():
        hlast_ref[...] = h[tc - 1:tc, :]


def _lru_prompt(z, n_b, seq, cw, cb, wa, ba, wi, bi, lam):
    tc = SCAN_TILE
    n_t = seq // tc
    vec = lambda: pl.BlockSpec((1, LRU_W), lambda b, t: (0, 0))
    bd = lambda: pl.BlockSpec((LRU_W // LANES, LANES, LANES), lambda b, t: (0, 0, 0))
    return pl.pallas_call(
        functools.partial(_lru_prompt_kernel, tc=tc, n_t=n_t),
        grid=(n_b, n_t),
        in_specs=[pl.BlockSpec((tc, LRU_W), lambda b, t: (b * n_t + t, 0)),
                  pl.BlockSpec((tc, LRU_W), lambda b, t: (b * n_t + t, 1)),
                  pl.BlockSpec((LRU_CONV, LRU_W), lambda b, t: (0, 0)),
                  vec(), bd(), vec(), bd(), vec(), vec()],
        out_specs=[pl.BlockSpec((tc, LRU_W), lambda b, t: (b * n_t + t, 0)),
                   pl.BlockSpec((None, 1, LRU_W), lambda b, t: (b, 0, 0))],
        out_shape=[jax.ShapeDtypeStruct((n_b * seq, LRU_W), BF16),
                   jax.ShapeDtypeStruct((n_b, 1, LRU_W), F32)],
        scratch_shapes=[pltpu.VMEM((tc + 8, LRU_W), F32), pltpu.VMEM((1, LRU_W), F32)],
        compiler_params=_cp("parallel", "arbitrary"), name="lru_prompt")(
            z, z, cw, cb, wa, ba, wi, bi, lam)


def _lru_sample_kernel(z_ref, pre_ref, h0_ref, cw_ref, cb_ref, wa_ref, ba_ref, wi_ref, bi_ref, lam_ref,
                       out_ref, h_ref):
    xl = z_ref[:, 0:LRU_W]
    gate = z_ref[:, LRU_W:2 * LRU_W]
    xc = cb_ref[...] + cw_ref[LRU_CONV - 1:LRU_CONV, :] * xl
    for k in range(LRU_CONV - 1):
        xc = xc + cw_ref[k:k + 1, :] * pre_ref[k]
    a, bx = _lru_gates(xc, wa_ref, ba_ref, wi_ref, bi_ref, lam_ref)
    h = a * h0_ref[...] + bx
    h_ref[...] = h
    out_ref[...] = (_gelu_tanh(gate) * h).astype(BF16)


def _lru_sample(z, prefix_t, h0, cw, cb, wa, ba, wi, bi, lam):
    n_b = z.shape[0]
    return pl.pallas_call(
        _lru_sample_kernel,
        out_shape=[jax.ShapeDtypeStruct((n_b, LRU_W), BF16), jax.ShapeDtypeStruct((n_b, LRU_W), F32)],
        compiler_params=pltpu.CompilerParams(vmem_limit_bytes=VMEM_LIMIT),
        name="lru_sample")(z, prefix_t, h0, cw, cb, wa, ba, wi, bi, lam)


def _log_sigmoid(x):
    return jnp.minimum(x, 0.0) - jnp.log1p(jnp.exp(-jnp.abs(x)))


def _lane_cumsum(x):
    n = x.shape[1]
    lane = lax.broadcasted_iota(jnp.int32, x.shape, 1)
    d = 1
    while d < n:
        x = x + jnp.where(lane >= d, pltpu.roll(x, d, 1), 0.0)
        d *= 2
    return x


def _fox_prep_kernel(fl_ref, bf_ref, lf_ref, c_ref):
    lf = _log_sigmoid(fl_ref[0:FOX_H, :] + bf_ref[...])
    lf_ref[...] = lf
    c_ref[...] = _lane_cumsum(lf)


def _fox_prep(fl_t, bf, n_b, seq):
    return pl.pallas_call(
        _fox_prep_kernel, grid=(n_b,),
        in_specs=[pl.BlockSpec((fl_t.shape[0], seq), lambda b: (0, b)),
                  pl.BlockSpec((FOX_H, 1), lambda b: (0, 0))],
        out_specs=[pl.BlockSpec((FOX_H, seq), lambda b: (0, b)),
                   pl.BlockSpec((FOX_H, seq), lambda b: (0, b))],
        out_shape=[jax.ShapeDtypeStruct((FOX_H, n_b * seq), F32)] * 2,
        compiler_params=_cp("parallel"), name="fox_prep")(fl_t, bf)


def _logsig_kernel(fl_ref, bf_ref, lf_ref):
    lf_ref[...] = _log_sigmoid(fl_ref[0:FOX_H, :] + bf_ref[...])


def _logsig(fl_t, bf):
    return pl.pallas_call(
        _logsig_kernel, out_shape=jax.ShapeDtypeStruct((FOX_H, fl_t.shape[1]), F32),
        name="fox_logf_sample")(fl_t, bf)


def _fox_prompt_kernel(qi_ref, ki_ref, q_ref, k_ref, v_ref, cq_ref, ck_ref, o_ref,
                       m_sc, l_sc, acc_sc, s_sc, p_sc, al_sc, *, tb):
    hp = pl.program_id(1)
    t = pl.program_id(2)
    qi = qi_ref[t]
    ki = ki_ref[t]

    @pl.when(ki == 0)
    def _():
        m_sc[...] = jnp.full(m_sc.shape, NEG, F32)
        l_sc[...] = jnp.zeros(l_sc.shape, F32)
        acc_sc[...] = jnp.zeros(acc_sc.shape, F32)

    def step(diagonal):
        q = q_ref[...] * (FOX_HD ** -0.5)
        k = k_ref[...].astype(BF16)
        v = v_ref[...].astype(BF16)
        lane = lax.broadcasted_iota(jnp.int32, (tb, LANES), 1)
        rc = ATT_ROWS
        if diagonal:
            row_i = lax.broadcasted_iota(jnp.int32, (rc, tb), 0)
            col_i = lax.broadcasted_iota(jnp.int32, (rc, tb), 1)
        n_lt = tb // LANES
        for h in range(2):
            in_head = (lane < FOX_HD) if h == 0 else (lane >= FOX_HD)
            s_sc[h] = _dot_nt(jnp.where(in_head, q, 0.0).astype(BF16), k)
            c_q = cq_ref[pl.ds(2 * hp + h, 1), :]
            c_k = ck_ref[pl.ds(2 * hp + h, 1), :]
            bias = c_q[:, 0:1] - c_k
            for c in range(tb // rc):
                rows = slice(c * rc, (c + 1) * rc)
                s = s_sc[h, rows, :] + bias
                if diagonal:
                    s = jnp.where(col_i <= row_i + c * rc, s, NEG)
                s_sc[h, rows, :] = s
                m_prev = m_sc[h, rows, :]
                m_new = jnp.maximum(m_prev, jnp.max(s, axis=1, keepdims=True))
                m_sc[h, rows, :] = m_new
                al_sc[h, rows, :] = jnp.exp(m_prev - m_new)
            for c in range(tb // rc):
                rows = slice(c * rc, (c + 1) * rc)
                m_new = m_sc[h, rows, :]
                p_sum = None
                for j in range(n_lt):
                    cols = slice(j * LANES, (j + 1) * LANES)
                    p = jnp.exp(s_sc[h, rows, cols] - m_new)
                    p_sc[h, rows, cols] = p.astype(BF16)
                    p_sum = p if p_sum is None else p_sum + p
                l_sc[h, rows, :] = al_sc[h, rows, :] * l_sc[h, rows, :] + p_sum
            acc_sc[h] = al_sc[h] * acc_sc[h] + _dot(p_sc[h], v)

    @pl.when(ki < qi)
    def _():
        step(False)

    @pl.when(ki == qi)
    def _():
        step(True)
        lane = lax.broadcasted_iota(jnp.int32, (tb, LANES), 1)
        o0 = acc_sc[0] / jnp.sum(l_sc[0], axis=1, keepdims=True)
        o1 = acc_sc[1] / jnp.sum(l_sc[1], axis=1, keepdims=True)
        o_ref[...] = jnp.where(lane < FOX_HD, o0, o1).astype(BF16)


def _fox_prompt(z, c_t, n_b, seq):
    tb = SEQ_TILE
    n_q = seq // tb
    q_blk = (2 * LRU_W) // LANES
    k_blk = q_blk + FOX_W // LANES
    v_blk = k_blk + FOX_W // LANES
    pairs = [(qi, ki) for qi in range(n_q) for ki in range(qi + 1)]
    qi_list = jnp.asarray([p[0] for p in pairs], jnp.int32)
    ki_list = jnp.asarray([p[1] for p in pairs], jnp.int32)
    grid_spec = pltpu.PrefetchScalarGridSpec(
        num_scalar_prefetch=2, grid=(n_b, FOX_H // 2, len(pairs)),
        in_specs=[
            pl.BlockSpec((tb, LANES), lambda b, hp, t, qi, ki: (b * n_q + qi[t], q_blk + hp)),
            pl.BlockSpec((tb, LANES), lambda b, hp, t, qi, ki: (b * n_q + ki[t], k_blk + hp)),
            pl.BlockSpec((tb, LANES), lambda b, hp, t, qi, ki: (b * n_q + ki[t], v_blk + hp)),
            pl.BlockSpec((FOX_H, tb), lambda b, hp, t, qi, ki: (0, b * n_q + qi[t])),
            pl.BlockSpec((FOX_H, tb), lambda b, hp, t, qi, ki: (0, b * n_q + ki[t])),
        ],
        out_specs=pl.BlockSpec((tb, LANES), lambda b, hp, t, qi, ki: (b * n_q + qi[t], hp)),
        scratch_shapes=[pltpu.VMEM((2, tb, LANES), F32), pltpu.VMEM((2, tb, LANES), F32),
                        pltpu.VMEM((2, tb, LANES), F32), pltpu.VMEM((2, tb, tb), F32),
                        pltpu.VMEM((2, tb, tb), BF16), pltpu.VMEM((2, tb, LANES), F32)])
    return pl.pallas_call(
        functools.partial(_fox_prompt_kernel, tb=tb), grid_spec=grid_spec,
        out_shape=jax.ShapeDtypeStruct((n_b * seq, FOX_W), BF16),
        compiler_params=_cp("parallel", "parallel", "arbitrary"),
        name="fox_prompt")(qi_list, ki_list, z, z, z, c_t, c_t)


def _fox_sample_kernel(pt_ref, q_ref, kn_ref, vn_ref, lfn_ref, *rest, n_pg, n_g):
    k_refs = rest[0:n_pg]
    v_refs = rest[n_pg:2 * n_pg]
    lf_refs = rest[2 * n_pg:3 * n_pg]
    o_ref, m_sc, l_sc, acc_sc, cc_sc = rest[3 * n_pg:]
    g = pl.program_id(1)
    page = lf_refs[0].shape[1]

    @pl.when(g == 0)
    def _():
        m_sc[...] = jnp.full(m_sc.shape, NEG, F32)
        l_sc[...] = jnp.zeros(l_sc.shape, F32)
        acc_sc[...] = jnp.zeros(acc_sc.shape, F32)
        cc_sc[...] = jnp.zeros(cc_sc.shape, F32)

    def head_rows(ref, h):
        return ref[pl.ds(h, page, stride=FOX_H), :].astype(BF16)

    q = q_ref[...] * (FOX_HD ** -0.5)
    q16 = jnp.concatenate([q, jnp.zeros_like(q)], axis=0)
    row_q = lax.broadcasted_iota(jnp.int32, (2 * FOX_H, FOX_HD), 0)
    q_only = [jnp.where(row_q == h, q16, 0.0).astype(BF16) for h in range(FOX_H)]

    s_parts = []
    for j in range(n_pg):
        s_j = None
        for h in range(FOX_H):
            part = _dot_nt(q_only[h], head_rows(k_refs[j], h))
            s_j = part if s_j is None else s_j + part
        s_parts.append(s_j[0:FOX_H])
    s = jnp.concatenate(s_parts, axis=1)
    lf = jnp.concatenate([lf_refs[j][...] for j in range(n_pg)], axis=1)
    c = _lane_cumsum(lf) + cc_sc[...]
    cc_sc[...] = c[:, c.shape[1] - 1:]
    s = s - c
    m_prev = m_sc[...]
    m_new = jnp.maximum(m_prev, jnp.max(s, axis=1, keepdims=True))
    alpha = jnp.exp(m_prev - m_new)
    p = jnp.exp(s - m_new)
    l_sc[...] = alpha * l_sc[...] + jnp.sum(p, axis=1, keepdims=True)
    p16 = jnp.concatenate([p, jnp.zeros_like(p)], axis=0)
    row_p = lax.broadcasted_iota(jnp.int32, (2 * FOX_H, page), 0)
    pv = None
    for j in range(n_pg):
        p_j = p16[:, j * page:(j + 1) * page]
        for h in range(FOX_H):
            part = _dot(jnp.where(row_p == h, p_j, 0.0).astype(BF16), head_rows(v_refs[j], h))
            pv = part if pv is None else pv + part
    acc_sc[...] = alpha * acc_sc[...] + pv[0:FOX_H]
    m_sc[...] = m_new

    @pl.when(g == n_g - 1)
    def _():
        s_n = jnp.sum(q * kn_ref[...], axis=1, keepdims=True)
        s_n = s_n - (cc_sc[...] + lfn_ref[...])
        m_p = m_sc[...]
        m_n = jnp.maximum(m_p, s_n)
        al = jnp.exp(m_p - m_n)
        p_n = jnp.exp(s_n - m_n)
        l_n = al * l_sc[...] + p_n
        o_ref[...] = ((al * acc_sc[...] + p_n * vn_ref[...]) / l_n).astype(BF16)


def _fox_sample(page_table, layer, q, k_new, v_new, lf_new, cache_k, cache_v, cache_lf_t):
    n_b, n_pages = page_table.shape
    n_pg = PAGES_PER_STEP
    n_g = n_pages // n_pg
    page = cache_k.shape[2] // FOX_H
    head_spec = lambda: pl.BlockSpec((None, FOX_H, FOX_HD), lambda b, g, pt: (b, 0, 0))

    def kv_spec(j):
        return pl.BlockSpec((None, None, page * FOX_H, FOX_HD),
                            lambda b, g, pt, j=j: (layer, pt[b * n_pages + g * n_pg + j], 0, 0))

    def lf_spec(j):
        return pl.BlockSpec((None, FOX_H, page),
                            lambda b, g, pt, j=j: (pt[b * n_pages + g * n_pg + j], 0, 0))

    in_specs = [head_spec(), head_spec(), head_spec(),
                pl.BlockSpec((None, FOX_H, 1), lambda b, g, pt: (b, 0, 0))]
    in_specs += [kv_spec(j) for j in range(n_pg)]
    in_specs += [kv_spec(j) for j in range(n_pg)]
    in_specs += [lf_spec(j) for j in range(n_pg)]
    grid_spec = pltpu.PrefetchScalarGridSpec(
        num_scalar_prefetch=1, grid=(n_b, n_g), in_specs=in_specs,
        out_specs=pl.BlockSpec((None, FOX_H, FOX_HD), lambda b, g, pt: (b, 0, 0)),
        scratch_shapes=[pltpu.VMEM((FOX_H, 1), F32), pltpu.VMEM((FOX_H, 1), F32),
                        pltpu.VMEM((FOX_H, FOX_HD), F32), pltpu.VMEM((FOX_H, 1), F32)])
    return pl.pallas_call(
        functools.partial(_fox_sample_kernel, n_pg=n_pg, n_g=n_g),
        grid_spec=grid_spec,
        out_shape=jax.ShapeDtypeStruct((n_b, FOX_H, FOX_HD), BF16),
        compiler_params=_cp("parallel", "arbitrary"), name="fox_sample")(
            page_table.reshape(-1), q, k_new, v_new, lf_new,
            *([cache_k] * n_pg), *([cache_v] * n_pg), *([cache_lf_t] * n_pg))


def _xattn_prompt_kernel(q_ref, mk_ref, mv_ref, o_ref):
    outs = []
    for h in range(MEM_H):
        cs = slice(h * MEM_HD, (h + 1) * MEM_HD)
        qh = q_ref[:, cs].astype(BF16)
        s = _dot_nt(qh, mk_ref[:, cs].astype(BF16)) * (MEM_HD ** -0.5)
        p = jnp.exp(s - jnp.max(s, axis=1, keepdims=True))
        l = jnp.sum(p, axis=1, keepdims=True)
        outs.append(_dot(p.astype(BF16), mv_ref[:, cs].astype(BF16)) / l)
    o_ref[...] = jnp.concatenate(outs, axis=1).astype(BF16)


def _xattn_prompt(q, memkv, n_b, seq):
    tb = SEQ_TILE
    n_t = seq // tb
    mem_len = memkv.shape[0] // n_b
    return pl.pallas_call(
        _xattn_prompt_kernel, grid=(n_b, n_t),
        in_specs=[pl.BlockSpec((tb, MEM_W), lambda b, t: (b * n_t + t, 0)),
                  pl.BlockSpec((mem_len, MEM_W), lambda b, t: (b, 0)),
                  pl.BlockSpec((mem_len, MEM_W), lambda b, t: (b, 1))],
        out_specs=pl.BlockSpec((tb, MEM_W), lambda b, t: (b * n_t + t, 0)),
        out_shape=jax.ShapeDtypeStruct((n_b * seq, MEM_W), BF16),
        compiler_params=_cp("parallel", "parallel"), name="xattn_prompt")(q, memkv, memkv)


def _xattn_sample_kernel(q_ref, mk_ref, mv_ref, o_ref):
    q = q_ref[...]
    row = lax.broadcasted_iota(jnp.int32, (16, MEM_W), 0)
    col = lax.broadcasted_iota(jnp.int32, (16, MEM_W), 1)
    head_cols = jnp.right_shift(col, MEM_HD.bit_length() - 1) == row
    q_rows = jnp.where(head_cols, q, 0.0).astype(BF16)
    s = _dot_nt(q_rows, mk_ref[...].astype(BF16)) * (MEM_HD ** -0.5)
    p = jnp.exp(s - jnp.max(s, axis=1, keepdims=True))
    l = jnp.sum(p, axis=1, keepdims=True)
    o = _dot(p.astype(BF16), mv_ref[...].astype(BF16)) / l
    o_ref[...] = jnp.sum(jnp.where(head_cols, o, 0.0), axis=0, keepdims=True).astype(BF16)


def _xattn_sample(q, mk, mv):
    n_b, mem_len, _ = mk.shape
    return pl.pallas_call(
        _xattn_sample_kernel, grid=(n_b,),
        in_specs=[pl.BlockSpec((None, 1, MEM_W), lambda b: (b, 0, 0)),
                  pl.BlockSpec((None, mem_len, MEM_W), lambda b: (b, 0, 0)),
                  pl.BlockSpec((None, mem_len, MEM_W), lambda b: (b, 0, 0))],
        out_specs=pl.BlockSpec((None, 1, MEM_W), lambda b: (b, 0, 0)),
        out_shape=jax.ShapeDtypeStruct((n_b, 1, MEM_W), BF16),
        compiler_params=_cp("parallel"), name="xattn_sample")(q, mk, mv)


def _ffn_kernel(x_ref, g4_ref, g5_ref, w1_ref, w3_ref, w2_ref, o_ref, hn_sc, acc_sc, *, n_f):
    f = pl.program_id(1)

    @pl.when(f == 0)
    def _():
        hn_sc[...] = _rms(x_ref[...], g4_ref[...]).astype(BF16)
        acc_sc[...] = jnp.zeros(acc_sc.shape, F32)

    hn = hn_sc[...]
    h1 = _dot(hn, w1_ref[...])
    h3 = _dot(hn, w3_ref[...])
    hh = (h1 * _sigmoid(h1) * h3).astype(BF16)
    acc_sc[...] += _dot(hh, w2_ref[...])

    @pl.when(f == n_f - 1)
    def _():
        o_ref[...] = x_ref[...] + _rms(acc_sc[...], g5_ref[...])


def _ffn(x, g4, g5, w1, w3, w2, *, tm):
    m, d = x.shape
    ff = w1.shape[1]
    tf = FFN_FF_TILE
    n_f = ff // tf
    return pl.pallas_call(
        functools.partial(_ffn_kernel, n_f=n_f), grid=(m // tm, n_f),
        in_specs=[pl.BlockSpec((tm, d), lambda i, f: (i, 0)),
                  pl.BlockSpec((1, d), lambda i, f: (0, 0)),
                  pl.BlockSpec((1, d), lambda i, f: (0, 0)),
                  pl.BlockSpec((d, tf), lambda i, f: (0, f)),
                  pl.BlockSpec((d, tf), lambda i, f: (0, f)),
                  pl.BlockSpec((tf, d), lambda i, f: (f, 0))],
        out_specs=pl.BlockSpec((tm, d), lambda i, f: (i, 0)),
        out_shape=jax.ShapeDtypeStruct((m, d), F32),
        scratch_shapes=[pltpu.VMEM((tm, d), BF16), pltpu.VMEM((tm, d), F32)],
        compiler_params=_cp("parallel", "arbitrary"), name="ffn")(x, g4, g5, w1, w3, w2)


def _layernorm_silu(x, g, b):
    mu = jnp.mean(x, axis=-1, keepdims=True)
    xc = x - mu
    y = xc * lax.rsqrt(jnp.mean(xc * xc, axis=-1, keepdims=True) + EPS) * g + b
    return y * _sigmoid(y)


def _odd_prompt_kernel(a_ref, gt_ref, up_ref, dww_ref, dwb_ref, lng_ref, lnb_ref, pw_ref, ps_ref,
                       cv_ref, pool_ref, cbuf_ref, pbuf_ref, eg_ref, eu_ref, *, tc, n_t):
    t = pl.program_id(1)
    halo_g, halo_u = 32, 16

    @pl.when(t == 0)
    def _():
        eg_ref[0:halo_g, :] = jnp.zeros((halo_g, CONV_W), F32)
        eu_ref[0:halo_u, :] = jnp.zeros((halo_u, POOL_W), F32)

    glu = a_ref[...] * _sigmoid(gt_ref[...])
    up = up_ref[...]
    eg_ref[halo_g:halo_g + tc, :] = glu
    eu_ref[halo_u:halo_u + tc, :] = up

    acc = dwb_ref[...] + dww_ref[CONV_K - 1:CONV_K, :] * glu
    for k in range(CONV_K - 1):
        acc = acc + dww_ref[k:k + 1, :] * eg_ref[pl.ds(halo_g - (CONV_K - 1) + k, tc), :]
    cv_ref[...] = _layernorm_silu(acc, lng_ref[...], lnb_ref[...]).astype(BF16)

    pos = t * tc + lax.broadcasted_iota(jnp.int32, (tc, 1), 0)
    outs = []
    for gi, w in enumerate(POOL_WINDOWS):
        cs = slice(gi * POOL_GW, (gi + 1) * POOL_GW)
        u_g = up[:, cs]
        win = u_g
        for j in range(1, w):
            win = win + eu_ref[pl.ds(halo_u - j, tc), cs]
        cnt = jnp.minimum(pos + 1, w).astype(F32)
        dlt = win / cnt - u_g
        outs.append(_dot(dlt.astype(BF16), pw_ref[gi]))
    pool_ref[...] = (jnp.concatenate(outs, axis=1) * ps_ref[...]).astype(BF16)

    @pl.when(t == n_t - 1)
    def _():
        cbuf_ref[...] = eg_ref[pl.ds(halo_g + tc - (CONV_K - 1), CONV_K - 1), :]
        pbuf_ref[...] = eu_ref[pl.ds(halo_u + tc - POOL_BUF, POOL_BUF), :]

    eg_ref[0:halo_g, :] = eg_ref[tc:tc + halo_g, :]
    eu_ref[0:halo_u, :] = eu_ref[tc:tc + halo_u, :]


def _odd_prompt(z, n_b, seq, dww, dwb, lng, lnb, pw, ps):
    tc = SEQ_TILE
    n_t = seq // tc
    vec = lambda: pl.BlockSpec((1, CONV_W), lambda b, t: (0, 0))
    return pl.pallas_call(
        functools.partial(_odd_prompt_kernel, tc=tc, n_t=n_t), grid=(n_b, n_t),
        in_specs=[pl.BlockSpec((tc, CONV_W), lambda b, t: (b * n_t + t, 0)),
                  pl.BlockSpec((tc, CONV_W), lambda b, t: (b * n_t + t, 1)),
                  pl.BlockSpec((tc, POOL_W), lambda b, t: (b * n_t + t, 2)),
                  pl.BlockSpec((CONV_K, CONV_W), lambda b, t: (0, 0)),
                  vec(), vec(), vec(),
                  pl.BlockSpec((len(POOL_WINDOWS), POOL_GW, POOL_GW), lambda b, t: (0, 0, 0)),
                  vec()],
        out_specs=[pl.BlockSpec((tc, CONV_W), lambda b, t: (b * n_t + t, 0)),
                   pl.BlockSpec((tc, POOL_W), lambda b, t: (b * n_t + t, 0)),
                   pl.BlockSpec((None, CONV_K - 1, CONV_W), lambda b, t: (b, 0, 0)),
                   pl.BlockSpec((None, POOL_BUF, POOL_W), lambda b, t: (b, 0, 0))],
        out_shape=[jax.ShapeDtypeStruct((n_b * seq, CONV_W), BF16),
                   jax.ShapeDtypeStruct((n_b * seq, POOL_W), BF16),
                   jax.ShapeDtypeStruct((n_b, CONV_K - 1, CONV_W), F32),
                   jax.ShapeDtypeStruct((n_b, POOL_BUF, POOL_W), F32)],
        scratch_shapes=[pltpu.VMEM((tc + 32, CONV_W), F32), pltpu.VMEM((tc + 16, POOL_W), F32)],
        compiler_params=_cp("parallel", "arbitrary"), name="odd_prompt")(
            z, z, z, dww, dwb, lng, lnb, pw, ps)


def _odd_sample_kernel(z_ref, cbuf_ref, pbuf_ref, dww_ref, dwb_ref, lng_ref, lnb_ref, pw_ref, ps_ref,
                       cv_ref, pool_ref, glu_ref, *, pos0):
    glu = z_ref[:, 0:CONV_W] * _sigmoid(z_ref[:, CONV_W:2 * CONV_W])
    up = z_ref[:, 2 * CONV_W:2 * CONV_W + POOL_W]
    glu_ref[...] = glu
    acc = dwb_ref[...] + dww_ref[CONV_K - 1:CONV_K, :] * glu
    for k in range(CONV_K - 1):
        acc = acc + dww_ref[k:k + 1, :] * cbuf_ref[k]
    cv_ref[...] = _layernorm_silu(acc, lng_ref[...], lnb_ref[...]).astype(BF16)
    outs = []
    for gi, w in enumerate(POOL_WINDOWS):
        cs = slice(gi * POOL_GW, (gi + 1) * POOL_GW)
        u_g = up[:, cs]
        win = u_g
        for j in range(1, w):
            win = win + pbuf_ref[POOL_BUF - j][:, cs]
        dlt = win / float(min(pos0 + 1, w)) - u_g
        outs.append(_dot(dlt.astype(BF16), pw_ref[gi]))
    pool_ref[...] = (jnp.concatenate(outs, axis=1) * ps_ref[...]).astype(BF16)


def _odd_sample(z, cbuf_t, pbuf_t, dww, dwb, lng, lnb, pw, ps, pos0):
    n_b = z.shape[0]
    return pl.pallas_call(
        functools.partial(_odd_sample_kernel, pos0=pos0),
        out_shape=[jax.ShapeDtypeStruct((n_b, CONV_W), BF16),
                   jax.ShapeDtypeStruct((n_b, POOL_W), BF16),
                   jax.ShapeDtypeStruct((n_b, CONV_W), F32)],
        compiler_params=pltpu.CompilerParams(vmem_limit_bytes=VMEM_LIMIT),
        name="odd_sample")(z, cbuf_t, pbuf_t, dww, dwb, lng, lnb, pw, ps)


def _router_kernel(x_ref, g_ref, whi_ref, wlo_ref, b_ref, hn_ref, idx_ref, gate_ref):
    hn = _rms(x_ref[...], g_ref[...])
    hb = hn.astype(BF16)
    hn_ref[...] = hn
    hlo = (hn - hb.astype(F32)).astype(BF16)
    logits = _dot(hb, whi_ref[...]) + (_dot(hb, wlo_ref[...]) + _dot(hlo, whi_ref[...])) + b_ref[...]
    lane = lax.broadcasted_iota(jnp.int32, logits.shape, 1)
    lane_f = lane.astype(F32)
    logits = jnp.where(lane < N_EXPERTS, logits, NEG)
    m1 = jnp.max(logits, axis=1, keepdims=True)
    i1 = jnp.min(jnp.where(logits == m1, lane_f, float(LANES)), axis=1, keepdims=True)
    rest = jnp.where(lane_f == i1, NEG, logits)
    m2 = jnp.max(rest, axis=1, keepdims=True)
    i2 = jnp.min(jnp.where(rest == m2, lane_f, float(LANES)), axis=1, keepdims=True)
    e = jnp.exp(m2 - m1)
    g1 = 1.0 / (1.0 + e)
    g2 = e / (1.0 + e)
    idx_ref[...] = jnp.where(lane == 0, i1, jnp.where(lane == 1, i2, 0.0)).astype(jnp.int32)
    gate_ref[...] = jnp.where(lane == 0, g1, jnp.where(lane == 1, g2, 0.0))


def _router(x, g, whi, wlo, b, *, tm):
    m, d = x.shape
    return pl.pallas_call(
        _router_kernel, grid=(m // tm,),
        in_specs=[pl.BlockSpec((tm, d), lambda i: (i, 0)),
                  pl.BlockSpec((1, d), lambda i: (0, 0)),
                  pl.BlockSpec((d, LANES), lambda i: (0, 0)),
                  pl.BlockSpec((d, LANES), lambda i: (0, 0)),
                  pl.BlockSpec((1, LANES), lambda i: (0, 0))],
        out_specs=[pl.BlockSpec((tm, d), lambda i: (i, 0)),
                   pl.BlockSpec((tm, LANES), lambda i: (i, 0)),
                   pl.BlockSpec((tm, LANES), lambda i: (i, 0))],
        out_shape=[jax.ShapeDtypeStruct((m, d), F32),
                   jax.ShapeDtypeStruct((m, LANES), jnp.int32),
                   jax.ShapeDtypeStruct((m, LANES), F32)],
        compiler_params=_cp("parallel"), name="router")(x, g, whi, wlo, b)


def _gmm_kernel(te_ref, tv_ref, x_ref, gate_ref, w1_ref, w3_ref, w2_ref, o_ref, acc_sc, xb_sc, *, n_f):
    t = pl.program_id(0)
    f = pl.program_id(1)

    @pl.when(f == 0)
    def _():
        acc_sc[...] = jnp.zeros(acc_sc.shape, F32)
        xb_sc[...] = x_ref[...].astype(BF16)

    @pl.when(tv_ref[t] != 0)
    def _():
        x = xb_sc[...]
        h1 = _dot(x, w1_ref[...])
        h3 = _dot(x, w3_ref[...])
        hh = (h1 * _sigmoid(h1) * h3).astype(BF16)
        acc_sc[...] += _dot(hh, w2_ref[...])

    @pl.when(f == n_f - 1)
    def _():
        o_ref[...] = acc_sc[...] * gate_ref[...]


def _gmm(tile_expert, tile_valid, x_sorted, gate_sorted, w1, w3, w2):
    rows, d = x_sorted.shape
    ff = w1.shape[2]
    tm, tf = MOE_TILE, MOE_FF_TILE
    n_f = ff // tf
    grid_spec = pltpu.PrefetchScalarGridSpec(
        num_scalar_prefetch=2, grid=(rows // tm, n_f),
        in_specs=[pl.BlockSpec((tm, d), lambda t, f, te, tv: (t, 0)),
                  pl.BlockSpec((tm, 1), lambda t, f, te, tv: (t, 0)),
                  pl.BlockSpec((None, d, tf), lambda t, f, te, tv: (te[t], 0, f)),
                  pl.BlockSpec((None, d, tf), lambda t, f, te, tv: (te[t], 0, f)),
                  pl.BlockSpec((None, tf, d), lambda t, f, te, tv: (te[t], f, 0))],
        out_specs=pl.BlockSpec((tm, d), lambda t, f, te, tv: (t, 0)),
        scratch_shapes=[pltpu.VMEM((tm, d), F32), pltpu.VMEM((tm, d), BF16)])
    return pl.pallas_call(
        functools.partial(_gmm_kernel, n_f=n_f), grid_spec=grid_spec,
        out_shape=jax.ShapeDtypeStruct((rows, d), F32),
        compiler_params=_cp("parallel", "arbitrary"), name="moe_experts")(
            tile_expert, tile_valid, x_sorted, gate_sorted, w1, w3, w2)


def _combine_kernel(x_ref, y1_ref, y2_ref, g_ref, o_ref):
    o_ref[...] = x_ref[...] + _rms(y1_ref[...] + y2_ref[...], g_ref[...])


def _combine(x, y1, y2, g, *, tm):
    m, d = x.shape
    row = lambda: pl.BlockSpec((tm, d), lambda i: (i, 0))
    return pl.pallas_call(
        _combine_kernel, grid=(m // tm,),
        in_specs=[row(), row(), row(), pl.BlockSpec((1, d), lambda i: (0, 0))],
        out_specs=row(), out_shape=jax.ShapeDtypeStruct((m, d), F32),
        compiler_params=_cp("parallel"), name="moe_combine")(x, y1, y2, g)


def _moe_block(xs, g4, g5, w_r, b_r, w1, w3, w2):
    pad = LANES - N_EXPERTS
    w_r_p = jnp.pad(w_r, ((0, 0), (0, pad)))
    whi = w_r_p.astype(BF16)
    wlo = (w_r_p - whi.astype(F32)).astype(BF16)
    b_p = jnp.pad(b_r, (0, pad)).reshape(1, LANES)
    hn, idx, gate = [], [], []
    for x in xs:
        h, i, gt = _router(x, g4, whi, wlo, b_p, tm=min(ROW_TILE, x.shape[0]))
        hn.append(h)
        idx.append(i[:, :2])
        gate.append(gt[:, :2])
    hn = jnp.concatenate(hn, axis=0)
    e_flat = jnp.concatenate(idx, axis=0).reshape(-1)
    g_flat = jnp.concatenate(gate, axis=0).reshape(-1)
    n_assign = e_flat.shape[0]
    tm = MOE_TILE
    n_tiles = n_assign // tm + N_EXPERTS
    rows = n_tiles * tm

    onehot = (e_flat[:, None] == jnp.arange(N_EXPERTS, dtype=jnp.int32)[None, :]).astype(jnp.int32)
    rank = jnp.sum((jnp.cumsum(onehot, axis=0) - onehot) * onehot, axis=1)
    cnt = jnp.sum(onehot, axis=0)
    tiles_e = (cnt + tm - 1) // tm
    tile_end = jnp.cumsum(tiles_e)
    row_start = (tile_end - tiles_e) * tm
    dest = row_start[e_flat] + rank
    tile_ids = jnp.arange(n_tiles, dtype=jnp.int32)
    tile_expert = jnp.minimum(jnp.sum((tile_end[None, :] <= tile_ids[:, None]).astype(jnp.int32), axis=1),
                              N_EXPERTS - 1)
    tile_valid = (tile_ids < tile_end[-1]).astype(jnp.int32)
    row_assign = jnp.zeros((rows,), jnp.int32).at[dest].set(jnp.arange(n_assign, dtype=jnp.int32))
    row_ids = jnp.arange(rows, dtype=jnp.int32)
    row_expert = jnp.repeat(tile_expert, tm)
    row_valid = (row_ids - row_start[row_expert]) < cnt[row_expert]
    src_tok = row_assign // 2
    gate_sorted = jnp.where(row_valid, g_flat[row_assign], 0.0).reshape(rows, 1)

    x_sorted = jnp.take(hn, src_tok, axis=0)
    y_sorted = _gmm(tile_expert, tile_valid, x_sorted, gate_sorted, w1, w3, w2)

    outs = []
    off = 0
    dest2 = dest.reshape(-1, 2)
    for x in xs:
        m = x.shape[0]
        d1 = dest2[off:off + m, 0]
        d2 = dest2[off:off + m, 1]
        outs.append(_combine(x, jnp.take(y_sorted, d1, axis=0), jnp.take(y_sorted, d2, axis=0), g5,
                             tm=min(ROW_TILE, m)))
        off += m
    return outs


def _block_diag_pairs(w):
    nb, bw, _ = w.shape
    w = w.reshape(nb // 2, 2, bw, bw)
    z = jnp.zeros((nb // 2, bw, bw), w.dtype)
    top = jnp.concatenate([w[:, 0], z], axis=2)
    bot = jnp.concatenate([z, w[:, 1]], axis=2)
    return jnp.concatenate([top, bot], axis=1).astype(BF16)


def kernel(x_prompt, x_sample, cache_fox_k, cache_fox_v, cache_fox_logf, state_lru_h, state_lru_conv, state_conv_buf, state_pool_buf, cache_mem_k, cache_mem_v, page_table, mem_prompt, norm_g, w_xq, w_xk, w_xv, w_xo, w_in_e, b_f, lru_conv_w, lru_conv_b, lru_wa, lru_ba, lru_wi, lru_bi, lru_lam, w_out_e, w_ff1, w_ff3, w_ff2, w_in_o, cc_dw_w, cc_dw_b, cc_ln_g, cc_ln_b, pool_w, pool_scale, w_out_o, w_router, b_router, w_e1, w_e3, w_e2):
    bp, seq, d = x_prompt.shape
    bs = x_sample.shape[0]
    depth = norm_g.shape[0]
    page = cache_fox_k.shape[2]
    past_len = page_table.shape[1] * page
    mem_len = mem_prompt.shape[1]
    tm_p = ROW_TILE

    xp = x_prompt.reshape(bp * seq, d)
    xs = x_sample.reshape(bs, d)
    mem = mem_prompt.reshape(bp * mem_len, d)
    vec = lambda v: v.reshape(1, -1)

    fk_p, fv_p, fl_p, lh_p, lc_p, cb_p, pb_p, mk_pl, mv_pl = [], [], [], [], [], [], [], [], []
    fk_s, fv_s, fl_s, lh_s, lc_s, cb_s, pb_s = [], [], [], [], [], [], []

    for l in range(depth):
        g = [vec(norm_g[l, i]) for i in range(norm_g.shape[1])]
        w_kv = jnp.concatenate([w_xk[l], w_xv[l]], axis=1).astype(BF16)
        memkv = _norm_matmul(mem, g[6], w_kv, tm=tm_p, name="mem_kv")
        mk_pl.append(memkv[:, :MEM_W].reshape(bp, mem_len, MEM_H, MEM_HD))
        mv_pl.append(memkv[:, MEM_W:].reshape(bp, mem_len, MEM_H, MEM_HD))

        if l % 2 == 0:
            i = l // 2
            n_main = 2 * LRU_W + 3 * FOX_W
            w_main = w_in_e[i][:, :n_main].astype(BF16)
            w_fl_t = jnp.pad(w_in_e[i][:, n_main:].T, ((0, 16 - FOX_H), (0, 0))).astype(BF16)
            cw, cb = lru_conv_w[i], vec(lru_conv_b[i])
            wa, wi = _block_diag_pairs(lru_wa[i]), _block_diag_pairs(lru_wi[i])
            ba, bi, lam = vec(lru_ba[i]), vec(lru_bi[i]), vec(lru_lam[i])
            bf = b_f[i].reshape(FOX_H, 1)
            w_out = w_out_e[i].astype(BF16)

            z, fl_t = _norm_matmul(xp, g[0], w_main, tm=tm_p, wt=w_fl_t, name="in_proj_even")
            lf_t, c_t = _fox_prep(fl_t, bf, bp, seq)
            lru_out, h_last = _lru_prompt(z, bp, seq, cw, cb, wa, ba, wi, bi, lam)
            att = _fox_prompt(z, c_t, bp, seq)
            xp = _matmul_norm_res([lru_out, att], w_out, g[1], xp, tm=tm_p, name="out_proj_even")
            z3 = z.reshape(bp, seq, n_main)
            fk_p.append(z3[:, :, 2 * LRU_W + FOX_W:2 * LRU_W + 2 * FOX_W].reshape(bp, seq, FOX_H, FOX_HD))
            fv_p.append(z3[:, :, 2 * LRU_W + 2 * FOX_W:].reshape(bp, seq, FOX_H, FOX_HD))
            fl_p.append(lf_t.T.reshape(bp, seq, FOX_H))
            lh_p.append(h_last.reshape(bp, LRU_W))
            lc_p.append(z3[:, seq - (LRU_CONV - 1):, :LRU_W])

            zs, fls_t = _norm_matmul(xs, g[0], w_main, tm=bs, wt=w_fl_t, name="in_proj_even_s")
            lfs_t = _logsig(fls_t, bf)
            pre_t = jnp.swapaxes(state_lru_conv[i], 0, 1)
            lru_out_s, h_s = _lru_sample(zs, pre_t, state_lru_h[i], cw, cb, wa, ba, wi, bi, lam)
            q_s = zs[:, 2 * LRU_W:2 * LRU_W + FOX_W]
            k_s = zs[:, 2 * LRU_W + FOX_W:2 * LRU_W + 2 * FOX_W]
            v_s = zs[:, 2 * LRU_W + 2 * FOX_W:]
            att_s = _fox_sample(page_table, i, q_s.reshape(bs, FOX_H, FOX_HD), k_s.reshape(bs, FOX_H, FOX_HD),
                                v_s.reshape(bs, FOX_H, FOX_HD), lfs_t.T.reshape(bs, FOX_H, 1),
                                cache_fox_k.reshape(cache_fox_k.shape[:2] + (page * FOX_H, FOX_HD)),
                                cache_fox_v.reshape(cache_fox_v.shape[:2] + (page * FOX_H, FOX_HD)),
                                jnp.swapaxes(cache_fox_logf[i], 1, 2))
            xs = _matmul_norm_res([lru_out_s, att_s.reshape(bs, FOX_W)], w_out, g[1], xs, tm=bs,
                                  name="out_proj_even_s")
            fk_s.append(k_s.reshape(bs, 1, FOX_H, FOX_HD))
            fv_s.append(v_s.reshape(bs, 1, FOX_H, FOX_HD))
            fl_s.append(lfs_t.T.reshape(bs, 1, FOX_H))
            lh_s.append(h_s)
            lc_s.append(jnp.concatenate([state_lru_conv[i][:, 1:], zs[:, None, :LRU_W]], axis=1))
        else:
            j = l // 2
            w_in = w_in_o[j].astype(BF16)
            dww, dwb = cc_dw_w[j], vec(cc_dw_b[j])
            lng, lnb = vec(cc_ln_g[j]), vec(cc_ln_b[j])
            pw, ps = pool_w[j].astype(BF16), vec(pool_scale[j])
            w_out = w_out_o[j].astype(BF16)

            z = _norm_matmul(xp, g[0], w_in, tm=tm_p, name="in_proj_odd")
            cv, pool, cbuf, pbuf = _odd_prompt(z, bp, seq, dww, dwb, lng, lnb, pw, ps)
            xp = _matmul_norm_res([cv, pool], w_out, g[1], xp, tm=tm_p, name="out_proj_odd")
            cb_p.append(cbuf)
            pb_p.append(pbuf)

            zs = _norm_matmul(xs, g[0], w_in, tm=bs, name="in_proj_odd_s")
            cv_s, pool_s, glu_s = _odd_sample(zs, jnp.swapaxes(state_conv_buf[j], 0, 1),
                                              jnp.swapaxes(state_pool_buf[j], 0, 1),
                                              dww, dwb, lng, lnb, pw, ps, past_len)
            xs = _matmul_norm_res([cv_s, pool_s], w_out, g[1], xs, tm=bs, name="out_proj_odd_s")
            cb_s.append(jnp.concatenate([state_conv_buf[j][:, 1:], glu_s[:, None, :]], axis=1))
            pb_s.append(jnp.concatenate([state_pool_buf[j][:, 1:], zs[:, None, 2 * CONV_W:]], axis=1))

        wq, wo = w_xq[l].astype(BF16), w_xo[l].astype(BF16)
        q = _norm_matmul(xp, g[2], wq, tm=tm_p, name="xattn_q")
        o = _xattn_prompt(q, memkv, bp, seq)
        xp = _matmul_norm_res([o], wo, g[3], xp, tm=tm_p, name="xattn_o")
        q_s = _norm_matmul(xs, g[2], wq, tm=bs, name="xattn_q_s")
        o_s = _xattn_sample(q_s.reshape(bs, 1, MEM_W), cache_mem_k[l].reshape(bs, mem_len, MEM_W),
                            cache_mem_v[l].reshape(bs, mem_len, MEM_W))
        xs = _matmul_norm_res([o_s.reshape(bs, MEM_W)], wo, g[3], xs, tm=bs, name="xattn_o_s")

        if l % 2 == 0:
            i = l // 2
            w1, w3, w2 = w_ff1[i].astype(BF16), w_ff3[i].astype(BF16), w_ff2[i].astype(BF16)
            xp = _ffn(xp, g[4], g[5], w1, w3, w2, tm=tm_p)
            xs = _ffn(xs, g[4], g[5], w1, w3, w2, tm=bs)
        else:
            j = l // 2
            xp, xs = _moe_block([xp, xs], g[4], g[5], w_router[j], b_router[j],
                                w_e1[j].astype(BF16), w_e3[j].astype(BF16), w_e2[j].astype(BF16))

    return (xp.reshape(bp, seq, d), xs.reshape(bs, 1, d),
            jnp.stack(fk_p), jnp.stack(fv_p), jnp.stack(fl_p), jnp.stack(lh_p), jnp.stack(lc_p),
            jnp.stack(cb_p), jnp.stack(pb_p), jnp.stack(mk_pl), jnp.stack(mv_pl),
            jnp.stack(fk_s), jnp.stack(fv_s), jnp.stack(fl_s), jnp.stack(lh_s), jnp.stack(lc_s),
            jnp.stack(cb_s), jnp.stack(pb_s))
```

```python
import functools

import jax
import jax.numpy as jnp
from jax import lax
from jax.experimental import pallas as pl
from jax.experimental.pallas import tpu as pltpu

F32 = jnp.float32
BF16 = jnp.bfloat16

D_MODEL = 1024
LRU_W = 512
LRU_CONV = 4
LRU_C = 8.0
FOX_H = 8
FOX_HD = 64
FOX_W = FOX_H * FOX_HD
CONV_W = 512
CONV_K = 31
POOL_W = 512
POOL_WINDOWS = (2, 4, 8, 16)
POOL_GW = POOL_W // len(POOL_WINDOWS)
POOL_BUF = max(POOL_WINDOWS) - 1
MEM_H = 4
MEM_HD = 128
MEM_W = MEM_H * MEM_HD
N_EXPERTS = 8
EPS = 1e-6
NEG = -1e30

LANES = 128
ROW_TILE = 512
SEQ_TILE = 512
SCAN_TILE = 256
ATT_ROWS = 32
PAGES_PER_STEP = 8
MOE_TILE = 512
FF_ROW_SLAB = 256
MOE_FF_TILE = 1792
FFN_FF_TILE = 1408
VMEM_LIMIT = 56 * 1024 * 1024


def _cp(*sem):
    return pltpu.CompilerParams(dimension_semantics=sem, vmem_limit_bytes=VMEM_LIMIT)


def _rms(x, g):
    return x * lax.rsqrt(jnp.mean(x * x, axis=-1, keepdims=True) + EPS) * g


def _sigmoid(x):
    return 1.0 / (1.0 + jnp.exp(-x))


def _softplus(x):
    return jnp.maximum(x, 0.0) + jnp.log1p(jnp.exp(-jnp.abs(x)))


def _gelu_tanh(x):
    return 0.5 * x * (1.0 + jnp.tanh(0.7978845608028654 * (x + 0.044715 * (x * x * x))))


def _dot(a, b):
    return jnp.dot(a, b, preferred_element_type=F32)


def _dot_nt(a, b):
    return lax.dot_general(a, b, (((1,), (1,)), ((), ())), preferred_element_type=F32)


def _norm_matmul_kernel(x_ref, g_ref, w_ref, *rest, has_t):
    hn = _rms(x_ref[...], g_ref[...]).astype(BF16)
    if has_t:
        wt_ref, o_ref, ot_ref = rest
        ot_ref[...] = _dot_nt(wt_ref[...], hn)
    else:
        (o_ref,) = rest
    o_ref[...] = _dot(hn, w_ref[...])


def _norm_matmul(x, g, w, *, tm, wt=None, name):
    m, d = x.shape
    n = w.shape[1]
    in_specs = [pl.BlockSpec((tm, d), lambda i: (i, 0)),
                pl.BlockSpec((1, d), lambda i: (0, 0)),
                pl.BlockSpec((d, n), lambda i: (0, 0))]
    out_shape = [jax.ShapeDtypeStruct((m, n), F32)]
    out_specs = [pl.BlockSpec((tm, n), lambda i: (i, 0))]
    args = [x, g, w]
    if wt is not None:
        in_specs.append(pl.BlockSpec(wt.shape, lambda i: (0, 0)))
        out_shape.append(jax.ShapeDtypeStruct((wt.shape[0], m), F32))
        out_specs.append(pl.BlockSpec((wt.shape[0], tm), lambda i: (0, i)))
        args.append(wt)
    res = pl.pallas_call(
        functools.partial(_norm_matmul_kernel, has_t=wt is not None),
        grid=(m // tm,), in_specs=in_specs, out_specs=out_specs, out_shape=out_shape,
        compiler_params=_cp("parallel"), name=name)(*args)
    return res if wt is not None else res[0]


def _matmul_norm_res_kernel(*refs, widths):
    n_a = len(widths)
    a_refs = refs[:n_a]
    w_ref, g_ref, r_ref, o_ref = refs[n_a:]
    y = None
    off = 0
    for a_ref, k in zip(a_refs, widths):
        part = _dot(a_ref[...].astype(BF16), w_ref[off:off + k, :])
        y = part if y is None else y + part
        off += k
    o_ref[...] = r_ref[...] + _rms(y, g_ref[...])


def _matmul_norm_res(a_list, w, g, resid, *, tm, name):
    m, d = resid.shape
    widths = tuple(a.shape[1] for a in a_list)
    in_specs = [pl.BlockSpec((tm, k), lambda i: (i, 0)) for k in widths]
    in_specs += [pl.BlockSpec(w.shape, lambda i: (0, 0)),
                 pl.BlockSpec((1, d), lambda i: (0, 0)),
                 pl.BlockSpec((tm, d), lambda i: (i, 0))]
    return pl.pallas_call(
        functools.partial(_matmul_norm_res_kernel, widths=widths),
        grid=(m // tm,), in_specs=in_specs,
        out_specs=pl.BlockSpec((tm, d), lambda i: (i, 0)),
        out_shape=jax.ShapeDtypeStruct((m, d), F32),
        compiler_params=_cp("parallel"), name=name)(*a_list, w, g, resid)


def _lru_gates(xc, wa_ref, ba_ref, wi_ref, bi_ref, lam_ref):
    xb = xc.astype(BF16)
    ra, ia = [], []
    for c in range(LRU_W // LANES):
        xs = xb[:, c * LANES:(c + 1) * LANES]
        ra.append(_dot(xs, wa_ref[c]))
        ia.append(_dot(xs, wi_ref[c]))
    r = _sigmoid(jnp.concatenate(ra, axis=1) + ba_ref[...])
    ig = _sigmoid(jnp.concatenate(ia, axis=1) + bi_ref[...])
    log_a = -LRU_C * r * _softplus(-lam_ref[...])
    a = jnp.exp(log_a)
    bx = jnp.sqrt(-jnp.tanh(log_a) * (a * a + 1.0)) * (ig * xc)
    return a, bx


def _lru_prompt_kernel(xl_ref, gate_ref, cw_ref, cb_ref, wa_ref, ba_ref, wi_ref, bi_ref, lam_ref,
                       out_ref, hlast_ref, ext_ref, hc_ref, *, tc, n_t):
    t = pl.program_id(1)

    @pl.when(t == 0)
    def _():
        ext_ref[0:8, :] = jnp.zeros((8, LRU_W), F32)
        hc_ref[...] = jnp.zeros((1, LRU_W), F32)

    xl = xl_ref[...]
    ext_ref[8:8 + tc, :] = xl
    xc = cb_ref[...] + cw_ref[LRU_CONV - 1:LRU_CONV, :] * xl
    for j in range(1, LRU_CONV):
        xc = xc + cw_ref[LRU_CONV - 1 - j:LRU_CONV - j, :] * ext_ref[pl.ds(8 - j, tc), :]
    ext_ref[0:8, :] = ext_ref[tc:tc + 8, :]

    a, b = _lru_gates(xc, wa_ref, ba_ref, wi_ref, bi_ref, lam_ref)
    row = lax.broadcasted_iota(jnp.int32, (tc, 1), 0)
    d = 1
    while d < tc:
        keep = row >= d
        a_sh = jnp.where(keep, pltpu.roll(a, d, 0), 1.0)
        b_sh = jnp.where(keep, pltpu.roll(b, d, 0), 0.0)
        b = a * b_sh + b
        a = a * a_sh
        d *= 2
    h = a * hc_ref[...] + b
    hc_ref[...] = h[tc - 1:tc, :]
    out_ref[...] = (_gelu_tanh(gate_ref[...]) * h).astype(BF16)

    @pl.when(t == n_t - 1)
    def _():
        hlast_ref[...] = h[tc - 1:tc, :]


def _lru_prompt(z, n_b, seq, cw, cb, wa, ba, wi, bi, lam):
    tc = SCAN_TILE
    n_t = seq // tc
    vec = lambda: pl.BlockSpec((1, LRU_W), lambda b, t: (0, 0))
    bd = lambda: pl.BlockSpec((LRU_W // LANES, LANES, LANES), lambda b, t: (0, 0, 0))
    return pl.pallas_call(
        functools.partial(_lru_prompt_kernel, tc=tc, n_t=n_t),
        grid=(n_b, n_t),
        in_specs=[pl.BlockSpec((tc, LRU_W), lambda b, t: (b * n_t + t, 0)),
                  pl.BlockSpec((tc, LRU_W), lambda b, t: (b * n_t + t, 1)),
                  pl.BlockSpec((LRU_CONV, LRU_W), lambda b, t: (0, 0)),
                  vec(), bd(), vec(), bd(), vec(), vec()],
        out_specs=[pl.BlockSpec((tc, LRU_W), lambda b, t: (b * n_t + t, 0)),
                   pl.BlockSpec((None, 1, LRU_W), lambda b, t: (b, 0, 0))],
        out_shape=[jax.ShapeDtypeStruct((n_b * seq, LRU_W), BF16),
                   jax.ShapeDtypeStruct((n_b, 1, LRU_W), F32)],
        scratch_shapes=[pltpu.VMEM((tc + 8, LRU_W), F32), pltpu.VMEM((1, LRU_W), F32)],
        compiler_params=_cp("parallel", "arbitrary"), name="lru_prompt")(
            z, z, cw, cb, wa, ba, wi, bi, lam)


def _lru_sample_kernel(z_ref, pre_ref, h0_ref, cw_ref, cb_ref, wa_ref, ba_ref, wi_ref, bi_ref, lam_ref,
                       out_ref, h_ref):
    xl = z_ref[:, 0:LRU_W]
    gate = z_ref[:, LRU_W:2 * LRU_W]
    xc = cb_ref[...] + cw_ref[LRU_CONV - 1:LRU_CONV, :] * xl
    for k in range(LRU_CONV - 1):
        xc = xc + cw_ref[k:k + 1, :] * pre_ref[k]
    a, bx = _lru_gates(xc, wa_ref, ba_ref, wi_ref, bi_ref, lam_ref)
    h = a * h0_ref[...] + bx
    h_ref[...] = h
    out_ref[...] = (_gelu_tanh(gate) * h).astype(BF16)


def _lru_sample(z, prefix_t, h0, cw, cb, wa, ba, wi, bi, lam):
    n_b = z.shape[0]
    return pl.pallas_call(
        _lru_sample_kernel,
        out_shape=[jax.ShapeDtypeStruct((n_b, LRU_W), BF16), jax.ShapeDtypeStruct((n_b, LRU_W), F32)],
        compiler_params=pltpu.CompilerParams(vmem_limit_bytes=VMEM_LIMIT),
        name="lru_sample")(z, prefix_t, h0, cw, cb, wa, ba, wi, bi, lam)


def _log_sigmoid(x):
    return jnp.minimum(x, 0.0) - jnp.log1p(jnp.exp(-jnp.abs(x)))


def _lane_cumsum(x):
    n = x.shape[1]
    lane = lax.broadcasted_iota(jnp.int32, x.shape, 1)
    d = 1
    while d < n:
        x = x + jnp.where(lane >= d, pltpu.roll(x, d, 1), 0.0)
        d *= 2
    return x


def _fox_prep_kernel(fl_ref, bf_ref, lf_ref, c_ref):
    lf = _log_sigmoid(fl_ref[0:FOX_H, :] + bf_ref[...])
    lf_ref[...] = lf
    c_ref[...] = _lane_cumsum(lf)


def _fox_prep(fl_t, bf, n_b, seq):
    return pl.pallas_call(
        _fox_prep_kernel, grid=(n_b,),
        in_specs=[pl.BlockSpec((fl_t.shape[0], seq), lambda b: (0, b)),
                  pl.BlockSpec((FOX_H, 1), lambda b: (0, 0))],
        out_specs=[pl.BlockSpec((FOX_H, seq), lambda b: (0, b)),
                   pl.BlockSpec((FOX_H, seq), lambda b: (0, b))],
        out_shape=[jax.ShapeDtypeStruct((FOX_H, n_b * seq), F32)] * 2,
        compiler_params=_cp("parallel"), name="fox_prep")(fl_t, bf)


def _logsig_kernel(fl_ref, bf_ref, lf_ref):
    lf_ref[...] = _log_sigmoid(fl_ref[0:FOX_H, :] + bf_ref[...])


def _logsig(fl_t, bf):
    return pl.pallas_call(
        _logsig_kernel, out_shape=jax.ShapeDtypeStruct((FOX_H, fl_t.shape[1]), F32),
        name="fox_logf_sample")(fl_t, bf)


def _fox_prompt_kernel(qi_ref, ki_ref, q_ref, k_ref, v_ref, cq_ref, ck_ref, o_ref,
                       m_sc, l_sc, acc_sc, s_sc, p_sc, al_sc, *, tb):
    hp = pl.program_id(1)
    t = pl.program_id(2)
    qi = qi_ref[t]
    ki = ki_ref[t]

    @pl.when(ki == 0)
    def _():
        m_sc[...] = jnp.full(m_sc.shape, NEG, F32)
        l_sc[...] = jnp.zeros(l_sc.shape, F32)
        acc_sc[...] = jnp.zeros(acc_sc.shape, F32)

    def step(diagonal):
        q = q_ref[...] * (FOX_HD ** -0.5)
        k = k_ref[...].astype(BF16)
        v = v_ref[...].astype(BF16)
        lane = lax.broadcasted_iota(jnp.int32, (tb, LANES), 1)
        rc = ATT_ROWS
        if diagonal:
            row_i = lax.broadcasted_iota(jnp.int32, (rc, tb), 0)
            col_i = lax.broadcasted_iota(jnp.int32, (rc, tb), 1)
        n_lt = tb // LANES
        for h in range(2):
            in_head = (lane < FOX_HD) if h == 0 else (lane >= FOX_HD)
            s_sc[h] = _dot_nt(jnp.where(in_head, q, 0.0).astype(BF16), k)
            c_q = cq_ref[pl.ds(2 * hp + h, 1), :]
            c_k = ck_ref[pl.ds(2 * hp + h, 1), :]
            bias = c_q[:, 0:1] - c_k
            for c in range(tb // rc):
                rows = slice(c * rc, (c + 1) * rc)
                s = s_sc[h, rows, :] + bias
                if diagonal:
                    s = jnp.where(col_i <= row_i + c * rc, s, NEG)
                s_sc[h, rows, :] = s
                m_prev = m_sc[h, rows, :]
                m_new = jnp.maximum(m_prev, jnp.max(s, axis=1, keepdims=True))
                m_sc[h, rows, :] = m_new
                al_sc[h, rows, :] = jnp.exp(m_prev - m_new)
            for c in range(tb // rc):
                rows = slice(c * rc, (c + 1) * rc)
                m_new = m_sc[h, rows, :]
                p_sum = None
                for j in range(n_lt):
                    cols = slice(j * LANES, (j + 1) * LANES)
                    p = jnp.exp(s_sc[h, rows, cols] - m_new)
                    p_sc[h, rows, cols] = p.astype(BF16)
                    p_sum = p if p_sum is None else p_sum + p
                l_sc[h, rows, :] = al_sc[h, rows, :] * l_sc[h, rows, :] + p_sum
            acc_sc[h] = al_sc[h] * acc_sc[h] + _dot(p_sc[h], v)

    @pl.when(ki < qi)
    def _():
        step(False)

    @pl.when(ki == qi)
    def _():
        step(True)
        lane = lax.broadcasted_iota(jnp.int32, (tb, LANES), 1)
        o0 = acc_sc[0] / jnp.sum(l_sc[0], axis=1, keepdims=True)
        o1 = acc_sc[1] / jnp.sum(l_sc[1], axis=1, keepdims=True)
        o_ref[...] = jnp.where(lane < FOX_HD, o0, o1).astype(BF16)


def _fox_prompt(z, c_t, n_b, seq):
    tb = SEQ_TILE
    n_q = seq // tb
    q_blk = (2 * LRU_W) // LANES
    k_blk = q_blk + FOX_W // LANES
    v_blk = k_blk + FOX_W // LANES
    pairs = [(qi, ki) for qi in range(n_q) for ki in range(qi + 1)]
    qi_list = jnp.asarray([p[0] for p in pairs], jnp.int32)
    ki_list = jnp.asarray([p[1] for p in pairs], jnp.int32)
    grid_spec = pltpu.PrefetchScalarGridSpec(
        num_scalar_prefetch=2, grid=(n_b, FOX_H // 2, len(pairs)),
        in_specs=[
            pl.BlockSpec((tb, LANES), lambda b, hp, t, qi, ki: (b * n_q + qi[t], q_blk + hp)),
            pl.BlockSpec((tb, LANES), lambda b, hp, t, qi, ki: (b * n_q + ki[t], k_blk + hp)),
            pl.BlockSpec((tb, LANES), lambda b, hp, t, qi, ki: (b * n_q + ki[t], v_blk + hp)),
            pl.BlockSpec((FOX_H, tb), lambda b, hp, t, qi, ki: (0, b * n_q + qi[t])),
            pl.BlockSpec((FOX_H, tb), lambda b, hp, t, qi, ki: (0, b * n_q + ki[t])),
        ],
        out_specs=pl.BlockSpec((tb, LANES), lambda b, hp, t, qi, ki: (b * n_q + qi[t], hp)),
        scratch_shapes=[pltpu.VMEM((2, tb, LANES), F32), pltpu.VMEM((2, tb, LANES), F32),
                        pltpu.VMEM((2, tb, LANES), F32), pltpu.VMEM((2, tb, tb), F32),
                        pltpu.VMEM((2, tb, tb), BF16), pltpu.VMEM((2, tb, LANES), F32)])
    return pl.pallas_call(
        functools.partial(_fox_prompt_kernel, tb=tb), grid_spec=grid_spec,
        out_shape=jax.ShapeDtypeStruct((n_b * seq, FOX_W), BF16),
        compiler_params=_cp("parallel", "parallel", "arbitrary"),
        name="fox_prompt")(qi_list, ki_list, z, z, z, c_t, c_t)


def _fox_sample_kernel(pt_ref, q_ref, kn_ref, vn_ref, lfn_ref, *rest, n_pg, n_g):
    k_refs = rest[0:n_pg]
    v_refs = rest[n_pg:2 * n_pg]
    lf_refs = rest[2 * n_pg:3 * n_pg]
    o_ref, m_sc, l_sc, acc_sc, cc_sc = rest[3 * n_pg:]
    g = pl.program_id(1)
    page = lf_refs[0].shape[1]

    @pl.when(g == 0)
    def _():
        m_sc[...] = jnp.full(m_sc.shape, NEG, F32)
        l_sc[...] = jnp.zeros(l_sc.shape, F32)
        acc_sc[...] = jnp.zeros(acc_sc.shape, F32)
        cc_sc[...] = jnp.zeros(cc_sc.shape, F32)


    q = q_ref[...] * (FOX_HD ** -0.5)
    q16 = jnp.concatenate([q, jnp.zeros_like(q)], axis=0)
    row_q = lax.broadcasted_iota(jnp.int32, (2 * FOX_H, FOX_HD), 0)
    q_only = [jnp.where(row_q == h, q16, 0.0).astype(BF16) for h in range(FOX_H)]

    s_parts = []
    for j in range(n_pg):
        s_j = None
        for h in range(FOX_H):
            part = _dot(q_only[h], k_refs[j][h].astype(BF16))
            s_j = part if s_j is None else s_j + part
        s_parts.append(s_j[0:FOX_H])
    s = jnp.concatenate(s_parts, axis=1)
    lf = jnp.concatenate([lf_refs[j][...] for j in range(n_pg)], axis=1)
    c = _lane_cumsum(lf) + cc_sc[...]
    cc_sc[...] = c[:, c.shape[1] - 1:]
    s = s - c
    m_prev = m_sc[...]
    m_new = jnp.maximum(m_prev, jnp.max(s, axis=1, keepdims=True))
    alpha = jnp.exp(m_prev - m_new)
    p = jnp.exp(s - m_new)
    l_sc[...] = alpha * l_sc[...] + jnp.sum(p, axis=1, keepdims=True)
    p16 = jnp.concatenate([p, jnp.zeros_like(p)], axis=0)
    row_p = lax.broadcasted_iota(jnp.int32, (2 * FOX_H, page), 0)
    pv = None
    for j in range(n_pg):
        p_j = p16[:, j * page:(j + 1) * page]
        for h in range(FOX_H):
            part = _dot_nt(jnp.where(row_p == h, p_j, 0.0).astype(BF16), v_refs[j][h].astype(BF16))
            pv = part if pv is None else pv + part
    acc_sc[...] = alpha * acc_sc[...] + pv[0:FOX_H]
    m_sc[...] = m_new

    @pl.when(g == n_g - 1)
    def _():
        s_n = jnp.sum(q * kn_ref[...], axis=1, keepdims=True)
        s_n = s_n - (cc_sc[...] + lfn_ref[...])
        m_p = m_sc[...]
        m_n = jnp.maximum(m_p, s_n)
        al = jnp.exp(m_p - m_n)
        p_n = jnp.exp(s_n - m_n)
        l_n = al * l_sc[...] + p_n
        o_ref[...] = ((al * acc_sc[...] + p_n * vn_ref[...]) / l_n).astype(BF16)


def _fox_sample(page_table, layer, q, k_new, v_new, lf_new, cache_k, cache_v, cache_lf_t):
    n_b, n_pages = page_table.shape
    n_pg = PAGES_PER_STEP
    n_g = n_pages // n_pg
    page = cache_k.shape[4]
    head_spec = lambda: pl.BlockSpec((None, FOX_H, FOX_HD), lambda b, g, pt: (b, 0, 0))

    def kv_spec(j):
        return pl.BlockSpec((None, None, FOX_H, FOX_HD, page),
                            lambda b, g, pt, j=j: (layer, pt[b * n_pages + g * n_pg + j], 0, 0, 0))

    def lf_spec(j):
        return pl.BlockSpec((None, FOX_H, page),
                            lambda b, g, pt, j=j: (pt[b * n_pages + g * n_pg + j], 0, 0))

    in_specs = [head_spec(), head_spec(), head_spec(),
                pl.BlockSpec((None, FOX_H, 1), lambda b, g, pt: (b, 0, 0))]
    in_specs += [kv_spec(j) for j in range(n_pg)]
    in_specs += [kv_spec(j) for j in range(n_pg)]
    in_specs += [lf_spec(j) for j in range(n_pg)]
    grid_spec = pltpu.PrefetchScalarGridSpec(
        num_scalar_prefetch=1, grid=(n_b, n_g), in_specs=in_specs,
        out_specs=pl.BlockSpec((None, FOX_H, FOX_HD), lambda b, g, pt: (b, 0, 0)),
        scratch_shapes=[pltpu.VMEM((FOX_H, 1), F32), pltpu.VMEM((FOX_H, 1), F32),
                        pltpu.VMEM((FOX_H, FOX_HD), F32), pltpu.VMEM((FOX_H, 1), F32)])
    return pl.pallas_call(
        functools.partial(_fox_sample_kernel, n_pg=n_pg, n_g=n_g),
        grid_spec=grid_spec,
        out_shape=jax.ShapeDtypeStruct((n_b, FOX_H, FOX_HD), BF16),
        compiler_params=_cp("parallel", "arbitrary"), name="fox_sample")(
            page_table.reshape(-1), q, k_new, v_new, lf_new,
            *([cache_k] * n_pg), *([cache_v] * n_pg), *([cache_lf_t] * n_pg))


def _xattn_prompt_kernel(q_ref, mk_ref, mv_ref, o_ref):
    outs = []
    for h in range(MEM_H):
        cs = slice(h * MEM_HD, (h + 1) * MEM_HD)
        qh = q_ref[:, cs].astype(BF16)
        s = _dot_nt(qh, mk_ref[:, cs].astype(BF16)) * (MEM_HD ** -0.5)
        p = jnp.exp(s - jnp.max(s, axis=1, keepdims=True))
        l = jnp.sum(p, axis=1, keepdims=True)
        outs.append(_dot(p.astype(BF16), mv_ref[:, cs].astype(BF16)) / l)
    o_ref[...] = jnp.concatenate(outs, axis=1).astype(BF16)


def _xattn_prompt(q, memkv, n_b, seq):
    tb = SEQ_TILE
    n_t = seq // tb
    mem_len = memkv.shape[0] // n_b
    return pl.pallas_call(
        _xattn_prompt_kernel, grid=(n_b, n_t),
        in_specs=[pl.BlockSpec((tb, MEM_W), lambda b, t: (b * n_t + t, 0)),
                  pl.BlockSpec((mem_len, MEM_W), lambda b, t: (b, 0)),
                  pl.BlockSpec((mem_len, MEM_W), lambda b, t: (b, 1))],
        out_specs=pl.BlockSpec((tb, MEM_W), lambda b, t: (b * n_t + t, 0)),
        out_shape=jax.ShapeDtypeStruct((n_b * seq, MEM_W), BF16),
        compiler_params=_cp("parallel", "parallel"), name="xattn_prompt")(q, memkv, memkv)


def _xattn_sample_kernel(q_ref, mk_ref, mv_ref, o_ref):
    q = q_ref[...]
    row = lax.broadcasted_iota(jnp.int32, (16, MEM_W), 0)
    col = lax.broadcasted_iota(jnp.int32, (16, MEM_W), 1)
    head_cols = jnp.right_shift(col, MEM_HD.bit_length() - 1) == row
    q_rows = jnp.where(head_cols, q, 0.0).astype(BF16)
    s = _dot_nt(q_rows, mk_ref[...].astype(BF16)) * (MEM_HD ** -0.5)
    p = jnp.exp(s - jnp.max(s, axis=1, keepdims=True))
    l = jnp.sum(p, axis=1, keepdims=True)
    o = _dot(p.astype(BF16), mv_ref[...].astype(BF16)) / l
    o_ref[...] = jnp.sum(jnp.where(head_cols, o, 0.0), axis=0, keepdims=True).astype(BF16)


def _xattn_sample(q, mk, mv):
    n_b, mem_len, _ = mk.shape
    return pl.pallas_call(
        _xattn_sample_kernel, grid=(n_b,),
        in_specs=[pl.BlockSpec((None, 1, MEM_W), lambda b: (b, 0, 0)),
                  pl.BlockSpec((None, mem_len, MEM_W), lambda b: (b, 0, 0)),
                  pl.BlockSpec((None, mem_len, MEM_W), lambda b: (b, 0, 0))],
        out_specs=pl.BlockSpec((None, 1, MEM_W), lambda b: (b, 0, 0)),
        out_shape=jax.ShapeDtypeStruct((n_b, 1, MEM_W), BF16),
        compiler_params=_cp("parallel"), name="xattn_sample")(q, mk, mv)


def _swiglu_accumulate(x_sc, w1_ref, w3_ref, w2_ref, acc_sc):
    tm = x_sc.shape[0]
    slab = min(tm, FF_ROW_SLAB)
    for r in range(tm // slab):
        rows = slice(r * slab, (r + 1) * slab)
        x = x_sc[rows, :]
        h1 = _dot(x, w1_ref[...])
        h3 = _dot(x, w3_ref[...])
        hh = (h1 * _sigmoid(h1) * h3).astype(BF16)
        acc_sc[rows, :] += _dot(hh, w2_ref[...])


def _ffn_kernel(x_ref, g4_ref, g5_ref, w1_ref, w3_ref, w2_ref, o_ref, hn_sc, acc_sc, *, n_f):
    f = pl.program_id(1)

    @pl.when(f == 0)
    def _():
        hn_sc[...] = _rms(x_ref[...], g4_ref[...]).astype(BF16)
        acc_sc[...] = jnp.zeros(acc_sc.shape, F32)

    _swiglu_accumulate(hn_sc, w1_ref, w3_ref, w2_ref, acc_sc)

    @pl.when(f == n_f - 1)
    def _():
        o_ref[...] = x_ref[...] + _rms(acc_sc[...], g5_ref[...])


def _ffn(x, g4, g5, w1, w3, w2, *, tm):
    m, d = x.shape
    ff = w1.shape[1]
    tf = FFN_FF_TILE
    n_f = ff // tf
    return pl.pallas_call(
        functools.partial(_ffn_kernel, n_f=n_f), grid=(m // tm, n_f),
        in_specs=[pl.BlockSpec((tm, d), lambda i, f: (i, 0)),
                  pl.BlockSpec((1, d), lambda i, f: (0, 0)),
                  pl.BlockSpec((1, d), lambda i, f: (0, 0)),
                  pl.BlockSpec((d, tf), lambda i, f: (0, f)),
                  pl.BlockSpec((d, tf), lambda i, f: (0, f)),
                  pl.BlockSpec((tf, d), lambda i, f: (f, 0))],
        out_specs=pl.BlockSpec((tm, d), lambda i, f: (i, 0)),
        out_shape=jax.ShapeDtypeStruct((m, d), F32),
        scratch_shapes=[pltpu.VMEM((tm, d), BF16), pltpu.VMEM((tm, d), F32)],
        compiler_params=_cp("parallel", "arbitrary"), name="ffn")(x, g4, g5, w1, w3, w2)


def _layernorm_silu(x, g, b):
    mu = jnp.mean(x, axis=-1, keepdims=True)
    xc = x - mu
    y = xc * lax.rsqrt(jnp.mean(xc * xc, axis=-1, keepdims=True) + EPS) * g + b
    return y * _sigmoid(y)


def _odd_prompt_kernel(a_ref, gt_ref, up_ref, dww_ref, dwb_ref, lng_ref, lnb_ref, pw_ref, ps_ref,
                       cv_ref, pool_ref, cbuf_ref, pbuf_ref, eg_ref, eu_ref, *, tc, n_t):
    t = pl.program_id(1)
    halo_g, halo_u = 32, 16

    @pl.when(t == 0)
    def _():
        eg_ref[0:halo_g, :] = jnp.zeros((halo_g, CONV_W), F32)
        eu_ref[0:halo_u, :] = jnp.zeros((halo_u, POOL_W), F32)

    glu = a_ref[...] * _sigmoid(gt_ref[...])
    up = up_ref[...]
    eg_ref[halo_g:halo_g + tc, :] = glu
    eu_ref[halo_u:halo_u + tc, :] = up

    acc = dwb_ref[...] + dww_ref[CONV_K - 1:CONV_K, :] * glu
    for k in range(CONV_K - 1):
        acc = acc + dww_ref[k:k + 1, :] * eg_ref[pl.ds(halo_g - (CONV_K - 1) + k, tc), :]
    cv_ref[...] = _layernorm_silu(acc, lng_ref[...], lnb_ref[...]).astype(BF16)

    pos = t * tc + lax.broadcasted_iota(jnp.int32, (tc, 1), 0)
    outs = []
    for gi, w in enumerate(POOL_WINDOWS):
        cs = slice(gi * POOL_GW, (gi + 1) * POOL_GW)
        u_g = up[:, cs]
        win = u_g
        for j in range(1, w):
            win = win + eu_ref[pl.ds(halo_u - j, tc), cs]
        cnt = jnp.minimum(pos + 1, w).astype(F32)
        dlt = win / cnt - u_g
        outs.append(_dot(dlt.astype(BF16), pw_ref[gi]))
    pool_ref[...] = (jnp.concatenate(outs, axis=1) * ps_ref[...]).astype(BF16)

    @pl.when(t == n_t - 1)
    def _():
        cbuf_ref[...] = eg_ref[pl.ds(halo_g + tc - (CONV_K - 1), CONV_K - 1), :]
        pbuf_ref[...] = eu_ref[pl.ds(halo_u + tc - POOL_BUF, POOL_BUF), :]

    eg_ref[0:halo_g, :] = eg_ref[tc:tc + halo_g, :]
    eu_ref[0:halo_u, :] = eu_ref[tc:tc + halo_u, :]


def _odd_prompt(z, n_b, seq, dww, dwb, lng, lnb, pw, ps):
    tc = SEQ_TILE
    n_t = seq // tc
    vec = lambda: pl.BlockSpec((1, CONV_W), lambda b, t: (0, 0))
    return pl.pallas_call(
        functools.partial(_odd_prompt_kernel, tc=tc, n_t=n_t), grid=(n_b, n_t),
        in_specs=[pl.BlockSpec((tc, CONV_W), lambda b, t: (b * n_t + t, 0)),
                  pl.BlockSpec((tc, CONV_W), lambda b, t: (b * n_t + t, 1)),
                  pl.BlockSpec((tc, POOL_W), lambda b, t: (b * n_t + t, 2)),
                  pl.BlockSpec((CONV_K, CONV_W), lambda b, t: (0, 0)),
                  vec(), vec(), vec(),
                  pl.BlockSpec((len(POOL_WINDOWS), POOL_GW, POOL_GW), lambda b, t: (0, 0, 0)),
                  vec()],
        out_specs=[pl.BlockSpec((tc, CONV_W), lambda b, t: (b * n_t + t, 0)),
                   pl.BlockSpec((tc, POOL_W), lambda b, t: (b * n_t + t, 0)),
                   pl.BlockSpec((None, CONV_K - 1, CONV_W), lambda b, t: (b, 0, 0)),
                   pl.BlockSpec((None, POOL_BUF, POOL_W), lambda b, t: (b, 0, 0))],
        out_shape=[jax.ShapeDtypeStruct((n_b * seq, CONV_W), BF16),
                   jax.ShapeDtypeStruct((n_b * seq, POOL_W), BF16),
                   jax.ShapeDtypeStruct((n_b, CONV_K - 1, CONV_W), F32),
                   jax.ShapeDtypeStruct((n_b, POOL_BUF, POOL_W), F32)],
        scratch_shapes=[pltpu.VMEM((tc + 32, CONV_W), F32), pltpu.VMEM((tc + 16, POOL_W), F32)],
        compiler_params=_cp("parallel", "arbitrary"), name="odd_prompt")(
            z, z, z, dww, dwb, lng, lnb, pw, ps)


def _odd_sample_kernel(z_ref, cbuf_ref, pbuf_ref, dww_ref, dwb_ref, lng_ref, lnb_ref, pw_ref, ps_ref,
                       cv_ref, pool_ref, glu_ref, *, pos0):
    glu = z_ref[:, 0:CONV_W] * _sigmoid(z_ref[:, CONV_W:2 * CONV_W])
    up = z_ref[:, 2 * CONV_W:2 * CONV_W + POOL_W]
    glu_ref[...] = glu
    acc = dwb_ref[...] + dww_ref[CONV_K - 1:CONV_K, :] * glu
    for k in range(CONV_K - 1):
        acc = acc + dww_ref[k:k + 1, :] * cbuf_ref[k]
    cv_ref[...] = _layernorm_silu(acc, lng_ref[...], lnb_ref[...]).astype(BF16)
    outs = []
    for gi, w in enumerate(POOL_WINDOWS):
        cs = slice(gi * POOL_GW, (gi + 1) * POOL_GW)
        u_g = up[:, cs]
        win = u_g
        for j in range(1, w):
            win = win + pbuf_ref[POOL_BUF - j][:, cs]
        dlt = win / float(min(pos0 + 1, w)) - u_g
        outs.append(_dot(dlt.astype(BF16), pw_ref[gi]))
    pool_ref[...] = (jnp.concatenate(outs, axis=1) * ps_ref[...]).astype(BF16)


def _odd_sample(z, cbuf_t, pbuf_t, dww, dwb, lng, lnb, pw, ps, pos0):
    n_b = z.shape[0]
    return pl.pallas_call(
        functools.partial(_odd_sample_kernel, pos0=pos0),
        out_shape=[jax.ShapeDtypeStruct((n_b, CONV_W), BF16),
                   jax.ShapeDtypeStruct((n_b, POOL_W), BF16),
                   jax.ShapeDtypeStruct((n_b, CONV_W), F32)],
        compiler_params=pltpu.CompilerParams(vmem_limit_bytes=VMEM_LIMIT),
        name="odd_sample")(z, cbuf_t, pbuf_t, dww, dwb, lng, lnb, pw, ps)


def _router_kernel(x_ref, g_ref, whi_ref, wlo_ref, b_ref, hn_ref, idx_ref, gate_ref):
    hn = _rms(x_ref[...], g_ref[...])
    hb = hn.astype(BF16)
    hn_ref[...] = hn
    hlo = (hn - hb.astype(F32)).astype(BF16)
    logits = _dot(hb, whi_ref[...]) + (_dot(hb, wlo_ref[...]) + _dot(hlo, whi_ref[...])) + b_ref[...]
    lane = lax.broadcasted_iota(jnp.int32, logits.shape, 1)
    lane_f = lane.astype(F32)
    logits = jnp.where(lane < N_EXPERTS, logits, NEG)
    m1 = jnp.max(logits, axis=1, keepdims=True)
    i1 = jnp.min(jnp.where(logits == m1, lane_f, float(LANES)), axis=1, keepdims=True)
    rest = jnp.where(lane_f == i1, NEG, logits)
    m2 = jnp.max(rest, axis=1, keepdims=True)
    i2 = jnp.min(jnp.where(rest == m2, lane_f, float(LANES)), axis=1, keepdims=True)
    e = jnp.exp(m2 - m1)
    g1 = 1.0 / (1.0 + e)
    g2 = e / (1.0 + e)
    idx_ref[...] = jnp.where(lane == 0, i1, jnp.where(lane == 1, i2, 0.0)).astype(jnp.int32)
    gate_ref[...] = jnp.where(lane == 0, g1, jnp.where(lane == 1, g2, 0.0))


def _router(x, g, whi, wlo, b, *, tm):
    m, d = x.shape
    return pl.pallas_call(
        _router_kernel, grid=(m // tm,),
        in_specs=[pl.BlockSpec((tm, d), lambda i: (i, 0)),
                  pl.BlockSpec((1, d), lambda i: (0, 0)),
                  pl.BlockSpec((d, LANES), lambda i: (0, 0)),
                  pl.BlockSpec((d, LANES), lambda i: (0, 0)),
                  pl.BlockSpec((1, LANES), lambda i: (0, 0))],
        out_specs=[pl.BlockSpec((tm, d), lambda i: (i, 0)),
                   pl.BlockSpec((tm, LANES), lambda i: (i, 0)),
                   pl.BlockSpec((tm, LANES), lambda i: (i, 0))],
        out_shape=[jax.ShapeDtypeStruct((m, d), F32),
                   jax.ShapeDtypeStruct((m, LANES), jnp.int32),
                   jax.ShapeDtypeStruct((m, LANES), F32)],
        compiler_params=_cp("parallel"), name="router")(x, g, whi, wlo, b)


def _gmm_kernel(te_ref, tv_ref, x_ref, gate_ref, w1_ref, w3_ref, w2_ref, o_ref, acc_sc, xb_sc, *, n_f):
    t = pl.program_id(0)
    f = pl.program_id(1)

    @pl.when(f == 0)
    def _():
        acc_sc[...] = jnp.zeros(acc_sc.shape, F32)
        xb_sc[...] = x_ref[...].astype(BF16)

    @pl.when(tv_ref[t] != 0)
    def _():
        _swiglu_accumulate(xb_sc, w1_ref, w3_ref, w2_ref, acc_sc)

    @pl.when(f == n_f - 1)
    def _():
        o_ref[...] = acc_sc[...] * gate_ref[...]


def _gmm(tile_expert, tile_valid, x_sorted, gate_sorted, w1, w3, w2):
    rows, d = x_sorted.shape
    ff = w1.shape[2]
    tm, tf = MOE_TILE, MOE_FF_TILE
    n_f = ff // tf
    grid_spec = pltpu.PrefetchScalarGridSpec(
        num_scalar_prefetch=2, grid=(rows // tm, n_f),
        in_specs=[pl.BlockSpec((tm, d), lambda t, f, te, tv: (t, 0)),
                  pl.BlockSpec((tm, 1), lambda t, f, te, tv: (t, 0)),
                  pl.BlockSpec((None, d, tf), lambda t, f, te, tv: (te[t], 0, f)),
                  pl.BlockSpec((None, d, tf), lambda t, f, te, tv: (te[t], 0, f)),
                  pl.BlockSpec((None, tf, d), lambda t, f, te, tv: (te[t], f, 0))],
        out_specs=pl.BlockSpec((tm, d), lambda t, f, te, tv: (t, 0)),
        scratch_shapes=[pltpu.VMEM((tm, d), F32), pltpu.VMEM((tm, d), BF16)])
    return pl.pallas_call(
        functools.partial(_gmm_kernel, n_f=n_f), grid_spec=grid_spec,
        out_shape=jax.ShapeDtypeStruct((rows, d), F32),
        compiler_params=_cp("parallel", "arbitrary"), name="moe_experts")(
            tile_expert, tile_valid, x_sorted, gate_sorted, w1, w3, w2)


def _combine_kernel(x_ref, y1_ref, y2_ref, g_ref, o_ref):
    o_ref[...] = x_ref[...] + _rms(y1_ref[...] + y2_ref[...], g_ref[...])


def _combine(x, y1, y2, g, *, tm):
    m, d = x.shape
    row = lambda: pl.BlockSpec((tm, d), lambda i: (i, 0))
    return pl.pallas_call(
        _combine_kernel, grid=(m // tm,),
        in_specs=[row(), row(), row(), pl.BlockSpec((1, d), lambda i: (0, 0))],
        out_specs=row(), out_shape=jax.ShapeDtypeStruct((m, d), F32),
        compiler_params=_cp("parallel"), name="moe_combine")(x, y1, y2, g)


def _moe_block(xs, g4, g5, w_r, b_r, w1, w3, w2):
    pad = LANES - N_EXPERTS
    w_r_p = jnp.pad(w_r, ((0, 0), (0, pad)))
    whi = w_r_p.astype(BF16)
    wlo = (w_r_p - whi.astype(F32)).astype(BF16)
    b_p = jnp.pad(b_r, (0, pad)).reshape(1, LANES)
    hn, idx, gate = [], [], []
    for x in xs:
        h, i, gt = _router(x, g4, whi, wlo, b_p, tm=min(ROW_TILE, x.shape[0]))
        hn.append(h)
        idx.append(i[:, :2])
        gate.append(gt[:, :2])
    hn = jnp.concatenate(hn, axis=0)
    e_flat = jnp.concatenate(idx, axis=0).reshape(-1)
    g_flat = jnp.concatenate(gate, axis=0).reshape(-1)
    n_assign = e_flat.shape[0]
    tm = MOE_TILE
    n_tiles = n_assign // tm + N_EXPERTS
    rows = n_tiles * tm

    onehot = (e_flat[:, None] == jnp.arange(N_EXPERTS, dtype=jnp.int32)[None, :]).astype(jnp.int32)
    rank = jnp.sum((jnp.cumsum(onehot, axis=0) - onehot) * onehot, axis=1)
    cnt = jnp.sum(onehot, axis=0)
    tiles_e = (cnt + tm - 1) // tm
    tile_end = jnp.cumsum(tiles_e)
    row_start = (tile_end - tiles_e) * tm
    dest = row_start[e_flat] + rank
    tile_ids = jnp.arange(n_tiles, dtype=jnp.int32)
    tile_expert = jnp.minimum(jnp.sum((tile_end[None, :] <= tile_ids[:, None]).astype(jnp.int32), axis=1),
                              N_EXPERTS - 1)
    tile_valid = (tile_ids < tile_end[-1]).astype(jnp.int32)
    row_assign = jnp.zeros((rows,), jnp.int32).at[dest].set(jnp.arange(n_assign, dtype=jnp.int32))
    row_ids = jnp.arange(rows, dtype=jnp.int32)
    row_expert = jnp.repeat(tile_expert, tm)
    row_valid = (row_ids - row_start[row_expert]) < cnt[row_expert]
    src_tok = row_assign // 2
    gate_sorted = jnp.where(row_valid, g_flat[row_assign], 0.0).reshape(rows, 1)

    take_rows = lambda a, idx: a.at[idx].get(mode="promise_in_bounds")
    x_sorted = take_rows(hn, src_tok)
    y_sorted = _gmm(tile_expert, tile_valid, x_sorted, gate_sorted, w1, w3, w2)

    outs = []
    off = 0
    dest2 = dest.reshape(-1, 2)
    for x in xs:
        m = x.shape[0]
        d1 = dest2[off:off + m, 0]
        d2 = dest2[off:off + m, 1]
        outs.append(_combine(x, take_rows(y_sorted, d1), take_rows(y_sorted, d2), g5,
                             tm=min(ROW_TILE, m)))
        off += m
    return outs


def _block_diag_pairs(w):
    nb, bw, _ = w.shape
    w = w.reshape(nb // 2, 2, bw, bw)
    z = jnp.zeros((nb // 2, bw, bw), w.dtype)
    top = jnp.concatenate([w[:, 0], z], axis=2)
    bot = jnp.concatenate([z, w[:, 1]], axis=2)
    return jnp.concatenate([top, bot], axis=1).astype(BF16)


def kernel(x_prompt, x_sample, cache_fox_k, cache_fox_v, cache_fox_logf, state_lru_h, state_lru_conv, state_conv_buf, state_pool_buf, cache_mem_k, cache_mem_v, page_table, mem_prompt, norm_g, w_xq, w_xk, w_xv, w_xo, w_in_e, b_f, lru_conv_w, lru_conv_b, lru_wa, lru_ba, lru_wi, lru_bi, lru_lam, w_out_e, w_ff1, w_ff3, w_ff2, w_in_o, cc_dw_w, cc_dw_b, cc_ln_g, cc_ln_b, pool_w, pool_scale, w_out_o, w_router, b_router, w_e1, w_e3, w_e2):
    bp, seq, d = x_prompt.shape
    bs = x_sample.shape[0]
    depth = norm_g.shape[0]
    page = cache_fox_k.shape[2]
    past_len = page_table.shape[1] * page
    mem_len = mem_prompt.shape[1]
    tm_p = ROW_TILE

    xp = x_prompt.reshape(bp * seq, d)
    xs = x_sample.reshape(bs, d)
    mem = mem_prompt.reshape(bp * mem_len, d)
    vec = lambda v: v.reshape(1, -1)

    fk_p, fv_p, fl_p, lh_p, lc_p, cb_p, pb_p, mk_pl, mv_pl = [], [], [], [], [], [], [], [], []
    fk_s, fv_s, fl_s, lh_s, lc_s, cb_s, pb_s = [], [], [], [], [], [], []

    for l in range(depth):
        g = [vec(norm_g[l, i]) for i in range(norm_g.shape[1])]
        w_kv = jnp.concatenate([w_xk[l], w_xv[l]], axis=1).astype(BF16)
        memkv = _norm_matmul(mem, g[6], w_kv, tm=tm_p, name="mem_kv")
        mk_pl.append(memkv[:, :MEM_W].reshape(bp, mem_len, MEM_H, MEM_HD))
        mv_pl.append(memkv[:, MEM_W:].reshape(bp, mem_len, MEM_H, MEM_HD))

        if l % 2 == 0:
            i = l // 2
            n_main = 2 * LRU_W + 3 * FOX_W
            w_main = w_in_e[i][:, :n_main].astype(BF16)
            w_fl_t = jnp.pad(w_in_e[i][:, n_main:].T, ((0, 16 - FOX_H), (0, 0))).astype(BF16)
            cw, cb = lru_conv_w[i], vec(lru_conv_b[i])
            wa, wi = _block_diag_pairs(lru_wa[i]), _block_diag_pairs(lru_wi[i])
            ba, bi, lam = vec(lru_ba[i]), vec(lru_bi[i]), vec(lru_lam[i])
            bf = b_f[i].reshape(FOX_H, 1)
            w_out = w_out_e[i].astype(BF16)

            z, fl_t = _norm_matmul(xp, g[0], w_main, tm=tm_p, wt=w_fl_t, name="in_proj_even")
            lf_t, c_t = _fox_prep(fl_t, bf, bp, seq)
            lru_out, h_last = _lru_prompt(z, bp, seq, cw, cb, wa, ba, wi, bi, lam)
            att = _fox_prompt(z, c_t, bp, seq)
            xp = _matmul_norm_res([lru_out, att], w_out, g[1], xp, tm=tm_p, name="out_proj_even")
            z3 = z.reshape(bp, seq, n_main)
            fk_p.append(z3[:, :, 2 * LRU_W + FOX_W:2 * LRU_W + 2 * FOX_W].reshape(bp, seq, FOX_H, FOX_HD))
            fv_p.append(z3[:, :, 2 * LRU_W + 2 * FOX_W:].reshape(bp, seq, FOX_H, FOX_HD))
            fl_p.append(lf_t.T.reshape(bp, seq, FOX_H))
            lh_p.append(h_last.reshape(bp, LRU_W))
            lc_p.append(z3[:, seq - (LRU_CONV - 1):, :LRU_W])

            zs, fls_t = _norm_matmul(xs, g[0], w_main, tm=bs, wt=w_fl_t, name="in_proj_even_s")
            lfs_t = _logsig(fls_t, bf)
            pre_t = jnp.swapaxes(state_lru_conv[i], 0, 1)
            lru_out_s, h_s = _lru_sample(zs, pre_t, state_lru_h[i], cw, cb, wa, ba, wi, bi, lam)
            q_s = zs[:, 2 * LRU_W:2 * LRU_W + FOX_W]
            k_s = zs[:, 2 * LRU_W + FOX_W:2 * LRU_W + 2 * FOX_W]
            v_s = zs[:, 2 * LRU_W + 2 * FOX_W:]
            att_s = _fox_sample(page_table, i, q_s.reshape(bs, FOX_H, FOX_HD), k_s.reshape(bs, FOX_H, FOX_HD),
                                v_s.reshape(bs, FOX_H, FOX_HD), lfs_t.T.reshape(bs, FOX_H, 1),
                                jnp.transpose(cache_fox_k, (0, 1, 3, 4, 2)),
                                jnp.transpose(cache_fox_v, (0, 1, 3, 4, 2)),
                                jnp.swapaxes(cache_fox_logf[i], 1, 2))
            xs = _matmul_norm_res([lru_out_s, att_s.reshape(bs, FOX_W)], w_out, g[1], xs, tm=bs,
                                  name="out_proj_even_s")
            fk_s.append(k_s.reshape(bs, 1, FOX_H, FOX_HD))
            fv_s.append(v_s.reshape(bs, 1, FOX_H, FOX_HD))
            fl_s.append(lfs_t.T.reshape(bs, 1, FOX_H))
            lh_s.append(h_s)
            lc_s.append(jnp.concatenate([state_lru_conv[i][:, 1:], zs[:, None, :LRU_W]], axis=1))
        else:
            j = l // 2
            w_in = w_in_o[j].astype(BF16)
            dww, dwb = cc_dw_w[j], vec(cc_dw_b[j])
            lng, lnb = vec(cc_ln_g[j]), vec(cc_ln_b[j])
            pw, ps = pool_w[j].astype(BF16), vec(pool_scale[j])
            w_out = w_out_o[j].astype(BF16)

            z = _norm_matmul(xp, g[0], w_in, tm=tm_p, name="in_proj_odd")
            cv, pool, cbuf, pbuf = _odd_prompt(z, bp, seq, dww, dwb, lng, lnb, pw, ps)
            xp = _matmul_norm_res([cv, pool], w_out, g[1], xp, tm=tm_p, name="out_proj_odd")
            cb_p.append(cbuf)
            pb_p.append(pbuf)

            zs = _norm_matmul(xs, g[0], w_in, tm=bs, name="in_proj_odd_s")
            cv_s, pool_s, glu_s = _odd_sample(zs, jnp.swapaxes(state_conv_buf[j], 0, 1),
                                              jnp.swapaxes(state_pool_buf[j], 0, 1),
                                              dww, dwb, lng, lnb, pw, ps, past_len)
            xs = _matmul_norm_res([cv_s, pool_s], w_out, g[1], xs, tm=bs, name="out_proj_odd_s")
            cb_s.append(jnp.concatenate([state_conv_buf[j][:, 1:], glu_s[:, None, :]], axis=1))
            pb_s.append(jnp.concatenate([state_pool_buf[j][:, 1:], zs[:, None, 2 * CONV_W:]], axis=1))

        wq, wo = w_xq[l].astype(BF16), w_xo[l].astype(BF16)
        q = _norm_matmul(xp, g[2], wq, tm=tm_p, name="xattn_q")
        o = _xattn_prompt(q, memkv, bp, seq)
        xp = _matmul_norm_res([o], wo, g[3], xp, tm=tm_p, name="xattn_o")
        q_s = _norm_matmul(xs, g[2], wq, tm=bs, name="xattn_q_s")
        o_s = _xattn_sample(q_s.reshape(bs, 1, MEM_W), cache_mem_k[l].reshape(bs, mem_len, MEM_W),
                            cache_mem_v[l].reshape(bs, mem_len, MEM_W))
        xs = _matmul_norm_res([o_s.reshape(bs, MEM_W)], wo, g[3], xs, tm=bs, name="xattn_o_s")

        if l % 2 == 0:
            i = l // 2
            w1, w3, w2 = w_ff1[i].astype(BF16), w_ff3[i].astype(BF16), w_ff2[i].astype(BF16)
            xp = _ffn(xp, g[4], g[5], w1, w3, w2, tm=tm_p)
            xs = _ffn(xs, g[4], g[5], w1, w3, w2, tm=bs)
        else:
            j = l // 2
            xp, xs = _moe_block([xp, xs], g[4], g[5], w_router[j], b_router[j],
                                w_e1[j].astype(BF16), w_e3[j].astype(BF16), w_e2[j].astype(BF16))

    return (xp.reshape(bp, seq, d), xs.reshape(bs, 1, d),
            jnp.stack(fk_p), jnp.stack(fv_p), jnp.stack(fl_p), jnp.stack(lh_p), jnp.stack(lc_p),
            jnp.stack(cb_p), jnp.stack(pb_p), jnp.stack(mk_pl), jnp.stack(mv_pl),
            jnp.stack(fk_s), jnp.stack(fv_s), jnp.stack(fl_s), jnp.stack(lh_s), jnp.stack(lc_s),
            jnp.stack(cb_s), jnp.stack(pb_s))
```

```python
import functools

import jax
import jax.numpy as jnp
from jax import lax
from jax.experimental import pallas as pl
from jax.experimental.pallas import tpu as pltpu

F32 = jnp.float32
BF16 = jnp.bfloat16

D_MODEL = 1024
LRU_W = 512
LRU_CONV = 4
LRU_C = 8.0
FOX_H = 8
FOX_HD = 64
FOX_W = FOX_H * FOX_HD
CONV_W = 512
CONV_K = 31
POOL_W = 512
POOL_WINDOWS = (2, 4, 8, 16)
POOL_GW = POOL_W // len(POOL_WINDOWS)
POOL_BUF = max(POOL_WINDOWS) - 1
MEM_H = 4
MEM_HD = 128
MEM_W = MEM_H * MEM_HD
N_EXPERTS = 8
EPS = 1e-6
NEG = -1e30

LANES = 128
ROW_TILE = 512
SEQ_TILE = 512
SCAN_TILE = 256
ATT_ROWS = 32
CONV_ROWS = 64
ATT_HEADS = 4
PAGES_PER_STEP = 8
MOE_TILE = 512
FF_ROW_SLAB = 256
MOE_FF_TILE = 1792
FFN_FF_TILE = 1408
CAST_BLOCK_BYTES = 8 * 1024 * 1024
DMA_BLOCK_ROWS = 32
DMA_BLOCKS_IN_FLIGHT = 8
VMEM_LIMIT = 56 * 1024 * 1024


def _cp(*sem):
    return pltpu.CompilerParams(dimension_semantics=sem, vmem_limit_bytes=VMEM_LIMIT)


def _rms(x, g):
    return x * lax.rsqrt(jnp.mean(x * x, axis=-1, keepdims=True) + EPS) * g


def _sigmoid(x):
    return 1.0 / (1.0 + jnp.exp(-x))


def _softplus(x):
    return jnp.maximum(x, 0.0) + jnp.log1p(jnp.exp(-jnp.abs(x)))


def _gelu_tanh(x):
    return 0.5 * x * (1.0 + jnp.tanh(0.7978845608028654 * (x + 0.044715 * (x * x * x))))


def _dot(a, b):
    return jnp.dot(a, b, preferred_element_type=F32)


def _dot_nt(a, b):
    return lax.dot_general(a, b, (((1,), (1,)), ((), ())), preferred_element_type=F32)


def _cast_kernel(x_ref, o_ref):
    o_ref[...] = x_ref[...].astype(BF16)


def _cast_bf16(w):
    e, k, n = w.shape
    n_k = pl.cdiv(k * n * 4, CAST_BLOCK_BYTES)
    assert k % (16 * n_k) == 0
    return pl.pallas_call(
        _cast_kernel, grid=(e, n_k),
        in_specs=[pl.BlockSpec((None, k // n_k, n), lambda i, j: (i, j, 0))],
        out_specs=pl.BlockSpec((None, k // n_k, n), lambda i, j: (i, j, 0)),
        out_shape=jax.ShapeDtypeStruct(w.shape, BF16),
        compiler_params=_cp("parallel", "parallel"), name="cast_bf16")(w)


def _norm_matmul_kernel(x_ref, g_ref, w_ref, *rest, has_t):
    hn = _rms(x_ref[...], g_ref[...]).astype(BF16)
    if has_t:
        wt_ref, o_ref, ot_ref = rest
        ot_ref[...] = _dot_nt(wt_ref[...], hn)
    else:
        (o_ref,) = rest
    o_ref[...] = _dot(hn, w_ref[...])


def _norm_matmul(x, g, w, *, tm, wt=None, name):
    m, d = x.shape
    n = w.shape[1]
    in_specs = [pl.BlockSpec((tm, d), lambda i: (i, 0)),
                pl.BlockSpec((1, d), lambda i: (0, 0)),
                pl.BlockSpec((d, n), lambda i: (0, 0))]
    out_shape = [jax.ShapeDtypeStruct((m, n), F32)]
    out_specs = [pl.BlockSpec((tm, n), lambda i: (i, 0))]
    args = [x, g, w]
    if wt is not None:
        in_specs.append(pl.BlockSpec(wt.shape, lambda i: (0, 0)))
        out_shape.append(jax.ShapeDtypeStruct((wt.shape[0], m), F32))
        out_specs.append(pl.BlockSpec((wt.shape[0], tm), lambda i: (0, i)))
        args.append(wt)
    res = pl.pallas_call(
        functools.partial(_norm_matmul_kernel, has_t=wt is not None),
        grid=(m // tm,), in_specs=in_specs, out_specs=out_specs, out_shape=out_shape,
        compiler_params=_cp("parallel"), name=name)(*args)
    return res if wt is not None else res[0]


def _matmul_norm_res_kernel(*refs, widths):
    n_a = len(widths)
    a_refs = refs[:n_a]
    w_ref, g_ref, r_ref, o_ref = refs[n_a:]
    y = None
    off = 0
    for a_ref, k in zip(a_refs, widths):
        part = _dot(a_ref[...].astype(BF16), w_ref[off:off + k, :])
        y = part if y is None else y + part
        off += k
    o_ref[...] = r_ref[...] + _rms(y, g_ref[...])


def _matmul_norm_res(a_list, w, g, resid, *, tm, name):
    m, d = resid.shape
    widths = tuple(a.shape[1] for a in a_list)
    in_specs = [pl.BlockSpec((tm, k), lambda i: (i, 0)) for k in widths]
    in_specs += [pl.BlockSpec(w.shape, lambda i: (0, 0)),
                 pl.BlockSpec((1, d), lambda i: (0, 0)),
                 pl.BlockSpec((tm, d), lambda i: (i, 0))]
    return pl.pallas_call(
        functools.partial(_matmul_norm_res_kernel, widths=widths),
        grid=(m // tm,), in_specs=in_specs,
        out_specs=pl.BlockSpec((tm, d), lambda i: (i, 0)),
        out_shape=jax.ShapeDtypeStruct((m, d), F32),
        compiler_params=_cp("parallel"), name=name)(*a_list, w, g, resid)


def _lru_gates(xc, wa_ref, ba_ref, wi_ref, bi_ref, lam_ref):
    xb = xc.astype(BF16)
    ra, ia = [], []
    for c in range(LRU_W // LANES):
        xs = xb[:, c * LANES:(c + 1) * LANES]
        ra.append(_dot(xs, wa_ref[c]))
        ia.append(_dot(xs, wi_ref[c]))
    r = _sigmoid(jnp.concatenate(ra, axis=1) + ba_ref[...])
    ig = _sigmoid(jnp.concatenate(ia, axis=1) + bi_ref[...])
    log_a = -LRU_C * r * _softplus(-lam_ref[...])
    a = jnp.exp(log_a)
    bx = jnp.sqrt(-jnp.tanh(log_a) * (a * a + 1.0)) * (ig * xc)
    return a, bx


def _lru_prompt_kernel(xl_ref, gate_ref, cw_ref, cb_ref, wa_ref, ba_ref, wi_ref, bi_ref, lam_ref,
                       out_ref, hlast_ref, ext_ref, hc_ref, *, tc, n_t):
    t = pl.program_id(1)

    @pl.when(t == 0)
    def _():
        ext_ref[0:8, :] = jnp.zeros((8, LRU_W), F32)
        hc_ref[...] = jnp.zeros((1, LRU_W), F32)

    xl = xl_ref[...]
    ext_ref[8:8 + tc, :] = xl
    xc = cb_ref[...] + cw_ref[LRU_CONV - 1:LRU_CONV, :] * xl
    for j in range(1, LRU_CONV):
        xc = xc + cw_ref[LRU_CONV - 1 - j:LRU_CONV - j, :] * ext_ref[pl.ds(8 - j, tc), :]
    ext_ref[0:8, :] = ext_ref[tc:tc + 8, :]

    a, b = _lru_gates(xc, wa_ref, ba_ref, wi_ref, bi_ref, lam_ref)
    row = lax.broadcasted_iota(jnp.int32, (tc, 1), 0)
    d = 1
    while d < tc:
        keep = row >= d
        a_sh = jnp.where(keep, pltpu.roll(a, d, 0), 1.0)
        b_sh = jnp.where(keep, pltpu.roll(b, d, 0), 0.0)
        b = a * b_sh + b
        a = a * a_sh
        d *= 2
    h = a * hc_ref[...] + b
    hc_ref[...] = h[tc - 1:tc, :]
    out_ref[...] = (_gelu_tanh(gate_ref[...]) * h).astype(BF16)

    @pl.when(t == n_t - 1)
    def _():
        hlast_ref[...] = h[tc - 1:tc, :]


def _lru_prompt(z, n_b, seq, cw, cb, wa, ba, wi, bi, lam):
    tc = SCAN_TILE
    n_t = seq // tc
    vec = lambda: pl.BlockSpec((1, LRU_W), lambda b, t: (0, 0))
    bd = lambda: pl.BlockSpec((LRU_W // LANES, LANES, LANES), lambda b, t: (0, 0, 0))
    return pl.pallas_call(
        functools.partial(_lru_prompt_kernel, tc=tc, n_t=n_t),
        grid=(n_b, n_t),
        in_specs=[pl.BlockSpec((tc, LRU_W), lambda b, t: (b * n_t + t, 0)),
                  pl.BlockSpec((tc, LRU_W), lambda b, t: (b * n_t + t, 1)),
                  pl.BlockSpec((LRU_CONV, LRU_W), lambda b, t: (0, 0)),
                  vec(), bd(), vec(), bd(), vec(), vec()],
        out_specs=[pl.BlockSpec((tc, LRU_W), lambda b, t: (b * n_t + t, 0)),
                   pl.BlockSpec((None, 1, LRU_W), lambda b, t: (b, 0, 0))],
        out_shape=[jax.ShapeDtypeStruct((n_b * seq, LRU_W), BF16),
                   jax.ShapeDtypeStruct((n_b, 1, LRU_W), F32)],
        scratch_shapes=[pltpu.VMEM((tc + 8, LRU_W), F32), pltpu.VMEM((1, LRU_W), F32)],
        compiler_params=_cp("parallel", "arbitrary"), name="lru_prompt")(
            z, z, cw, cb, wa, ba, wi, bi, lam)


def _lru_sample_kernel(z_ref, pre_ref, h0_ref, cw_ref, cb_ref, wa_ref, ba_ref, wi_ref, bi_ref, lam_ref,
                       out_ref, h_ref):
    xl = z_ref[:, 0:LRU_W]
    gate = z_ref[:, LRU_W:2 * LRU_W]
    xc = cb_ref[...] + cw_ref[LRU_CONV - 1:LRU_CONV, :] * xl
    for k in range(LRU_CONV - 1):
        xc = xc + cw_ref[k:k + 1, :] * pre_ref[k]
    a, bx = _lru_gates(xc, wa_ref, ba_ref, wi_ref, bi_ref, lam_ref)
    h = a * h0_ref[...] + bx
    h_ref[...] = h
    out_ref[...] = (_gelu_tanh(gate) * h).astype(BF16)


def _lru_sample(z, prefix_t, h0, cw, cb, wa, ba, wi, bi, lam):
    n_b = z.shape[0]
    return pl.pallas_call(
        _lru_sample_kernel,
        out_shape=[jax.ShapeDtypeStruct((n_b, LRU_W), BF16), jax.ShapeDtypeStruct((n_b, LRU_W), F32)],
        compiler_params=pltpu.CompilerParams(vmem_limit_bytes=VMEM_LIMIT),
        name="lru_sample")(z, prefix_t, h0, cw, cb, wa, ba, wi, bi, lam)


def _log_sigmoid(x):
    return jnp.minimum(x, 0.0) - jnp.log1p(jnp.exp(-jnp.abs(x)))


def _lane_cumsum(x):
    n = x.shape[1]
    lane = lax.broadcasted_iota(jnp.int32, x.shape, 1)
    d = 1
    while d < n:
        x = x + jnp.where(lane >= d, pltpu.roll(x, d, 1), 0.0)
        d *= 2
    return x


def _fox_prep_kernel(fl_ref, bf_ref, lf_ref, c_ref):
    lf = _log_sigmoid(fl_ref[0:FOX_H, :] + bf_ref[...])
    lf_ref[...] = lf
    c_ref[...] = _lane_cumsum(lf)


def _fox_prep(fl_t, bf, n_b, seq):
    return pl.pallas_call(
        _fox_prep_kernel, grid=(n_b,),
        in_specs=[pl.BlockSpec((fl_t.shape[0], seq), lambda b: (0, b)),
                  pl.BlockSpec((FOX_H, 1), lambda b: (0, 0))],
        out_specs=[pl.BlockSpec((FOX_H, seq), lambda b: (0, b)),
                   pl.BlockSpec((FOX_H, seq), lambda b: (0, b))],
        out_shape=[jax.ShapeDtypeStruct((FOX_H, n_b * seq), F32)] * 2,
        compiler_params=_cp("parallel"), name="fox_prep")(fl_t, bf)


def _logsig_kernel(fl_ref, bf_ref, lf_ref):
    lf_ref[...] = _log_sigmoid(fl_ref[0:FOX_H, :] + bf_ref[...])


def _logsig(fl_t, bf):
    return pl.pallas_call(
        _logsig_kernel, out_shape=jax.ShapeDtypeStruct((FOX_H, fl_t.shape[1]), F32),
        name="fox_logf_sample")(fl_t, bf)


def _fox_prompt_kernel(qi_ref, ki_ref, q_ref, k_ref, v_ref, cq_ref, ck_ref, o_ref,
                       m_sc, l_sc, acc_sc, s_sc, p_sc, al_sc, *, tb, hg):
    grp = pl.program_id(1)
    t = pl.program_id(2)
    qi = qi_ref[t]
    ki = ki_ref[t]
    width = hg * FOX_HD

    @pl.when(ki == 0)
    def _():
        m_sc[...] = jnp.full(m_sc.shape, NEG, F32)
        l_sc[...] = jnp.zeros(l_sc.shape, F32)
        acc_sc[...] = jnp.zeros(acc_sc.shape, F32)

    def step(diagonal):
        q = q_ref[...] * (FOX_HD ** -0.5)
        k = k_ref[...].astype(BF16)
        v = v_ref[...].astype(BF16)
        lane_head = jnp.right_shift(lax.broadcasted_iota(jnp.int32, (tb, width), 1),
                                    FOX_HD.bit_length() - 1)
        rc = ATT_ROWS
        if diagonal:
            row_i = lax.broadcasted_iota(jnp.int32, (rc, tb), 0)
            col_i = lax.broadcasted_iota(jnp.int32, (rc, tb), 1)
        n_lt = tb // LANES
        for h in range(hg):
            s_sc[h] = _dot_nt(jnp.where(lane_head == h, q, 0.0).astype(BF16), k)
            c_q = cq_ref[pl.ds(hg * grp + h, 1), :]
            c_k = ck_ref[pl.ds(hg * grp + h, 1), :]
            bias = c_q[:, 0:1] - c_k
            for c in range(tb // rc):
                rows = slice(c * rc, (c + 1) * rc)
                s = s_sc[h, rows, :] + bias
                if diagonal:
                    s = jnp.where(col_i <= row_i + c * rc, s, NEG)
                s_sc[h, rows, :] = s
                m_prev = m_sc[h, rows, :]
                m_new = jnp.maximum(m_prev, jnp.max(s, axis=1, keepdims=True))
                m_sc[h, rows, :] = m_new
                al_sc[h, rows, :] = jnp.exp(m_prev - m_new)
            for c in range(tb // rc):
                rows = slice(c * rc, (c + 1) * rc)
                m_new = m_sc[h, rows, :]
                p_sum = None
                for j in range(n_lt):
                    cols = slice(j * LANES, (j + 1) * LANES)
                    p = jnp.exp(s_sc[h, rows, cols] - m_new)
                    p_sc[h, rows, cols] = p.astype(BF16)
                    p_sum = p if p_sum is None else p_sum + p
                l_sc[h, rows, :] = al_sc[h, rows, :] * l_sc[h, rows, :] + p_sum
            lt = (h * FOX_HD) // LANES
            pv = _dot(p_sc[h], v)[:, lt * LANES:(lt + 1) * LANES]
            acc_sc[h] = al_sc[h] * acc_sc[h] + pv

    @pl.when(ki < qi)
    def _():
        step(False)

    @pl.when(ki == qi)
    def _():
        step(True)
        lane = lax.broadcasted_iota(jnp.int32, (tb, LANES), 1)
        heads_per_tile = LANES // FOX_HD
        for lt in range(width // LANES):
            o = None
            for i in range(heads_per_tile):
                h = lt * heads_per_tile + i
                o_h = acc_sc[h] / jnp.sum(l_sc[h], axis=1, keepdims=True)
                o = o_h if o is None else jnp.where(lane < i * FOX_HD, o, o_h)
            o_ref[:, lt * LANES:(lt + 1) * LANES] = o.astype(BF16)


def _fox_prompt(z, c_t, n_b, seq):
    tb = SEQ_TILE
    hg = ATT_HEADS
    width = hg * FOX_HD
    n_q = seq // tb
    q_blk = (2 * LRU_W) // width
    k_blk = q_blk + FOX_W // width
    v_blk = k_blk + FOX_W // width
    pairs = [(qi, ki) for qi in range(n_q) for ki in range(qi + 1)]
    qi_list = jnp.asarray([p[0] for p in pairs], jnp.int32)
    ki_list = jnp.asarray([p[1] for p in pairs], jnp.int32)
    grid_spec = pltpu.PrefetchScalarGridSpec(
        num_scalar_prefetch=2, grid=(n_b, FOX_H // hg, len(pairs)),
        in_specs=[
            pl.BlockSpec((tb, width), lambda b, g, t, qi, ki: (b * n_q + qi[t], q_blk + g)),
            pl.BlockSpec((tb, width), lambda b, g, t, qi, ki: (b * n_q + ki[t], k_blk + g)),
            pl.BlockSpec((tb, width), lambda b, g, t, qi, ki: (b * n_q + ki[t], v_blk + g)),
            pl.BlockSpec((FOX_H, tb), lambda b, g, t, qi, ki: (0, b * n_q + qi[t])),
            pl.BlockSpec((FOX_H, tb), lambda b, g, t, qi, ki: (0, b * n_q + ki[t])),
        ],
        out_specs=pl.BlockSpec((tb, width), lambda b, g, t, qi, ki: (b * n_q + qi[t], g)),
        scratch_shapes=[pltpu.VMEM((hg, tb, LANES), F32), pltpu.VMEM((hg, tb, LANES), F32),
                        pltpu.VMEM((hg, tb, LANES), F32), pltpu.VMEM((hg, tb, tb), F32),
                        pltpu.VMEM((hg, tb, tb), BF16), pltpu.VMEM((hg, tb, LANES), F32)])
    return pl.pallas_call(
        functools.partial(_fox_prompt_kernel, tb=tb, hg=hg), grid_spec=grid_spec,
        out_shape=jax.ShapeDtypeStruct((n_b * seq, FOX_W), BF16),
        compiler_params=_cp("parallel", "parallel", "arbitrary"),
        name="fox_prompt")(qi_list, ki_list, z, z, z, c_t, c_t)


def _fox_sample_kernel(pt_ref, q_ref, kn_ref, vn_ref, lfn_ref, *rest, n_pg, n_g):
    k_refs = rest[0:n_pg]
    v_refs = rest[n_pg:2 * n_pg]
    lf_refs = rest[2 * n_pg:3 * n_pg]
    o_ref, m_sc, l_sc, acc_sc, cc_sc = rest[3 * n_pg:]
    g = pl.program_id(1)
    page = lf_refs[0].shape[1]

    @pl.when(g == 0)
    def _():
        m_sc[...] = jnp.full(m_sc.shape, NEG, F32)
        l_sc[...] = jnp.zeros(l_sc.shape, F32)
        acc_sc[...] = jnp.zeros(acc_sc.shape, F32)
        cc_sc[...] = jnp.zeros(cc_sc.shape, F32)


    q = q_ref[...] * (FOX_HD ** -0.5)
    q16 = jnp.concatenate([q, jnp.zeros_like(q)], axis=0)
    row_q = lax.broadcasted_iota(jnp.int32, (2 * FOX_H, FOX_HD), 0)
    q_only = [jnp.where(row_q == h, q16, 0.0).astype(BF16) for h in range(FOX_H)]

    s_parts = []
    for j in range(n_pg):
        s_j = None
        for h in range(FOX_H):
            part = _dot(q_only[h], k_refs[j][h].astype(BF16))
            s_j = part if s_j is None else s_j + part
        s_parts.append(s_j[0:FOX_H])
    s = jnp.concatenate(s_parts, axis=1)
    lf = jnp.concatenate([lf_refs[j][...] for j in range(n_pg)], axis=1)
    c = _lane_cumsum(lf) + cc_sc[...]
    cc_sc[...] = c[:, c.shape[1] - 1:]
    s = s - c
    m_prev = m_sc[...]
    m_new = jnp.maximum(m_prev, jnp.max(s, axis=1, keepdims=True))
    alpha = jnp.exp(m_prev - m_new)
    p = jnp.exp(s - m_new)
    l_sc[...] = alpha * l_sc[...] + jnp.sum(p, axis=1, keepdims=True)
    p16 = jnp.concatenate([p, jnp.zeros_like(p)], axis=0)
    row_p = lax.broadcasted_iota(jnp.int32, (2 * FOX_H, page), 0)
    pv = None
    for j in range(n_pg):
        p_j = p16[:, j * page:(j + 1) * page]
        for h in range(FOX_H):
            part = _dot_nt(jnp.where(row_p == h, p_j, 0.0).astype(BF16), v_refs[j][h].astype(BF16))
            pv = part if pv is None else pv + part
    acc_sc[...] = alpha * acc_sc[...] + pv[0:FOX_H]
    m_sc[...] = m_new

    @pl.when(g == n_g - 1)
    def _():
        s_n = jnp.sum(q * kn_ref[...], axis=1, keepdims=True)
        s_n = s_n - (cc_sc[...] + lfn_ref[...])
        m_p = m_sc[...]
        m_n = jnp.maximum(m_p, s_n)
        al = jnp.exp(m_p - m_n)
        p_n = jnp.exp(s_n - m_n)
        l_n = al * l_sc[...] + p_n
        o_ref[...] = ((al * acc_sc[...] + p_n * vn_ref[...]) / l_n).astype(BF16)


def _fox_sample(page_table, layer, q, k_new, v_new, lf_new, cache_k, cache_v, cache_lf_t):
    n_b, n_pages = page_table.shape
    n_pg = PAGES_PER_STEP
    n_g = n_pages // n_pg
    page = cache_k.shape[4]
    head_spec = lambda: pl.BlockSpec((None, FOX_H, FOX_HD), lambda b, g, pt: (b, 0, 0))

    def kv_spec(j):
        return pl.BlockSpec((None, None, FOX_H, FOX_HD, page),
                            lambda b, g, pt, j=j: (layer, pt[b * n_pages + g * n_pg + j], 0, 0, 0))

    def lf_spec(j):
        return pl.BlockSpec((None, FOX_H, page),
                            lambda b, g, pt, j=j: (pt[b * n_pages + g * n_pg + j], 0, 0))

    in_specs = [head_spec(), head_spec(), head_spec(),
                pl.BlockSpec((None, FOX_H, 1), lambda b, g, pt: (b, 0, 0))]
    in_specs += [kv_spec(j) for j in range(n_pg)]
    in_specs += [kv_spec(j) for j in range(n_pg)]
    in_specs += [lf_spec(j) for j in range(n_pg)]
    grid_spec = pltpu.PrefetchScalarGridSpec(
        num_scalar_prefetch=1, grid=(n_b, n_g), in_specs=in_specs,
        out_specs=pl.BlockSpec((None, FOX_H, FOX_HD), lambda b, g, pt: (b, 0, 0)),
        scratch_shapes=[pltpu.VMEM((FOX_H, 1), F32), pltpu.VMEM((FOX_H, 1), F32),
                        pltpu.VMEM((FOX_H, FOX_HD), F32), pltpu.VMEM((FOX_H, 1), F32)])
    return pl.pallas_call(
        functools.partial(_fox_sample_kernel, n_pg=n_pg, n_g=n_g),
        grid_spec=grid_spec,
        out_shape=jax.ShapeDtypeStruct((n_b, FOX_H, FOX_HD), BF16),
        compiler_params=_cp("parallel", "arbitrary"), name="fox_sample")(
            page_table.reshape(-1), q, k_new, v_new, lf_new,
            *([cache_k] * n_pg), *([cache_v] * n_pg), *([cache_lf_t] * n_pg))


def _xattn_prompt_kernel(q_ref, mk_ref, mv_ref, o_ref):
    outs = []
    for h in range(MEM_H):
        cs = slice(h * MEM_HD, (h + 1) * MEM_HD)
        qh = q_ref[:, cs].astype(BF16)
        s = _dot_nt(qh, mk_ref[:, cs].astype(BF16)) * (MEM_HD ** -0.5)
        p = jnp.exp(s - jnp.max(s, axis=1, keepdims=True))
        l = jnp.sum(p, axis=1, keepdims=True)
        outs.append(_dot(p.astype(BF16), mv_ref[:, cs].astype(BF16)) / l)
    o_ref[...] = jnp.concatenate(outs, axis=1).astype(BF16)


def _xattn_prompt(q, memkv, n_b, seq):
    tb = SEQ_TILE
    n_t = seq // tb
    mem_len = memkv.shape[0] // n_b
    return pl.pallas_call(
        _xattn_prompt_kernel, grid=(n_b, n_t),
        in_specs=[pl.BlockSpec((tb, MEM_W), lambda b, t: (b * n_t + t, 0)),
                  pl.BlockSpec((mem_len, MEM_W), lambda b, t: (b, 0)),
                  pl.BlockSpec((mem_len, MEM_W), lambda b, t: (b, 1))],
        out_specs=pl.BlockSpec((tb, MEM_W), lambda b, t: (b * n_t + t, 0)),
        out_shape=jax.ShapeDtypeStruct((n_b * seq, MEM_W), BF16),
        compiler_params=_cp("parallel", "parallel"), name="xattn_prompt")(q, memkv, memkv)


def _xattn_sample_kernel(q_ref, mk_ref, mv_ref, o_ref):
    q = q_ref[...]
    row = lax.broadcasted_iota(jnp.int32, (16, MEM_W), 0)
    col = lax.broadcasted_iota(jnp.int32, (16, MEM_W), 1)
    head_cols = jnp.right_shift(col, MEM_HD.bit_length() - 1) == row
    q_rows = jnp.where(head_cols, q, 0.0).astype(BF16)
    mem_len = mk_ref.shape[0] // MEM_H

    def heads_on_lanes(ref):
        return jnp.concatenate([ref[pl.ds(h, mem_len, stride=MEM_H), :] for h in range(MEM_H)],
                               axis=1).astype(BF16)

    s = _dot_nt(q_rows, heads_on_lanes(mk_ref)) * (MEM_HD ** -0.5)
    p = jnp.exp(s - jnp.max(s, axis=1, keepdims=True))
    l = jnp.sum(p, axis=1, keepdims=True)
    o = _dot(p.astype(BF16), heads_on_lanes(mv_ref)) / l
    o_ref[...] = jnp.sum(jnp.where(head_cols, o, 0.0), axis=0, keepdims=True).astype(BF16)


def _xattn_sample(q, layer, mk, mv):
    n_b, rows = mk.shape[1], mk.shape[2]
    return pl.pallas_call(
        _xattn_sample_kernel, grid=(n_b,),
        in_specs=[pl.BlockSpec((None, 1, MEM_W), lambda b: (b, 0, 0)),
                  pl.BlockSpec((None, None, rows, MEM_HD), lambda b: (layer, b, 0, 0)),
                  pl.BlockSpec((None, None, rows, MEM_HD), lambda b: (layer, b, 0, 0))],
        out_specs=pl.BlockSpec((None, 1, MEM_W), lambda b: (b, 0, 0)),
        out_shape=jax.ShapeDtypeStruct((n_b, 1, MEM_W), BF16),
        compiler_params=_cp("parallel"), name="xattn_sample")(q, mk, mv)


def _swiglu_accumulate(x_sc, w1_ref, w3_ref, w2_ref, acc_sc):
    tm = x_sc.shape[0]
    slab = min(tm, FF_ROW_SLAB)
    for r in range(tm // slab):
        rows = slice(r * slab, (r + 1) * slab)
        x = x_sc[rows, :]
        h1 = _dot(x, w1_ref[...])
        h3 = _dot(x, w3_ref[...])
        hh = (h1 * _sigmoid(h1) * h3).astype(BF16)
        acc_sc[rows, :] += _dot(hh, w2_ref[...])


def _ffn_kernel(x_ref, g4_ref, g5_ref, w1_ref, w3_ref, w2_ref, o_ref, hn_sc, acc_sc, *, n_f):
    f = pl.program_id(1)

    @pl.when(f == 0)
    def _():
        hn_sc[...] = _rms(x_ref[...], g4_ref[...]).astype(BF16)
        acc_sc[...] = jnp.zeros(acc_sc.shape, F32)

    _swiglu_accumulate(hn_sc, w1_ref, w3_ref, w2_ref, acc_sc)

    @pl.when(f == n_f - 1)
    def _():
        o_ref[...] = x_ref[...] + _rms(acc_sc[...], g5_ref[...])


def _ffn(x, g4, g5, w1, w3, w2, *, tm):
    m, d = x.shape
    ff = w1.shape[1]
    tf = FFN_FF_TILE
    n_f = ff // tf
    return pl.pallas_call(
        functools.partial(_ffn_kernel, n_f=n_f), grid=(m // tm, n_f),
        in_specs=[pl.BlockSpec((tm, d), lambda i, f: (i, 0)),
                  pl.BlockSpec((1, d), lambda i, f: (0, 0)),
                  pl.BlockSpec((1, d), lambda i, f: (0, 0)),
                  pl.BlockSpec((d, tf), lambda i, f: (0, f)),
                  pl.BlockSpec((d, tf), lambda i, f: (0, f)),
                  pl.BlockSpec((tf, d), lambda i, f: (f, 0))],
        out_specs=pl.BlockSpec((tm, d), lambda i, f: (i, 0)),
        out_shape=jax.ShapeDtypeStruct((m, d), F32),
        scratch_shapes=[pltpu.VMEM((tm, d), BF16), pltpu.VMEM((tm, d), F32)],
        compiler_params=_cp("parallel", "arbitrary"), name="ffn")(x, g4, g5, w1, w3, w2)


def _layernorm_silu(x, g, b):
    mu = jnp.mean(x, axis=-1, keepdims=True)
    xc = x - mu
    y = xc * lax.rsqrt(jnp.mean(xc * xc, axis=-1, keepdims=True) + EPS) * g + b
    return y * _sigmoid(y)


def _odd_prompt_kernel(a_ref, gt_ref, up_ref, dww_ref, dwb_ref, lng_ref, lnb_ref, pw_ref, ps_ref,
                       cv_ref, pool_ref, cbuf_ref, pbuf_ref, eg_ref, eu_ref, sh_ref, *, tc, n_t):
    t = pl.program_id(1)
    halo_g, halo_u = 32, 16

    @pl.when(t == 0)
    def _():
        eg_ref[0:halo_g, :] = jnp.zeros((halo_g, CONV_W), F32)
        eu_ref[0:halo_u, :] = jnp.zeros((halo_u, POOL_W), F32)

    glu = a_ref[...] * _sigmoid(gt_ref[...])
    up = up_ref[...]
    eg_ref[halo_g:halo_g + tc, :] = glu
    eu_ref[halo_u:halo_u + tc, :] = up

    ext = eg_ref[...]
    n_ext = tc + halo_g
    sh_ref[0] = ext
    for s in range(1, 8):
        sh_ref[s] = pltpu.roll(ext, n_ext - s, 0)
    rc = CONV_ROWS
    for c in range(tc // rc):
        parts = []
        for j in range(CONV_W // LANES):
            cols = slice(j * LANES, (j + 1) * LANES)
            acc = jnp.broadcast_to(dwb_ref[:, cols], (rc, LANES))
            for k in range(CONV_K):
                off = k + halo_g - (CONV_K - 1)
                acc = acc + dww_ref[k:k + 1, cols] * sh_ref[off % 8, pl.ds(c * rc + (off // 8) * 8, rc), cols]
            parts.append(acc)
        cv_ref[c * rc:(c + 1) * rc, :] = _layernorm_silu(
            jnp.concatenate(parts, axis=1), lng_ref[...], lnb_ref[...]).astype(BF16)

    pos = t * tc + lax.broadcasted_iota(jnp.int32, (tc, 1), 0)
    outs = []
    for gi, w in enumerate(POOL_WINDOWS):
        cs = slice(gi * POOL_GW, (gi + 1) * POOL_GW)
        u_g = up[:, cs]
        win = u_g
        for j in range(1, w):
            win = win + eu_ref[pl.ds(halo_u - j, tc), cs]
        cnt = jnp.minimum(pos + 1, w).astype(F32)
        dlt = win / cnt - u_g
        outs.append(_dot(dlt.astype(BF16), pw_ref[gi]))
    pool_ref[...] = (jnp.concatenate(outs, axis=1) * ps_ref[...]).astype(BF16)

    @pl.when(t == n_t - 1)
    def _():
        cbuf_ref[...] = eg_ref[pl.ds(halo_g + tc - (CONV_K - 1), CONV_K - 1), :]
        pbuf_ref[...] = eu_ref[pl.ds(halo_u + tc - POOL_BUF, POOL_BUF), :]

    eg_ref[0:halo_g, :] = eg_ref[tc:tc + halo_g, :]
    eu_ref[0:halo_u, :] = eu_ref[tc:tc + halo_u, :]


def _odd_prompt(z, n_b, seq, dww, dwb, lng, lnb, pw, ps):
    tc = SEQ_TILE
    n_t = seq // tc
    vec = lambda: pl.BlockSpec((1, CONV_W), lambda b, t: (0, 0))
    return pl.pallas_call(
        functools.partial(_odd_prompt_kernel, tc=tc, n_t=n_t), grid=(n_b, n_t),
        in_specs=[pl.BlockSpec((tc, CONV_W), lambda b, t: (b * n_t + t, 0)),
                  pl.BlockSpec((tc, CONV_W), lambda b, t: (b * n_t + t, 1)),
                  pl.BlockSpec((tc, POOL_W), lambda b, t: (b * n_t + t, 2)),
                  pl.BlockSpec((CONV_K, CONV_W), lambda b, t: (0, 0)),
                  vec(), vec(), vec(),
                  pl.BlockSpec((len(POOL_WINDOWS), POOL_GW, POOL_GW), lambda b, t: (0, 0, 0)),
                  vec()],
        out_specs=[pl.BlockSpec((tc, CONV_W), lambda b, t: (b * n_t + t, 0)),
                   pl.BlockSpec((tc, POOL_W), lambda b, t: (b * n_t + t, 0)),
                   pl.BlockSpec((None, CONV_K - 1, CONV_W), lambda b, t: (b, 0, 0)),
                   pl.BlockSpec((None, POOL_BUF, POOL_W), lambda b, t: (b, 0, 0))],
        out_shape=[jax.ShapeDtypeStruct((n_b * seq, CONV_W), BF16),
                   jax.ShapeDtypeStruct((n_b * seq, POOL_W), BF16),
                   jax.ShapeDtypeStruct((n_b, CONV_K - 1, CONV_W), F32),
                   jax.ShapeDtypeStruct((n_b, POOL_BUF, POOL_W), F32)],
        scratch_shapes=[pltpu.VMEM((tc + 32, CONV_W), F32), pltpu.VMEM((tc + 16, POOL_W), F32),
                        pltpu.VMEM((8, tc + 32, CONV_W), F32)],
        compiler_params=_cp("parallel", "arbitrary"), name="odd_prompt")(
            z, z, z, dww, dwb, lng, lnb, pw, ps)


def _odd_sample_kernel(z_ref, cbuf_ref, pbuf_ref, dww_ref, dwb_ref, lng_ref, lnb_ref, pw_ref, ps_ref,
                       cv_ref, pool_ref, glu_ref, *, pos0):
    glu = z_ref[:, 0:CONV_W] * _sigmoid(z_ref[:, CONV_W:2 * CONV_W])
    up = z_ref[:, 2 * CONV_W:2 * CONV_W + POOL_W]
    glu_ref[...] = glu
    acc = dwb_ref[...] + dww_ref[CONV_K - 1:CONV_K, :] * glu
    for k in range(CONV_K - 1):
        acc = acc + dww_ref[k:k + 1, :] * cbuf_ref[k]
    cv_ref[...] = _layernorm_silu(acc, lng_ref[...], lnb_ref[...]).astype(BF16)
    outs = []
    for gi, w in enumerate(POOL_WINDOWS):
        cs = slice(gi * POOL_GW, (gi + 1) * POOL_GW)
        u_g = up[:, cs]
        win = u_g
        for j in range(1, w):
            win = win + pbuf_ref[POOL_BUF - j][:, cs]
        dlt = win / float(min(pos0 + 1, w)) - u_g
        outs.append(_dot(dlt.astype(BF16), pw_ref[gi]))
    pool_ref[...] = (jnp.concatenate(outs, axis=1) * ps_ref[...]).astype(BF16)


def _odd_sample(z, cbuf_t, pbuf_t, dww, dwb, lng, lnb, pw, ps, pos0):
    n_b = z.shape[0]
    return pl.pallas_call(
        functools.partial(_odd_sample_kernel, pos0=pos0),
        out_shape=[jax.ShapeDtypeStruct((n_b, CONV_W), BF16),
                   jax.ShapeDtypeStruct((n_b, POOL_W), BF16),
                   jax.ShapeDtypeStruct((n_b, CONV_W), F32)],
        compiler_params=pltpu.CompilerParams(vmem_limit_bytes=VMEM_LIMIT),
        name="odd_sample")(z, cbuf_t, pbuf_t, dww, dwb, lng, lnb, pw, ps)


def _router_kernel(x_ref, g_ref, whi_ref, wlo_ref, b_ref, hn_ref, idx_ref, gate_ref):
    hn = _rms(x_ref[...], g_ref[...])
    hb = hn.astype(BF16)
    hn_ref[...] = hn
    hlo = (hn - hb.astype(F32)).astype(BF16)
    logits = _dot(hb, whi_ref[...]) + (_dot(hb, wlo_ref[...]) + _dot(hlo, whi_ref[...])) + b_ref[...]
    lane = lax.broadcasted_iota(jnp.int32, logits.shape, 1)
    lane_f = lane.astype(F32)
    logits = jnp.where(lane < N_EXPERTS, logits, NEG)
    m1 = jnp.max(logits, axis=1, keepdims=True)
    i1 = jnp.min(jnp.where(logits == m1, lane_f, float(LANES)), axis=1, keepdims=True)
    rest = jnp.where(lane_f == i1, NEG, logits)
    m2 = jnp.max(rest, axis=1, keepdims=True)
    i2 = jnp.min(jnp.where(rest == m2, lane_f, float(LANES)), axis=1, keepdims=True)
    e = jnp.exp(m2 - m1)
    g1 = 1.0 / (1.0 + e)
    g2 = e / (1.0 + e)
    idx_ref[...] = jnp.where(lane == 0, i1, jnp.where(lane == 1, i2, 0.0)).astype(jnp.int32)
    gate_ref[...] = jnp.where(lane == 0, g1, jnp.where(lane == 1, g2, 0.0))


def _router(x, g, whi, wlo, b, *, tm):
    m, d = x.shape
    return pl.pallas_call(
        _router_kernel, grid=(m // tm,),
        in_specs=[pl.BlockSpec((tm, d), lambda i: (i, 0)),
                  pl.BlockSpec((1, d), lambda i: (0, 0)),
                  pl.BlockSpec((d, LANES), lambda i: (0, 0)),
                  pl.BlockSpec((d, LANES), lambda i: (0, 0)),
                  pl.BlockSpec((1, LANES), lambda i: (0, 0))],
        out_specs=[pl.BlockSpec((tm, d), lambda i: (i, 0)),
                   pl.BlockSpec((tm, LANES), lambda i: (i, 0)),
                   pl.BlockSpec((tm, LANES), lambda i: (i, 0))],
        out_shape=[jax.ShapeDtypeStruct((m, d), F32),
                   jax.ShapeDtypeStruct((m, LANES), jnp.int32),
                   jax.ShapeDtypeStruct((m, LANES), F32)],
        compiler_params=_cp("parallel"), name="router")(x, g, whi, wlo, b)


def _gmm_kernel(te_ref, tv_ref, x_ref, gate_ref, w1_ref, w3_ref, w2_ref, o_ref, acc_sc, xb_sc, *, n_f):
    t = pl.program_id(0)
    f = pl.program_id(1)

    @pl.when(f == 0)
    def _():
        acc_sc[...] = jnp.zeros(acc_sc.shape, F32)
        xb_sc[...] = x_ref[...].astype(BF16)

    @pl.when(tv_ref[t] != 0)
    def _():
        _swiglu_accumulate(xb_sc, w1_ref, w3_ref, w2_ref, acc_sc)

    @pl.when(f == n_f - 1)
    def _():
        o_ref[...] = acc_sc[...] * gate_ref[...]


def _gmm(tile_expert, tile_valid, x_sorted, gate_sorted, w1, w3, w2):
    rows = x_sorted.shape[0]
    d, ff = w1.shape[1], w1.shape[2]
    tm, tf = MOE_TILE, MOE_FF_TILE
    n_f = ff // tf
    grid_spec = pltpu.PrefetchScalarGridSpec(
        num_scalar_prefetch=2, grid=(rows // tm, n_f),
        in_specs=[pl.BlockSpec((tm, d), lambda t, f, te, tv: (t, 0)),
                  pl.BlockSpec((tm, 1), lambda t, f, te, tv: (t, 0)),
                  pl.BlockSpec((None, d, tf), lambda t, f, te, tv: (te[t], 0, f)),
                  pl.BlockSpec((None, d, tf), lambda t, f, te, tv: (te[t], 0, f)),
                  pl.BlockSpec((None, tf, d), lambda t, f, te, tv: (te[t], f, 0))],
        out_specs=pl.BlockSpec((tm, d), lambda t, f, te, tv: (t, 0)),
        scratch_shapes=[pltpu.VMEM((tm, d), F32), pltpu.VMEM((tm, d), BF16)])
    return pl.pallas_call(
        functools.partial(_gmm_kernel, n_f=n_f), grid_spec=grid_spec,
        out_shape=jax.ShapeDtypeStruct((rows, d), F32),
        compiler_params=_cp("parallel", "arbitrary"), name="moe_experts")(
            tile_expert, tile_valid, x_sorted, gate_sorted, w1, w3, w2)


def _combine_kernel(x_ref, y1_ref, y2_ref, g_ref, o_ref):
    o_ref[...] = x_ref[...] + _rms(y1_ref[...] + y2_ref[...], g_ref[...])


def _combine(x, y, blk1, blk2, g, *, tm):
    m, d = x.shape
    row = lambda: pl.BlockSpec((tm, d), lambda i: (i, 0))
    return pl.pallas_call(
        _combine_kernel, grid=(m // tm,),
        in_specs=[row(), pl.BlockSpec((tm, d), lambda i: (blk1 + i, 0)),
                  pl.BlockSpec((tm, d), lambda i: (blk2 + i, 0)), pl.BlockSpec((1, d), lambda i: (0, 0))],
        out_specs=row(), out_shape=jax.ShapeDtypeStruct((m, d), F32),
        compiler_params=_cp("parallel"), name="moe_combine")(x, y, y, g)


def _row_copy_loop(src_row, src_ref, out_ref, sem, rows):
    blk, depth = DMA_BLOCK_ROWS, DMA_BLOCKS_IN_FLIGHT
    n_blk = rows // blk

    def row_copy(src, dst):
        return pltpu.make_async_copy(src_ref.at[pl.ds(src, 1)], out_ref.at[pl.ds(dst, 1)], sem)

    def wait_block():
        for _ in range(blk):
            row_copy(0, 0).wait()

    def body(b, carry):
        base = b * blk
        for i in range(blk):
            row_copy(src_row(base + i), base + i).start()
        pl.when(b >= depth)(wait_block)
        return carry

    lax.fori_loop(0, n_blk, body, 0)
    lax.fori_loop(0, min(depth, n_blk), lambda b, carry: (wait_block(), carry)[1], 0)


def _gather_rows_kernel(idx_ref, src_ref, out_ref, sem, *, rows):
    _row_copy_loop(lambda r: idx_ref[r], src_ref, out_ref, sem, rows)


def _gather_rows(src, idx):
    rows = idx.shape[0]
    assert rows % DMA_BLOCK_ROWS == 0
    grid_spec = pltpu.PrefetchScalarGridSpec(
        num_scalar_prefetch=1, grid=(1,),
        in_specs=[pl.BlockSpec(memory_space=pl.ANY)],
        out_specs=pl.BlockSpec(memory_space=pl.ANY),
        scratch_shapes=[pltpu.SemaphoreType.DMA])
    return pl.pallas_call(
        functools.partial(_gather_rows_kernel, rows=rows), grid_spec=grid_spec,
        out_shape=jax.ShapeDtypeStruct((rows,) + src.shape[1:], src.dtype),
        compiler_params=_cp("arbitrary"), name="gather_rows")(idx, src)


def _dispatch_kernel(dest_ref, src_ref, out_ref, inv_ref, sem, *, n_assign, rows):
    def zero(r, carry):
        inv_ref[r] = 0
        return carry

    def invert(a, carry):
        inv_ref[dest_ref[a]] = a
        return carry

    lax.fori_loop(0, rows, zero, 0, unroll=8)
    lax.fori_loop(0, n_assign, invert, 0, unroll=8)
    _row_copy_loop(lambda r: lax.shift_right_logical(inv_ref[r], 1), src_ref, out_ref, sem, rows)


def _dispatch(src, dest, rows):
    n_assign = dest.shape[0]
    assert rows % DMA_BLOCK_ROWS == 0
    grid_spec = pltpu.PrefetchScalarGridSpec(
        num_scalar_prefetch=1, grid=(1,),
        in_specs=[pl.BlockSpec(memory_space=pl.ANY)],
        out_specs=[pl.BlockSpec(memory_space=pl.ANY), pl.BlockSpec(memory_space=pltpu.SMEM)],
        scratch_shapes=[pltpu.SemaphoreType.DMA])
    return pl.pallas_call(
        functools.partial(_dispatch_kernel, n_assign=n_assign, rows=rows), grid_spec=grid_spec,
        out_shape=[jax.ShapeDtypeStruct((rows,) + src.shape[1:], src.dtype),
                   jax.ShapeDtypeStruct((rows,), jnp.int32)],
        compiler_params=_cp("arbitrary"), name="moe_dispatch")(dest, src)


def _moe_block(xs, g4, g5, w_r, b_r, w1, w3, w2):
    pad = LANES - N_EXPERTS
    w_r_p = jnp.pad(w_r, ((0, 0), (0, pad)))
    whi = w_r_p.astype(BF16)
    wlo = (w_r_p - whi.astype(F32)).astype(BF16)
    b_p = jnp.pad(b_r, (0, pad)).reshape(1, LANES)
    hn, idx, gate = [], [], []
    for x in xs:
        h, i, gt = _router(x, g4, whi, wlo, b_p, tm=min(ROW_TILE, x.shape[0]))
        hn.append(h)
        idx.append(i[:, :2])
        gate.append(gt[:, :2])
    hn = jnp.concatenate(hn, axis=0)
    e_flat = jnp.concatenate(idx, axis=0).reshape(-1)
    g_flat = jnp.concatenate(gate, axis=0).reshape(-1)
    n_assign = e_flat.shape[0]
    tm = MOE_TILE
    n_tiles = n_assign // tm + N_EXPERTS
    rows = n_tiles * tm

    onehot = (e_flat[:, None] == jnp.arange(N_EXPERTS, dtype=jnp.int32)[None, :]).astype(jnp.int32)
    rank = jnp.sum((jnp.cumsum(onehot, axis=0) - onehot) * onehot, axis=1)
    cnt = jnp.sum(onehot, axis=0)
    tiles_e = (cnt + tm - 1) // tm
    tile_end = jnp.cumsum(tiles_e)
    row_start = (tile_end - tiles_e) * tm
    dest = row_start[e_flat] + rank
    tile_ids = jnp.arange(n_tiles, dtype=jnp.int32)
    tile_expert = jnp.minimum(jnp.sum((tile_end[None, :] <= tile_ids[:, None]).astype(jnp.int32), axis=1),
                              N_EXPERTS - 1)
    tile_valid = (tile_ids < tile_end[-1]).astype(jnp.int32)
    x_sorted, row_assign = _dispatch(hn, dest, rows)
    row_ids = jnp.arange(rows, dtype=jnp.int32)
    row_expert = jnp.repeat(tile_expert, tm)
    row_valid = (row_ids - row_start[row_expert]) < cnt[row_expert]
    gate_sorted = jnp.where(row_valid, g_flat.at[row_assign].get(mode="promise_in_bounds"),
                            0.0).reshape(rows, 1)
    y_sorted = _gmm(tile_expert, tile_valid, x_sorted, gate_sorted, w1, w3, w2)

    dest2 = dest.reshape(-1, 2)
    groups, starts, off = [], [], 0
    for x in xs:
        m = x.shape[0]
        starts.append(sum(g.shape[0] for g in groups))
        groups += [dest2[off:off + m, 0], dest2[off:off + m, 1]]
        off += m
    y_tok = _gather_rows(y_sorted, jnp.concatenate(groups))
    outs = []
    for x, start in zip(xs, starts):
        m = x.shape[0]
        tm_x = min(ROW_TILE, m)
        assert start % tm_x == 0 and m % tm_x == 0
        outs.append(_combine(x, y_tok, start // tm_x, (start + m) // tm_x, g5, tm=tm_x))
    return outs


def _block_diag_pairs(w):
    nb, bw, _ = w.shape
    w = w.reshape(nb // 2, 2, bw, bw)
    z = jnp.zeros((nb // 2, bw, bw), w.dtype)
    top = jnp.concatenate([w[:, 0], z], axis=2)
    bot = jnp.concatenate([z, w[:, 1]], axis=2)
    return jnp.concatenate([top, bot], axis=1).astype(BF16)


def kernel(x_prompt, x_sample, cache_fox_k, cache_fox_v, cache_fox_logf, state_lru_h, state_lru_conv, state_conv_buf, state_pool_buf, cache_mem_k, cache_mem_v, page_table, mem_prompt, norm_g, w_xq, w_xk, w_xv, w_xo, w_in_e, b_f, lru_conv_w, lru_conv_b, lru_wa, lru_ba, lru_wi, lru_bi, lru_lam, w_out_e, w_ff1, w_ff3, w_ff2, w_in_o, cc_dw_w, cc_dw_b, cc_ln_g, cc_ln_b, pool_w, pool_scale, w_out_o, w_router, b_router, w_e1, w_e3, w_e2):
    bp, seq, d = x_prompt.shape
    bs = x_sample.shape[0]
    depth = norm_g.shape[0]
    page = cache_fox_k.shape[2]
    past_len = page_table.shape[1] * page
    mem_len = mem_prompt.shape[1]
    tm_p = ROW_TILE

    xp = x_prompt.reshape(bp * seq, d)
    xs = x_sample.reshape(bs, d)
    mem = mem_prompt.reshape(bp * mem_len, d)
    vec = lambda v: v.reshape(1, -1)

    fk_p, fv_p, fl_p, lh_p, lc_p, cb_p, pb_p, mk_pl, mv_pl = [], [], [], [], [], [], [], [], []
    fk_s, fv_s, fl_s, lh_s, lc_s, cb_s, pb_s = [], [], [], [], [], [], []

    for l in range(depth):
        g = [vec(norm_g[l, i]) for i in range(norm_g.shape[1])]
        w_kv = jnp.concatenate([w_xk[l], w_xv[l]], axis=1).astype(BF16)
        memkv = _norm_matmul(mem, g[6], w_kv, tm=tm_p, name="mem_kv")
        mk_pl.append(memkv[:, :MEM_W].reshape(bp, mem_len, MEM_H, MEM_HD))
        mv_pl.append(memkv[:, MEM_W:].reshape(bp, mem_len, MEM_H, MEM_HD))

        if l % 2 == 0:
            i = l // 2
            n_main = 2 * LRU_W + 3 * FOX_W
            w_main = w_in_e[i][:, :n_main].astype(BF16)
            w_fl_t = jnp.pad(w_in_e[i][:, n_main:].T, ((0, 16 - FOX_H), (0, 0))).astype(BF16)
            cw, cb = lru_conv_w[i], vec(lru_conv_b[i])
            wa, wi = _block_diag_pairs(lru_wa[i]), _block_diag_pairs(lru_wi[i])
            ba, bi, lam = vec(lru_ba[i]), vec(lru_bi[i]), vec(lru_lam[i])
            bf = b_f[i].reshape(FOX_H, 1)
            w_out = w_out_e[i].astype(BF16)

            z, fl_t = _norm_matmul(xp, g[0], w_main, tm=tm_p, wt=w_fl_t, name="in_proj_even")
            lf_t, c_t = _fox_prep(fl_t, bf, bp, seq)
            lru_out, h_last = _lru_prompt(z, bp, seq, cw, cb, wa, ba, wi, bi, lam)
            att = _fox_prompt(z, c_t, bp, seq)
            xp = _matmul_norm_res([lru_out, att], w_out, g[1], xp, tm=tm_p, name="out_proj_even")
            z3 = z.reshape(bp, seq, n_main)
            fk_p.append(z3[:, :, 2 * LRU_W + FOX_W:2 * LRU_W + 2 * FOX_W].reshape(bp, seq, FOX_H, FOX_HD))
            fv_p.append(z3[:, :, 2 * LRU_W + 2 * FOX_W:].reshape(bp, seq, FOX_H, FOX_HD))
            fl_p.append(lf_t.T.reshape(bp, seq, FOX_H))
            lh_p.append(h_last.reshape(bp, LRU_W))
            lc_p.append(z3[:, seq - (LRU_CONV - 1):, :LRU_W])

            zs, fls_t = _norm_matmul(xs, g[0], w_main, tm=bs, wt=w_fl_t, name="in_proj_even_s")
            lfs_t = _logsig(fls_t, bf)
            pre_t = jnp.swapaxes(state_lru_conv[i], 0, 1)
            lru_out_s, h_s = _lru_sample(zs, pre_t, state_lru_h[i], cw, cb, wa, ba, wi, bi, lam)
            q_s = zs[:, 2 * LRU_W:2 * LRU_W + FOX_W]
            k_s = zs[:, 2 * LRU_W + FOX_W:2 * LRU_W + 2 * FOX_W]
            v_s = zs[:, 2 * LRU_W + 2 * FOX_W:]
            att_s = _fox_sample(page_table, i, q_s.reshape(bs, FOX_H, FOX_HD), k_s.reshape(bs, FOX_H, FOX_HD),
                                v_s.reshape(bs, FOX_H, FOX_HD), lfs_t.T.reshape(bs, FOX_H, 1),
                                jnp.transpose(cache_fox_k, (0, 1, 3, 4, 2)),
                                jnp.transpose(cache_fox_v, (0, 1, 3, 4, 2)),
                                jnp.swapaxes(cache_fox_logf[i], 1, 2))
            xs = _matmul_norm_res([lru_out_s, att_s.reshape(bs, FOX_W)], w_out, g[1], xs, tm=bs,
                                  name="out_proj_even_s")
            fk_s.append(k_s.reshape(bs, 1, FOX_H, FOX_HD))
            fv_s.append(v_s.reshape(bs, 1, FOX_H, FOX_HD))
            fl_s.append(lfs_t.T.reshape(bs, 1, FOX_H))
            lh_s.append(h_s)
            lc_s.append(jnp.concatenate([state_lru_conv[i][:, 1:], zs[:, None, :LRU_W]], axis=1))
        else:
            j = l // 2
            w_in = w_in_o[j].astype(BF16)
            dww, dwb = cc_dw_w[j], vec(cc_dw_b[j])
            lng, lnb = vec(cc_ln_g[j]), vec(cc_ln_b[j])
            pw, ps = pool_w[j].astype(BF16), vec(pool_scale[j])
            w_out = w_out_o[j].astype(BF16)

            z = _norm_matmul(xp, g[0], w_in, tm=tm_p, name="in_proj_odd")
            cv, pool, cbuf, pbuf = _odd_prompt(z, bp, seq, dww, dwb, lng, lnb, pw, ps)
            xp = _matmul_norm_res([cv, pool], w_out, g[1], xp, tm=tm_p, name="out_proj_odd")
            cb_p.append(cbuf)
            pb_p.append(pbuf)

            zs = _norm_matmul(xs, g[0], w_in, tm=bs, name="in_proj_odd_s")
            cv_s, pool_s, glu_s = _odd_sample(zs, jnp.swapaxes(state_conv_buf[j], 0, 1),
                                              jnp.swapaxes(state_pool_buf[j], 0, 1),
                                              dww, dwb, lng, lnb, pw, ps, past_len)
            xs = _matmul_norm_res([cv_s, pool_s], w_out, g[1], xs, tm=bs, name="out_proj_odd_s")
            cb_s.append(jnp.concatenate([state_conv_buf[j][:, 1:], glu_s[:, None, :]], axis=1))
            pb_s.append(jnp.concatenate([state_pool_buf[j][:, 1:], zs[:, None, 2 * CONV_W:]], axis=1))

        wq, wo = w_xq[l].astype(BF16), w_xo[l].astype(BF16)
        q = _norm_matmul(xp, g[2], wq, tm=tm_p, name="xattn_q")
        o = _xattn_prompt(q, memkv, bp, seq)
        xp = _matmul_norm_res([o], wo, g[3], xp, tm=tm_p, name="xattn_o")
        q_s = _norm_matmul(xs, g[2], wq, tm=bs, name="xattn_q_s")
        o_s = _xattn_sample(q_s.reshape(bs, 1, MEM_W), l,
                            cache_mem_k.reshape(depth, bs, mem_len * MEM_H, MEM_HD),
                            cache_mem_v.reshape(depth, bs, mem_len * MEM_H, MEM_HD))
        xs = _matmul_norm_res([o_s.reshape(bs, MEM_W)], wo, g[3], xs, tm=bs, name="xattn_o_s")

        if l % 2 == 0:
            i = l // 2
            w1, w3, w2 = (_cast_bf16(w[i:i + 1])[0] for w in (w_ff1, w_ff3, w_ff2))
            xp = _ffn(xp, g[4], g[5], w1, w3, w2, tm=tm_p)
            xs = _ffn(xs, g[4], g[5], w1, w3, w2, tm=bs)
        else:
            j = l // 2
            xp, xs = _moe_block([xp, xs], g[4], g[5], w_router[j], b_router[j],
                                _cast_bf16(w_e1[j]), _cast_bf16(w_e3[j]), _cast_bf16(w_e2[j]))

    return (xp.reshape(bp, seq, d), xs.reshape(bs, 1, d),
            jnp.stack(fk_p), jnp.stack(fv_p), jnp.stack(fl_p), jnp.stack(lh_p), jnp.stack(lc_p),
            jnp.stack(cb_p), jnp.stack(pb_p), jnp.stack(mk_pl), jnp.stack(mv_pl),
            jnp.stack(fk_s), jnp.stack(fv_s), jnp.stack(fl_s), jnp.stack(lh_s), jnp.stack(lc_s),
            jnp.stack(cb_s), jnp.stack(pb_s))
```

```python
import functools

import jax
import jax.numpy as jnp
from jax import lax
from jax.experimental import pallas as pl
from jax.experimental.pallas import tpu as pltpu

F32 = jnp.float32
BF16 = jnp.bfloat16

D_MODEL = 1024
LRU_W = 512
LRU_CONV = 4
LRU_C = 8.0
FOX_H = 8
FOX_HD = 64
FOX_W = FOX_H * FOX_HD
CONV_W = 512
CONV_K = 31
POOL_W = 512
POOL_WINDOWS = (2, 4, 8, 16)
POOL_GW = POOL_W // len(POOL_WINDOWS)
POOL_BUF = max(POOL_WINDOWS) - 1
MEM_H = 4
MEM_HD = 128
MEM_W = MEM_H * MEM_HD
N_EXPERTS = 8
EPS = 1e-6
NEG = -1e30

LANES = 128
ROW_TILE = 512
SEQ_TILE = 512
SCAN_TILE = 256
ATT_ROWS = 32
CONV_ROWS = 64
ATT_HEADS = 4
PAGES_PER_STEP = 8
MOE_TILE = 512
FF_ROW_SLAB = 256
MOE_FF_TILE = 1792
FFN_FF_TILE = 1408
CAST_BLOCK_BYTES = 8 * 1024 * 1024
MOE_PARTS = 2
VMEM_LIMIT = 56 * 1024 * 1024


def _cp(*sem):
    return pltpu.CompilerParams(dimension_semantics=sem, vmem_limit_bytes=VMEM_LIMIT)


def _rms(x, g):
    return x * lax.rsqrt(jnp.mean(x * x, axis=-1, keepdims=True) + EPS) * g


def _sigmoid(x):
    return 1.0 / (1.0 + jnp.exp(-x))


def _softplus(x):
    return jnp.maximum(x, 0.0) + jnp.log1p(jnp.exp(-jnp.abs(x)))


def _gelu_tanh(x):
    return 0.5 * x * (1.0 + jnp.tanh(0.7978845608028654 * (x + 0.044715 * (x * x * x))))


def _dot(a, b):
    return jnp.dot(a, b, preferred_element_type=F32)


def _dot_nt(a, b):
    return lax.dot_general(a, b, (((1,), (1,)), ((), ())), preferred_element_type=F32)


def _cast_kernel(x_ref, o_ref):
    o_ref[...] = x_ref[...].astype(BF16)


def _cast_bf16(w):
    e, k, n = w.shape
    n_k = pl.cdiv(k * n * 4, CAST_BLOCK_BYTES)
    assert k % (16 * n_k) == 0
    return pl.pallas_call(
        _cast_kernel, grid=(e, n_k),
        in_specs=[pl.BlockSpec((None, k // n_k, n), lambda i, j: (i, j, 0))],
        out_specs=pl.BlockSpec((None, k // n_k, n), lambda i, j: (i, j, 0)),
        out_shape=jax.ShapeDtypeStruct(w.shape, BF16),
        compiler_params=_cp("parallel", "parallel"), name="cast_bf16")(w)


def _norm_matmul_kernel(x_ref, g_ref, w_ref, *rest, has_t):
    hn = _rms(x_ref[...], g_ref[...]).astype(BF16)
    if has_t:
        wt_ref, o_ref, ot_ref = rest
        ot_ref[...] = _dot_nt(wt_ref[...], hn)
    else:
        (o_ref,) = rest
    o_ref[...] = _dot(hn, w_ref[...])


def _norm_matmul(x, g, w, *, tm, wt=None, name):
    m, d = x.shape
    n = w.shape[1]
    in_specs = [pl.BlockSpec((tm, d), lambda i: (i, 0)),
                pl.BlockSpec((1, d), lambda i: (0, 0)),
                pl.BlockSpec((d, n), lambda i: (0, 0))]
    out_shape = [jax.ShapeDtypeStruct((m, n), F32)]
    out_specs = [pl.BlockSpec((tm, n), lambda i: (i, 0))]
    args = [x, g, w]
    if wt is not None:
        in_specs.append(pl.BlockSpec(wt.shape, lambda i: (0, 0)))
        out_shape.append(jax.ShapeDtypeStruct((wt.shape[0], m), F32))
        out_specs.append(pl.BlockSpec((wt.shape[0], tm), lambda i: (0, i)))
        args.append(wt)
    res = pl.pallas_call(
        functools.partial(_norm_matmul_kernel, has_t=wt is not None),
        grid=(m // tm,), in_specs=in_specs, out_specs=out_specs, out_shape=out_shape,
        compiler_params=_cp("parallel"), name=name)(*args)
    return res if wt is not None else res[0]


def _matmul_norm_res_kernel(*refs, widths):
    n_a = len(widths)
    a_refs = refs[:n_a]
    w_ref, g_ref, r_ref, o_ref = refs[n_a:]
    y = None
    off = 0
    for a_ref, k in zip(a_refs, widths):
        part = _dot(a_ref[...].astype(BF16), w_ref[off:off + k, :])
        y = part if y is None else y + part
        off += k
    o_ref[...] = r_ref[...] + _rms(y, g_ref[...])


def _matmul_norm_res(a_list, w, g, resid, *, tm, name):
    m, d = resid.shape
    widths = tuple(a.shape[1] for a in a_list)
    in_specs = [pl.BlockSpec((tm, k), lambda i: (i, 0)) for k in widths]
    in_specs += [pl.BlockSpec(w.shape, lambda i: (0, 0)),
                 pl.BlockSpec((1, d), lambda i: (0, 0)),
                 pl.BlockSpec((tm, d), lambda i: (i, 0))]
    return pl.pallas_call(
        functools.partial(_matmul_norm_res_kernel, widths=widths),
        grid=(m // tm,), in_specs=in_specs,
        out_specs=pl.BlockSpec((tm, d), lambda i: (i, 0)),
        out_shape=jax.ShapeDtypeStruct((m, d), F32),
        compiler_params=_cp("parallel"), name=name)(*a_list, w, g, resid)


def _lru_gates(xc, wa_ref, ba_ref, wi_ref, bi_ref, lam_ref):
    xb = xc.astype(BF16)
    ra, ia = [], []
    for c in range(LRU_W // LANES):
        xs = xb[:, c * LANES:(c + 1) * LANES]
        ra.append(_dot(xs, wa_ref[c]))
        ia.append(_dot(xs, wi_ref[c]))
    r = _sigmoid(jnp.concatenate(ra, axis=1) + ba_ref[...])
    ig = _sigmoid(jnp.concatenate(ia, axis=1) + bi_ref[...])
    log_a = -LRU_C * r * _softplus(-lam_ref[...])
    a = jnp.exp(log_a)
    bx = jnp.sqrt(-jnp.tanh(log_a) * (a * a + 1.0)) * (ig * xc)
    return a, bx


def _lru_prompt_kernel(xl_ref, gate_ref, cw_ref, cb_ref, wa_ref, ba_ref, wi_ref, bi_ref, lam_ref,
                       out_ref, hlast_ref, ext_ref, hc_ref, *, tc, n_t):
    t = pl.program_id(1)

    @pl.when(t == 0)
    def _():
        ext_ref[0:8, :] = jnp.zeros((8, LRU_W), F32)
        hc_ref[...] = jnp.zeros((1, LRU_W), F32)

    xl = xl_ref[...]
    ext_ref[8:8 + tc, :] = xl
    xc = cb_ref[...] + cw_ref[LRU_CONV - 1:LRU_CONV, :] * xl
    for j in range(1, LRU_CONV):
        xc = xc + cw_ref[LRU_CONV - 1 - j:LRU_CONV - j, :] * ext_ref[pl.ds(8 - j, tc), :]
    ext_ref[0:8, :] = ext_ref[tc:tc + 8, :]

    a, b = _lru_gates(xc, wa_ref, ba_ref, wi_ref, bi_ref, lam_ref)
    row = lax.broadcasted_iota(jnp.int32, (tc, 1), 0)
    d = 1
    while d < tc:
        keep = row >= d
        a_sh = jnp.where(keep, pltpu.roll(a, d, 0), 1.0)
        b_sh = jnp.where(keep, pltpu.roll(b, d, 0), 0.0)
        b = a * b_sh + b
        a = a * a_sh
        d *= 2
    h = a * hc_ref[...] + b
    hc_ref[...] = h[tc - 1:tc, :]
    out_ref[...] = (_gelu_tanh(gate_ref[...]) * h).astype(BF16)

    @pl.when(t == n_t - 1)
    def _():
        hlast_ref[...] = h[tc - 1:tc, :]


def _lru_prompt(z, n_b, seq, cw, cb, wa, ba, wi, bi, lam):
    tc = SCAN_TILE
    n_t = seq // tc
    vec = lambda: pl.BlockSpec((1, LRU_W), lambda b, t: (0, 0))
    bd = lambda: pl.BlockSpec((LRU_W // LANES, LANES, LANES), lambda b, t: (0, 0, 0))
    return pl.pallas_call(
        functools.partial(_lru_prompt_kernel, tc=tc, n_t=n_t),
        grid=(n_b, n_t),
        in_specs=[pl.BlockSpec((tc, LRU_W), lambda b, t: (b * n_t + t, 0)),
                  pl.BlockSpec((tc, LRU_W), lambda b, t: (b * n_t + t, 1)),
                  pl.BlockSpec((LRU_CONV, LRU_W), lambda b, t: (0, 0)),
                  vec(), bd(), vec(), bd(), vec(), vec()],
        out_specs=[pl.BlockSpec((tc, LRU_W), lambda b, t: (b * n_t + t, 0)),
                   pl.BlockSpec((None, 1, LRU_W), lambda b, t: (b, 0, 0))],
        out_shape=[jax.ShapeDtypeStruct((n_b * seq, LRU_W), BF16),
                   jax.ShapeDtypeStruct((n_b, 1, LRU_W), F32)],
        scratch_shapes=[pltpu.VMEM((tc + 8, LRU_W), F32), pltpu.VMEM((1, LRU_W), F32)],
        compiler_params=_cp("parallel", "arbitrary"), name="lru_prompt")(
            z, z, cw, cb, wa, ba, wi, bi, lam)


def _lru_sample_kernel(z_ref, pre_ref, h0_ref, cw_ref, cb_ref, wa_ref, ba_ref, wi_ref, bi_ref, lam_ref,
                       out_ref, h_ref):
    xl = z_ref[:, 0:LRU_W]
    gate = z_ref[:, LRU_W:2 * LRU_W]
    xc = cb_ref[...] + cw_ref[LRU_CONV - 1:LRU_CONV, :] * xl
    for k in range(LRU_CONV - 1):
        xc = xc + cw_ref[k:k + 1, :] * pre_ref[k]
    a, bx = _lru_gates(xc, wa_ref, ba_ref, wi_ref, bi_ref, lam_ref)
    h = a * h0_ref[...] + bx
    h_ref[...] = h
    out_ref[...] = (_gelu_tanh(gate) * h).astype(BF16)


def _lru_sample(z, prefix_t, h0, cw, cb, wa, ba, wi, bi, lam):
    n_b = z.shape[0]
    return pl.pallas_call(
        _lru_sample_kernel,
        out_shape=[jax.ShapeDtypeStruct((n_b, LRU_W), BF16), jax.ShapeDtypeStruct((n_b, LRU_W), F32)],
        compiler_params=pltpu.CompilerParams(vmem_limit_bytes=VMEM_LIMIT),
        name="lru_sample")(z, prefix_t, h0, cw, cb, wa, ba, wi, bi, lam)


def _log_sigmoid(x):
    return jnp.minimum(x, 0.0) - jnp.log1p(jnp.exp(-jnp.abs(x)))


def _lane_cumsum(x):
    n = x.shape[1]
    lane = lax.broadcasted_iota(jnp.int32, x.shape, 1)
    d = 1
    while d < n:
        x = x + jnp.where(lane >= d, pltpu.roll(x, d, 1), 0.0)
        d *= 2
    return x


def _fox_prep_kernel(fl_ref, bf_ref, lf_ref, c_ref):
    lf = _log_sigmoid(fl_ref[0:FOX_H, :] + bf_ref[...])
    lf_ref[...] = lf
    c_ref[...] = _lane_cumsum(lf)


def _fox_prep(fl_t, bf, n_b, seq):
    return pl.pallas_call(
        _fox_prep_kernel, grid=(n_b,),
        in_specs=[pl.BlockSpec((fl_t.shape[0], seq), lambda b: (0, b)),
                  pl.BlockSpec((FOX_H, 1), lambda b: (0, 0))],
        out_specs=[pl.BlockSpec((FOX_H, seq), lambda b: (0, b)),
                   pl.BlockSpec((FOX_H, seq), lambda b: (0, b))],
        out_shape=[jax.ShapeDtypeStruct((FOX_H, n_b * seq), F32)] * 2,
        compiler_params=_cp("parallel"), name="fox_prep")(fl_t, bf)


def _logsig_kernel(fl_ref, bf_ref, lf_ref):
    lf_ref[...] = _log_sigmoid(fl_ref[0:FOX_H, :] + bf_ref[...])


def _logsig(fl_t, bf):
    return pl.pallas_call(
        _logsig_kernel, out_shape=jax.ShapeDtypeStruct((FOX_H, fl_t.shape[1]), F32),
        name="fox_logf_sample")(fl_t, bf)


def _fox_prompt_kernel(qi_ref, ki_ref, q_ref, k_ref, v_ref, cq_ref, ck_ref, o_ref,
                       m_sc, l_sc, acc_sc, s_sc, p_sc, al_sc, *, tb, hg):
    grp = pl.program_id(1)
    t = pl.program_id(2)
    qi = qi_ref[t]
    ki = ki_ref[t]
    width = hg * FOX_HD

    @pl.when(ki == 0)
    def _():
        m_sc[...] = jnp.full(m_sc.shape, NEG, F32)
        l_sc[...] = jnp.zeros(l_sc.shape, F32)
        acc_sc[...] = jnp.zeros(acc_sc.shape, F32)

    def step(diagonal):
        q = q_ref[...] * (FOX_HD ** -0.5)
        k = k_ref[...].astype(BF16)
        v = v_ref[...].astype(BF16)
        lane_head = jnp.right_shift(lax.broadcasted_iota(jnp.int32, (tb, width), 1),
                                    FOX_HD.bit_length() - 1)
        rc = ATT_ROWS
        if diagonal:
            row_i = lax.broadcasted_iota(jnp.int32, (rc, tb), 0)
            col_i = lax.broadcasted_iota(jnp.int32, (rc, tb), 1)
        n_lt = tb // LANES
        for h in range(hg):
            s_sc[h] = _dot_nt(jnp.where(lane_head == h, q, 0.0).astype(BF16), k)
            c_q = cq_ref[pl.ds(hg * grp + h, 1), :]
            c_k = ck_ref[pl.ds(hg * grp + h, 1), :]
            bias = c_q[:, 0:1] - c_k
            for c in range(tb // rc):
                rows = slice(c * rc, (c + 1) * rc)
                s = s_sc[h, rows, :] + bias
                if diagonal:
                    s = jnp.where(col_i <= row_i + c * rc, s, NEG)
                s_sc[h, rows, :] = s
                m_prev = m_sc[h, rows, :]
                m_new = jnp.maximum(m_prev, jnp.max(s, axis=1, keepdims=True))
                m_sc[h, rows, :] = m_new
                al_sc[h, rows, :] = jnp.exp(m_prev - m_new)
            for c in range(tb // rc):
                rows = slice(c * rc, (c + 1) * rc)
                m_new = m_sc[h, rows, :]
                p_sum = None
                for j in range(n_lt):
                    cols = slice(j * LANES, (j + 1) * LANES)
                    p = jnp.exp(s_sc[h, rows, cols] - m_new)
                    p_sc[h, rows, cols] = p.astype(BF16)
                    p_sum = p if p_sum is None else p_sum + p
                l_sc[h, rows, :] = al_sc[h, rows, :] * l_sc[h, rows, :] + p_sum
            lt = (h * FOX_HD) // LANES
            pv = _dot(p_sc[h], v)[:, lt * LANES:(lt + 1) * LANES]
            acc_sc[h] = al_sc[h] * acc_sc[h] + pv

    @pl.when(ki < qi)
    def _():
        step(False)

    @pl.when(ki == qi)
    def _():
        step(True)
        lane = lax.broadcasted_iota(jnp.int32, (tb, LANES), 1)
        heads_per_tile = LANES // FOX_HD
        for lt in range(width // LANES):
            o = None
            for i in range(heads_per_tile):
                h = lt * heads_per_tile + i
                o_h = acc_sc[h] / jnp.sum(l_sc[h], axis=1, keepdims=True)
                o = o_h if o is None else jnp.where(lane < i * FOX_HD, o, o_h)
            o_ref[:, lt * LANES:(lt + 1) * LANES] = o.astype(BF16)


def _fox_prompt(z, c_t, n_b, seq):
    tb = SEQ_TILE
    hg = ATT_HEADS
    width = hg * FOX_HD
    n_q = seq // tb
    q_blk = (2 * LRU_W) // width
    k_blk = q_blk + FOX_W // width
    v_blk = k_blk + FOX_W // width
    pairs = [(qi, ki) for qi in range(n_q) for ki in range(qi + 1)]
    qi_list = jnp.asarray([p[0] for p in pairs], jnp.int32)
    ki_list = jnp.asarray([p[1] for p in pairs], jnp.int32)
    grid_spec = pltpu.PrefetchScalarGridSpec(
        num_scalar_prefetch=2, grid=(n_b, FOX_H // hg, len(pairs)),
        in_specs=[
            pl.BlockSpec((tb, width), lambda b, g, t, qi, ki: (b * n_q + qi[t], q_blk + g)),
            pl.BlockSpec((tb, width), lambda b, g, t, qi, ki: (b * n_q + ki[t], k_blk + g)),
            pl.BlockSpec((tb, width), lambda b, g, t, qi, ki: (b * n_q + ki[t], v_blk + g)),
            pl.BlockSpec((FOX_H, tb), lambda b, g, t, qi, ki: (0, b * n_q + qi[t])),
            pl.BlockSpec((FOX_H, tb), lambda b, g, t, qi, ki: (0, b * n_q + ki[t])),
        ],
        out_specs=pl.BlockSpec((tb, width), lambda b, g, t, qi, ki: (b * n_q + qi[t], g)),
        scratch_shapes=[pltpu.VMEM((hg, tb, LANES), F32), pltpu.VMEM((hg, tb, LANES), F32),
                        pltpu.VMEM((hg, tb, LANES), F32), pltpu.VMEM((hg, tb, tb), F32),
                        pltpu.VMEM((hg, tb, tb), BF16), pltpu.VMEM((hg, tb, LANES), F32)])
    return pl.pallas_call(
        functools.partial(_fox_prompt_kernel, tb=tb, hg=hg), grid_spec=grid_spec,
        out_shape=jax.ShapeDtypeStruct((n_b * seq, FOX_W), BF16),
        compiler_params=_cp("parallel", "parallel", "arbitrary"),
        name="fox_prompt")(qi_list, ki_list, z, z, z, c_t, c_t)


def _fox_sample_kernel(pt_ref, q_ref, kn_ref, vn_ref, lfn_ref, *rest, n_pg, n_g):
    k_refs = rest[0:n_pg]
    v_refs = rest[n_pg:2 * n_pg]
    lf_refs = rest[2 * n_pg:3 * n_pg]
    o_ref, m_sc, l_sc, acc_sc, cc_sc = rest[3 * n_pg:]
    g = pl.program_id(1)
    page = lf_refs[0].shape[1]

    @pl.when(g == 0)
    def _():
        m_sc[...] = jnp.full(m_sc.shape, NEG, F32)
        l_sc[...] = jnp.zeros(l_sc.shape, F32)
        acc_sc[...] = jnp.zeros(acc_sc.shape, F32)
        cc_sc[...] = jnp.zeros(cc_sc.shape, F32)


    q = q_ref[...] * (FOX_HD ** -0.5)
    q16 = jnp.concatenate([q, jnp.zeros_like(q)], axis=0)
    row_q = lax.broadcasted_iota(jnp.int32, (2 * FOX_H, FOX_HD), 0)
    q_only = [jnp.where(row_q == h, q16, 0.0).astype(BF16) for h in range(FOX_H)]

    s_parts = []
    for j in range(n_pg):
        s_j = None
        for h in range(FOX_H):
            part = _dot(q_only[h], k_refs[j][h].astype(BF16))
            s_j = part if s_j is None else s_j + part
        s_parts.append(s_j[0:FOX_H])
    s = jnp.concatenate(s_parts, axis=1)
    lf = jnp.concatenate([lf_refs[j][...] for j in range(n_pg)], axis=1)
    c = _lane_cumsum(lf) + cc_sc[...]
    cc_sc[...] = c[:, c.shape[1] - 1:]
    s = s - c
    m_prev = m_sc[...]
    m_new = jnp.maximum(m_prev, jnp.max(s, axis=1, keepdims=True))
    alpha = jnp.exp(m_prev - m_new)
    p = jnp.exp(s - m_new)
    l_sc[...] = alpha * l_sc[...] + jnp.sum(p, axis=1, keepdims=True)
    p16 = jnp.concatenate([p, jnp.zeros_like(p)], axis=0)
    row_p = lax.broadcasted_iota(jnp.int32, (2 * FOX_H, page), 0)
    pv = None
    for j in range(n_pg):
        p_j = p16[:, j * page:(j + 1) * page]
        for h in range(FOX_H):
            part = _dot_nt(jnp.where(row_p == h, p_j, 0.0).astype(BF16), v_refs[j][h].astype(BF16))
            pv = part if pv is None else pv + part
    acc_sc[...] = alpha * acc_sc[...] + pv[0:FOX_H]
    m_sc[...] = m_new

    @pl.when(g == n_g - 1)
    def _():
        s_n = jnp.sum(q * kn_ref[...], axis=1, keepdims=True)
        s_n = s_n - (cc_sc[...] + lfn_ref[...])
        m_p = m_sc[...]
        m_n = jnp.maximum(m_p, s_n)
        al = jnp.exp(m_p - m_n)
        p_n = jnp.exp(s_n - m_n)
        l_n = al * l_sc[...] + p_n
        o_ref[...] = ((al * acc_sc[...] + p_n * vn_ref[...]) / l_n).astype(BF16)


def _fox_sample(page_table, layer, q, k_new, v_new, lf_new, cache_k, cache_v, cache_lf_t):
    n_b, n_pages = page_table.shape
    n_pg = PAGES_PER_STEP
    n_g = n_pages // n_pg
    page = cache_k.shape[4]
    head_spec = lambda: pl.BlockSpec((None, FOX_H, FOX_HD), lambda b, g, pt: (b, 0, 0))

    def kv_spec(j):
        return pl.BlockSpec((None, None, FOX_H, FOX_HD, page),
                            lambda b, g, pt, j=j: (layer, pt[b * n_pages + g * n_pg + j], 0, 0, 0))

    def lf_spec(j):
        return pl.BlockSpec((None, FOX_H, page),
                            lambda b, g, pt, j=j: (pt[b * n_pages + g * n_pg + j], 0, 0))

    in_specs = [head_spec(), head_spec(), head_spec(),
                pl.BlockSpec((None, FOX_H, 1), lambda b, g, pt: (b, 0, 0))]
    in_specs += [kv_spec(j) for j in range(n_pg)]
    in_specs += [kv_spec(j) for j in range(n_pg)]
    in_specs += [lf_spec(j) for j in range(n_pg)]
    grid_spec = pltpu.PrefetchScalarGridSpec(
        num_scalar_prefetch=1, grid=(n_b, n_g), in_specs=in_specs,
        out_specs=pl.BlockSpec((None, FOX_H, FOX_HD), lambda b, g, pt: (b, 0, 0)),
        scratch_shapes=[pltpu.VMEM((FOX_H, 1), F32), pltpu.VMEM((FOX_H, 1), F32),
                        pltpu.VMEM((FOX_H, FOX_HD), F32), pltpu.VMEM((FOX_H, 1), F32)])
    return pl.pallas_call(
        functools.partial(_fox_sample_kernel, n_pg=n_pg, n_g=n_g),
        grid_spec=grid_spec,
        out_shape=jax.ShapeDtypeStruct((n_b, FOX_H, FOX_HD), BF16),
        compiler_params=_cp("parallel", "arbitrary"), name="fox_sample")(
            page_table.reshape(-1), q, k_new, v_new, lf_new,
            *([cache_k] * n_pg), *([cache_v] * n_pg), *([cache_lf_t] * n_pg))


def _xattn_prompt_kernel(q_ref, mk_ref, mv_ref, o_ref):
    outs = []
    for h in range(MEM_H):
        cs = slice(h * MEM_HD, (h + 1) * MEM_HD)
        qh = q_ref[:, cs].astype(BF16)
        s = _dot_nt(qh, mk_ref[:, cs].astype(BF16)) * (MEM_HD ** -0.5)
        p = jnp.exp(s - jnp.max(s, axis=1, keepdims=True))
        l = jnp.sum(p, axis=1, keepdims=True)
        outs.append(_dot(p.astype(BF16), mv_ref[:, cs].astype(BF16)) / l)
    o_ref[...] = jnp.concatenate(outs, axis=1).astype(BF16)


def _xattn_prompt(q, memkv, n_b, seq):
    tb = SEQ_TILE
    n_t = seq // tb
    mem_len = memkv.shape[0] // n_b
    return pl.pallas_call(
        _xattn_prompt_kernel, grid=(n_b, n_t),
        in_specs=[pl.BlockSpec((tb, MEM_W), lambda b, t: (b * n_t + t, 0)),
                  pl.BlockSpec((mem_len, MEM_W), lambda b, t: (b, 0)),
                  pl.BlockSpec((mem_len, MEM_W), lambda b, t: (b, 1))],
        out_specs=pl.BlockSpec((tb, MEM_W), lambda b, t: (b * n_t + t, 0)),
        out_shape=jax.ShapeDtypeStruct((n_b * seq, MEM_W), BF16),
        compiler_params=_cp("parallel", "parallel"), name="xattn_prompt")(q, memkv, memkv)


def _xattn_sample_kernel(q_ref, mk_ref, mv_ref, o_ref):
    q = q_ref[...]
    row = lax.broadcasted_iota(jnp.int32, (16, MEM_W), 0)
    col = lax.broadcasted_iota(jnp.int32, (16, MEM_W), 1)
    head_cols = jnp.right_shift(col, MEM_HD.bit_length() - 1) == row
    q_rows = jnp.where(head_cols, q, 0.0).astype(BF16)
    mem_len = mk_ref.shape[0] // MEM_H

    def heads_on_lanes(ref):
        return jnp.concatenate([ref[pl.ds(h, mem_len, stride=MEM_H), :] for h in range(MEM_H)],
                               axis=1).astype(BF16)

    s = _dot_nt(q_rows, heads_on_lanes(mk_ref)) * (MEM_HD ** -0.5)
    p = jnp.exp(s - jnp.max(s, axis=1, keepdims=True))
    l = jnp.sum(p, axis=1, keepdims=True)
    o = _dot(p.astype(BF16), heads_on_lanes(mv_ref)) / l
    o_ref[...] = jnp.sum(jnp.where(head_cols, o, 0.0), axis=0, keepdims=True).astype(BF16)


def _xattn_sample(q, layer, mk, mv):
    n_b, rows = mk.shape[1], mk.shape[2]
    return pl.pallas_call(
        _xattn_sample_kernel, grid=(n_b,),
        in_specs=[pl.BlockSpec((None, 1, MEM_W), lambda b: (b, 0, 0)),
                  pl.BlockSpec((None, None, rows, MEM_HD), lambda b: (layer, b, 0, 0)),
                  pl.BlockSpec((None, None, rows, MEM_HD), lambda b: (layer, b, 0, 0))],
        out_specs=pl.BlockSpec((None, 1, MEM_W), lambda b: (b, 0, 0)),
        out_shape=jax.ShapeDtypeStruct((n_b, 1, MEM_W), BF16),
        compiler_params=_cp("parallel"), name="xattn_sample")(q, mk, mv)


def _swiglu_accumulate(x_sc, w1_ref, w3_ref, w2_ref, acc_sc):
    tm = x_sc.shape[0]
    slab = min(tm, FF_ROW_SLAB)
    for r in range(tm // slab):
        rows = slice(r * slab, (r + 1) * slab)
        x = x_sc[rows, :]
        h1 = _dot(x, w1_ref[...])
        h3 = _dot(x, w3_ref[...])
        hh = (h1 * _sigmoid(h1) * h3).astype(BF16)
        acc_sc[rows, :] += _dot(hh, w2_ref[...])


def _ffn_kernel(x_ref, g4_ref, g5_ref, w1_ref, w3_ref, w2_ref, o_ref, hn_sc, acc_sc, *, n_f):
    f = pl.program_id(1)

    @pl.when(f == 0)
    def _():
        hn_sc[...] = _rms(x_ref[...], g4_ref[...]).astype(BF16)
        acc_sc[...] = jnp.zeros(acc_sc.shape, F32)

    _swiglu_accumulate(hn_sc, w1_ref, w3_ref, w2_ref, acc_sc)

    @pl.when(f == n_f - 1)
    def _():
        o_ref[...] = x_ref[...] + _rms(acc_sc[...], g5_ref[...])


def _ffn(x, g4, g5, w1, w3, w2, *, tm):
    m, d = x.shape
    ff = w1.shape[1]
    tf = FFN_FF_TILE
    n_f = ff // tf
    return pl.pallas_call(
        functools.partial(_ffn_kernel, n_f=n_f), grid=(m // tm, n_f),
        in_specs=[pl.BlockSpec((tm, d), lambda i, f: (i, 0)),
                  pl.BlockSpec((1, d), lambda i, f: (0, 0)),
                  pl.BlockSpec((1, d), lambda i, f: (0, 0)),
                  pl.BlockSpec((d, tf), lambda i, f: (0, f)),
                  pl.BlockSpec((d, tf), lambda i, f: (0, f)),
                  pl.BlockSpec((tf, d), lambda i, f: (f, 0))],
        out_specs=pl.BlockSpec((tm, d), lambda i, f: (i, 0)),
        out_shape=jax.ShapeDtypeStruct((m, d), F32),
        scratch_shapes=[pltpu.VMEM((tm, d), BF16), pltpu.VMEM((tm, d), F32)],
        compiler_params=_cp("parallel", "arbitrary"), name="ffn")(x, g4, g5, w1, w3, w2)


def _layernorm_silu(x, g, b):
    mu = jnp.mean(x, axis=-1, keepdims=True)
    xc = x - mu
    y = xc * lax.rsqrt(jnp.mean(xc * xc, axis=-1, keepdims=True) + EPS) * g + b
    return y * _sigmoid(y)


def _odd_prompt_kernel(a_ref, gt_ref, up_ref, dww_ref, dwb_ref, lng_ref, lnb_ref, pw_ref, ps_ref,
                       cv_ref, pool_ref, cbuf_ref, pbuf_ref, eg_ref, eu_ref, sh_ref, *, tc, n_t):
    t = pl.program_id(1)
    halo_g, halo_u = 32, 16

    @pl.when(t == 0)
    def _():
        eg_ref[0:halo_g, :] = jnp.zeros((halo_g, CONV_W), F32)
        eu_ref[0:halo_u, :] = jnp.zeros((halo_u, POOL_W), F32)

    glu = a_ref[...] * _sigmoid(gt_ref[...])
    up = up_ref[...]
    eg_ref[halo_g:halo_g + tc, :] = glu
    eu_ref[halo_u:halo_u + tc, :] = up

    ext = eg_ref[...]
    n_ext = tc + halo_g
    sh_ref[0] = ext
    for s in range(1, 8):
        sh_ref[s] = pltpu.roll(ext, n_ext - s, 0)
    rc = CONV_ROWS
    for c in range(tc // rc):
        parts = []
        for j in range(CONV_W // LANES):
            cols = slice(j * LANES, (j + 1) * LANES)
            acc = jnp.broadcast_to(dwb_ref[:, cols], (rc, LANES))
            for k in range(CONV_K):
                off = k + halo_g - (CONV_K - 1)
                acc = acc + dww_ref[k:k + 1, cols] * sh_ref[off % 8, pl.ds(c * rc + (off // 8) * 8, rc), cols]
            parts.append(acc)
        cv_ref[c * rc:(c + 1) * rc, :] = _layernorm_silu(
            jnp.concatenate(parts, axis=1), lng_ref[...], lnb_ref[...]).astype(BF16)

    pos = t * tc + lax.broadcasted_iota(jnp.int32, (tc, 1), 0)
    outs = []
    for gi, w in enumerate(POOL_WINDOWS):
        cs = slice(gi * POOL_GW, (gi + 1) * POOL_GW)
        u_g = up[:, cs]
        win = u_g
        for j in range(1, w):
            win = win + eu_ref[pl.ds(halo_u - j, tc), cs]
        cnt = jnp.minimum(pos + 1, w).astype(F32)
        dlt = win / cnt - u_g
        outs.append(_dot(dlt.astype(BF16), pw_ref[gi]))
    pool_ref[...] = (jnp.concatenate(outs, axis=1) * ps_ref[...]).astype(BF16)

    @pl.when(t == n_t - 1)
    def _():
        cbuf_ref[...] = eg_ref[pl.ds(halo_g + tc - (CONV_K - 1), CONV_K - 1), :]
        pbuf_ref[...] = eu_ref[pl.ds(halo_u + tc - POOL_BUF, POOL_BUF), :]

    eg_ref[0:halo_g, :] = eg_ref[tc:tc + halo_g, :]
    eu_ref[0:halo_u, :] = eu_ref[tc:tc + halo_u, :]


def _odd_prompt(z, n_b, seq, dww, dwb, lng, lnb, pw, ps):
    tc = SEQ_TILE
    n_t = seq // tc
    vec = lambda: pl.BlockSpec((1, CONV_W), lambda b, t: (0, 0))
    return pl.pallas_call(
        functools.partial(_odd_prompt_kernel, tc=tc, n_t=n_t), grid=(n_b, n_t),
        in_specs=[pl.BlockSpec((tc, CONV_W), lambda b, t: (b * n_t + t, 0)),
                  pl.BlockSpec((tc, CONV_W), lambda b, t: (b * n_t + t, 1)),
                  pl.BlockSpec((tc, POOL_W), lambda b, t: (b * n_t + t, 2)),
                  pl.BlockSpec((CONV_K, CONV_W), lambda b, t: (0, 0)),
                  vec(), vec(), vec(),
                  pl.BlockSpec((len(POOL_WINDOWS), POOL_GW, POOL_GW), lambda b, t: (0, 0, 0)),
                  vec()],
        out_specs=[pl.BlockSpec((tc, CONV_W), lambda b, t: (b * n_t + t, 0)),
                   pl.BlockSpec((tc, POOL_W), lambda b, t: (b * n_t + t, 0)),
                   pl.BlockSpec((None, CONV_K - 1, CONV_W), lambda b, t: (b, 0, 0)),
                   pl.BlockSpec((None, POOL_BUF, POOL_W), lambda b, t: (b, 0, 0))],
        out_shape=[jax.ShapeDtypeStruct((n_b * seq, CONV_W), BF16),
                   jax.ShapeDtypeStruct((n_b * seq, POOL_W), BF16),
                   jax.ShapeDtypeStruct((n_b, CONV_K - 1, CONV_W), F32),
                   jax.ShapeDtypeStruct((n_b, POOL_BUF, POOL_W), F32)],
        scratch_shapes=[pltpu.VMEM((tc + 32, CONV_W), F32), pltpu.VMEM((tc + 16, POOL_W), F32),
                        pltpu.VMEM((8, tc + 32, CONV_W), F32)],
        compiler_params=_cp("parallel", "arbitrary"), name="odd_prompt")(
            z, z, z, dww, dwb, lng, lnb, pw, ps)


def _odd_sample_kernel(z_ref, cbuf_ref, pbuf_ref, dww_ref, dwb_ref, lng_ref, lnb_ref, pw_ref, ps_ref,
                       cv_ref, pool_ref, glu_ref, *, pos0):
    glu = z_ref[:, 0:CONV_W] * _sigmoid(z_ref[:, CONV_W:2 * CONV_W])
    up = z_ref[:, 2 * CONV_W:2 * CONV_W + POOL_W]
    glu_ref[...] = glu
    acc = dwb_ref[...] + dww_ref[CONV_K - 1:CONV_K, :] * glu
    for k in range(CONV_K - 1):
        acc = acc + dww_ref[k:k + 1, :] * cbuf_ref[k]
    cv_ref[...] = _layernorm_silu(acc, lng_ref[...], lnb_ref[...]).astype(BF16)
    outs = []
    for gi, w in enumerate(POOL_WINDOWS):
        cs = slice(gi * POOL_GW, (gi + 1) * POOL_GW)
        u_g = up[:, cs]
        win = u_g
        for j in range(1, w):
            win = win + pbuf_ref[POOL_BUF - j][:, cs]
        dlt = win / float(min(pos0 + 1, w)) - u_g
        outs.append(_dot(dlt.astype(BF16), pw_ref[gi]))
    pool_ref[...] = (jnp.concatenate(outs, axis=1) * ps_ref[...]).astype(BF16)


def _odd_sample(z, cbuf_t, pbuf_t, dww, dwb, lng, lnb, pw, ps, pos0):
    n_b = z.shape[0]
    return pl.pallas_call(
        functools.partial(_odd_sample_kernel, pos0=pos0),
        out_shape=[jax.ShapeDtypeStruct((n_b, CONV_W), BF16),
                   jax.ShapeDtypeStruct((n_b, POOL_W), BF16),
                   jax.ShapeDtypeStruct((n_b, CONV_W), F32)],
        compiler_params=pltpu.CompilerParams(vmem_limit_bytes=VMEM_LIMIT),
        name="odd_sample")(z, cbuf_t, pbuf_t, dww, dwb, lng, lnb, pw, ps)


def _router_kernel(x_ref, g_ref, whi_ref, wlo_ref, b_ref, hn_ref, idx_ref, gate_ref):
    hn = _rms(x_ref[...], g_ref[...])
    hb = hn.astype(BF16)
    hn_ref[...] = hn
    hlo = (hn - hb.astype(F32)).astype(BF16)
    logits = _dot(hb, whi_ref[...]) + (_dot(hb, wlo_ref[...]) + _dot(hlo, whi_ref[...])) + b_ref[...]
    lane = lax.broadcasted_iota(jnp.int32, logits.shape, 1)
    lane_f = lane.astype(F32)
    logits = jnp.where(lane < N_EXPERTS, logits, NEG)
    m1 = jnp.max(logits, axis=1, keepdims=True)
    i1 = jnp.min(jnp.where(logits == m1, lane_f, float(LANES)), axis=1, keepdims=True)
    rest = jnp.where(lane_f == i1, NEG, logits)
    m2 = jnp.max(rest, axis=1, keepdims=True)
    i2 = jnp.min(jnp.where(rest == m2, lane_f, float(LANES)), axis=1, keepdims=True)
    e = jnp.exp(m2 - m1)
    g1 = 1.0 / (1.0 + e)
    g2 = e / (1.0 + e)
    idx_ref[...] = jnp.where(lane == 0, i1, jnp.where(lane == 1, i2, 0.0)).astype(jnp.int32)
    gate_ref[...] = jnp.where(lane == 0, g1, jnp.where(lane == 1, g2, 0.0))


def _router(x, g, whi, wlo, b, *, tm):
    m, d = x.shape
    return pl.pallas_call(
        _router_kernel, grid=(m // tm,),
        in_specs=[pl.BlockSpec((tm, d), lambda i: (i, 0)),
                  pl.BlockSpec((1, d), lambda i: (0, 0)),
                  pl.BlockSpec((d, LANES), lambda i: (0, 0)),
                  pl.BlockSpec((d, LANES), lambda i: (0, 0)),
                  pl.BlockSpec((1, LANES), lambda i: (0, 0))],
        out_specs=[pl.BlockSpec((tm, d), lambda i: (i, 0)),
                   pl.BlockSpec((tm, LANES), lambda i: (i, 0)),
                   pl.BlockSpec((tm, LANES), lambda i: (i, 0))],
        out_shape=[jax.ShapeDtypeStruct((m, d), F32),
                   jax.ShapeDtypeStruct((m, LANES), jnp.int32),
                   jax.ShapeDtypeStruct((m, LANES), F32)],
        compiler_params=_cp("parallel"), name="router")(x, g, whi, wlo, b)


def _gmm_kernel(te_ref, tv_ref, x_ref, gate_ref, w1_ref, w3_ref, w2_ref, *rest, n_f, tile0):
    o_ref, acc_sc, xb_sc = rest[-3:]
    t = tile0 + pl.program_id(0)
    f = pl.program_id(1)

    @pl.when(f == 0)
    def _():
        acc_sc[...] = jnp.zeros(acc_sc.shape, F32)
        xb_sc[...] = x_ref[...].astype(BF16)

    @pl.when(tv_ref[t] != 0)
    def _():
        _swiglu_accumulate(xb_sc, w1_ref, w3_ref, w2_ref, acc_sc)

    @pl.when(f == n_f - 1)
    def _():
        o_ref[...] = acc_sc[...] * gate_ref[...]


def _gmm(tile_expert, tile_valid, x_part, gate_sorted, w1, w3, w2, *, tile0, y_prev=None):
    rows = gate_sorted.shape[0]
    d, ff = w1.shape[1], w1.shape[2]
    tm, tf = MOE_TILE, MOE_FF_TILE
    n_f = ff // tf
    in_specs = [pl.BlockSpec((tm, d), lambda t, f, te, tv: (t, 0)),
                pl.BlockSpec((tm, 1), lambda t, f, te, tv: (tile0 + t, 0)),
                pl.BlockSpec((None, d, tf), lambda t, f, te, tv: (te[tile0 + t], 0, f)),
                pl.BlockSpec((None, d, tf), lambda t, f, te, tv: (te[tile0 + t], 0, f)),
                pl.BlockSpec((None, tf, d), lambda t, f, te, tv: (te[tile0 + t], f, 0))]
    args = [tile_expert, tile_valid, x_part, gate_sorted, w1, w3, w2]
    aliases = {}
    if y_prev is not None:
        in_specs.append(pl.BlockSpec(memory_space=pl.ANY))
        aliases = {len(args): 0}
        args.append(y_prev)
    grid_spec = pltpu.PrefetchScalarGridSpec(
        num_scalar_prefetch=2, grid=(x_part.shape[0] // tm, n_f), in_specs=in_specs,
        out_specs=pl.BlockSpec((tm, d), lambda t, f, te, tv: (tile0 + t, 0)),
        scratch_shapes=[pltpu.VMEM((tm, d), F32), pltpu.VMEM((tm, d), BF16)])
    return pl.pallas_call(
        functools.partial(_gmm_kernel, n_f=n_f, tile0=tile0), grid_spec=grid_spec,
        out_shape=jax.ShapeDtypeStruct((rows, d), F32), input_output_aliases=aliases,
        compiler_params=_cp("parallel", "arbitrary"), name="moe_experts")(*args)


def _combine_kernel(x_ref, y1_ref, y2_ref, g_ref, o_ref):
    o_ref[...] = x_ref[...] + _rms(y1_ref[...] + y2_ref[...], g_ref[...])


def _combine(x, y1, y2, g, *, tm):
    m, d = x.shape
    row = lambda: pl.BlockSpec((tm, d), lambda i: (i, 0))
    return pl.pallas_call(
        _combine_kernel, grid=(m // tm,),
        in_specs=[row(), row(), row(), pl.BlockSpec((1, d), lambda i: (0, 0))],
        out_specs=row(), out_shape=jax.ShapeDtypeStruct((m, d), F32),
        compiler_params=_cp("parallel"), name="moe_combine")(x, y1, y2, g)


def _moe_block(xs, g4, g5, w_r, b_r, w1, w3, w2):
    pad = LANES - N_EXPERTS
    w_r_p = jnp.pad(w_r, ((0, 0), (0, pad)))
    whi = w_r_p.astype(BF16)
    wlo = (w_r_p - whi.astype(F32)).astype(BF16)
    b_p = jnp.pad(b_r, (0, pad)).reshape(1, LANES)
    hn, idx, gate = [], [], []
    for x in xs:
        h, i, gt = _router(x, g4, whi, wlo, b_p, tm=min(ROW_TILE, x.shape[0]))
        hn.append(h)
        idx.append(i[:, :2])
        gate.append(gt[:, :2])
    hn = jnp.concatenate(hn, axis=0)
    e_flat = jnp.concatenate(idx, axis=0).reshape(-1)
    g_flat = jnp.concatenate(gate, axis=0).reshape(-1)
    n_assign = e_flat.shape[0]
    tm = MOE_TILE
    n_tiles = n_assign // tm + N_EXPERTS
    rows = n_tiles * tm

    onehot = (e_flat[:, None] == jnp.arange(N_EXPERTS, dtype=jnp.int32)[None, :]).astype(jnp.int32)
    rank = jnp.sum((jnp.cumsum(onehot, axis=0) - onehot) * onehot, axis=1)
    cnt = jnp.sum(onehot, axis=0)
    tiles_e = (cnt + tm - 1) // tm
    tile_end = jnp.cumsum(tiles_e)
    row_start = (tile_end - tiles_e) * tm
    dest = row_start[e_flat] + rank
    tile_ids = jnp.arange(n_tiles, dtype=jnp.int32)
    tile_expert = jnp.minimum(jnp.sum((tile_end[None, :] <= tile_ids[:, None]).astype(jnp.int32), axis=1),
                              N_EXPERTS - 1)
    tile_valid = (tile_ids < tile_end[-1]).astype(jnp.int32)
    row_assign = jnp.zeros((rows,), jnp.int32).at[dest].set(jnp.arange(n_assign, dtype=jnp.int32))
    row_ids = jnp.arange(rows, dtype=jnp.int32)
    row_expert = jnp.repeat(tile_expert, tm)
    row_valid = (row_ids - row_start[row_expert]) < cnt[row_expert]
    src_tok = row_assign // 2
    gate_sorted = jnp.where(row_valid, g_flat[row_assign], 0.0).reshape(rows, 1)

    take_rows = lambda a, idx: a.at[idx].get(mode="promise_in_bounds")
    y_sorted = None
    part_tiles = n_tiles // MOE_PARTS
    for part in range(MOE_PARTS):
        tile0 = part * part_tiles
        x_part = take_rows(hn, src_tok[tile0 * tm:(tile0 + part_tiles) * tm])
        y_sorted = _gmm(tile_expert, tile_valid, x_part, gate_sorted, w1, w3, w2, tile0=tile0, y_prev=y_sorted)

    outs = []
    off = 0
    dest2 = dest.reshape(-1, 2)
    for x in xs:
        m = x.shape[0]
        d1 = dest2[off:off + m, 0]
        d2 = dest2[off:off + m, 1]
        outs.append(_combine(x, take_rows(y_sorted, d1), take_rows(y_sorted, d2), g5,
                             tm=min(ROW_TILE, m)))
        off += m
    return outs


def _block_diag_pairs(w):
    nb, bw, _ = w.shape
    w = w.reshape(nb // 2, 2, bw, bw)
    z = jnp.zeros((nb // 2, bw, bw), w.dtype)
    top = jnp.concatenate([w[:, 0], z], axis=2)
    bot = jnp.concatenate([z, w[:, 1]], axis=2)
    return jnp.concatenate([top, bot], axis=1).astype(BF16)


def kernel(x_prompt, x_sample, cache_fox_k, cache_fox_v, cache_fox_logf, state_lru_h, state_lru_conv, state_conv_buf, state_pool_buf, cache_mem_k, cache_mem_v, page_table, mem_prompt, norm_g, w_xq, w_xk, w_xv, w_xo, w_in_e, b_f, lru_conv_w, lru_conv_b, lru_wa, lru_ba, lru_wi, lru_bi, lru_lam, w_out_e, w_ff1, w_ff3, w_ff2, w_in_o, cc_dw_w, cc_dw_b, cc_ln_g, cc_ln_b, pool_w, pool_scale, w_out_o, w_router, b_router, w_e1, w_e3, w_e2):
    bp, seq, d = x_prompt.shape
    bs = x_sample.shape[0]
    depth = norm_g.shape[0]
    page = cache_fox_k.shape[2]
    past_len = page_table.shape[1] * page
    mem_len = mem_prompt.shape[1]
    tm_p = ROW_TILE

    xp = x_prompt.reshape(bp * seq, d)
    xs = x_sample.reshape(bs, d)
    mem = mem_prompt.reshape(bp * mem_len, d)
    vec = lambda v: v.reshape(1, -1)

    fk_p, fv_p, fl_p, lh_p, lc_p, cb_p, pb_p, mk_pl, mv_pl = [], [], [], [], [], [], [], [], []
    fk_s, fv_s, fl_s, lh_s, lc_s, cb_s, pb_s = [], [], [], [], [], [], []

    for l in range(depth):
        g = [vec(norm_g[l, i]) for i in range(norm_g.shape[1])]
        w_kv = jnp.concatenate([w_xk[l], w_xv[l]], axis=1).astype(BF16)
        memkv = _norm_matmul(mem, g[6], w_kv, tm=tm_p, name="mem_kv")
        mk_pl.append(memkv[:, :MEM_W].reshape(bp, mem_len, MEM_H, MEM_HD))
        mv_pl.append(memkv[:, MEM_W:].reshape(bp, mem_len, MEM_H, MEM_HD))

        if l % 2 == 0:
            i = l // 2
            n_main = 2 * LRU_W + 3 * FOX_W
            w_main = w_in_e[i][:, :n_main].astype(BF16)
            w_fl_t = jnp.pad(w_in_e[i][:, n_main:].T, ((0, 16 - FOX_H), (0, 0))).astype(BF16)
            cw, cb = lru_conv_w[i], vec(lru_conv_b[i])
            wa, wi = _block_diag_pairs(lru_wa[i]), _block_diag_pairs(lru_wi[i])
            ba, bi, lam = vec(lru_ba[i]), vec(lru_bi[i]), vec(lru_lam[i])
            bf = b_f[i].reshape(FOX_H, 1)
            w_out = w_out_e[i].astype(BF16)

            z, fl_t = _norm_matmul(xp, g[0], w_main, tm=tm_p, wt=w_fl_t, name="in_proj_even")
            lf_t, c_t = _fox_prep(fl_t, bf, bp, seq)
            lru_out, h_last = _lru_prompt(z, bp, seq, cw, cb, wa, ba, wi, bi, lam)
            att = _fox_prompt(z, c_t, bp, seq)
            xp = _matmul_norm_res([lru_out, att], w_out, g[1], xp, tm=tm_p, name="out_proj_even")
            z3 = z.reshape(bp, seq, n_main)
            fk_p.append(z3[:, :, 2 * LRU_W + FOX_W:2 * LRU_W + 2 * FOX_W].reshape(bp, seq, FOX_H, FOX_HD))
            fv_p.append(z3[:, :, 2 * LRU_W + 2 * FOX_W:].reshape(bp, seq, FOX_H, FOX_HD))
            fl_p.append(lf_t.T.reshape(bp, seq, FOX_H))
            lh_p.append(h_last.reshape(bp, LRU_W))
            lc_p.append(z3[:, seq - (LRU_CONV - 1):, :LRU_W])

            zs, fls_t = _norm_matmul(xs, g[0], w_main, tm=bs, wt=w_fl_t, name="in_proj_even_s")
            lfs_t = _logsig(fls_t, bf)
            pre_t = jnp.swapaxes(state_lru_conv[i], 0, 1)
            lru_out_s, h_s = _lru_sample(zs, pre_t, state_lru_h[i], cw, cb, wa, ba, wi, bi, lam)
            q_s = zs[:, 2 * LRU_W:2 * LRU_W + FOX_W]
            k_s = zs[:, 2 * LRU_W + FOX_W:2 * LRU_W + 2 * FOX_W]
            v_s = zs[:, 2 * LRU_W + 2 * FOX_W:]
            att_s = _fox_sample(page_table, i, q_s.reshape(bs, FOX_H, FOX_HD), k_s.reshape(bs, FOX_H, FOX_HD),
                                v_s.reshape(bs, FOX_H, FOX_HD), lfs_t.T.reshape(bs, FOX_H, 1),
                                jnp.transpose(cache_fox_k, (0, 1, 3, 4, 2)),
                                jnp.transpose(cache_fox_v, (0, 1, 3, 4, 2)),
                                jnp.swapaxes(cache_fox_logf[i], 1, 2))
            xs = _matmul_norm_res([lru_out_s, att_s.reshape(bs, FOX_W)], w_out, g[1], xs, tm=bs,
                                  name="out_proj_even_s")
            fk_s.append(k_s.reshape(bs, 1, FOX_H, FOX_HD))
            fv_s.append(v_s.reshape(bs, 1, FOX_H, FOX_HD))
            fl_s.append(lfs_t.T.reshape(bs, 1, FOX_H))
            lh_s.append(h_s)
            lc_s.append(jnp.concatenate([state_lru_conv[i][:, 1:], zs[:, None, :LRU_W]], axis=1))
        else:
            j = l // 2
            w_in = w_in_o[j].astype(BF16)
            dww, dwb = cc_dw_w[j], vec(cc_dw_b[j])
            lng, lnb = vec(cc_ln_g[j]), vec(cc_ln_b[j])
            pw, ps = pool_w[j].astype(BF16), vec(pool_scale[j])
            w_out = w_out_o[j].astype(BF16)

            z = _norm_matmul(xp, g[0], w_in, tm=tm_p, name="in_proj_odd")
            cv, pool, cbuf, pbuf = _odd_prompt(z, bp, seq, dww, dwb, lng, lnb, pw, ps)
            xp = _matmul_norm_res([cv, pool], w_out, g[1], xp, tm=tm_p, name="out_proj_odd")
            cb_p.append(cbuf)
            pb_p.append(pbuf)

            zs = _norm_matmul(xs, g[0], w_in, tm=bs, name="in_proj_odd_s")
            cv_s, pool_s, glu_s = _odd_sample(zs, jnp.swapaxes(state_conv_buf[j], 0, 1),
                                              jnp.swapaxes(state_pool_buf[j], 0, 1),
                                              dww, dwb, lng, lnb, pw, ps, past_len)
            xs = _matmul_norm_res([cv_s, pool_s], w_out, g[1], xs, tm=bs, name="out_proj_odd_s")
            cb_s.append(jnp.concatenate([state_conv_buf[j][:, 1:], glu_s[:, None, :]], axis=1))
            pb_s.append(jnp.concatenate([state_pool_buf[j][:, 1:], zs[:, None, 2 * CONV_W:]], axis=1))

        wq, wo = w_xq[l].astype(BF16), w_xo[l].astype(BF16)
        q = _norm_matmul(xp, g[2], wq, tm=tm_p, name="xattn_q")
        o = _xattn_prompt(q, memkv, bp, seq)
        xp = _matmul_norm_res([o], wo, g[3], xp, tm=tm_p, name="xattn_o")
        q_s = _norm_matmul(xs, g[2], wq, tm=bs, name="xattn_q_s")
        o_s = _xattn_sample(q_s.reshape(bs, 1, MEM_W), l,
                            cache_mem_k.reshape(depth, bs, mem_len * MEM_H, MEM_HD),
                            cache_mem_v.reshape(depth, bs, mem_len * MEM_H, MEM_HD))
        xs = _matmul_norm_res([o_s.reshape(bs, MEM_W)], wo, g[3], xs, tm=bs, name="xattn_o_s")

        if l % 2 == 0:
            i = l // 2
            w1, w3, w2 = (_cast_bf16(w[i:i + 1])[0] for w in (w_ff1, w_ff3, w_ff2))
            xp = _ffn(xp, g[4], g[5], w1, w3, w2, tm=tm_p)
            xs = _ffn(xs, g[4], g[5], w1, w3, w2, tm=bs)
        else:
            j = l // 2
            xp, xs = _moe_block([xp, xs], g[4], g[5], w_router[j], b_router[j],
                                _cast_bf16(w_e1[j]), _cast_bf16(w_e3[j]), _cast_bf16(w_e2[j]))

    return (xp.reshape(bp, seq, d), xs.reshape(bs, 1, d),
            jnp.stack(fk_p), jnp.stack(fv_p), jnp.stack(fl_p), jnp.stack(lh_p), jnp.stack(lc_p),
            jnp.stack(cb_p), jnp.stack(pb_p), jnp.stack(mk_pl), jnp.stack(mv_pl),
            jnp.stack(fk_s), jnp.stack(fv_s), jnp.stack(fl_s), jnp.stack(lh_s), jnp.stack(lc_s),
            jnp.stack(cb_s), jnp.stack(pb_s))
```

```python
import functools

import jax
import jax.numpy as jnp
from jax import lax
from jax.experimental import pallas as pl
from jax.experimental.pallas import tpu as pltpu

F32 = jnp.float32
BF16 = jnp.bfloat16

D_MODEL = 1024
LRU_W = 512
LRU_CONV = 4
LRU_C = 8.0
FOX_H = 8
FOX_HD = 64
FOX_W = FOX_H * FOX_HD
CONV_W = 512
CONV_K = 31
POOL_W = 512
POOL_WINDOWS = (2, 4, 8, 16)
POOL_GW = POOL_W // len(POOL_WINDOWS)
POOL_BUF = max(POOL_WINDOWS) - 1
MEM_H = 4
MEM_HD = 128
MEM_W = MEM_H * MEM_HD
N_EXPERTS = 8
EPS = 1e-6
NEG = -1e30

LANES = 128
ROW_TILE = 512
SEQ_TILE = 512
SCAN_TILE = 256
ATT_ROWS = 32
CONV_ROWS = 64
ATT_HEADS = 4
PAGES_PER_STEP = 8
MOE_TILE = 512
FF_ROW_SLAB = 256
MOE_FF_TILE = 1792
FFN_FF_TILE = 1408
CAST_BLOCK_BYTES = 8 * 1024 * 1024
MOE_PARTS = 2
VMEM_LIMIT = 56 * 1024 * 1024


def _cp(*sem):
    return pltpu.CompilerParams(dimension_semantics=sem, vmem_limit_bytes=VMEM_LIMIT)


def _rms(x, g):
    return x * lax.rsqrt(jnp.mean(x * x, axis=-1, keepdims=True) + EPS) * g


def _sigmoid(x):
    return 1.0 / (1.0 + jnp.exp(-x))


def _softplus(x):
    return jnp.maximum(x, 0.0) + jnp.log1p(jnp.exp(-jnp.abs(x)))


def _gelu_tanh(x):
    return 0.5 * x * (1.0 + jnp.tanh(0.7978845608028654 * (x + 0.044715 * (x * x * x))))


def _dot(a, b):
    return jnp.dot(a, b, preferred_element_type=F32)


def _dot_nt(a, b):
    return lax.dot_general(a, b, (((1,), (1,)), ((), ())), preferred_element_type=F32)


def _cast_kernel(x_ref, o_ref):
    o_ref[...] = x_ref[...].astype(BF16)


def _cast_bf16(w):
    e, k, n = w.shape
    n_k = pl.cdiv(k * n * 4, CAST_BLOCK_BYTES)
    assert k % (16 * n_k) == 0
    return pl.pallas_call(
        _cast_kernel, grid=(e, n_k),
        in_specs=[pl.BlockSpec((None, k // n_k, n), lambda i, j: (i, j, 0))],
        out_specs=pl.BlockSpec((None, k // n_k, n), lambda i, j: (i, j, 0)),
        out_shape=jax.ShapeDtypeStruct(w.shape, BF16),
        compiler_params=_cp("parallel", "parallel"), name="cast_bf16")(w)


def _norm_matmul_kernel(x_ref, g_ref, w_ref, *rest, has_t):
    hn = _rms(x_ref[...], g_ref[...]).astype(BF16)
    if has_t:
        wt_ref, o_ref, ot_ref = rest
        ot_ref[...] = _dot_nt(wt_ref[...], hn)
    else:
        (o_ref,) = rest
    o_ref[...] = _dot(hn, w_ref[...])


def _norm_matmul(x, g, w, *, tm, wt=None, name):
    m, d = x.shape
    n = w.shape[1]
    in_specs = [pl.BlockSpec((tm, d), lambda i: (i, 0)),
                pl.BlockSpec((1, d), lambda i: (0, 0)),
                pl.BlockSpec((d, n), lambda i: (0, 0))]
    out_shape = [jax.ShapeDtypeStruct((m, n), F32)]
    out_specs = [pl.BlockSpec((tm, n), lambda i: (i, 0))]
    args = [x, g, w]
    if wt is not None:
        in_specs.append(pl.BlockSpec(wt.shape, lambda i: (0, 0)))
        out_shape.append(jax.ShapeDtypeStruct((wt.shape[0], m), F32))
        out_specs.append(pl.BlockSpec((wt.shape[0], tm), lambda i: (0, i)))
        args.append(wt)
    res = pl.pallas_call(
        functools.partial(_norm_matmul_kernel, has_t=wt is not None),
        grid=(m // tm,), in_specs=in_specs, out_specs=out_specs, out_shape=out_shape,
        compiler_params=_cp("parallel"), name=name)(*args)
    return res if wt is not None else res[0]


def _matmul_norm_res_kernel(*refs, widths):
    n_a = len(widths)
    a_refs = refs[:n_a]
    w_ref, g_ref, r_ref, o_ref = refs[n_a:]
    y = None
    off = 0
    for a_ref, k in zip(a_refs, widths):
        part = _dot(a_ref[...].astype(BF16), w_ref[off:off + k, :])
        y = part if y is None else y + part
        off += k
    o_ref[...] = r_ref[...] + _rms(y, g_ref[...])


def _matmul_norm_res(a_list, w, g, resid, *, tm, name):
    m, d = resid.shape
    widths = tuple(a.shape[1] for a in a_list)
    in_specs = [pl.BlockSpec((tm, k), lambda i: (i, 0)) for k in widths]
    in_specs += [pl.BlockSpec(w.shape, lambda i: (0, 0)),
                 pl.BlockSpec((1, d), lambda i: (0, 0)),
                 pl.BlockSpec((tm, d), lambda i: (i, 0))]
    return pl.pallas_call(
        functools.partial(_matmul_norm_res_kernel, widths=widths),
        grid=(m // tm,), in_specs=in_specs,
        out_specs=pl.BlockSpec((tm, d), lambda i: (i, 0)),
        out_shape=jax.ShapeDtypeStruct((m, d), F32),
        compiler_params=_cp("parallel"), name=name)(*a_list, w, g, resid)


def _lru_gates(xc, wa_ref, ba_ref, wi_ref, bi_ref, lam_ref):
    xb = xc.astype(BF16)
    ra, ia = [], []
    for c in range(LRU_W // LANES):
        xs = xb[:, c * LANES:(c + 1) * LANES]
        ra.append(_dot(xs, wa_ref[c]))
        ia.append(_dot(xs, wi_ref[c]))
    r = _sigmoid(jnp.concatenate(ra, axis=1) + ba_ref[...])
    ig = _sigmoid(jnp.concatenate(ia, axis=1) + bi_ref[...])
    log_a = -LRU_C * r * _softplus(-lam_ref[...])
    a = jnp.exp(log_a)
    bx = jnp.sqrt(-jnp.tanh(log_a) * (a * a + 1.0)) * (ig * xc)
    return a, bx


def _lru_prompt_kernel(xl_ref, gate_ref, cw_ref, cb_ref, wa_ref, ba_ref, wi_ref, bi_ref, lam_ref,
                       out_ref, hlast_ref, ext_ref, hc_ref, *, tc, n_t):
    t = pl.program_id(1)

    @pl.when(t == 0)
    def _():
        ext_ref[0:8, :] = jnp.zeros((8, LRU_W), F32)
        hc_ref[...] = jnp.zeros((1, LRU_W), F32)

    xl = xl_ref[...]
    ext_ref[8:8 + tc, :] = xl
    xc = cb_ref[...] + cw_ref[LRU_CONV - 1:LRU_CONV, :] * xl
    for j in range(1, LRU_CONV):
        xc = xc + cw_ref[LRU_CONV - 1 - j:LRU_CONV - j, :] * ext_ref[pl.ds(8 - j, tc), :]
    ext_ref[0:8, :] = ext_ref[tc:tc + 8, :]

    a, b = _lru_gates(xc, wa_ref, ba_ref, wi_ref, bi_ref, lam_ref)
    row = lax.broadcasted_iota(jnp.int32, (tc, 1), 0)
    d = 1
    while d < tc:
        keep = row >= d
        a_sh = jnp.where(keep, pltpu.roll(a, d, 0), 1.0)
        b_sh = jnp.where(keep, pltpu.roll(b, d, 0), 0.0)
        b = a * b_sh + b
        a = a * a_sh
        d *= 2
    h = a * hc_ref[...] + b
    hc_ref[...] = h[tc - 1:tc, :]
    out_ref[...] = (_gelu_tanh(gate_ref[...]) * h).astype(BF16)

    @pl.when(t == n_t - 1)
    def _():
        hlast_ref[...] = h[tc - 1:tc, :]


def _lru_prompt(z, n_b, seq, cw, cb, wa, ba, wi, bi, lam):
    tc = SCAN_TILE
    n_t = seq // tc
    vec = lambda: pl.BlockSpec((1, LRU_W), lambda b, t: (0, 0))
    bd = lambda: pl.BlockSpec((LRU_W // LANES, LANES, LANES), lambda b, t: (0, 0, 0))
    return pl.pallas_call(
        functools.partial(_lru_prompt_kernel, tc=tc, n_t=n_t),
        grid=(n_b, n_t),
        in_specs=[pl.BlockSpec((tc, LRU_W), lambda b, t: (b * n_t + t, 0)),
                  pl.BlockSpec((tc, LRU_W), lambda b, t: (b * n_t + t, 1)),
                  pl.BlockSpec((LRU_CONV, LRU_W), lambda b, t: (0, 0)),
                  vec(), bd(), vec(), bd(), vec(), vec()],
        out_specs=[pl.BlockSpec((tc, LRU_W), lambda b, t: (b * n_t + t, 0)),
                   pl.BlockSpec((None, 1, LRU_W), lambda b, t: (b, 0, 0))],
        out_shape=[jax.ShapeDtypeStruct((n_b * seq, LRU_W), BF16),
                   jax.ShapeDtypeStruct((n_b, 1, LRU_W), F32)],
        scratch_shapes=[pltpu.VMEM((tc + 8, LRU_W), F32), pltpu.VMEM((1, LRU_W), F32)],
        compiler_params=_cp("parallel", "arbitrary"), name="lru_prompt")(
            z, z, cw, cb, wa, ba, wi, bi, lam)


def _lru_sample_kernel(z_ref, pre_ref, h0_ref, cw_ref, cb_ref, wa_ref, ba_ref, wi_ref, bi_ref, lam_ref,
                       out_ref, h_ref):
    xl = z_ref[:, 0:LRU_W]
    gate = z_ref[:, LRU_W:2 * LRU_W]
    xc = cb_ref[...] + cw_ref[LRU_CONV - 1:LRU_CONV, :] * xl
    for k in range(LRU_CONV - 1):
        xc = xc + cw_ref[k:k + 1, :] * pre_ref[k]
    a, bx = _lru_gates(xc, wa_ref, ba_ref, wi_ref, bi_ref, lam_ref)
    h = a * h0_ref[...] + bx
    h_ref[...] = h
    out_ref[...] = (_gelu_tanh(gate) * h).astype(BF16)


def _lru_sample(z, prefix_t, h0, cw, cb, wa, ba, wi, bi, lam):
    n_b = z.shape[0]
    return pl.pallas_call(
        _lru_sample_kernel,
        out_shape=[jax.ShapeDtypeStruct((n_b, LRU_W), BF16), jax.ShapeDtypeStruct((n_b, LRU_W), F32)],
        compiler_params=pltpu.CompilerParams(vmem_limit_bytes=VMEM_LIMIT),
        name="lru_sample")(z, prefix_t, h0, cw, cb, wa, ba, wi, bi, lam)


def _log_sigmoid(x):
    return jnp.minimum(x, 0.0) - jnp.log1p(jnp.exp(-jnp.abs(x)))


def _lane_cumsum(x):
    n = x.shape[1]
    lane = lax.broadcasted_iota(jnp.int32, x.shape, 1)
    d = 1
    while d < n:
        x = x + jnp.where(lane >= d, pltpu.roll(x, d, 1), 0.0)
        d *= 2
    return x


def _fox_prep_kernel(fl_ref, bf_ref, lf_ref, c_ref):
    lf = _log_sigmoid(fl_ref[0:FOX_H, :] + bf_ref[...])
    lf_ref[...] = lf
    c_ref[...] = _lane_cumsum(lf)


def _fox_prep(fl_t, bf, n_b, seq):
    return pl.pallas_call(
        _fox_prep_kernel, grid=(n_b,),
        in_specs=[pl.BlockSpec((fl_t.shape[0], seq), lambda b: (0, b)),
                  pl.BlockSpec((FOX_H, 1), lambda b: (0, 0))],
        out_specs=[pl.BlockSpec((FOX_H, seq), lambda b: (0, b)),
                   pl.BlockSpec((FOX_H, seq), lambda b: (0, b))],
        out_shape=[jax.ShapeDtypeStruct((FOX_H, n_b * seq), F32)] * 2,
        compiler_params=_cp("parallel"), name="fox_prep")(fl_t, bf)


def _logsig_kernel(fl_ref, bf_ref, lf_ref):
    lf_ref[...] = _log_sigmoid(fl_ref[0:FOX_H, :] + bf_ref[...])


def _logsig(fl_t, bf):
    return pl.pallas_call(
        _logsig_kernel, out_shape=jax.ShapeDtypeStruct((FOX_H, fl_t.shape[1]), F32),
        name="fox_logf_sample")(fl_t, bf)


def _fox_prompt_kernel(qi_ref, ki_ref, q_ref, k_ref, v_ref, cq_ref, ck_ref, o_ref,
                       m_sc, l_sc, acc_sc, s_sc, p_sc, al_sc, *, tb, hg):
    grp = pl.program_id(1)
    t = pl.program_id(2)
    qi = qi_ref[t]
    ki = ki_ref[t]
    width = hg * FOX_HD

    @pl.when(ki == 0)
    def _():
        m_sc[...] = jnp.full(m_sc.shape, NEG, F32)
        l_sc[...] = jnp.zeros(l_sc.shape, F32)
        acc_sc[...] = jnp.zeros(acc_sc.shape, F32)

    def step(diagonal):
        q = q_ref[...] * (FOX_HD ** -0.5)
        k = k_ref[...].astype(BF16)
        v = v_ref[...].astype(BF16)
        lane_head = jnp.right_shift(lax.broadcasted_iota(jnp.int32, (tb, width), 1),
                                    FOX_HD.bit_length() - 1)
        rc = ATT_ROWS
        if diagonal:
            row_i = lax.broadcasted_iota(jnp.int32, (rc, tb), 0)
            col_i = lax.broadcasted_iota(jnp.int32, (rc, tb), 1)
        n_lt = tb // LANES
        for h in range(hg):
            s_sc[h] = _dot_nt(jnp.where(lane_head == h, q, 0.0).astype(BF16), k)
            c_q = cq_ref[pl.ds(hg * grp + h, 1), :]
            c_k = ck_ref[pl.ds(hg * grp + h, 1), :]
            bias = c_q[:, 0:1] - c_k
            for c in range(tb // rc):
                rows = slice(c * rc, (c + 1) * rc)
                s = s_sc[h, rows, :] + bias
                if diagonal:
                    s = jnp.where(col_i <= row_i + c * rc, s, NEG)
                s_sc[h, rows, :] = s
                m_prev = m_sc[h, rows, :]
                m_new = jnp.maximum(m_prev, jnp.max(s, axis=1, keepdims=True))
                m_sc[h, rows, :] = m_new
                al_sc[h, rows, :] = jnp.exp(m_prev - m_new)
            for c in range(tb // rc):
                rows = slice(c * rc, (c + 1) * rc)
                m_new = m_sc[h, rows, :]
                p_sum = None
                for j in range(n_lt):
                    cols = slice(j * LANES, (j + 1) * LANES)
                    p = jnp.exp(s_sc[h, rows, cols] - m_new)
                    p_sc[h, rows, cols] = p.astype(BF16)
                    p_sum = p if p_sum is None else p_sum + p
                l_sc[h, rows, :] = al_sc[h, rows, :] * l_sc[h, rows, :] + p_sum
            lt = (h * FOX_HD) // LANES
            pv = _dot(p_sc[h], v)[:, lt * LANES:(lt + 1) * LANES]
            acc_sc[h] = al_sc[h] * acc_sc[h] + pv

    @pl.when(ki < qi)
    def _():
        step(False)

    @pl.when(ki == qi)
    def _():
        step(True)
        lane = lax.broadcasted_iota(jnp.int32, (tb, LANES), 1)
        heads_per_tile = LANES // FOX_HD
        for lt in range(width // LANES):
            o = None
            for i in range(heads_per_tile):
                h = lt * heads_per_tile + i
                o_h = acc_sc[h] / jnp.sum(l_sc[h], axis=1, keepdims=True)
                o = o_h if o is None else jnp.where(lane < i * FOX_HD, o, o_h)
            o_ref[:, lt * LANES:(lt + 1) * LANES] = o.astype(BF16)


def _fox_prompt(z, c_t, n_b, seq):
    tb = SEQ_TILE
    hg = ATT_HEADS
    width = hg * FOX_HD
    n_q = seq // tb
    q_blk = (2 * LRU_W) // width
    k_blk = q_blk + FOX_W // width
    v_blk = k_blk + FOX_W // width
    pairs = [(qi, ki) for qi in range(n_q) for ki in range(qi + 1)]
    qi_list = jnp.asarray([p[0] for p in pairs], jnp.int32)
    ki_list = jnp.asarray([p[1] for p in pairs], jnp.int32)
    grid_spec = pltpu.PrefetchScalarGridSpec(
        num_scalar_prefetch=2, grid=(n_b, FOX_H // hg, len(pairs)),
        in_specs=[
            pl.BlockSpec((tb, width), lambda b, g, t, qi, ki: (b * n_q + qi[t], q_blk + g)),
            pl.BlockSpec((tb, width), lambda b, g, t, qi, ki: (b * n_q + ki[t], k_blk + g)),
            pl.BlockSpec((tb, width), lambda b, g, t, qi, ki: (b * n_q + ki[t], v_blk + g)),
            pl.BlockSpec((FOX_H, tb), lambda b, g, t, qi, ki: (0, b * n_q + qi[t])),
            pl.BlockSpec((FOX_H, tb), lambda b, g, t, qi, ki: (0, b * n_q + ki[t])),
        ],
        out_specs=pl.BlockSpec((tb, width), lambda b, g, t, qi, ki: (b * n_q + qi[t], g)),
        scratch_shapes=[pltpu.VMEM((hg, tb, LANES), F32), pltpu.VMEM((hg, tb, LANES), F32),
                        pltpu.VMEM((hg, tb, LANES), F32), pltpu.VMEM((hg, tb, tb), F32),
                        pltpu.VMEM((hg, tb, tb), BF16), pltpu.VMEM((hg, tb, LANES), F32)])
    return pl.pallas_call(
        functools.partial(_fox_prompt_kernel, tb=tb, hg=hg), grid_spec=grid_spec,
        out_shape=jax.ShapeDtypeStruct((n_b * seq, FOX_W), BF16),
        compiler_params=_cp("parallel", "parallel", "arbitrary"),
        name="fox_prompt")(qi_list, ki_list, z, z, z, c_t, c_t)


def _fox_sample_kernel(pt_ref, q_ref, kn_ref, vn_ref, lfn_ref, *rest, n_pg, n_g):
    k_refs = rest[0:n_pg]
    v_refs = rest[n_pg:2 * n_pg]
    lf_refs = rest[2 * n_pg:3 * n_pg]
    o_ref, m_sc, l_sc, acc_sc, cc_sc = rest[3 * n_pg:]
    g = pl.program_id(1)
    page = lf_refs[0].shape[1]

    @pl.when(g == 0)
    def _():
        m_sc[...] = jnp.full(m_sc.shape, NEG, F32)
        l_sc[...] = jnp.zeros(l_sc.shape, F32)
        acc_sc[...] = jnp.zeros(acc_sc.shape, F32)
        cc_sc[...] = jnp.zeros(cc_sc.shape, F32)


    q = q_ref[...] * (FOX_HD ** -0.5)
    q16 = jnp.concatenate([q, jnp.zeros_like(q)], axis=0)
    row_q = lax.broadcasted_iota(jnp.int32, (2 * FOX_H, FOX_HD), 0)
    q_only = [jnp.where(row_q == h, q16, 0.0).astype(BF16) for h in range(FOX_H)]

    s_parts = []
    for j in range(n_pg):
        s_j = None
        for h in range(FOX_H):
            part = _dot(q_only[h], k_refs[j][h].astype(BF16))
            s_j = part if s_j is None else s_j + part
        s_parts.append(s_j[0:FOX_H])
    s = jnp.concatenate(s_parts, axis=1)
    lf = jnp.concatenate([lf_refs[j][...] for j in range(n_pg)], axis=1)
    c = _lane_cumsum(lf) + cc_sc[...]
    cc_sc[...] = c[:, c.shape[1] - 1:]
    s = s - c
    m_prev = m_sc[...]
    m_new = jnp.maximum(m_prev, jnp.max(s, axis=1, keepdims=True))
    alpha = jnp.exp(m_prev - m_new)
    p = jnp.exp(s - m_new)
    l_sc[...] = alpha * l_sc[...] + jnp.sum(p, axis=1, keepdims=True)
    p16 = jnp.concatenate([p, jnp.zeros_like(p)], axis=0)
    row_p = lax.broadcasted_iota(jnp.int32, (2 * FOX_H, page), 0)
    pv = None
    for j in range(n_pg):
        p_j = p16[:, j * page:(j + 1) * page]
        for h in range(FOX_H):
            part = _dot_nt(jnp.where(row_p == h, p_j, 0.0).astype(BF16), v_refs[j][h].astype(BF16))
            pv = part if pv is None else pv + part
    acc_sc[...] = alpha * acc_sc[...] + pv[0:FOX_H]
    m_sc[...] = m_new

    @pl.when(g == n_g - 1)
    def _():
        s_n = jnp.sum(q * kn_ref[...], axis=1, keepdims=True)
        s_n = s_n - (cc_sc[...] + lfn_ref[...])
        m_p = m_sc[...]
        m_n = jnp.maximum(m_p, s_n)
        al = jnp.exp(m_p - m_n)
        p_n = jnp.exp(s_n - m_n)
        l_n = al * l_sc[...] + p_n
        o_ref[...] = ((al * acc_sc[...] + p_n * vn_ref[...]) / l_n).astype(BF16)


def _fox_sample(page_table, layer, q, k_new, v_new, lf_new, cache_k, cache_v, cache_lf_t):
    n_b, n_pages = page_table.shape
    n_pg = PAGES_PER_STEP
    n_g = n_pages // n_pg
    page = cache_k.shape[4]
    head_spec = lambda: pl.BlockSpec((None, FOX_H, FOX_HD), lambda b, g, pt: (b, 0, 0))

    def kv_spec(j):
        return pl.BlockSpec((None, None, FOX_H, FOX_HD, page),
                            lambda b, g, pt, j=j: (layer, pt[b * n_pages + g * n_pg + j], 0, 0, 0))

    def lf_spec(j):
        return pl.BlockSpec((None, FOX_H, page),
                            lambda b, g, pt, j=j: (pt[b * n_pages + g * n_pg + j], 0, 0))

    in_specs = [head_spec(), head_spec(), head_spec(),
                pl.BlockSpec((None, FOX_H, 1), lambda b, g, pt: (b, 0, 0))]
    in_specs += [kv_spec(j) for j in range(n_pg)]
    in_specs += [kv_spec(j) for j in range(n_pg)]
    in_specs += [lf_spec(j) for j in range(n_pg)]
    grid_spec = pltpu.PrefetchScalarGridSpec(
        num_scalar_prefetch=1, grid=(n_b, n_g), in_specs=in_specs,
        out_specs=pl.BlockSpec((None, FOX_H, FOX_HD), lambda b, g, pt: (b, 0, 0)),
        scratch_shapes=[pltpu.VMEM((FOX_H, 1), F32), pltpu.VMEM((FOX_H, 1), F32),
                        pltpu.VMEM((FOX_H, FOX_HD), F32), pltpu.VMEM((FOX_H, 1), F32)])
    return pl.pallas_call(
        functools.partial(_fox_sample_kernel, n_pg=n_pg, n_g=n_g),
        grid_spec=grid_spec,
        out_shape=jax.ShapeDtypeStruct((n_b, FOX_H, FOX_HD), BF16),
        compiler_params=_cp("parallel", "arbitrary"), name="fox_sample")(
            page_table.reshape(-1), q, k_new, v_new, lf_new,
            *([cache_k] * n_pg), *([cache_v] * n_pg), *([cache_lf_t] * n_pg))


def _xattn_prompt_kernel(q_ref, mk_ref, mv_ref, o_ref):
    outs = []
    for h in range(MEM_H):
        cs = slice(h * MEM_HD, (h + 1) * MEM_HD)
        qh = q_ref[:, cs].astype(BF16)
        s = _dot_nt(qh, mk_ref[:, cs].astype(BF16)) * (MEM_HD ** -0.5)
        p = jnp.exp(s - jnp.max(s, axis=1, keepdims=True))
        l = jnp.sum(p, axis=1, keepdims=True)
        outs.append(_dot(p.astype(BF16), mv_ref[:, cs].astype(BF16)) / l)
    o_ref[...] = jnp.concatenate(outs, axis=1).astype(BF16)


def _xattn_prompt(q, memkv, n_b, seq):
    tb = SEQ_TILE
    n_t = seq // tb
    mem_len = memkv.shape[0] // n_b
    return pl.pallas_call(
        _xattn_prompt_kernel, grid=(n_b, n_t),
        in_specs=[pl.BlockSpec((tb, MEM_W), lambda b, t: (b * n_t + t, 0)),
                  pl.BlockSpec((mem_len, MEM_W), lambda b, t: (b, 0)),
                  pl.BlockSpec((mem_len, MEM_W), lambda b, t: (b, 1))],
        out_specs=pl.BlockSpec((tb, MEM_W), lambda b, t: (b * n_t + t, 0)),
        out_shape=jax.ShapeDtypeStruct((n_b * seq, MEM_W), BF16),
        compiler_params=_cp("parallel", "parallel"), name="xattn_prompt")(q, memkv, memkv)


def _xattn_sample_kernel(q_ref, mk_ref, mv_ref, o_ref):
    q = q_ref[...]
    row = lax.broadcasted_iota(jnp.int32, (16, MEM_W), 0)
    col = lax.broadcasted_iota(jnp.int32, (16, MEM_W), 1)
    head_cols = jnp.right_shift(col, MEM_HD.bit_length() - 1) == row
    q_rows = jnp.where(head_cols, q, 0.0).astype(BF16)
    mem_len = mk_ref.shape[0] // MEM_H

    def heads_on_lanes(ref):
        return jnp.concatenate([ref[pl.ds(h, mem_len, stride=MEM_H), :] for h in range(MEM_H)],
                               axis=1).astype(BF16)

    s = _dot_nt(q_rows, heads_on_lanes(mk_ref)) * (MEM_HD ** -0.5)
    p = jnp.exp(s - jnp.max(s, axis=1, keepdims=True))
    l = jnp.sum(p, axis=1, keepdims=True)
    o = _dot(p.astype(BF16), heads_on_lanes(mv_ref)) / l
    o_ref[...] = jnp.sum(jnp.where(head_cols, o, 0.0), axis=0, keepdims=True).astype(BF16)


def _xattn_sample(q, layer, mk, mv):
    n_b, rows = mk.shape[1], mk.shape[2]
    return pl.pallas_call(
        _xattn_sample_kernel, grid=(n_b,),
        in_specs=[pl.BlockSpec((None, 1, MEM_W), lambda b: (b, 0, 0)),
                  pl.BlockSpec((None, None, rows, MEM_HD), lambda b: (layer, b, 0, 0)),
                  pl.BlockSpec((None, None, rows, MEM_HD), lambda b: (layer, b, 0, 0))],
        out_specs=pl.BlockSpec((None, 1, MEM_W), lambda b: (b, 0, 0)),
        out_shape=jax.ShapeDtypeStruct((n_b, 1, MEM_W), BF16),
        compiler_params=_cp("parallel"), name="xattn_sample")(q, mk, mv)


def _swiglu_accumulate(x_sc, w1_ref, w3_ref, w2_ref, acc_sc):
    tm = x_sc.shape[0]
    slab = min(tm, FF_ROW_SLAB)
    for r in range(tm // slab):
        rows = slice(r * slab, (r + 1) * slab)
        x = x_sc[rows, :]
        h1 = _dot(x, w1_ref[...])
        h3 = _dot(x, w3_ref[...])
        hh = (h1 * _sigmoid(h1) * h3).astype(BF16)
        acc_sc[rows, :] += _dot(hh, w2_ref[...])


def _ffn_kernel(x_ref, g4_ref, g5_ref, w1_ref, w3_ref, w2_ref, o_ref, hn_sc, acc_sc, *, n_f):
    f = pl.program_id(1)

    @pl.when(f == 0)
    def _():
        hn_sc[...] = _rms(x_ref[...], g4_ref[...]).astype(BF16)
        acc_sc[...] = jnp.zeros(acc_sc.shape, F32)

    _swiglu_accumulate(hn_sc, w1_ref, w3_ref, w2_ref, acc_sc)

    @pl.when(f == n_f - 1)
    def _():
        o_ref[...] = x_ref[...] + _rms(acc_sc[...], g5_ref[...])


def _ffn(x, g4, g5, w1, w3, w2, *, tm):
    m, d = x.shape
    ff = w1.shape[1]
    tf = FFN_FF_TILE
    n_f = ff // tf
    return pl.pallas_call(
        functools.partial(_ffn_kernel, n_f=n_f), grid=(m // tm, n_f),
        in_specs=[pl.BlockSpec((tm, d), lambda i, f: (i, 0)),
                  pl.BlockSpec((1, d), lambda i, f: (0, 0)),
                  pl.BlockSpec((1, d), lambda i, f: (0, 0)),
                  pl.BlockSpec((d, tf), lambda i, f: (0, f)),
                  pl.BlockSpec((d, tf), lambda i, f: (0, f)),
                  pl.BlockSpec((tf, d), lambda i, f: (f, 0))],
        out_specs=pl.BlockSpec((tm, d), lambda i, f: (i, 0)),
        out_shape=jax.ShapeDtypeStruct((m, d), F32),
        scratch_shapes=[pltpu.VMEM((tm, d), BF16), pltpu.VMEM((tm, d), F32)],
        compiler_params=_cp("parallel", "arbitrary"), name="ffn")(x, g4, g5, w1, w3, w2)


def _layernorm_silu(x, g, b):
    mu = jnp.mean(x, axis=-1, keepdims=True)
    xc = x - mu
    y = xc * lax.rsqrt(jnp.mean(xc * xc, axis=-1, keepdims=True) + EPS) * g + b
    return y * _sigmoid(y)


def _odd_prompt_kernel(a_ref, gt_ref, up_ref, dww_ref, dwb_ref, lng_ref, lnb_ref, pw_ref, ps_ref,
                       cv_ref, pool_ref, cbuf_ref, pbuf_ref, eg_ref, eu_ref, sh_ref, *, tc, n_t):
    t = pl.program_id(1)
    halo_g, halo_u = 32, 16

    @pl.when(t == 0)
    def _():
        eg_ref[0:halo_g, :] = jnp.zeros((halo_g, CONV_W), F32)
        eu_ref[0:halo_u, :] = jnp.zeros((halo_u, POOL_W), F32)

    glu = a_ref[...] * _sigmoid(gt_ref[...])
    up = up_ref[...]
    eg_ref[halo_g:halo_g + tc, :] = glu
    eu_ref[halo_u:halo_u + tc, :] = up

    ext = eg_ref[...]
    n_ext = tc + halo_g
    sh_ref[0] = ext
    for s in range(1, 8):
        sh_ref[s] = pltpu.roll(ext, n_ext - s, 0)
    rc = CONV_ROWS
    for c in range(tc // rc):
        parts = []
        for j in range(CONV_W // LANES):
            cols = slice(j * LANES, (j + 1) * LANES)
            acc = jnp.broadcast_to(dwb_ref[:, cols], (rc, LANES))
            for k in range(CONV_K):
                off = k + halo_g - (CONV_K - 1)
                acc = acc + dww_ref[k:k + 1, cols] * sh_ref[off % 8, pl.ds(c * rc + (off // 8) * 8, rc), cols]
            parts.append(acc)
        cv_ref[c * rc:(c + 1) * rc, :] = _layernorm_silu(
            jnp.concatenate(parts, axis=1), lng_ref[...], lnb_ref[...]).astype(BF16)

    pos = t * tc + lax.broadcasted_iota(jnp.int32, (tc, 1), 0)
    outs = []
    for gi, w in enumerate(POOL_WINDOWS):
        cs = slice(gi * POOL_GW, (gi + 1) * POOL_GW)
        u_g = up[:, cs]
        win = u_g
        for j in range(1, w):
            win = win + eu_ref[pl.ds(halo_u - j, tc), cs]
        cnt = jnp.minimum(pos + 1, w).astype(F32)
        dlt = win / cnt - u_g
        outs.append(_dot(dlt.astype(BF16), pw_ref[gi]))
    pool_ref[...] = (jnp.concatenate(outs, axis=1) * ps_ref[...]).astype(BF16)

    @pl.when(t == n_t - 1)
    def _():
        cbuf_ref[...] = eg_ref[pl.ds(halo_g + tc - (CONV_K - 1), CONV_K - 1), :]
        pbuf_ref[...] = eu_ref[pl.ds(halo_u + tc - POOL_BUF, POOL_BUF), :]

    eg_ref[0:halo_g, :] = eg_ref[tc:tc + halo_g, :]
    eu_ref[0:halo_u, :] = eu_ref[tc:tc + halo_u, :]


def _odd_prompt(z, n_b, seq, dww, dwb, lng, lnb, pw, ps):
    tc = SEQ_TILE
    n_t = seq // tc
    vec = lambda: pl.BlockSpec((1, CONV_W), lambda b, t: (0, 0))
    return pl.pallas_call(
        functools.partial(_odd_prompt_kernel, tc=tc, n_t=n_t), grid=(n_b, n_t),
        in_specs=[pl.BlockSpec((tc, CONV_W), lambda b, t: (b * n_t + t, 0)),
                  pl.BlockSpec((tc, CONV_W), lambda b, t: (b * n_t + t, 1)),
                  pl.BlockSpec((tc, POOL_W), lambda b, t: (b * n_t + t, 2)),
                  pl.BlockSpec((CONV_K, CONV_W), lambda b, t: (0, 0)),
                  vec(), vec(), vec(),
                  pl.BlockSpec((len(POOL_WINDOWS), POOL_GW, POOL_GW), lambda b, t: (0, 0, 0)),
                  vec()],
        out_specs=[pl.BlockSpec((tc, CONV_W), lambda b, t: (b * n_t + t, 0)),
                   pl.BlockSpec((tc, POOL_W), lambda b, t: (b * n_t + t, 0)),
                   pl.BlockSpec((None, CONV_K - 1, CONV_W), lambda b, t: (b, 0, 0)),
                   pl.BlockSpec((None, POOL_BUF, POOL_W), lambda b, t: (b, 0, 0))],
        out_shape=[jax.ShapeDtypeStruct((n_b * seq, CONV_W), BF16),
                   jax.ShapeDtypeStruct((n_b * seq, POOL_W), BF16),
                   jax.ShapeDtypeStruct((n_b, CONV_K - 1, CONV_W), F32),
                   jax.ShapeDtypeStruct((n_b, POOL_BUF, POOL_W), F32)],
        scratch_shapes=[pltpu.VMEM((tc + 32, CONV_W), F32), pltpu.VMEM((tc + 16, POOL_W), F32),
                        pltpu.VMEM((8, tc + 32, CONV_W), F32)],
        compiler_params=_cp("parallel", "arbitrary"), name="odd_prompt")(
            z, z, z, dww, dwb, lng, lnb, pw, ps)


def _odd_sample_kernel(z_ref, cbuf_ref, pbuf_ref, dww_ref, dwb_ref, lng_ref, lnb_ref, pw_ref, ps_ref,
                       cv_ref, pool_ref, glu_ref, *, pos0):
    glu = z_ref[:, 0:CONV_W] * _sigmoid(z_ref[:, CONV_W:2 * CONV_W])
    up = z_ref[:, 2 * CONV_W:2 * CONV_W + POOL_W]
    glu_ref[...] = glu
    acc = dwb_ref[...] + dww_ref[CONV_K - 1:CONV_K, :] * glu
    for k in range(CONV_K - 1):
        acc = acc + dww_ref[k:k + 1, :] * cbuf_ref[k]
    cv_ref[...] = _layernorm_silu(acc, lng_ref[...], lnb_ref[...]).astype(BF16)
    outs = []
    for gi, w in enumerate(POOL_WINDOWS):
        cs = slice(gi * POOL_GW, (gi + 1) * POOL_GW)
        u_g = up[:, cs]
        win = u_g
        for j in range(1, w):
            win = win + pbuf_ref[POOL_BUF - j][:, cs]
        dlt = win / float(min(pos0 + 1, w)) - u_g
        outs.append(_dot(dlt.astype(BF16), pw_ref[gi]))
    pool_ref[...] = (jnp.concatenate(outs, axis=1) * ps_ref[...]).astype(BF16)


def _odd_sample(z, cbuf_t, pbuf_t, dww, dwb, lng, lnb, pw, ps, pos0):
    n_b = z.shape[0]
    return pl.pallas_call(
        functools.partial(_odd_sample_kernel, pos0=pos0),
        out_shape=[jax.ShapeDtypeStruct((n_b, CONV_W), BF16),
                   jax.ShapeDtypeStruct((n_b, POOL_W), BF16),
                   jax.ShapeDtypeStruct((n_b, CONV_W), F32)],
        compiler_params=pltpu.CompilerParams(vmem_limit_bytes=VMEM_LIMIT),
        name="odd_sample")(z, cbuf_t, pbuf_t, dww, dwb, lng, lnb, pw, ps)


def _store_row_tiles(ref, x):
    rows, width = x.shape
    n = width // LANES
    for j in range(n):
        ref[pl.ds(j, rows, stride=n), :] = x[:, j * LANES:(j + 1) * LANES]


def _load_row_tiles(ref, n):
    rows = ref.shape[0] // n
    return jnp.concatenate([ref[pl.ds(j, rows, stride=n), :] for j in range(n)], axis=1)


def _router_kernel(x_ref, g_ref, whi_ref, wlo_ref, b_ref, hn_ref, idx_ref, gate_ref):
    hn = _rms(x_ref[...], g_ref[...])
    hb = hn.astype(BF16)
    _store_row_tiles(hn_ref, hn)
    hlo = (hn - hb.astype(F32)).astype(BF16)
    logits = _dot(hb, whi_ref[...]) + (_dot(hb, wlo_ref[...]) + _dot(hlo, whi_ref[...])) + b_ref[...]
    lane = lax.broadcasted_iota(jnp.int32, logits.shape, 1)
    lane_f = lane.astype(F32)
    logits = jnp.where(lane < N_EXPERTS, logits, NEG)
    m1 = jnp.max(logits, axis=1, keepdims=True)
    i1 = jnp.min(jnp.where(logits == m1, lane_f, float(LANES)), axis=1, keepdims=True)
    rest = jnp.where(lane_f == i1, NEG, logits)
    m2 = jnp.max(rest, axis=1, keepdims=True)
    i2 = jnp.min(jnp.where(rest == m2, lane_f, float(LANES)), axis=1, keepdims=True)
    e = jnp.exp(m2 - m1)
    g1 = 1.0 / (1.0 + e)
    g2 = e / (1.0 + e)
    idx_ref[...] = jnp.where(lane == 0, i1, jnp.where(lane == 1, i2, 0.0)).astype(jnp.int32)
    gate_ref[...] = jnp.where(lane == 0, g1, jnp.where(lane == 1, g2, 0.0))


def _router(x, g, whi, wlo, b, *, tm):
    m, d = x.shape
    return pl.pallas_call(
        _router_kernel, grid=(m // tm,),
        in_specs=[pl.BlockSpec((tm, d), lambda i: (i, 0)),
                  pl.BlockSpec((1, d), lambda i: (0, 0)),
                  pl.BlockSpec((d, LANES), lambda i: (0, 0)),
                  pl.BlockSpec((d, LANES), lambda i: (0, 0)),
                  pl.BlockSpec((1, LANES), lambda i: (0, 0))],
        out_specs=[pl.BlockSpec((tm * (d // LANES), LANES), lambda i: (i, 0)),
                   pl.BlockSpec((tm, LANES), lambda i: (i, 0)),
                   pl.BlockSpec((tm, LANES), lambda i: (i, 0))],
        out_shape=[jax.ShapeDtypeStruct((m * (d // LANES), LANES), F32),
                   jax.ShapeDtypeStruct((m, LANES), jnp.int32),
                   jax.ShapeDtypeStruct((m, LANES), F32)],
        compiler_params=_cp("parallel"), name="router")(x, g, whi, wlo, b)


def _gmm_kernel(te_ref, tv_ref, x_ref, gate_ref, w1_ref, w3_ref, w2_ref, *rest, n_f, tile0):
    o_ref, acc_sc, xb_sc = rest[-3:]
    t = tile0 + pl.program_id(0)
    f = pl.program_id(1)

    @pl.when(f == 0)
    def _():
        acc_sc[...] = jnp.zeros(acc_sc.shape, F32)
        xb_sc[...] = _load_row_tiles(x_ref, xb_sc.shape[1] // LANES).astype(BF16)

    @pl.when(tv_ref[t] != 0)
    def _():
        _swiglu_accumulate(xb_sc, w1_ref, w3_ref, w2_ref, acc_sc)

    @pl.when(f == n_f - 1)
    def _():
        _store_row_tiles(o_ref, acc_sc[...] * gate_ref[...])


def _gmm(tile_expert, tile_valid, x_part, gate_sorted, w1, w3, w2, *, tile0, y_prev=None):
    rows = gate_sorted.shape[0]
    d, ff = w1.shape[1], w1.shape[2]
    tm, tf = MOE_TILE, MOE_FF_TILE
    n_f = ff // tf
    n_lt = d // LANES
    in_specs = [pl.BlockSpec((tm * n_lt, LANES), lambda t, f, te, tv: (t, 0)),
                pl.BlockSpec((tm, 1), lambda t, f, te, tv: (tile0 + t, 0)),
                pl.BlockSpec((None, d, tf), lambda t, f, te, tv: (te[tile0 + t], 0, f)),
                pl.BlockSpec((None, d, tf), lambda t, f, te, tv: (te[tile0 + t], 0, f)),
                pl.BlockSpec((None, tf, d), lambda t, f, te, tv: (te[tile0 + t], f, 0))]
    args = [tile_expert, tile_valid, x_part, gate_sorted, w1, w3, w2]
    aliases = {}
    if y_prev is not None:
        in_specs.append(pl.BlockSpec(memory_space=pl.ANY))
        aliases = {len(args): 0}
        args.append(y_prev)
    grid_spec = pltpu.PrefetchScalarGridSpec(
        num_scalar_prefetch=2, grid=(x_part.shape[0] // (tm * n_lt), n_f), in_specs=in_specs,
        out_specs=pl.BlockSpec((tm * n_lt, LANES), lambda t, f, te, tv: (tile0 + t, 0)),
        scratch_shapes=[pltpu.VMEM((tm, d), F32), pltpu.VMEM((tm, d), BF16)])
    return pl.pallas_call(
        functools.partial(_gmm_kernel, n_f=n_f, tile0=tile0), grid_spec=grid_spec,
        out_shape=jax.ShapeDtypeStruct((rows * n_lt, LANES), F32), input_output_aliases=aliases,
        compiler_params=_cp("parallel", "arbitrary"), name="moe_experts")(*args)


def _combine_kernel(x_ref, y1_ref, y2_ref, g_ref, o_ref):
    n_lt = x_ref.shape[1] // LANES
    y = _load_row_tiles(y1_ref, n_lt) + _load_row_tiles(y2_ref, n_lt)
    o_ref[...] = x_ref[...] + _rms(y, g_ref[...])


def _combine(x, y1, y2, g, *, tm):
    m, d = x.shape
    row = lambda: pl.BlockSpec((tm, d), lambda i: (i, 0))
    tiles = lambda: pl.BlockSpec((tm * (d // LANES), LANES), lambda i: (i, 0))
    return pl.pallas_call(
        _combine_kernel, grid=(m // tm,),
        in_specs=[row(), tiles(), tiles(), pl.BlockSpec((1, d), lambda i: (0, 0))],
        out_specs=row(), out_shape=jax.ShapeDtypeStruct((m, d), F32),
        compiler_params=_cp("parallel"), name="moe_combine")(x, y1, y2, g)


def _moe_block(xs, g4, g5, w_r, b_r, w1, w3, w2):
    pad = LANES - N_EXPERTS
    w_r_p = jnp.pad(w_r, ((0, 0), (0, pad)))
    whi = w_r_p.astype(BF16)
    wlo = (w_r_p - whi.astype(F32)).astype(BF16)
    b_p = jnp.pad(b_r, (0, pad)).reshape(1, LANES)
    hn, idx, gate = [], [], []
    for x in xs:
        h, i, gt = _router(x, g4, whi, wlo, b_p, tm=min(ROW_TILE, x.shape[0]))
        hn.append(h.reshape(x.shape[0], -1, LANES))
        idx.append(i[:, :2])
        gate.append(gt[:, :2])
    hn = jnp.concatenate(hn, axis=0)
    e_flat = jnp.concatenate(idx, axis=0).reshape(-1)
    g_flat = jnp.concatenate(gate, axis=0).reshape(-1)
    n_assign = e_flat.shape[0]
    tm = MOE_TILE
    n_tiles = n_assign // tm + N_EXPERTS
    rows = n_tiles * tm

    onehot = (e_flat[:, None] == jnp.arange(N_EXPERTS, dtype=jnp.int32)[None, :]).astype(jnp.int32)
    rank = jnp.sum((jnp.cumsum(onehot, axis=0) - onehot) * onehot, axis=1)
    cnt = jnp.sum(onehot, axis=0)
    tiles_e = (cnt + tm - 1) // tm
    tile_end = jnp.cumsum(tiles_e)
    row_start = (tile_end - tiles_e) * tm
    dest = row_start[e_flat] + rank
    tile_ids = jnp.arange(n_tiles, dtype=jnp.int32)
    tile_expert = jnp.minimum(jnp.sum((tile_end[None, :] <= tile_ids[:, None]).astype(jnp.int32), axis=1),
                              N_EXPERTS - 1)
    tile_valid = (tile_ids < tile_end[-1]).astype(jnp.int32)
    row_assign = jnp.zeros((rows,), jnp.int32).at[dest].set(jnp.arange(n_assign, dtype=jnp.int32))
    row_ids = jnp.arange(rows, dtype=jnp.int32)
    row_expert = jnp.repeat(tile_expert, tm)
    row_valid = (row_ids - row_start[row_expert]) < cnt[row_expert]
    src_tok = row_assign // 2
    gate_sorted = jnp.where(row_valid, g_flat[row_assign], 0.0).reshape(rows, 1)

    take_rows = lambda a, idx: a.at[idx].get(mode="promise_in_bounds").reshape(-1, LANES)
    y_sorted = None
    part_tiles = n_tiles // MOE_PARTS
    for part in range(MOE_PARTS):
        tile0 = part * part_tiles
        x_part = take_rows(hn, src_tok[tile0 * tm:(tile0 + part_tiles) * tm])
        y_sorted = _gmm(tile_expert, tile_valid, x_part, gate_sorted, w1, w3, w2, tile0=tile0, y_prev=y_sorted)
    y_rows = y_sorted.reshape(rows, -1, LANES)

    outs = []
    off = 0
    dest2 = dest.reshape(-1, 2)
    for x in xs:
        m = x.shape[0]
        d1 = dest2[off:off + m, 0]
        d2 = dest2[off:off + m, 1]
        outs.append(_combine(x, take_rows(y_rows, d1), take_rows(y_rows, d2), g5,
                             tm=min(ROW_TILE, m)))
        off += m
    return outs


def _block_diag_pairs(w):
    nb, bw, _ = w.shape
    w = w.reshape(nb // 2, 2, bw, bw)
    z = jnp.zeros((nb // 2, bw, bw), w.dtype)
    top = jnp.concatenate([w[:, 0], z], axis=2)
    bot = jnp.concatenate([z, w[:, 1]], axis=2)
    return jnp.concatenate([top, bot], axis=1).astype(BF16)


def kernel(x_prompt, x_sample, cache_fox_k, cache_fox_v, cache_fox_logf, state_lru_h, state_lru_conv, state_conv_buf, state_pool_buf, cache_mem_k, cache_mem_v, page_table, mem_prompt, norm_g, w_xq, w_xk, w_xv, w_xo, w_in_e, b_f, lru_conv_w, lru_conv_b, lru_wa, lru_ba, lru_wi, lru_bi, lru_lam, w_out_e, w_ff1, w_ff3, w_ff2, w_in_o, cc_dw_w, cc_dw_b, cc_ln_g, cc_ln_b, pool_w, pool_scale, w_out_o, w_router, b_router, w_e1, w_e3, w_e2):
    bp, seq, d = x_prompt.shape
    bs = x_sample.shape[0]
    depth = norm_g.shape[0]
    page = cache_fox_k.shape[2]
    past_len = page_table.shape[1] * page
    mem_len = mem_prompt.shape[1]
    tm_p = ROW_TILE

    xp = x_prompt.reshape(bp * seq, d)
    xs = x_sample.reshape(bs, d)
    mem = mem_prompt.reshape(bp * mem_len, d)
    vec = lambda v: v.reshape(1, -1)

    fk_p, fv_p, fl_p, lh_p, lc_p, cb_p, pb_p, mk_pl, mv_pl = [], [], [], [], [], [], [], [], []
    fk_s, fv_s, fl_s, lh_s, lc_s, cb_s, pb_s = [], [], [], [], [], [], []

    for l in range(depth):
        g = [vec(norm_g[l, i]) for i in range(norm_g.shape[1])]
        w_kv = jnp.concatenate([w_xk[l], w_xv[l]], axis=1).astype(BF16)
        memkv = _norm_matmul(mem, g[6], w_kv, tm=tm_p, name="mem_kv")
        mk_pl.append(memkv[:, :MEM_W].reshape(bp, mem_len, MEM_H, MEM_HD))
        mv_pl.append(memkv[:, MEM_W:].reshape(bp, mem_len, MEM_H, MEM_HD))

        if l % 2 == 0:
            i = l // 2
            n_main = 2 * LRU_W + 3 * FOX_W
            w_main = w_in_e[i][:, :n_main].astype(BF16)
            w_fl_t = jnp.pad(w_in_e[i][:, n_main:].T, ((0, 16 - FOX_H), (0, 0))).astype(BF16)
            cw, cb = lru_conv_w[i], vec(lru_conv_b[i])
            wa, wi = _block_diag_pairs(lru_wa[i]), _block_diag_pairs(lru_wi[i])
            ba, bi, lam = vec(lru_ba[i]), vec(lru_bi[i]), vec(lru_lam[i])
            bf = b_f[i].reshape(FOX_H, 1)
            w_out = w_out_e[i].astype(BF16)

            z, fl_t = _norm_matmul(xp, g[0], w_main, tm=tm_p, wt=w_fl_t, name="in_proj_even")
            lf_t, c_t = _fox_prep(fl_t, bf, bp, seq)
            lru_out, h_last = _lru_prompt(z, bp, seq, cw, cb, wa, ba, wi, bi, lam)
            att = _fox_prompt(z, c_t, bp, seq)
            xp = _matmul_norm_res([lru_out, att], w_out, g[1], xp, tm=tm_p, name="out_proj_even")
            z3 = z.reshape(bp, seq, n_main)
            fk_p.append(z3[:, :, 2 * LRU_W + FOX_W:2 * LRU_W + 2 * FOX_W].reshape(bp, seq, FOX_H, FOX_HD))
            fv_p.append(z3[:, :, 2 * LRU_W + 2 * FOX_W:].reshape(bp, seq, FOX_H, FOX_HD))
            fl_p.append(lf_t.T.reshape(bp, seq, FOX_H))
            lh_p.append(h_last.reshape(bp, LRU_W))
            lc_p.append(z3[:, seq - (LRU_CONV - 1):, :LRU_W])

            zs, fls_t = _norm_matmul(xs, g[0], w_main, tm=bs, wt=w_fl_t, name="in_proj_even_s")
            lfs_t = _logsig(fls_t, bf)
            pre_t = jnp.swapaxes(state_lru_conv[i], 0, 1)
            lru_out_s, h_s = _lru_sample(zs, pre_t, state_lru_h[i], cw, cb, wa, ba, wi, bi, lam)
            q_s = zs[:, 2 * LRU_W:2 * LRU_W + FOX_W]
            k_s = zs[:, 2 * LRU_W + FOX_W:2 * LRU_W + 2 * FOX_W]
            v_s = zs[:, 2 * LRU_W + 2 * FOX_W:]
            att_s = _fox_sample(page_table, i, q_s.reshape(bs, FOX_H, FOX_HD), k_s.reshape(bs, FOX_H, FOX_HD),
                                v_s.reshape(bs, FOX_H, FOX_HD), lfs_t.T.reshape(bs, FOX_H, 1),
                                jnp.transpose(cache_fox_k, (0, 1, 3, 4, 2)),
                                jnp.transpose(cache_fox_v, (0, 1, 3, 4, 2)),
                                jnp.swapaxes(cache_fox_logf[i], 1, 2))
            xs = _matmul_norm_res([lru_out_s, att_s.reshape(bs, FOX_W)], w_out, g[1], xs, tm=bs,
                                  name="out_proj_even_s")
            fk_s.append(k_s.reshape(bs, 1, FOX_H, FOX_HD))
            fv_s.append(v_s.reshape(bs, 1, FOX_H, FOX_HD))
            fl_s.append(lfs_t.T.reshape(bs, 1, FOX_H))
            lh_s.append(h_s)
            lc_s.append(jnp.concatenate([state_lru_conv[i][:, 1:], zs[:, None, :LRU_W]], axis=1))
        else:
            j = l // 2
            w_in = w_in_o[j].astype(BF16)
            dww, dwb = cc_dw_w[j], vec(cc_dw_b[j])
            lng, lnb = vec(cc_ln_g[j]), vec(cc_ln_b[j])
            pw, ps = pool_w[j].astype(BF16), vec(pool_scale[j])
            w_out = w_out_o[j].astype(BF16)

            z = _norm_matmul(xp, g[0], w_in, tm=tm_p, name="in_proj_odd")
            cv, pool, cbuf, pbuf = _odd_prompt(z, bp, seq, dww, dwb, lng, lnb, pw, ps)
            xp = _matmul_norm_res([cv, pool], w_out, g[1], xp, tm=tm_p, name="out_proj_odd")
            cb_p.append(cbuf)
            pb_p.append(pbuf)

            zs = _norm_matmul(xs, g[0], w_in, tm=bs, name="in_proj_odd_s")
            cv_s, pool_s, glu_s = _odd_sample(zs, jnp.swapaxes(state_conv_buf[j], 0, 1),
                                              jnp.swapaxes(state_pool_buf[j], 0, 1),
                                              dww, dwb, lng, lnb, pw, ps, past_len)
            xs = _matmul_norm_res([cv_s, pool_s], w_out, g[1], xs, tm=bs, name="out_proj_odd_s")
            cb_s.append(jnp.concatenate([state_conv_buf[j][:, 1:], glu_s[:, None, :]], axis=1))
            pb_s.append(jnp.concatenate([state_pool_buf[j][:, 1:], zs[:, None, 2 * CONV_W:]], axis=1))

        wq, wo = w_xq[l].astype(BF16), w_xo[l].astype(BF16)
        q = _norm_matmul(xp, g[2], wq, tm=tm_p, name="xattn_q")
        o = _xattn_prompt(q, memkv, bp, seq)
        xp = _matmul_norm_res([o], wo, g[3], xp, tm=tm_p, name="xattn_o")
        q_s = _norm_matmul(xs, g[2], wq, tm=bs, name="xattn_q_s")
        o_s = _xattn_sample(q_s.reshape(bs, 1, MEM_W), l,
                            cache_mem_k.reshape(depth, bs, mem_len * MEM_H, MEM_HD),
                            cache_mem_v.reshape(depth, bs, mem_len * MEM_H, MEM_HD))
        xs = _matmul_norm_res([o_s.reshape(bs, MEM_W)], wo, g[3], xs, tm=bs, name="xattn_o_s")

        if l % 2 == 0:
            i = l // 2
            w1, w3, w2 = (_cast_bf16(w[i:i + 1])[0] for w in (w_ff1, w_ff3, w_ff2))
            xp = _ffn(xp, g[4], g[5], w1, w3, w2, tm=tm_p)
            xs = _ffn(xs, g[4], g[5], w1, w3, w2, tm=bs)
        else:
            j = l // 2
            xp, xs = _moe_block([xp, xs], g[4], g[5], w_router[j], b_router[j],
                                _cast_bf16(w_e1[j]), _cast_bf16(w_e3[j]), _cast_bf16(w_e2[j]))

    return (xp.reshape(bp, seq, d), xs.reshape(bs, 1, d),
            jnp.stack(fk_p), jnp.stack(fv_p), jnp.stack(fl_p), jnp.stack(lh_p), jnp.stack(lc_p),
            jnp.stack(cb_p), jnp.stack(pb_p), jnp.stack(mk_pl), jnp.stack(mv_pl),
            jnp.stack(fk_s), jnp.stack(fv_s), jnp.stack(fl_s), jnp.stack(lh_s), jnp.stack(lc_s),
            jnp.stack(cb_s), jnp.stack(pb_s))
```

```python
import functools

import jax
import jax.numpy as jnp
from jax import lax
from jax.experimental import pallas as pl
from jax.experimental.pallas import tpu as pltpu

F32 = jnp.float32
BF16 = jnp.bfloat16

D_MODEL = 1024
LRU_W = 512
LRU_CONV = 4
LRU_C = 8.0
FOX_H = 8
FOX_HD = 64
FOX_W = FOX_H * FOX_HD
CONV_W = 512
CONV_K = 31
POOL_W = 512
POOL_WINDOWS = (2, 4, 8, 16)
POOL_GW = POOL_W // len(POOL_WINDOWS)
POOL_BUF = max(POOL_WINDOWS) - 1
MEM_H = 4
MEM_HD = 128
MEM_W = MEM_H * MEM_HD
N_EXPERTS = 8
EPS = 1e-6
NEG = -1e30

LANES = 128
ROW_TILE = 512
SEQ_TILE = 512
SCAN_TILE = 256
ATT_ROWS = 32
CONV_ROWS = 64
ATT_HEADS = 4
PAGES_PER_STEP = 16
MOE_TILE = 512
FF_ROW_SLAB = 256
MOE_FF_TILE = 1792
FFN_FF_TILE = 1408
CAST_BLOCK_BYTES = 8 * 1024 * 1024
MOE_PARTS = 4
VMEM_LIMIT = 56 * 1024 * 1024


def _cp(*sem):
    return pltpu.CompilerParams(dimension_semantics=sem, vmem_limit_bytes=VMEM_LIMIT)


def _rms(x, g):
    return x * lax.rsqrt(jnp.mean(x * x, axis=-1, keepdims=True) + EPS) * g


def _sigmoid(x):
    return 1.0 / (1.0 + jnp.exp(-x))


def _softplus(x):
    return jnp.maximum(x, 0.0) + jnp.log1p(jnp.exp(-jnp.abs(x)))


def _gelu_tanh(x):
    return 0.5 * x * (1.0 + jnp.tanh(0.7978845608028654 * (x + 0.044715 * (x * x * x))))


def _dot(a, b):
    return jnp.dot(a, b, preferred_element_type=F32)


def _dot_nt(a, b):
    return lax.dot_general(a, b, (((1,), (1,)), ((), ())), preferred_element_type=F32)


def _cast_kernel(x_ref, o_ref):
    o_ref[...] = x_ref[...].astype(BF16)


def _cast_bf16(w):
    e, k, n = w.shape
    n_k = pl.cdiv(k * n * 4, CAST_BLOCK_BYTES)
    assert k % (16 * n_k) == 0
    return pl.pallas_call(
        _cast_kernel, grid=(e, n_k),
        in_specs=[pl.BlockSpec((None, k // n_k, n), lambda i, j: (i, j, 0))],
        out_specs=pl.BlockSpec((None, k // n_k, n), lambda i, j: (i, j, 0)),
        out_shape=jax.ShapeDtypeStruct(w.shape, BF16),
        compiler_params=_cp("parallel", "parallel"), name="cast_bf16")(w)


def _norm_matmul_kernel(x_ref, g_ref, w_ref, *rest, has_t):
    hn = _rms(x_ref[...], g_ref[...]).astype(BF16)
    if has_t:
        wt_ref, o_ref, ot_ref = rest
        ot_ref[...] = _dot_nt(wt_ref[...], hn)
    else:
        (o_ref,) = rest
    o_ref[...] = _dot(hn, w_ref[...])


def _norm_matmul(x, g, w, *, tm, wt=None, name):
    m, d = x.shape
    n = w.shape[1]
    in_specs = [pl.BlockSpec((tm, d), lambda i: (i, 0)),
                pl.BlockSpec((1, d), lambda i: (0, 0)),
                pl.BlockSpec((d, n), lambda i: (0, 0))]
    out_shape = [jax.ShapeDtypeStruct((m, n), F32)]
    out_specs = [pl.BlockSpec((tm, n), lambda i: (i, 0))]
    args = [x, g, w]
    if wt is not None:
        in_specs.append(pl.BlockSpec(wt.shape, lambda i: (0, 0)))
        out_shape.append(jax.ShapeDtypeStruct((wt.shape[0], m), F32))
        out_specs.append(pl.BlockSpec((wt.shape[0], tm), lambda i: (0, i)))
        args.append(wt)
    res = pl.pallas_call(
        functools.partial(_norm_matmul_kernel, has_t=wt is not None),
        grid=(m // tm,), in_specs=in_specs, out_specs=out_specs, out_shape=out_shape,
        compiler_params=_cp("parallel"), name=name)(*args)
    return res if wt is not None else res[0]


def _matmul_norm_res_kernel(*refs, widths):
    n_a = len(widths)
    a_refs = refs[:n_a]
    w_ref, g_ref, r_ref, o_ref = refs[n_a:]
    y = None
    off = 0
    for a_ref, k in zip(a_refs, widths):
        part = _dot(a_ref[...].astype(BF16), w_ref[off:off + k, :])
        y = part if y is None else y + part
        off += k
    o_ref[...] = r_ref[...] + _rms(y, g_ref[...])


def _matmul_norm_res(a_list, w, g, resid, *, tm, name):
    m, d = resid.shape
    widths = tuple(a.shape[1] for a in a_list)
    in_specs = [pl.BlockSpec((tm, k), lambda i: (i, 0)) for k in widths]
    in_specs += [pl.BlockSpec(w.shape, lambda i: (0, 0)),
                 pl.BlockSpec((1, d), lambda i: (0, 0)),
                 pl.BlockSpec((tm, d), lambda i: (i, 0))]
    return pl.pallas_call(
        functools.partial(_matmul_norm_res_kernel, widths=widths),
        grid=(m // tm,), in_specs=in_specs,
        out_specs=pl.BlockSpec((tm, d), lambda i: (i, 0)),
        out_shape=jax.ShapeDtypeStruct((m, d), F32),
        compiler_params=_cp("parallel"), name=name)(*a_list, w, g, resid)


def _lru_gates(xc, wa_ref, ba_ref, wi_ref, bi_ref, lam_ref):
    xb = xc.astype(BF16)
    ra, ia = [], []
    for c in range(LRU_W // LANES):
        xs = xb[:, c * LANES:(c + 1) * LANES]
        ra.append(_dot(xs, wa_ref[c]))
        ia.append(_dot(xs, wi_ref[c]))
    r = _sigmoid(jnp.concatenate(ra, axis=1) + ba_ref[...])
    ig = _sigmoid(jnp.concatenate(ia, axis=1) + bi_ref[...])
    log_a = -LRU_C * r * _softplus(-lam_ref[...])
    a = jnp.exp(log_a)
    bx = jnp.sqrt(-jnp.tanh(log_a) * (a * a + 1.0)) * (ig * xc)
    return a, bx


def _lru_prompt_kernel(xl_ref, gate_ref, cw_ref, cb_ref, wa_ref, ba_ref, wi_ref, bi_ref, lam_ref,
                       out_ref, hlast_ref, ext_ref, hc_ref, *, tc, n_t):
    t = pl.program_id(1)

    @pl.when(t == 0)
    def _():
        ext_ref[0:8, :] = jnp.zeros((8, LRU_W), F32)
        hc_ref[...] = jnp.zeros((1, LRU_W), F32)

    xl = xl_ref[...]
    ext_ref[8:8 + tc, :] = xl
    xc = cb_ref[...] + cw_ref[LRU_CONV - 1:LRU_CONV, :] * xl
    for j in range(1, LRU_CONV):
        xc = xc + cw_ref[LRU_CONV - 1 - j:LRU_CONV - j, :] * ext_ref[pl.ds(8 - j, tc), :]
    ext_ref[0:8, :] = ext_ref[tc:tc + 8, :]

    a, b = _lru_gates(xc, wa_ref, ba_ref, wi_ref, bi_ref, lam_ref)
    row = lax.broadcasted_iota(jnp.int32, (tc, 1), 0)
    d = 1
    while d < tc:
        keep = row >= d
        a_sh = jnp.where(keep, pltpu.roll(a, d, 0), 1.0)
        b_sh = jnp.where(keep, pltpu.roll(b, d, 0), 0.0)
        b = a * b_sh + b
        a = a * a_sh
        d *= 2
    h = a * hc_ref[...] + b
    hc_ref[...] = h[tc - 1:tc, :]
    out_ref[...] = (_gelu_tanh(gate_ref[...]) * h).astype(BF16)

    @pl.when(t == n_t - 1)
    def _():
        hlast_ref[...] = h[tc - 1:tc, :]


def _lru_prompt(z, n_b, seq, cw, cb, wa, ba, wi, bi, lam):
    tc = SCAN_TILE
    n_t = seq // tc
    vec = lambda: pl.BlockSpec((1, LRU_W), lambda b, t: (0, 0))
    bd = lambda: pl.BlockSpec((LRU_W // LANES, LANES, LANES), lambda b, t: (0, 0, 0))
    return pl.pallas_call(
        functools.partial(_lru_prompt_kernel, tc=tc, n_t=n_t),
        grid=(n_b, n_t),
        in_specs=[pl.BlockSpec((tc, LRU_W), lambda b, t: (b * n_t + t, 0)),
                  pl.BlockSpec((tc, LRU_W), lambda b, t: (b * n_t + t, 1)),
                  pl.BlockSpec((LRU_CONV, LRU_W), lambda b, t: (0, 0)),
                  vec(), bd(), vec(), bd(), vec(), vec()],
        out_specs=[pl.BlockSpec((tc, LRU_W), lambda b, t: (b * n_t + t, 0)),
                   pl.BlockSpec((None, 1, LRU_W), lambda b, t: (b, 0, 0))],
        out_shape=[jax.ShapeDtypeStruct((n_b * seq, LRU_W), BF16),
                   jax.ShapeDtypeStruct((n_b, 1, LRU_W), F32)],
        scratch_shapes=[pltpu.VMEM((tc + 8, LRU_W), F32), pltpu.VMEM((1, LRU_W), F32)],
        compiler_params=_cp("parallel", "arbitrary"), name="lru_prompt")(
            z, z, cw, cb, wa, ba, wi, bi, lam)


def _lru_sample_kernel(z_ref, pre_ref, h0_ref, cw_ref, cb_ref, wa_ref, ba_ref, wi_ref, bi_ref, lam_ref,
                       out_ref, h_ref):
    xl = z_ref[:, 0:LRU_W]
    gate = z_ref[:, LRU_W:2 * LRU_W]
    xc = cb_ref[...] + cw_ref[LRU_CONV - 1:LRU_CONV, :] * xl
    for k in range(LRU_CONV - 1):
        xc = xc + cw_ref[k:k + 1, :] * pre_ref[k]
    a, bx = _lru_gates(xc, wa_ref, ba_ref, wi_ref, bi_ref, lam_ref)
    h = a * h0_ref[...] + bx
    h_ref[...] = h
    out_ref[...] = (_gelu_tanh(gate) * h).astype(BF16)


def _lru_sample(z, prefix_t, h0, cw, cb, wa, ba, wi, bi, lam):
    n_b = z.shape[0]
    return pl.pallas_call(
        _lru_sample_kernel,
        out_shape=[jax.ShapeDtypeStruct((n_b, LRU_W), BF16), jax.ShapeDtypeStruct((n_b, LRU_W), F32)],
        compiler_params=pltpu.CompilerParams(vmem_limit_bytes=VMEM_LIMIT),
        name="lru_sample")(z, prefix_t, h0, cw, cb, wa, ba, wi, bi, lam)


def _log_sigmoid(x):
    return jnp.minimum(x, 0.0) - jnp.log1p(jnp.exp(-jnp.abs(x)))


def _lane_cumsum(x):
    n = x.shape[1]
    lane = lax.broadcasted_iota(jnp.int32, x.shape, 1)
    d = 1
    while d < n:
        x = x + jnp.where(lane >= d, pltpu.roll(x, d, 1), 0.0)
        d *= 2
    return x


def _fox_prep_kernel(fl_ref, bf_ref, lf_ref, c_ref):
    lf = _log_sigmoid(fl_ref[0:FOX_H, :] + bf_ref[...])
    lf_ref[...] = lf
    c_ref[...] = _lane_cumsum(lf)


def _fox_prep(fl_t, bf, n_b, seq):
    return pl.pallas_call(
        _fox_prep_kernel, grid=(n_b,),
        in_specs=[pl.BlockSpec((fl_t.shape[0], seq), lambda b: (0, b)),
                  pl.BlockSpec((FOX_H, 1), lambda b: (0, 0))],
        out_specs=[pl.BlockSpec((FOX_H, seq), lambda b: (0, b)),
                   pl.BlockSpec((FOX_H, seq), lambda b: (0, b))],
        out_shape=[jax.ShapeDtypeStruct((FOX_H, n_b * seq), F32)] * 2,
        compiler_params=_cp("parallel"), name="fox_prep")(fl_t, bf)


def _logsig_kernel(fl_ref, bf_ref, lf_ref):
    lf_ref[...] = _log_sigmoid(fl_ref[0:FOX_H, :] + bf_ref[...])


def _logsig(fl_t, bf):
    return pl.pallas_call(
        _logsig_kernel, out_shape=jax.ShapeDtypeStruct((FOX_H, fl_t.shape[1]), F32),
        name="fox_logf_sample")(fl_t, bf)


def _fox_prompt_kernel(qi_ref, ki_ref, q_ref, k_ref, v_ref, cq_ref, ck_ref, o_ref,
                       m_sc, l_sc, acc_sc, s_sc, p_sc, al_sc, *, tb, hg):
    grp = pl.program_id(1)
    t = pl.program_id(2)
    qi = qi_ref[t]
    ki = ki_ref[t]
    width = hg * FOX_HD

    @pl.when(ki == 0)
    def _():
        m_sc[...] = jnp.full(m_sc.shape, NEG, F32)
        l_sc[...] = jnp.zeros(l_sc.shape, F32)
        acc_sc[...] = jnp.zeros(acc_sc.shape, F32)

    def step(diagonal):
        q = q_ref[...] * (FOX_HD ** -0.5)
        k = k_ref[...].astype(BF16)
        v = v_ref[...].astype(BF16)
        lane_head = jnp.right_shift(lax.broadcasted_iota(jnp.int32, (tb, width), 1),
                                    FOX_HD.bit_length() - 1)
        rc = ATT_ROWS
        if diagonal:
            row_i = lax.broadcasted_iota(jnp.int32, (rc, tb), 0)
            col_i = lax.broadcasted_iota(jnp.int32, (rc, tb), 1)
        n_lt = tb // LANES
        for h in range(hg):
            s_sc[h] = _dot_nt(jnp.where(lane_head == h, q, 0.0).astype(BF16), k)
            c_q = cq_ref[pl.ds(hg * grp + h, 1), :]
            c_k = ck_ref[pl.ds(hg * grp + h, 1), :]
            bias = c_q[:, 0:1] - c_k
            for c in range(tb // rc):
                rows = slice(c * rc, (c + 1) * rc)
                s = s_sc[h, rows, :] + bias
                if diagonal:
                    s = jnp.where(col_i <= row_i + c * rc, s, NEG)
                s_sc[h, rows, :] = s
                m_prev = m_sc[h, rows, :]
                m_new = jnp.maximum(m_prev, jnp.max(s, axis=1, keepdims=True))
                m_sc[h, rows, :] = m_new
                al_sc[h, rows, :] = jnp.exp(m_prev - m_new)
            for c in range(tb // rc):
                rows = slice(c * rc, (c + 1) * rc)
                m_new = m_sc[h, rows, :]
                p_sum = None
                for j in range(n_lt):
                    cols = slice(j * LANES, (j + 1) * LANES)
                    p = jnp.exp(s_sc[h, rows, cols] - m_new)
                    p_sc[h, rows, cols] = p.astype(BF16)
                    p_sum = p if p_sum is None else p_sum + p
                l_sc[h, rows, :] = al_sc[h, rows, :] * l_sc[h, rows, :] + p_sum
            lt = (h * FOX_HD) // LANES
            pv = _dot(p_sc[h], v)[:, lt * LANES:(lt + 1) * LANES]
            acc_sc[h] = al_sc[h] * acc_sc[h] + pv

    @pl.when(ki < qi)
    def _():
        step(False)

    @pl.when(ki == qi)
    def _():
        step(True)
        lane = lax.broadcasted_iota(jnp.int32, (tb, LANES), 1)
        heads_per_tile = LANES // FOX_HD
        for lt in range(width // LANES):
            o = None
            for i in range(heads_per_tile):
                h = lt * heads_per_tile + i
                o_h = acc_sc[h] / jnp.sum(l_sc[h], axis=1, keepdims=True)
                o = o_h if o is None else jnp.where(lane < i * FOX_HD, o, o_h)
            o_ref[:, lt * LANES:(lt + 1) * LANES] = o.astype(BF16)


def _fox_prompt(z, c_t, n_b, seq):
    tb = SEQ_TILE
    hg = ATT_HEADS
    width = hg * FOX_HD
    n_q = seq // tb
    q_blk = (2 * LRU_W) // width
    k_blk = q_blk + FOX_W // width
    v_blk = k_blk + FOX_W // width
    pairs = [(qi, ki) for qi in range(n_q) for ki in range(qi + 1)]
    qi_list = jnp.asarray([p[0] for p in pairs], jnp.int32)
    ki_list = jnp.asarray([p[1] for p in pairs], jnp.int32)
    grid_spec = pltpu.PrefetchScalarGridSpec(
        num_scalar_prefetch=2, grid=(n_b, FOX_H // hg, len(pairs)),
        in_specs=[
            pl.BlockSpec((tb, width), lambda b, g, t, qi, ki: (b * n_q + qi[t], q_blk + g)),
            pl.BlockSpec((tb, width), lambda b, g, t, qi, ki: (b * n_q + ki[t], k_blk + g)),
            pl.BlockSpec((tb, width), lambda b, g, t, qi, ki: (b * n_q + ki[t], v_blk + g)),
            pl.BlockSpec((FOX_H, tb), lambda b, g, t, qi, ki: (0, b * n_q + qi[t])),
            pl.BlockSpec((FOX_H, tb), lambda b, g, t, qi, ki: (0, b * n_q + ki[t])),
        ],
        out_specs=pl.BlockSpec((tb, width), lambda b, g, t, qi, ki: (b * n_q + qi[t], g)),
        scratch_shapes=[pltpu.VMEM((hg, tb, LANES), F32), pltpu.VMEM((hg, tb, LANES), F32),
                        pltpu.VMEM((hg, tb, LANES), F32), pltpu.VMEM((hg, tb, tb), F32),
                        pltpu.VMEM((hg, tb, tb), BF16), pltpu.VMEM((hg, tb, LANES), F32)])
    return pl.pallas_call(
        functools.partial(_fox_prompt_kernel, tb=tb, hg=hg), grid_spec=grid_spec,
        out_shape=jax.ShapeDtypeStruct((n_b * seq, FOX_W), BF16),
        compiler_params=_cp("parallel", "parallel", "arbitrary"),
        name="fox_prompt")(qi_list, ki_list, z, z, z, c_t, c_t)


def _fox_sample_kernel(pt_ref, q_ref, kn_ref, vn_ref, lfn_ref, *rest, n_pg, n_g):
    k_refs = rest[0:n_pg]
    v_refs = rest[n_pg:2 * n_pg]
    lf_refs = rest[2 * n_pg:3 * n_pg]
    o_ref, m_sc, l_sc, acc_sc, cc_sc = rest[3 * n_pg:]
    g = pl.program_id(1)
    page = lf_refs[0].shape[1]

    @pl.when(g == 0)
    def _():
        m_sc[...] = jnp.full(m_sc.shape, NEG, F32)
        l_sc[...] = jnp.zeros(l_sc.shape, F32)
        acc_sc[...] = jnp.zeros(acc_sc.shape, F32)
        cc_sc[...] = jnp.zeros(cc_sc.shape, F32)


    q = q_ref[...] * (FOX_HD ** -0.5)
    q16 = jnp.concatenate([q, jnp.zeros_like(q)], axis=0)
    row_q = lax.broadcasted_iota(jnp.int32, (2 * FOX_H, FOX_HD), 0)
    q_only = [jnp.where(row_q == h, q16, 0.0).astype(BF16) for h in range(FOX_H)]

    s_parts = []
    for j in range(n_pg):
        s_j = None
        for h in range(FOX_H):
            part = _dot(q_only[h], k_refs[j][h].astype(BF16))
            s_j = part if s_j is None else s_j + part
        s_parts.append(s_j[0:FOX_H])
    s = jnp.concatenate(s_parts, axis=1)
    lf = jnp.concatenate([lf_refs[j][...] for j in range(n_pg)], axis=1)
    c = _lane_cumsum(lf) + cc_sc[...]
    cc_sc[...] = c[:, c.shape[1] - 1:]
    s = s - c
    m_prev = m_sc[...]
    m_new = jnp.maximum(m_prev, jnp.max(s, axis=1, keepdims=True))
    alpha = jnp.exp(m_prev - m_new)
    p = jnp.exp(s - m_new)
    l_sc[...] = alpha * l_sc[...] + jnp.sum(p, axis=1, keepdims=True)
    p16 = jnp.concatenate([p, jnp.zeros_like(p)], axis=0)
    row_p = lax.broadcasted_iota(jnp.int32, (2 * FOX_H, page), 0)
    pv = None
    for j in range(n_pg):
        p_j = p16[:, j * page:(j + 1) * page]
        for h in range(FOX_H):
            part = _dot_nt(jnp.where(row_p == h, p_j, 0.0).astype(BF16), v_refs[j][h].astype(BF16))
            pv = part if pv is None else pv + part
    acc_sc[...] = alpha * acc_sc[...] + pv[0:FOX_H]
    m_sc[...] = m_new

    @pl.when(g == n_g - 1)
    def _():
        s_n = jnp.sum(q * kn_ref[...], axis=1, keepdims=True)
        s_n = s_n - (cc_sc[...] + lfn_ref[...])
        m_p = m_sc[...]
        m_n = jnp.maximum(m_p, s_n)
        al = jnp.exp(m_p - m_n)
        p_n = jnp.exp(s_n - m_n)
        l_n = al * l_sc[...] + p_n
        o_ref[...] = ((al * acc_sc[...] + p_n * vn_ref[...]) / l_n).astype(BF16)


def _fox_sample(page_table, layer, q, k_new, v_new, lf_new, cache_k, cache_v, cache_lf_t):
    n_b, n_pages = page_table.shape
    n_pg = PAGES_PER_STEP
    n_g = n_pages // n_pg
    page = cache_k.shape[4]
    head_spec = lambda: pl.BlockSpec((None, FOX_H, FOX_HD), lambda b, g, pt: (b, 0, 0))

    def kv_spec(j):
        return pl.BlockSpec((None, None, FOX_H, FOX_HD, page),
                            lambda b, g, pt, j=j: (layer, pt[b * n_pages + g * n_pg + j], 0, 0, 0))

    def lf_spec(j):
        return pl.BlockSpec((None, FOX_H, page),
                            lambda b, g, pt, j=j: (pt[b * n_pages + g * n_pg + j], 0, 0))

    in_specs = [head_spec(), head_spec(), head_spec(),
                pl.BlockSpec((None, FOX_H, 1), lambda b, g, pt: (b, 0, 0))]
    in_specs += [kv_spec(j) for j in range(n_pg)]
    in_specs += [kv_spec(j) for j in range(n_pg)]
    in_specs += [lf_spec(j) for j in range(n_pg)]
    grid_spec = pltpu.PrefetchScalarGridSpec(
        num_scalar_prefetch=1, grid=(n_b, n_g), in_specs=in_specs,
        out_specs=pl.BlockSpec((None, FOX_H, FOX_HD), lambda b, g, pt: (b, 0, 0)),
        scratch_shapes=[pltpu.VMEM((FOX_H, 1), F32), pltpu.VMEM((FOX_H, 1), F32),
                        pltpu.VMEM((FOX_H, FOX_HD), F32), pltpu.VMEM((FOX_H, 1), F32)])
    return pl.pallas_call(
        functools.partial(_fox_sample_kernel, n_pg=n_pg, n_g=n_g),
        grid_spec=grid_spec,
        out_shape=jax.ShapeDtypeStruct((n_b, FOX_H, FOX_HD), BF16),
        compiler_params=_cp("parallel", "arbitrary"), name="fox_sample")(
            page_table.reshape(-1), q, k_new, v_new, lf_new,
            *([cache_k] * n_pg), *([cache_v] * n_pg), *([cache_lf_t] * n_pg))


def _xattn_prompt_kernel(q_ref, mk_ref, mv_ref, o_ref):
    outs = []
    for h in range(MEM_H):
        cs = slice(h * MEM_HD, (h + 1) * MEM_HD)
        qh = q_ref[:, cs].astype(BF16)
        s = _dot_nt(qh, mk_ref[:, cs].astype(BF16)) * (MEM_HD ** -0.5)
        p = jnp.exp(s - jnp.max(s, axis=1, keepdims=True))
        l = jnp.sum(p, axis=1, keepdims=True)
        outs.append(_dot(p.astype(BF16), mv_ref[:, cs].astype(BF16)) / l)
    o_ref[...] = jnp.concatenate(outs, axis=1).astype(BF16)


def _xattn_prompt(q, memkv, n_b, seq):
    tb = SEQ_TILE
    n_t = seq // tb
    mem_len = memkv.shape[0] // n_b
    return pl.pallas_call(
        _xattn_prompt_kernel, grid=(n_b, n_t),
        in_specs=[pl.BlockSpec((tb, MEM_W), lambda b, t: (b * n_t + t, 0)),
                  pl.BlockSpec((mem_len, MEM_W), lambda b, t: (b, 0)),
                  pl.BlockSpec((mem_len, MEM_W), lambda b, t: (b, 1))],
        out_specs=pl.BlockSpec((tb, MEM_W), lambda b, t: (b * n_t + t, 0)),
        out_shape=jax.ShapeDtypeStruct((n_b * seq, MEM_W), BF16),
        compiler_params=_cp("parallel", "parallel"), name="xattn_prompt")(q, memkv, memkv)


def _xattn_sample_kernel(q_ref, mk_ref, mv_ref, o_ref):
    q = q_ref[...]
    row = lax.broadcasted_iota(jnp.int32, (16, MEM_W), 0)
    col = lax.broadcasted_iota(jnp.int32, (16, MEM_W), 1)
    head_cols = jnp.right_shift(col, MEM_HD.bit_length() - 1) == row
    q_rows = jnp.where(head_cols, q, 0.0).astype(BF16)
    mem_len = mk_ref.shape[0] // MEM_H

    def heads_on_lanes(ref):
        return jnp.concatenate([ref[pl.ds(h, mem_len, stride=MEM_H), :] for h in range(MEM_H)],
                               axis=1).astype(BF16)

    s = _dot_nt(q_rows, heads_on_lanes(mk_ref)) * (MEM_HD ** -0.5)
    p = jnp.exp(s - jnp.max(s, axis=1, keepdims=True))
    l = jnp.sum(p, axis=1, keepdims=True)
    o = _dot(p.astype(BF16), heads_on_lanes(mv_ref)) / l
    o_ref[...] = jnp.sum(jnp.where(head_cols, o, 0.0), axis=0, keepdims=True).astype(BF16)


def _xattn_sample(q, layer, mk, mv):
    n_b, rows = mk.shape[1], mk.shape[2]
    return pl.pallas_call(
        _xattn_sample_kernel, grid=(n_b,),
        in_specs=[pl.BlockSpec((None, 1, MEM_W), lambda b: (b, 0, 0)),
                  pl.BlockSpec((None, None, rows, MEM_HD), lambda b: (layer, b, 0, 0)),
                  pl.BlockSpec((None, None, rows, MEM_HD), lambda b: (layer, b, 0, 0))],
        out_specs=pl.BlockSpec((None, 1, MEM_W), lambda b: (b, 0, 0)),
        out_shape=jax.ShapeDtypeStruct((n_b, 1, MEM_W), BF16),
        compiler_params=_cp("parallel"), name="xattn_sample")(q, mk, mv)


def _swiglu_accumulate(x_sc, w1_ref, w3_ref, w2_ref, acc_sc):
    tm = x_sc.shape[0]
    slab = min(tm, FF_ROW_SLAB)
    for r in range(tm // slab):
        rows = slice(r * slab, (r + 1) * slab)
        x = x_sc[rows, :]
        h1 = _dot(x, w1_ref[...])
        h3 = _dot(x, w3_ref[...])
        hh = (h1 * _sigmoid(h1) * h3).astype(BF16)
        acc_sc[rows, :] += _dot(hh, w2_ref[...])


def _ffn_kernel(x_ref, g4_ref, g5_ref, w1_ref, w3_ref, w2_ref, o_ref, hn_sc, acc_sc, *, n_f):
    f = pl.program_id(1)

    @pl.when(f == 0)
    def _():
        hn_sc[...] = _rms(x_ref[...], g4_ref[...]).astype(BF16)
        acc_sc[...] = jnp.zeros(acc_sc.shape, F32)

    _swiglu_accumulate(hn_sc, w1_ref, w3_ref, w2_ref, acc_sc)

    @pl.when(f == n_f - 1)
    def _():
        o_ref[...] = x_ref[...] + _rms(acc_sc[...], g5_ref[...])


def _ffn(x, g4, g5, w1, w3, w2, *, tm):
    m, d = x.shape
    ff = w1.shape[1]
    tf = FFN_FF_TILE
    n_f = ff // tf
    return pl.pallas_call(
        functools.partial(_ffn_kernel, n_f=n_f), grid=(m // tm, n_f),
        in_specs=[pl.BlockSpec((tm, d), lambda i, f: (i, 0)),
                  pl.BlockSpec((1, d), lambda i, f: (0, 0)),
                  pl.BlockSpec((1, d), lambda i, f: (0, 0)),
                  pl.BlockSpec((d, tf), lambda i, f: (0, f)),
                  pl.BlockSpec((d, tf), lambda i, f: (0, f)),
                  pl.BlockSpec((tf, d), lambda i, f: (f, 0))],
        out_specs=pl.BlockSpec((tm, d), lambda i, f: (i, 0)),
        out_shape=jax.ShapeDtypeStruct((m, d), F32),
        scratch_shapes=[pltpu.VMEM((tm, d), BF16), pltpu.VMEM((tm, d), F32)],
        compiler_params=_cp("parallel", "arbitrary"), name="ffn")(x, g4, g5, w1, w3, w2)


def _layernorm_silu(x, g, b):
    mu = jnp.mean(x, axis=-1, keepdims=True)
    xc = x - mu
    y = xc * lax.rsqrt(jnp.mean(xc * xc, axis=-1, keepdims=True) + EPS) * g + b
    return y * _sigmoid(y)


def _odd_prompt_kernel(a_ref, gt_ref, up_ref, dww_ref, dwb_ref, lng_ref, lnb_ref, pw_ref, ps_ref,
                       cv_ref, pool_ref, cbuf_ref, pbuf_ref, eg_ref, eu_ref, sh_ref, *, tc, n_t):
    t = pl.program_id(1)
    halo_g, halo_u = 32, 16

    @pl.when(t == 0)
    def _():
        eg_ref[0:halo_g, :] = jnp.zeros((halo_g, CONV_W), F32)
        eu_ref[0:halo_u, :] = jnp.zeros((halo_u, POOL_W), F32)

    glu = a_ref[...] * _sigmoid(gt_ref[...])
    up = up_ref[...]
    eg_ref[halo_g:halo_g + tc, :] = glu
    eu_ref[halo_u:halo_u + tc, :] = up

    ext = eg_ref[...]
    n_ext = tc + halo_g
    sh_ref[0] = ext
    for s in range(1, 8):
        sh_ref[s] = pltpu.roll(ext, n_ext - s, 0)
    rc = CONV_ROWS
    for c in range(tc // rc):
        parts = []
        for j in range(CONV_W // LANES):
            cols = slice(j * LANES, (j + 1) * LANES)
            acc = jnp.broadcast_to(dwb_ref[:, cols], (rc, LANES))
            for k in range(CONV_K):
                off = k + halo_g - (CONV_K - 1)
                acc = acc + dww_ref[k:k + 1, cols] * sh_ref[off % 8, pl.ds(c * rc + (off // 8) * 8, rc), cols]
            parts.append(acc)
        cv_ref[c * rc:(c + 1) * rc, :] = _layernorm_silu(
            jnp.concatenate(parts, axis=1), lng_ref[...], lnb_ref[...]).astype(BF16)

    pos = t * tc + lax.broadcasted_iota(jnp.int32, (tc, 1), 0)
    outs = []
    for gi, w in enumerate(POOL_WINDOWS):
        cs = slice(gi * POOL_GW, (gi + 1) * POOL_GW)
        u_g = up[:, cs]
        win = u_g
        for j in range(1, w):
            win = win + eu_ref[pl.ds(halo_u - j, tc), cs]
        cnt = jnp.minimum(pos + 1, w).astype(F32)
        dlt = win / cnt - u_g
        outs.append(_dot(dlt.astype(BF16), pw_ref[gi]))
    pool_ref[...] = (jnp.concatenate(outs, axis=1) * ps_ref[...]).astype(BF16)

    @pl.when(t == n_t - 1)
    def _():
        cbuf_ref[...] = eg_ref[pl.ds(halo_g + tc - (CONV_K - 1), CONV_K - 1), :]
        pbuf_ref[...] = eu_ref[pl.ds(halo_u + tc - POOL_BUF, POOL_BUF), :]

    eg_ref[0:halo_g, :] = eg_ref[tc:tc + halo_g, :]
    eu_ref[0:halo_u, :] = eu_ref[tc:tc + halo_u, :]


def _odd_prompt(z, n_b, seq, dww, dwb, lng, lnb, pw, ps):
    tc = SEQ_TILE
    n_t = seq // tc
    vec = lambda: pl.BlockSpec((1, CONV_W), lambda b, t: (0, 0))
    return pl.pallas_call(
        functools.partial(_odd_prompt_kernel, tc=tc, n_t=n_t), grid=(n_b, n_t),
        in_specs=[pl.BlockSpec((tc, CONV_W), lambda b, t: (b * n_t + t, 0)),
                  pl.BlockSpec((tc, CONV_W), lambda b, t: (b * n_t + t, 1)),
                  pl.BlockSpec((tc, POOL_W), lambda b, t: (b * n_t + t, 2)),
                  pl.BlockSpec((CONV_K, CONV_W), lambda b, t: (0, 0)),
                  vec(), vec(), vec(),
                  pl.BlockSpec((len(POOL_WINDOWS), POOL_GW, POOL_GW), lambda b, t: (0, 0, 0)),
                  vec()],
        out_specs=[pl.BlockSpec((tc, CONV_W), lambda b, t: (b * n_t + t, 0)),
                   pl.BlockSpec((tc, POOL_W), lambda b, t: (b * n_t + t, 0)),
                   pl.BlockSpec((None, CONV_K - 1, CONV_W), lambda b, t: (b, 0, 0)),
                   pl.BlockSpec((None, POOL_BUF, POOL_W), lambda b, t: (b, 0, 0))],
        out_shape=[jax.ShapeDtypeStruct((n_b * seq, CONV_W), BF16),
                   jax.ShapeDtypeStruct((n_b * seq, POOL_W), BF16),
                   jax.ShapeDtypeStruct((n_b, CONV_K - 1, CONV_W), F32),
                   jax.ShapeDtypeStruct((n_b, POOL_BUF, POOL_W), F32)],
        scratch_shapes=[pltpu.VMEM((tc + 32, CONV_W), F32), pltpu.VMEM((tc + 16, POOL_W), F32),
                        pltpu.VMEM((8, tc + 32, CONV_W), F32)],
        compiler_params=_cp("parallel", "arbitrary"), name="odd_prompt")(
            z, z, z, dww, dwb, lng, lnb, pw, ps)


def _odd_sample_kernel(z_ref, cbuf_ref, pbuf_ref, dww_ref, dwb_ref, lng_ref, lnb_ref, pw_ref, ps_ref,
                       cv_ref, pool_ref, glu_ref, *, pos0):
    glu = z_ref[:, 0:CONV_W] * _sigmoid(z_ref[:, CONV_W:2 * CONV_W])
    up = z_ref[:, 2 * CONV_W:2 * CONV_W + POOL_W]
    glu_ref[...] = glu
    acc = dwb_ref[...] + dww_ref[CONV_K - 1:CONV_K, :] * glu
    for k in range(CONV_K - 1):
        acc = acc + dww_ref[k:k + 1, :] * cbuf_ref[k]
    cv_ref[...] = _layernorm_silu(acc, lng_ref[...], lnb_ref[...]).astype(BF16)
    outs = []
    for gi, w in enumerate(POOL_WINDOWS):
        cs = slice(gi * POOL_GW, (gi + 1) * POOL_GW)
        u_g = up[:, cs]
        win = u_g
        for j in range(1, w):
            win = win + pbuf_ref[POOL_BUF - j][:, cs]
        dlt = win / float(min(pos0 + 1, w)) - u_g
        outs.append(_dot(dlt.astype(BF16), pw_ref[gi]))
    pool_ref[...] = (jnp.concatenate(outs, axis=1) * ps_ref[...]).astype(BF16)


def _odd_sample(z, cbuf_t, pbuf_t, dww, dwb, lng, lnb, pw, ps, pos0):
    n_b = z.shape[0]
    return pl.pallas_call(
        functools.partial(_odd_sample_kernel, pos0=pos0),
        out_shape=[jax.ShapeDtypeStruct((n_b, CONV_W), BF16),
                   jax.ShapeDtypeStruct((n_b, POOL_W), BF16),
                   jax.ShapeDtypeStruct((n_b, CONV_W), F32)],
        compiler_params=pltpu.CompilerParams(vmem_limit_bytes=VMEM_LIMIT),
        name="odd_sample")(z, cbuf_t, pbuf_t, dww, dwb, lng, lnb, pw, ps)


def _router_kernel(x_ref, g_ref, whi_ref, wlo_ref, b_ref, hn_ref, idx_ref, gate_ref):
    hn = _rms(x_ref[...], g_ref[...])
    hb = hn.astype(BF16)
    hn_ref[...] = hn
    hlo = (hn - hb.astype(F32)).astype(BF16)
    logits = _dot(hb, whi_ref[...]) + (_dot(hb, wlo_ref[...]) + _dot(hlo, whi_ref[...])) + b_ref[...]
    lane = lax.broadcasted_iota(jnp.int32, logits.shape, 1)
    lane_f = lane.astype(F32)
    logits = jnp.where(lane < N_EXPERTS, logits, NEG)
    m1 = jnp.max(logits, axis=1, keepdims=True)
    i1 = jnp.min(jnp.where(logits == m1, lane_f, float(LANES)), axis=1, keepdims=True)
    rest = jnp.where(lane_f == i1, NEG, logits)
    m2 = jnp.max(rest, axis=1, keepdims=True)
    i2 = jnp.min(jnp.where(rest == m2, lane_f, float(LANES)), axis=1, keepdims=True)
    e = jnp.exp(m2 - m1)
    g1 = 1.0 / (1.0 + e)
    g2 = e / (1.0 + e)
    idx_ref[...] = jnp.where(lane == 0, i1, jnp.where(lane == 1, i2, 0.0)).astype(jnp.int32)
    gate_ref[...] = jnp.where(lane == 0, g1, jnp.where(lane == 1, g2, 0.0))


def _router(x, g, whi, wlo, b, *, tm):
    m, d = x.shape
    return pl.pallas_call(
        _router_kernel, grid=(m // tm,),
        in_specs=[pl.BlockSpec((tm, d), lambda i: (i, 0)),
                  pl.BlockSpec((1, d), lambda i: (0, 0)),
                  pl.BlockSpec((d, LANES), lambda i: (0, 0)),
                  pl.BlockSpec((d, LANES), lambda i: (0, 0)),
                  pl.BlockSpec((1, LANES), lambda i: (0, 0))],
        out_specs=[pl.BlockSpec((tm, d), lambda i: (i, 0)),
                   pl.BlockSpec((tm, LANES), lambda i: (i, 0)),
                   pl.BlockSpec((tm, LANES), lambda i: (i, 0))],
        out_shape=[jax.ShapeDtypeStruct((m, d), F32),
                   jax.ShapeDtypeStruct((m, LANES), jnp.int32),
                   jax.ShapeDtypeStruct((m, LANES), F32)],
        compiler_params=_cp("parallel"), name="router")(x, g, whi, wlo, b)


def _gmm_kernel(te_ref, tv_ref, x_ref, gate_ref, w1_ref, w3_ref, w2_ref, *rest, n_f, tile0):
    o_ref, acc_sc, xb_sc = rest[-3:]
    t = tile0 + pl.program_id(0)
    f = pl.program_id(1)

    @pl.when(f == 0)
    def _():
        acc_sc[...] = jnp.zeros(acc_sc.shape, F32)
        xb_sc[...] = x_ref[...].astype(BF16)

    @pl.when(tv_ref[t] != 0)
    def _():
        _swiglu_accumulate(xb_sc, w1_ref, w3_ref, w2_ref, acc_sc)

    @pl.when(f == n_f - 1)
    def _():
        o_ref[...] = acc_sc[...] * gate_ref[...]


def _gmm(tile_expert, tile_valid, x_part, gate_sorted, w1, w3, w2, *, tile0, y_prev=None):
    rows = gate_sorted.shape[0]
    d, ff = w1.shape[1], w1.shape[2]
    tm, tf = MOE_TILE, MOE_FF_TILE
    n_f = ff // tf
    in_specs = [pl.BlockSpec((tm, d), lambda t, f, te, tv: (t, 0)),
                pl.BlockSpec((tm, 1), lambda t, f, te, tv: (tile0 + t, 0)),
                pl.BlockSpec((None, d, tf), lambda t, f, te, tv: (te[tile0 + t], 0, f)),
                pl.BlockSpec((None, d, tf), lambda t, f, te, tv: (te[tile0 + t], 0, f)),
                pl.BlockSpec((None, tf, d), lambda t, f, te, tv: (te[tile0 + t], f, 0))]
    args = [tile_expert, tile_valid, x_part, gate_sorted, w1, w3, w2]
    aliases = {}
    if y_prev is not None:
        in_specs.append(pl.BlockSpec(memory_space=pl.ANY))
        aliases = {len(args): 0}
        args.append(y_prev)
    grid_spec = pltpu.PrefetchScalarGridSpec(
        num_scalar_prefetch=2, grid=(x_part.shape[0] // tm, n_f), in_specs=in_specs,
        out_specs=pl.BlockSpec((tm, d), lambda t, f, te, tv: (tile0 + t, 0)),
        scratch_shapes=[pltpu.VMEM((tm, d), F32), pltpu.VMEM((tm, d), BF16)])
    return pl.pallas_call(
        functools.partial(_gmm_kernel, n_f=n_f, tile0=tile0), grid_spec=grid_spec,
        out_shape=jax.ShapeDtypeStruct((rows, d), F32), input_output_aliases=aliases,
        compiler_params=_cp("parallel", "arbitrary"), name="moe_experts")(*args)


def _combine_kernel(x_ref, y1_ref, y2_ref, g_ref, o_ref):
    o_ref[...] = x_ref[...] + _rms(y1_ref[...] + y2_ref[...], g_ref[...])


def _combine(x, y1, y2, g, *, tm):
    m, d = x.shape
    row = lambda: pl.BlockSpec((tm, d), lambda i: (i, 0))
    return pl.pallas_call(
        _combine_kernel, grid=(m // tm,),
        in_specs=[row(), row(), row(), pl.BlockSpec((1, d), lambda i: (0, 0))],
        out_specs=row(), out_shape=jax.ShapeDtypeStruct((m, d), F32),
        compiler_params=_cp("parallel"), name="moe_combine")(x, y1, y2, g)


def _moe_block(xs, g4, g5, w_r, b_r, w1, w3, w2):
    pad = LANES - N_EXPERTS
    w_r_p = jnp.pad(w_r, ((0, 0), (0, pad)))
    whi = w_r_p.astype(BF16)
    wlo = (w_r_p - whi.astype(F32)).astype(BF16)
    b_p = jnp.pad(b_r, (0, pad)).reshape(1, LANES)
    hn, idx, gate = [], [], []
    for x in xs:
        h, i, gt = _router(x, g4, whi, wlo, b_p, tm=min(ROW_TILE, x.shape[0]))
        hn.append(h)
        idx.append(i[:, :2])
        gate.append(gt[:, :2])
    hn = jnp.concatenate(hn, axis=0)
    e_flat = jnp.concatenate(idx, axis=0).reshape(-1)
    g_flat = jnp.concatenate(gate, axis=0).reshape(-1)
    n_assign = e_flat.shape[0]
    tm = MOE_TILE
    n_tiles = n_assign // tm + N_EXPERTS
    rows = n_tiles * tm

    onehot = (e_flat[:, None] == jnp.arange(N_EXPERTS, dtype=jnp.int32)[None, :]).astype(jnp.int32)
    rank = jnp.sum((jnp.cumsum(onehot, axis=0) - onehot) * onehot, axis=1)
    cnt = jnp.sum(onehot, axis=0)
    tiles_e = (cnt + tm - 1) // tm
    tile_end = jnp.cumsum(tiles_e)
    row_start = (tile_end - tiles_e) * tm
    dest = row_start[e_flat] + rank
    tile_ids = jnp.arange(n_tiles, dtype=jnp.int32)
    tile_expert = jnp.minimum(jnp.sum((tile_end[None, :] <= tile_ids[:, None]).astype(jnp.int32), axis=1),
                              N_EXPERTS - 1)
    tile_valid = (tile_ids < tile_end[-1]).astype(jnp.int32)
    row_assign = jnp.zeros((rows,), jnp.int32).at[dest].set(jnp.arange(n_assign, dtype=jnp.int32))
    row_ids = jnp.arange(rows, dtype=jnp.int32)
    row_expert = jnp.repeat(tile_expert, tm)
    row_valid = (row_ids - row_start[row_expert]) < cnt[row_expert]
    src_tok = row_assign // 2
    gate_sorted = jnp.where(row_valid, g_flat[row_assign], 0.0).reshape(rows, 1)

    take_rows = lambda a, idx: a.at[idx].get(mode="promise_in_bounds")
    y_sorted = None
    part_tiles = n_tiles // MOE_PARTS
    for part in range(MOE_PARTS):
        tile0 = part * part_tiles
        x_part = take_rows(hn, src_tok[tile0 * tm:(tile0 + part_tiles) * tm])
        y_sorted = _gmm(tile_expert, tile_valid, x_part, gate_sorted, w1, w3, w2, tile0=tile0, y_prev=y_sorted)

    outs = []
    off = 0
    dest2 = dest.reshape(-1, 2)
    for x in xs:
        m = x.shape[0]
        d1 = dest2[off:off + m, 0]
        d2 = dest2[off:off + m, 1]
        outs.append(_combine(x, take_rows(y_sorted, d1), take_rows(y_sorted, d2), g5,
                             tm=min(ROW_TILE, m)))
        off += m
    return outs


def _block_diag_pairs(w):
    nb, bw, _ = w.shape
    w = w.reshape(nb // 2, 2, bw, bw)
    z = jnp.zeros((nb // 2, bw, bw), w.dtype)
    top = jnp.concatenate([w[:, 0], z], axis=2)
    bot = jnp.concatenate([z, w[:, 1]], axis=2)
    return jnp.concatenate([top, bot], axis=1).astype(BF16)


def kernel(x_prompt, x_sample, cache_fox_k, cache_fox_v, cache_fox_logf, state_lru_h, state_lru_conv, state_conv_buf, state_pool_buf, cache_mem_k, cache_mem_v, page_table, mem_prompt, norm_g, w_xq, w_xk, w_xv, w_xo, w_in_e, b_f, lru_conv_w, lru_conv_b, lru_wa, lru_ba, lru_wi, lru_bi, lru_lam, w_out_e, w_ff1, w_ff3, w_ff2, w_in_o, cc_dw_w, cc_dw_b, cc_ln_g, cc_ln_b, pool_w, pool_scale, w_out_o, w_router, b_router, w_e1, w_e3, w_e2):
    bp, seq, d = x_prompt.shape
    bs = x_sample.shape[0]
    depth = norm_g.shape[0]
    page = cache_fox_k.shape[2]
    past_len = page_table.shape[1] * page
    mem_len = mem_prompt.shape[1]
    tm_p = ROW_TILE

    xp = x_prompt.reshape(bp * seq, d)
    xs = x_sample.reshape(bs, d)
    mem = mem_prompt.reshape(bp * mem_len, d)
    vec = lambda v: v.reshape(1, -1)

    fk_p, fv_p, fl_p, lh_p, lc_p, cb_p, pb_p, mk_pl, mv_pl = [], [], [], [], [], [], [], [], []
    fk_s, fv_s, fl_s, lh_s, lc_s, cb_s, pb_s = [], [], [], [], [], [], []

    for l in range(depth):
        g = [vec(norm_g[l, i]) for i in range(norm_g.shape[1])]
        w_kv = jnp.concatenate([w_xk[l], w_xv[l]], axis=1).astype(BF16)
        memkv = _norm_matmul(mem, g[6], w_kv, tm=tm_p, name="mem_kv")
        mk_pl.append(memkv[:, :MEM_W].reshape(bp, mem_len, MEM_H, MEM_HD))
        mv_pl.append(memkv[:, MEM_W:].reshape(bp, mem_len, MEM_H, MEM_HD))

        if l % 2 == 0:
            i = l // 2
            n_main = 2 * LRU_W + 3 * FOX_W
            w_main = w_in_e[i][:, :n_main].astype(BF16)
            w_fl_t = jnp.pad(w_in_e[i][:, n_main:].T, ((0, 16 - FOX_H), (0, 0))).astype(BF16)
            cw, cb = lru_conv_w[i], vec(lru_conv_b[i])
            wa, wi = _block_diag_pairs(lru_wa[i]), _block_diag_pairs(lru_wi[i])
            ba, bi, lam = vec(lru_ba[i]), vec(lru_bi[i]), vec(lru_lam[i])
            bf = b_f[i].reshape(FOX_H, 1)
            w_out = w_out_e[i].astype(BF16)

            z, fl_t = _norm_matmul(xp, g[0], w_main, tm=tm_p, wt=w_fl_t, name="in_proj_even")
            lf_t, c_t = _fox_prep(fl_t, bf, bp, seq)
            lru_out, h_last = _lru_prompt(z, bp, seq, cw, cb, wa, ba, wi, bi, lam)
            att = _fox_prompt(z, c_t, bp, seq)
            xp = _matmul_norm_res([lru_out, att], w_out, g[1], xp, tm=tm_p, name="out_proj_even")
            z3 = z.reshape(bp, seq, n_main)
            fk_p.append(z3[:, :, 2 * LRU_W + FOX_W:2 * LRU_W + 2 * FOX_W].reshape(bp, seq, FOX_H, FOX_HD))
            fv_p.append(z3[:, :, 2 * LRU_W + 2 * FOX_W:].reshape(bp, seq, FOX_H, FOX_HD))
            fl_p.append(lf_t.T.reshape(bp, seq, FOX_H))
            lh_p.append(h_last.reshape(bp, LRU_W))
            lc_p.append(z3[:, seq - (LRU_CONV - 1):, :LRU_W])

            zs, fls_t = _norm_matmul(xs, g[0], w_main, tm=bs, wt=w_fl_t, name="in_proj_even_s")
            lfs_t = _logsig(fls_t, bf)
            pre_t = jnp.swapaxes(state_lru_conv[i], 0, 1)
            lru_out_s, h_s = _lru_sample(zs, pre_t, state_lru_h[i], cw, cb, wa, ba, wi, bi, lam)
            q_s = zs[:, 2 * LRU_W:2 * LRU_W + FOX_W]
            k_s = zs[:, 2 * LRU_W + FOX_W:2 * LRU_W + 2 * FOX_W]
            v_s = zs[:, 2 * LRU_W + 2 * FOX_W:]
            att_s = _fox_sample(page_table, i, q_s.reshape(bs, FOX_H, FOX_HD), k_s.reshape(bs, FOX_H, FOX_HD),
                                v_s.reshape(bs, FOX_H, FOX_HD), lfs_t.T.reshape(bs, FOX_H, 1),
                                jnp.transpose(cache_fox_k, (0, 1, 3, 4, 2)),
                                jnp.transpose(cache_fox_v, (0, 1, 3, 4, 2)),
                                jnp.swapaxes(cache_fox_logf[i], 1, 2))
            xs = _matmul_norm_res([lru_out_s, att_s.reshape(bs, FOX_W)], w_out, g[1], xs, tm=bs,
                                  name="out_proj_even_s")
            fk_s.append(k_s.reshape(bs, 1, FOX_H, FOX_HD))
            fv_s.append(v_s.reshape(bs, 1, FOX_H, FOX_HD))
            fl_s.append(lfs_t.T.reshape(bs, 1, FOX_H))
            lh_s.append(h_s)
            lc_s.append(jnp.concatenate([state_lru_conv[i][:, 1:], zs[:, None, :LRU_W]], axis=1))
        else:
            j = l // 2
            w_in = w_in_o[j].astype(BF16)
            dww, dwb = cc_dw_w[j], vec(cc_dw_b[j])
            lng, lnb = vec(cc_ln_g[j]), vec(cc_ln_b[j])
            pw, ps = pool_w[j].astype(BF16), vec(pool_scale[j])
            w_out = w_out_o[j].astype(BF16)

            z = _norm_matmul(xp, g[0], w_in, tm=tm_p, name="in_proj_odd")
            cv, pool, cbuf, pbuf = _odd_prompt(z, bp, seq, dww, dwb, lng, lnb, pw, ps)
            xp = _matmul_norm_res([cv, pool], w_out, g[1], xp, tm=tm_p, name="out_proj_odd")
            cb_p.append(cbuf)
            pb_p.append(pbuf)

            zs = _norm_matmul(xs, g[0], w_in, tm=bs, name="in_proj_odd_s")
            cv_s, pool_s, glu_s = _odd_sample(zs, jnp.swapaxes(state_conv_buf[j], 0, 1),
                                              jnp.swapaxes(state_pool_buf[j], 0, 1),
                                              dww, dwb, lng, lnb, pw, ps, past_len)
            xs = _matmul_norm_res([cv_s, pool_s], w_out, g[1], xs, tm=bs, name="out_proj_odd_s")
            cb_s.append(jnp.concatenate([state_conv_buf[j][:, 1:], glu_s[:, None, :]], axis=1))
            pb_s.append(jnp.concatenate([state_pool_buf[j][:, 1:], zs[:, None, 2 * CONV_W:]], axis=1))

        wq, wo = w_xq[l].astype(BF16), w_xo[l].astype(BF16)
        q = _norm_matmul(xp, g[2], wq, tm=tm_p, name="xattn_q")
        o = _xattn_prompt(q, memkv, bp, seq)
        xp = _matmul_norm_res([o], wo, g[3], xp, tm=tm_p, name="xattn_o")
        q_s = _norm_matmul(xs, g[2], wq, tm=bs, name="xattn_q_s")
        o_s = _xattn_sample(q_s.reshape(bs, 1, MEM_W), l,
                            cache_mem_k.reshape(depth, bs, mem_len * MEM_H, MEM_HD),
                            cache_mem_v.reshape(depth, bs, mem_len * MEM_H, MEM_HD))
        xs = _matmul_norm_res([o_s.reshape(bs, MEM_W)], wo, g[3], xs, tm=bs, name="xattn_o_s")

        if l % 2 == 0:
            i = l // 2
            w1, w3, w2 = (_cast_bf16(w[i:i + 1])[0] for w in (w_ff1, w_ff3, w_ff2))
            xp = _ffn(xp, g[4], g[5], w1, w3, w2, tm=tm_p)
            xs = _ffn(xs, g[4], g[5], w1, w3, w2, tm=bs)
        else:
            j = l // 2
            xp, xs = _moe_block([xp, xs], g[4], g[5], w_router[j], b_router[j],
                                _cast_bf16(w_e1[j]), _cast_bf16(w_e3[j]), _cast_bf16(w_e2[j]))

    return (xp.reshape(bp, seq, d), xs.reshape(bs, 1, d),
            jnp.stack(fk_p), jnp.stack(fv_p), jnp.stack(fl_p), jnp.stack(lh_p), jnp.stack(lc_p),
            jnp.stack(cb_p), jnp.stack(pb_p), jnp.stack(mk_pl), jnp.stack(mv_pl),
            jnp.stack(fk_s), jnp.stack(fv_s), jnp.stack(fl_s), jnp.stack(lh_s), jnp.stack(lc_s),
            jnp.stack(cb_s), jnp.stack(pb_s))
```

```python
import functools

import jax
import jax.numpy as jnp
from jax import lax
from jax.experimental import pallas as pl
from jax.experimental.pallas import tpu as pltpu

F32 = jnp.float32
BF16 = jnp.bfloat16

D_MODEL = 1024
LRU_W = 512
LRU_CONV = 4
LRU_C = 8.0
FOX_H = 8
FOX_HD = 64
FOX_W = FOX_H * FOX_HD
CONV_W = 512
CONV_K = 31
POOL_W = 512
POOL_WINDOWS = (2, 4, 8, 16)
POOL_GW = POOL_W // len(POOL_WINDOWS)
POOL_BUF = max(POOL_WINDOWS) - 1
MEM_H = 4
MEM_HD = 128
MEM_W = MEM_H * MEM_HD
N_EXPERTS = 8
EPS = 1e-6
NEG = -1e30

LANES = 128
ROW_TILE = 512
SEQ_TILE = 512
SCAN_TILE = 256
ATT_ROWS = 32
CONV_ROWS = 64
ATT_HEADS = 4
PAGES_PER_STEP = 16
MOE_TILE = 512
FF_ROW_SLAB = 256
MOE_FF_TILE = 1792
FFN_FF_TILE = 1408
CAST_BLOCK_BYTES = 8 * 1024 * 1024
MOE_PARTS = 4
VMEM_LIMIT = 56 * 1024 * 1024


def _cp(*sem):
    return pltpu.CompilerParams(dimension_semantics=sem, vmem_limit_bytes=VMEM_LIMIT)


def _rms(x, g):
    return x * lax.rsqrt(jnp.mean(x * x, axis=-1, keepdims=True) + EPS) * g


def _sigmoid(x):
    return 1.0 / (1.0 + jnp.exp(-x))


def _softplus(x):
    return jnp.maximum(x, 0.0) + jnp.log1p(jnp.exp(-jnp.abs(x)))


def _gelu_tanh(x):
    return 0.5 * x * (1.0 + jnp.tanh(0.7978845608028654 * (x + 0.044715 * (x * x * x))))


def _dot(a, b):
    return jnp.dot(a, b, preferred_element_type=F32)


def _dot_nt(a, b):
    return lax.dot_general(a, b, (((1,), (1,)), ((), ())), preferred_element_type=F32)


def _cast_kernel(x_ref, o_ref):
    o_ref[...] = x_ref[...].astype(BF16)


def _cast_bf16(w):
    e, k, n = w.shape
    n_k = pl.cdiv(k * n * 4, CAST_BLOCK_BYTES)
    assert k % (16 * n_k) == 0
    return pl.pallas_call(
        _cast_kernel, grid=(e, n_k),
        in_specs=[pl.BlockSpec((None, k // n_k, n), lambda i, j: (i, j, 0))],
        out_specs=pl.BlockSpec((None, k // n_k, n), lambda i, j: (i, j, 0)),
        out_shape=jax.ShapeDtypeStruct(w.shape, BF16),
        compiler_params=_cp("parallel", "parallel"), name="cast_bf16")(w)


def _norm_matmul_kernel(x_ref, g_ref, w_ref, *rest, has_t):
    hn = _rms(x_ref[...], g_ref[...]).astype(BF16)
    if has_t:
        wt_ref, o_ref, ot_ref = rest
        ot_ref[...] = _dot_nt(wt_ref[...], hn)
    else:
        (o_ref,) = rest
    o_ref[...] = _dot(hn, w_ref[...])


def _norm_matmul(x, g, w, *, tm, wt=None, name):
    m, d = x.shape
    n = w.shape[1]
    in_specs = [pl.BlockSpec((tm, d), lambda i: (i, 0)),
                pl.BlockSpec((1, d), lambda i: (0, 0)),
                pl.BlockSpec((d, n), lambda i: (0, 0))]
    out_shape = [jax.ShapeDtypeStruct((m, n), F32)]
    out_specs = [pl.BlockSpec((tm, n), lambda i: (i, 0))]
    args = [x, g, w]
    if wt is not None:
        in_specs.append(pl.BlockSpec(wt.shape, lambda i: (0, 0)))
        out_shape.append(jax.ShapeDtypeStruct((wt.shape[0], m), F32))
        out_specs.append(pl.BlockSpec((wt.shape[0], tm), lambda i: (0, i)))
        args.append(wt)
    res = pl.pallas_call(
        functools.partial(_norm_matmul_kernel, has_t=wt is not None),
        grid=(m // tm,), in_specs=in_specs, out_specs=out_specs, out_shape=out_shape,
        compiler_params=_cp("parallel"), name=name)(*args)
    return res if wt is not None else res[0]


def _matmul_norm_res_kernel(*refs, widths):
    n_a = len(widths)
    a_refs = refs[:n_a]
    w_ref, g_ref, r_ref, o_ref = refs[n_a:]
    y = None
    off = 0
    for a_ref, k in zip(a_refs, widths):
        part = _dot(a_ref[...].astype(BF16), w_ref[off:off + k, :])
        y = part if y is None else y + part
        off += k
    o_ref[...] = r_ref[...] + _rms(y, g_ref[...])


def _matmul_norm_res(a_list, w, g, resid, *, tm, name):
    m, d = resid.shape
    widths = tuple(a.shape[1] for a in a_list)
    in_specs = [pl.BlockSpec((tm, k), lambda i: (i, 0)) for k in widths]
    in_specs += [pl.BlockSpec(w.shape, lambda i: (0, 0)),
                 pl.BlockSpec((1, d), lambda i: (0, 0)),
                 pl.BlockSpec((tm, d), lambda i: (i, 0))]
    return pl.pallas_call(
        functools.partial(_matmul_norm_res_kernel, widths=widths),
        grid=(m // tm,), in_specs=in_specs,
        out_specs=pl.BlockSpec((tm, d), lambda i: (i, 0)),
        out_shape=jax.ShapeDtypeStruct((m, d), F32),
        compiler_params=_cp("parallel"), name=name)(*a_list, w, g, resid)


def _lru_gates(xc, wa_ref, ba_ref, wi_ref, bi_ref, lam_ref):
    xb = xc.astype(BF16)
    ra, ia = [], []
    for c in range(LRU_W // LANES):
        xs = xb[:, c * LANES:(c + 1) * LANES]
        ra.append(_dot(xs, wa_ref[c]))
        ia.append(_dot(xs, wi_ref[c]))
    r = _sigmoid(jnp.concatenate(ra, axis=1) + ba_ref[...])
    ig = _sigmoid(jnp.concatenate(ia, axis=1) + bi_ref[...])
    log_a = -LRU_C * r * _softplus(-lam_ref[...])
    a = jnp.exp(log_a)
    bx = jnp.sqrt(-jnp.tanh(log_a) * (a * a + 1.0)) * (ig * xc)
    return a, bx


def _lru_prompt_kernel(xl_ref, gate_ref, cw_ref, cb_ref, wa_ref, ba_ref, wi_ref, bi_ref, lam_ref,
                       out_ref, hlast_ref, ext_ref, hc_ref, *, tc, n_t):
    t = pl.program_id(1)

    @pl.when(t == 0)
    def _():
        ext_ref[0:8, :] = jnp.zeros((8, LRU_W), F32)
        hc_ref[...] = jnp.zeros((1, LRU_W), F32)

    xl = xl_ref[...]
    ext_ref[8:8 + tc, :] = xl
    xc = cb_ref[...] + cw_ref[LRU_CONV - 1:LRU_CONV, :] * xl
    for j in range(1, LRU_CONV):
        xc = xc + cw_ref[LRU_CONV - 1 - j:LRU_CONV - j, :] * ext_ref[pl.ds(8 - j, tc), :]
    ext_ref[0:8, :] = ext_ref[tc:tc + 8, :]

    a, b = _lru_gates(xc, wa_ref, ba_ref, wi_ref, bi_ref, lam_ref)
    row = lax.broadcasted_iota(jnp.int32, (tc, 1), 0)
    d = 1
    while d < tc:
        keep = row >= d
        a_sh = jnp.where(keep, pltpu.roll(a, d, 0), 1.0)
        b_sh = jnp.where(keep, pltpu.roll(b, d, 0), 0.0)
        b = a * b_sh + b
        a = a * a_sh
        d *= 2
    h = a * hc_ref[...] + b
    hc_ref[...] = h[tc - 1:tc, :]
    out_ref[...] = (_gelu_tanh(gate_ref[...]) * h).astype(BF16)

    @pl.when(t == n_t - 1)
    def _():
        hlast_ref[...] = h[tc - 1:tc, :]


def _lru_prompt(z, n_b, seq, cw, cb, wa, ba, wi, bi, lam):
    tc = SCAN_TILE
    n_t = seq // tc
    vec = lambda: pl.BlockSpec((1, LRU_W), lambda b, t: (0, 0))
    bd = lambda: pl.BlockSpec((LRU_W // LANES, LANES, LANES), lambda b, t: (0, 0, 0))
    return pl.pallas_call(
        functools.partial(_lru_prompt_kernel, tc=tc, n_t=n_t),
        grid=(n_b, n_t),
        in_specs=[pl.BlockSpec((tc, LRU_W), lambda b, t: (b * n_t + t, 0)),
                  pl.BlockSpec((tc, LRU_W), lambda b, t: (b * n_t + t, 1)),
                  pl.BlockSpec((LRU_CONV, LRU_W), lambda b, t: (0, 0)),
                  vec(), bd(), vec(), bd(), vec(), vec()],
        out_specs=[pl.BlockSpec((tc, LRU_W), lambda b, t: (b * n_t + t, 0)),
                   pl.BlockSpec((None, 1, LRU_W), lambda b, t: (b, 0, 0))],
        out_shape=[jax.ShapeDtypeStruct((n_b * seq, LRU_W), BF16),
                   jax.ShapeDtypeStruct((n_b, 1, LRU_W), F32)],
        scratch_shapes=[pltpu.VMEM((tc + 8, LRU_W), F32), pltpu.VMEM((1, LRU_W), F32)],
        compiler_params=_cp("parallel", "arbitrary"), name="lru_prompt")(
            z, z, cw, cb, wa, ba, wi, bi, lam)


def _lru_sample_kernel(z_ref, pre_ref, h0_ref, cw_ref, cb_ref, wa_ref, ba_ref, wi_ref, bi_ref, lam_ref,
                       out_ref, h_ref):
    xl = z_ref[:, 0:LRU_W]
    gate = z_ref[:, LRU_W:2 * LRU_W]
    xc = cb_ref[...] + cw_ref[LRU_CONV - 1:LRU_CONV, :] * xl
    for k in range(LRU_CONV - 1):
        xc = xc + cw_ref[k:k + 1, :] * pre_ref[k]
    a, bx = _lru_gates(xc, wa_ref, ba_ref, wi_ref, bi_ref, lam_ref)
    h = a * h0_ref[...] + bx
    h_ref[...] = h
    out_ref[...] = (_gelu_tanh(gate) * h).astype(BF16)


def _lru_sample(z, prefix_t, h0, cw, cb, wa, ba, wi, bi, lam):
    n_b = z.shape[0]
    return pl.pallas_call(
        _lru_sample_kernel,
        out_shape=[jax.ShapeDtypeStruct((n_b, LRU_W), BF16), jax.ShapeDtypeStruct((n_b, LRU_W), F32)],
        compiler_params=pltpu.CompilerParams(vmem_limit_bytes=VMEM_LIMIT),
        name="lru_sample")(z, prefix_t, h0, cw, cb, wa, ba, wi, bi, lam)


def _log_sigmoid(x):
    return jnp.minimum(x, 0.0) - jnp.log1p(jnp.exp(-jnp.abs(x)))


def _lane_cumsum(x):
    n = x.shape[1]
    lane = lax.broadcasted_iota(jnp.int32, x.shape, 1)
    d = 1
    while d < n:
        x = x + jnp.where(lane >= d, pltpu.roll(x, d, 1), 0.0)
        d *= 2
    return x


def _fox_prep_kernel(fl_ref, bf_ref, lf_ref, c_ref):
    lf = _log_sigmoid(fl_ref[0:FOX_H, :] + bf_ref[...])
    lf_ref[...] = lf
    c_ref[...] = _lane_cumsum(lf)


def _fox_prep(fl_t, bf, n_b, seq):
    return pl.pallas_call(
        _fox_prep_kernel, grid=(n_b,),
        in_specs=[pl.BlockSpec((fl_t.shape[0], seq), lambda b: (0, b)),
                  pl.BlockSpec((FOX_H, 1), lambda b: (0, 0))],
        out_specs=[pl.BlockSpec((FOX_H, seq), lambda b: (0, b)),
                   pl.BlockSpec((FOX_H, seq), lambda b: (0, b))],
        out_shape=[jax.ShapeDtypeStruct((FOX_H, n_b * seq), F32)] * 2,
        compiler_params=_cp("parallel"), name="fox_prep")(fl_t, bf)


def _logsig_kernel(fl_ref, bf_ref, lf_ref):
    lf_ref[...] = _log_sigmoid(fl_ref[0:FOX_H, :] + bf_ref[...])


def _logsig(fl_t, bf):
    return pl.pallas_call(
        _logsig_kernel, out_shape=jax.ShapeDtypeStruct((FOX_H, fl_t.shape[1]), F32),
        name="fox_logf_sample")(fl_t, bf)


def _fox_prompt_kernel(qi_ref, ki_ref, q_ref, k_ref, v_ref, cq_ref, ck_ref, o_ref,
                       m_sc, l_sc, acc_sc, s_sc, p_sc, al_sc, *, tb, hg):
    grp = pl.program_id(1)
    t = pl.program_id(2)
    qi = qi_ref[t]
    ki = ki_ref[t]
    width = hg * FOX_HD

    @pl.when(ki == 0)
    def _():
        m_sc[...] = jnp.full(m_sc.shape, NEG, F32)
        l_sc[...] = jnp.zeros(l_sc.shape, F32)
        acc_sc[...] = jnp.zeros(acc_sc.shape, F32)

    def step(diagonal):
        q = q_ref[...] * (FOX_HD ** -0.5)
        k = k_ref[...].astype(BF16)
        v = v_ref[...].astype(BF16)
        lane_head = jnp.right_shift(lax.broadcasted_iota(jnp.int32, (tb, width), 1),
                                    FOX_HD.bit_length() - 1)
        rc = ATT_ROWS
        if diagonal:
            row_i = lax.broadcasted_iota(jnp.int32, (rc, tb), 0)
            col_i = lax.broadcasted_iota(jnp.int32, (rc, tb), 1)
        n_lt = tb // LANES
        for h in range(hg):
            s_sc[h] = _dot_nt(jnp.where(lane_head == h, q, 0.0).astype(BF16), k)
            c_q = cq_ref[pl.ds(hg * grp + h, 1), :]
            c_k = ck_ref[pl.ds(hg * grp + h, 1), :]
            bias = c_q[:, 0:1] - c_k
            for c in range(tb // rc):
                rows = slice(c * rc, (c + 1) * rc)
                s = s_sc[h, rows, :] + bias
                if diagonal:
                    s = jnp.where(col_i <= row_i + c * rc, s, NEG)
                s_sc[h, rows, :] = s
                m_prev = m_sc[h, rows, :]
                m_new = jnp.maximum(m_prev, jnp.max(s, axis=1, keepdims=True))
                m_sc[h, rows, :] = m_new
                al_sc[h, rows, :] = jnp.exp(m_prev - m_new)
            for c in range(tb // rc):
                rows = slice(c * rc, (c + 1) * rc)
                m_new = m_sc[h, rows, :]
                p_sum = None
                for j in range(n_lt):
                    cols = slice(j * LANES, (j + 1) * LANES)
                    p = jnp.exp(s_sc[h, rows, cols] - m_new)
                    p_sc[h, rows, cols] = p.astype(BF16)
                    p_sum = p if p_sum is None else p_sum + p
                l_sc[h, rows, :] = al_sc[h, rows, :] * l_sc[h, rows, :] + p_sum
            lt = (h * FOX_HD) // LANES
            pv = _dot(p_sc[h], v)[:, lt * LANES:(lt + 1) * LANES]
            acc_sc[h] = al_sc[h] * acc_sc[h] + pv

    @pl.when(ki < qi)
    def _():
        step(False)

    @pl.when(ki == qi)
    def _():
        step(True)
        lane = lax.broadcasted_iota(jnp.int32, (tb, LANES), 1)
        heads_per_tile = LANES // FOX_HD
        for lt in range(width // LANES):
            o = None
            for i in range(heads_per_tile):
                h = lt * heads_per_tile + i
                o_h = acc_sc[h] / jnp.sum(l_sc[h], axis=1, keepdims=True)
                o = o_h if o is None else jnp.where(lane < i * FOX_HD, o, o_h)
            o_ref[:, lt * LANES:(lt + 1) * LANES] = o.astype(BF16)


def _fox_prompt(z, c_t, n_b, seq):
    tb = SEQ_TILE
    hg = ATT_HEADS
    width = hg * FOX_HD
    n_q = seq // tb
    q_blk = (2 * LRU_W) // width
    k_blk = q_blk + FOX_W // width
    v_blk = k_blk + FOX_W // width
    pairs = [(qi, ki) for qi in range(n_q) for ki in range(qi + 1)]
    qi_list = jnp.asarray([p[0] for p in pairs], jnp.int32)
    ki_list = jnp.asarray([p[1] for p in pairs], jnp.int32)
    grid_spec = pltpu.PrefetchScalarGridSpec(
        num_scalar_prefetch=2, grid=(n_b, FOX_H // hg, len(pairs)),
        in_specs=[
            pl.BlockSpec((tb, width), lambda b, g, t, qi, ki: (b * n_q + qi[t], q_blk + g)),
            pl.BlockSpec((tb, width), lambda b, g, t, qi, ki: (b * n_q + ki[t], k_blk + g)),
            pl.BlockSpec((tb, width), lambda b, g, t, qi, ki: (b * n_q + ki[t], v_blk + g)),
            pl.BlockSpec((FOX_H, tb), lambda b, g, t, qi, ki: (0, b * n_q + qi[t])),
            pl.BlockSpec((FOX_H, tb), lambda b, g, t, qi, ki: (0, b * n_q + ki[t])),
        ],
        out_specs=pl.BlockSpec((tb, width), lambda b, g, t, qi, ki: (b * n_q + qi[t], g)),
        scratch_shapes=[pltpu.VMEM((hg, tb, LANES), F32), pltpu.VMEM((hg, tb, LANES), F32),
                        pltpu.VMEM((hg, tb, LANES), F32), pltpu.VMEM((hg, tb, tb), F32),
                        pltpu.VMEM((hg, tb, tb), BF16), pltpu.VMEM((hg, tb, LANES), F32)])
    return pl.pallas_call(
        functools.partial(_fox_prompt_kernel, tb=tb, hg=hg), grid_spec=grid_spec,
        out_shape=jax.ShapeDtypeStruct((n_b * seq, FOX_W), BF16),
        compiler_params=_cp("parallel", "parallel", "arbitrary"),
        name="fox_prompt")(qi_list, ki_list, z, z, z, c_t, c_t)


def _fox_sample_kernel(pt_ref, q_ref, kn_ref, vn_ref, lfn_ref, *rest, n_pg, n_g):
    k_refs = rest[0:n_pg]
    v_refs = rest[n_pg:2 * n_pg]
    lf_refs = rest[2 * n_pg:3 * n_pg]
    o_ref, m_sc, l_sc, acc_sc, cc_sc = rest[3 * n_pg:]
    g = pl.program_id(1)
    page = lf_refs[0].shape[1]

    @pl.when(g == 0)
    def _():
        m_sc[...] = jnp.full(m_sc.shape, NEG, F32)
        l_sc[...] = jnp.zeros(l_sc.shape, F32)
        acc_sc[...] = jnp.zeros(acc_sc.shape, F32)
        cc_sc[...] = jnp.zeros(cc_sc.shape, F32)


    q = q_ref[...] * (FOX_HD ** -0.5)
    q16 = jnp.concatenate([q, jnp.zeros_like(q)], axis=0)
    row_q = lax.broadcasted_iota(jnp.int32, (2 * FOX_H, FOX_HD), 0)
    q_only = [jnp.where(row_q == h, q16, 0.0).astype(BF16) for h in range(FOX_H)]

    s_parts = []
    for j in range(n_pg):
        s_j = None
        for h in range(FOX_H):
            part = _dot(q_only[h], k_refs[j][h].astype(BF16))
            s_j = part if s_j is None else s_j + part
        s_parts.append(s_j[0:FOX_H])
    s = jnp.concatenate(s_parts, axis=1)
    lf = jnp.concatenate([lf_refs[j][...] for j in range(n_pg)], axis=1)
    c = _lane_cumsum(lf) + cc_sc[...]
    cc_sc[...] = c[:, c.shape[1] - 1:]
    s = s - c
    m_prev = m_sc[...]
    m_new = jnp.maximum(m_prev, jnp.max(s, axis=1, keepdims=True))
    alpha = jnp.exp(m_prev - m_new)
    p = jnp.exp(s - m_new)
    l_sc[...] = alpha * l_sc[...] + jnp.sum(p, axis=1, keepdims=True)
    p16 = jnp.concatenate([p, jnp.zeros_like(p)], axis=0)
    row_p = lax.broadcasted_iota(jnp.int32, (2 * FOX_H, page), 0)
    pv = None
    for j in range(n_pg):
        p_j = p16[:, j * page:(j + 1) * page]
        for h in range(FOX_H):
            part = _dot_nt(jnp.where(row_p == h, p_j, 0.0).astype(BF16), v_refs[j][h].astype(BF16))
            pv = part if pv is None else pv + part
    acc_sc[...] = alpha * acc_sc[...] + pv[0:FOX_H]
    m_sc[...] = m_new

    @pl.when(g == n_g - 1)
    def _():
        s_n = jnp.sum(q * kn_ref[...], axis=1, keepdims=True)
        s_n = s_n - (cc_sc[...] + lfn_ref[...])
        m_p = m_sc[...]
        m_n = jnp.maximum(m_p, s_n)
        al = jnp.exp(m_p - m_n)
        p_n = jnp.exp(s_n - m_n)
        l_n = al * l_sc[...] + p_n
        o_ref[...] = ((al * acc_sc[...] + p_n * vn_ref[...]) / l_n).astype(BF16)


def _fox_sample(page_table, layer, q, k_new, v_new, lf_new, cache_k, cache_v, cache_lf_t):
    n_b, n_pages = page_table.shape
    n_pg = PAGES_PER_STEP
    n_g = n_pages // n_pg
    page = cache_k.shape[4]
    head_spec = lambda: pl.BlockSpec((None, FOX_H, FOX_HD), lambda b, g, pt: (b, 0, 0))

    def kv_spec(j):
        return pl.BlockSpec((None, None, FOX_H, FOX_HD, page),
                            lambda b, g, pt, j=j: (layer, pt[b * n_pages + g * n_pg + j], 0, 0, 0))

    def lf_spec(j):
        return pl.BlockSpec((None, FOX_H, page),
                            lambda b, g, pt, j=j: (pt[b * n_pages + g * n_pg + j], 0, 0))

    in_specs = [head_spec(), head_spec(), head_spec(),
                pl.BlockSpec((None, FOX_H, 1), lambda b, g, pt: (b, 0, 0))]
    in_specs += [kv_spec(j) for j in range(n_pg)]
    in_specs += [kv_spec(j) for j in range(n_pg)]
    in_specs += [lf_spec(j) for j in range(n_pg)]
    grid_spec = pltpu.PrefetchScalarGridSpec(
        num_scalar_prefetch=1, grid=(n_b, n_g), in_specs=in_specs,
        out_specs=pl.BlockSpec((None, FOX_H, FOX_HD), lambda b, g, pt: (b, 0, 0)),
        scratch_shapes=[pltpu.VMEM((FOX_H, 1), F32), pltpu.VMEM((FOX_H, 1), F32),
                        pltpu.VMEM((FOX_H, FOX_HD), F32), pltpu.VMEM((FOX_H, 1), F32)])
    return pl.pallas_call(
        functools.partial(_fox_sample_kernel, n_pg=n_pg, n_g=n_g),
        grid_spec=grid_spec,
        out_shape=jax.ShapeDtypeStruct((n_b, FOX_H, FOX_HD), BF16),
        compiler_params=_cp("parallel", "arbitrary"), name="fox_sample")(
            page_table.reshape(-1), q, k_new, v_new, lf_new,
            *([cache_k] * n_pg), *([cache_v] * n_pg), *([cache_lf_t] * n_pg))


def _xattn_prompt_kernel(q_ref, mk_ref, mv_ref, o_ref):
    outs = []
    for h in range(MEM_H):
        cs = slice(h * MEM_HD, (h + 1) * MEM_HD)
        qh = q_ref[:, cs].astype(BF16)
        s = _dot_nt(qh, mk_ref[:, cs].astype(BF16)) * (MEM_HD ** -0.5)
        p = jnp.exp(s - jnp.max(s, axis=1, keepdims=True))
        l = jnp.sum(p, axis=1, keepdims=True)
        outs.append(_dot(p.astype(BF16), mv_ref[:, cs].astype(BF16)) / l)
    o_ref[...] = jnp.concatenate(outs, axis=1).astype(BF16)


def _xattn_prompt(q, memkv, n_b, seq):
    tb = SEQ_TILE
    n_t = seq // tb
    mem_len = memkv.shape[0] // n_b
    return pl.pallas_call(
        _xattn_prompt_kernel, grid=(n_b, n_t),
        in_specs=[pl.BlockSpec((tb, MEM_W), lambda b, t: (b * n_t + t, 0)),
                  pl.BlockSpec((mem_len, MEM_W), lambda b, t: (b, 0)),
                  pl.BlockSpec((mem_len, MEM_W), lambda b, t: (b, 1))],
        out_specs=pl.BlockSpec((tb, MEM_W), lambda b, t: (b * n_t + t, 0)),
        out_shape=jax.ShapeDtypeStruct((n_b * seq, MEM_W), BF16),
        compiler_params=_cp("parallel", "parallel"), name="xattn_prompt")(q, memkv, memkv)


def _xattn_sample_kernel(q_ref, mk_ref, mv_ref, o_ref):
    q = q_ref[...]
    row = lax.broadcasted_iota(jnp.int32, (16, MEM_W), 0)
    col = lax.broadcasted_iota(jnp.int32, (16, MEM_W), 1)
    head_cols = jnp.right_shift(col, MEM_HD.bit_length() - 1) == row
    q_rows = jnp.where(head_cols, q, 0.0).astype(BF16)
    mem_len = mk_ref.shape[0] // MEM_H

    def heads_on_lanes(ref):
        return jnp.concatenate([ref[pl.ds(h, mem_len, stride=MEM_H), :] for h in range(MEM_H)],
                               axis=1).astype(BF16)

    s = _dot_nt(q_rows, heads_on_lanes(mk_ref)) * (MEM_HD ** -0.5)
    p = jnp.exp(s - jnp.max(s, axis=1, keepdims=True))
    l = jnp.sum(p, axis=1, keepdims=True)
    o = _dot(p.astype(BF16), heads_on_lanes(mv_ref)) / l
    o_ref[...] = jnp.sum(jnp.where(head_cols, o, 0.0), axis=0, keepdims=True).astype(BF16)


def _xattn_sample(q, layer, mk, mv):
    n_b, rows = mk.shape[1], mk.shape[2]
    return pl.pallas_call(
        _xattn_sample_kernel, grid=(n_b,),
        in_specs=[pl.BlockSpec((None, 1, MEM_W), lambda b: (b, 0, 0)),
                  pl.BlockSpec((None, None, rows, MEM_HD), lambda b: (layer, b, 0, 0)),
                  pl.BlockSpec((None, None, rows, MEM_HD), lambda b: (layer, b, 0, 0))],
        out_specs=pl.BlockSpec((None, 1, MEM_W), lambda b: (b, 0, 0)),
        out_shape=jax.ShapeDtypeStruct((n_b, 1, MEM_W), BF16),
        compiler_params=_cp("parallel"), name="xattn_sample")(q, mk, mv)


def _swiglu_accumulate(x_sc, w1_ref, w3_ref, w2_ref, acc_sc):
    tm = x_sc.shape[0]
    slab = min(tm, FF_ROW_SLAB)
    for r in range(tm // slab):
        rows = slice(r * slab, (r + 1) * slab)
        x = x_sc[rows, :]
        h1 = _dot(x, w1_ref[...])
        h3 = _dot(x, w3_ref[...])
        hh = (h1 * _sigmoid(h1) * h3).astype(BF16)
        acc_sc[rows, :] += _dot(hh, w2_ref[...])


def _ffn_kernel(x_ref, g4_ref, g5_ref, w1_ref, w3_ref, w2_ref, o_ref, hn_sc, acc_sc, *, n_f):
    f = pl.program_id(1)

    @pl.when(f == 0)
    def _():
        hn_sc[...] = _rms(x_ref[...], g4_ref[...]).astype(BF16)
        acc_sc[...] = jnp.zeros(acc_sc.shape, F32)

    _swiglu_accumulate(hn_sc, w1_ref, w3_ref, w2_ref, acc_sc)

    @pl.when(f == n_f - 1)
    def _():
        o_ref[...] = x_ref[...] + _rms(acc_sc[...], g5_ref[...])


def _ffn(x, g4, g5, w1, w3, w2, *, tm):
    m, d = x.shape
    ff = w1.shape[1]
    tf = FFN_FF_TILE
    n_f = ff // tf
    return pl.pallas_call(
        functools.partial(_ffn_kernel, n_f=n_f), grid=(m // tm, n_f),
        in_specs=[pl.BlockSpec((tm, d), lambda i, f: (i, 0)),
                  pl.BlockSpec((1, d), lambda i, f: (0, 0)),
                  pl.BlockSpec((1, d), lambda i, f: (0, 0)),
                  pl.BlockSpec((d, tf), lambda i, f: (0, f)),
                  pl.BlockSpec((d, tf), lambda i, f: (0, f)),
                  pl.BlockSpec((tf, d), lambda i, f: (f, 0))],
        out_specs=pl.BlockSpec((tm, d), lambda i, f: (i, 0)),
        out_shape=jax.ShapeDtypeStruct((m, d), F32),
        scratch_shapes=[pltpu.VMEM((tm, d), BF16), pltpu.VMEM((tm, d), F32)],
        compiler_params=_cp("parallel", "arbitrary"), name="ffn")(x, g4, g5, w1, w3, w2)


def _layernorm_silu(x, g, b):
    mu = jnp.mean(x, axis=-1, keepdims=True)
    xc = x - mu
    y = xc * lax.rsqrt(jnp.mean(xc * xc, axis=-1, keepdims=True) + EPS) * g + b
    return y * _sigmoid(y)


def _odd_prompt_kernel(a_ref, gt_ref, up_ref, dww_ref, dwb_ref, lng_ref, lnb_ref, pw_ref, ps_ref,
                       cv_ref, pool_ref, cbuf_ref, pbuf_ref, eg_ref, eu_ref, sh_ref, *, tc, n_t):
    t = pl.program_id(1)
    halo_g, halo_u = 32, 16

    @pl.when(t == 0)
    def _():
        eg_ref[0:halo_g, :] = jnp.zeros((halo_g, CONV_W), F32)
        eu_ref[0:halo_u, :] = jnp.zeros((halo_u, POOL_W), F32)

    glu = a_ref[...] * _sigmoid(gt_ref[...])
    up = up_ref[...]
    eg_ref[halo_g:halo_g + tc, :] = glu
    eu_ref[halo_u:halo_u + tc, :] = up

    ext = eg_ref[...]
    n_ext = tc + halo_g
    sh_ref[0] = ext
    for s in range(1, 8):
        sh_ref[s] = pltpu.roll(ext, n_ext - s, 0)
    rc = CONV_ROWS
    for c in range(tc // rc):
        parts = []
        for j in range(CONV_W // LANES):
            cols = slice(j * LANES, (j + 1) * LANES)
            acc = jnp.broadcast_to(dwb_ref[:, cols], (rc, LANES))
            for k in range(CONV_K):
                off = k + halo_g - (CONV_K - 1)
                acc = acc + dww_ref[k:k + 1, cols] * sh_ref[off % 8, pl.ds(c * rc + (off // 8) * 8, rc), cols]
            parts.append(acc)
        cv_ref[c * rc:(c + 1) * rc, :] = _layernorm_silu(
            jnp.concatenate(parts, axis=1), lng_ref[...], lnb_ref[...]).astype(BF16)

    pos = t * tc + lax.broadcasted_iota(jnp.int32, (tc, 1), 0)
    outs = []
    for gi, w in enumerate(POOL_WINDOWS):
        cs = slice(gi * POOL_GW, (gi + 1) * POOL_GW)
        u_g = up[:, cs]
        win = u_g
        for j in range(1, w):
            win = win + eu_ref[pl.ds(halo_u - j, tc), cs]
        cnt = jnp.minimum(pos + 1, w).astype(F32)
        dlt = win / cnt - u_g
        outs.append(_dot(dlt.astype(BF16), pw_ref[gi]))
    pool_ref[...] = (jnp.concatenate(outs, axis=1) * ps_ref[...]).astype(BF16)

    @pl.when(t == n_t - 1)
    def _():
        cbuf_ref[...] = eg_ref[pl.ds(halo_g + tc - (CONV_K - 1), CONV_K - 1), :]
        pbuf_ref[...] = eu_ref[pl.ds(halo_u + tc - POOL_BUF, POOL_BUF), :]

    eg_ref[0:halo_g, :] = eg_ref[tc:tc + halo_g, :]
    eu_ref[0:halo_u, :] = eu_ref[tc:tc + halo_u, :]


def _odd_prompt(z, n_b, seq, dww, dwb, lng, lnb, pw, ps):
    tc = SEQ_TILE
    n_t = seq // tc
    vec = lambda: pl.BlockSpec((1, CONV_W), lambda b, t: (0, 0))
    return pl.pallas_call(
        functools.partial(_odd_prompt_kernel, tc=tc, n_t=n_t), grid=(n_b, n_t),
        in_specs=[pl.BlockSpec((tc, CONV_W), lambda b, t: (b * n_t + t, 0)),
                  pl.BlockSpec((tc, CONV_W), lambda b, t: (b * n_t + t, 1)),
                  pl.BlockSpec((tc, POOL_W), lambda b, t: (b * n_t + t, 2)),
                  pl.BlockSpec((CONV_K, CONV_W), lambda b, t: (0, 0)),
                  vec(), vec(), vec(),
                  pl.BlockSpec((len(POOL_WINDOWS), POOL_GW, POOL_GW), lambda b, t: (0, 0, 0)),
                  vec()],
        out_specs=[pl.BlockSpec((tc, CONV_W), lambda b, t: (b * n_t + t, 0)),
                   pl.BlockSpec((tc, POOL_W), lambda b, t: (b * n_t + t, 0)),
                   pl.BlockSpec((None, CONV_K - 1, CONV_W), lambda b, t: (b, 0, 0)),
                   pl.BlockSpec((None, POOL_BUF, POOL_W), lambda b, t: (b, 0, 0))],
        out_shape=[jax.ShapeDtypeStruct((n_b * seq, CONV_W), BF16),
                   jax.ShapeDtypeStruct((n_b * seq, POOL_W), BF16),
                   jax.ShapeDtypeStruct((n_b, CONV_K - 1, CONV_W), F32),
                   jax.ShapeDtypeStruct((n_b, POOL_BUF, POOL_W), F32)],
        scratch_shapes=[pltpu.VMEM((tc + 32, CONV_W), F32), pltpu.VMEM((tc + 16, POOL_W), F32),
                        pltpu.VMEM((8, tc + 32, CONV_W), F32)],
        compiler_params=_cp("parallel", "arbitrary"), name="odd_prompt")(
            z, z, z, dww, dwb, lng, lnb, pw, ps)


def _odd_sample_kernel(z_ref, cbuf_ref, pbuf_ref, dww_ref, dwb_ref, lng_ref, lnb_ref, pw_ref, ps_ref,
                       cv_ref, pool_ref, glu_ref, *, pos0):
    glu = z_ref[:, 0:CONV_W] * _sigmoid(z_ref[:, CONV_W:2 * CONV_W])
    up = z_ref[:, 2 * CONV_W:2 * CONV_W + POOL_W]
    glu_ref[...] = glu
    acc = dwb_ref[...] + dww_ref[CONV_K - 1:CONV_K, :] * glu
    for k in range(CONV_K - 1):
        acc = acc + dww_ref[k:k + 1, :] * cbuf_ref[k]
    cv_ref[...] = _layernorm_silu(acc, lng_ref[...], lnb_ref[...]).astype(BF16)
    outs = []
    for gi, w in enumerate(POOL_WINDOWS):
        cs = slice(gi * POOL_GW, (gi + 1) * POOL_GW)
        u_g = up[:, cs]
        win = u_g
        for j in range(1, w):
            win = win + pbuf_ref[POOL_BUF - j][:, cs]
        dlt = win / float(min(pos0 + 1, w)) - u_g
        outs.append(_dot(dlt.astype(BF16), pw_ref[gi]))
    pool_ref[...] = (jnp.concatenate(outs, axis=1) * ps_ref[...]).astype(BF16)


def _odd_sample(z, cbuf_t, pbuf_t, dww, dwb, lng, lnb, pw, ps, pos0):
    n_b = z.shape[0]
    return pl.pallas_call(
        functools.partial(_odd_sample_kernel, pos0=pos0),
        out_shape=[jax.ShapeDtypeStruct((n_b, CONV_W), BF16),
                   jax.ShapeDtypeStruct((n_b, POOL_W), BF16),
                   jax.ShapeDtypeStruct((n_b, CONV_W), F32)],
        compiler_params=pltpu.CompilerParams(vmem_limit_bytes=VMEM_LIMIT),
        name="odd_sample")(z, cbuf_t, pbuf_t, dww, dwb, lng, lnb, pw, ps)


def _router_kernel(x_ref, g_ref, whi_ref, wlo_ref, b_ref, *rest):
    hn_ref, idx_ref, gate_ref = rest[-3:]
    hn = _rms(x_ref[...], g_ref[...])
    hb = hn.astype(BF16)
    hn_ref[...] = hn
    hlo = (hn - hb.astype(F32)).astype(BF16)
    logits = _dot(hb, whi_ref[...]) + (_dot(hb, wlo_ref[...]) + _dot(hlo, whi_ref[...])) + b_ref[...]
    lane = lax.broadcasted_iota(jnp.int32, logits.shape, 1)
    lane_f = lane.astype(F32)
    logits = jnp.where(lane < N_EXPERTS, logits, NEG)
    m1 = jnp.max(logits, axis=1, keepdims=True)
    i1 = jnp.min(jnp.where(logits == m1, lane_f, float(LANES)), axis=1, keepdims=True)
    rest = jnp.where(lane_f == i1, NEG, logits)
    m2 = jnp.max(rest, axis=1, keepdims=True)
    i2 = jnp.min(jnp.where(rest == m2, lane_f, float(LANES)), axis=1, keepdims=True)
    e = jnp.exp(m2 - m1)
    g1 = 1.0 / (1.0 + e)
    g2 = e / (1.0 + e)
    idx_ref[...] = jnp.where(lane == 0, i1, jnp.where(lane == 1, i2, 0.0)).astype(jnp.int32)
    gate_ref[...] = jnp.where(lane == 0, g1, jnp.where(lane == 1, g2, 0.0))


def _router(x, g, whi, wlo, b, *, tm, after=()):
    m, d = x.shape
    return pl.pallas_call(
        _router_kernel, grid=(m // tm,),
        in_specs=[pl.BlockSpec((tm, d), lambda i: (i, 0)),
                  pl.BlockSpec((1, d), lambda i: (0, 0)),
                  pl.BlockSpec((d, LANES), lambda i: (0, 0)),
                  pl.BlockSpec((d, LANES), lambda i: (0, 0)),
                  pl.BlockSpec((1, LANES), lambda i: (0, 0))]
        + [pl.BlockSpec(memory_space=pl.ANY) for _ in after],
        out_specs=[pl.BlockSpec((tm, d), lambda i: (i, 0)),
                   pl.BlockSpec((tm, LANES), lambda i: (i, 0)),
                   pl.BlockSpec((tm, LANES), lambda i: (i, 0))],
        out_shape=[jax.ShapeDtypeStruct((m, d), F32),
                   jax.ShapeDtypeStruct((m, LANES), jnp.int32),
                   jax.ShapeDtypeStruct((m, LANES), F32)],
        compiler_params=_cp("parallel"), name="router")(x, g, whi, wlo, b, *after)


def _gmm_kernel(te_ref, tv_ref, x_ref, gate_ref, w1_ref, w3_ref, w2_ref, *rest, n_f, tile0):
    o_ref, acc_sc, xb_sc = rest[-3:]
    t = tile0 + pl.program_id(0)
    f = pl.program_id(1)

    @pl.when(f == 0)
    def _():
        acc_sc[...] = jnp.zeros(acc_sc.shape, F32)
        xb_sc[...] = x_ref[...].astype(BF16)

    @pl.when(tv_ref[t] != 0)
    def _():
        _swiglu_accumulate(xb_sc, w1_ref, w3_ref, w2_ref, acc_sc)

    @pl.when(f == n_f - 1)
    def _():
        o_ref[...] = acc_sc[...] * gate_ref[...]


def _gmm(tile_expert, tile_valid, x_part, gate_sorted, w1, w3, w2, *, tile0, y_prev=None):
    rows = gate_sorted.shape[0]
    d, ff = w1.shape[1], w1.shape[2]
    tm, tf = MOE_TILE, MOE_FF_TILE
    n_f = ff // tf
    in_specs = [pl.BlockSpec((tm, d), lambda t, f, te, tv: (t, 0)),
                pl.BlockSpec((tm, 1), lambda t, f, te, tv: (tile0 + t, 0)),
                pl.BlockSpec((None, d, tf), lambda t, f, te, tv: (te[tile0 + t], 0, f)),
                pl.BlockSpec((None, d, tf), lambda t, f, te, tv: (te[tile0 + t], 0, f)),
                pl.BlockSpec((None, tf, d), lambda t, f, te, tv: (te[tile0 + t], f, 0))]
    args = [tile_expert, tile_valid, x_part, gate_sorted, w1, w3, w2]
    aliases = {}
    if y_prev is not None:
        in_specs.append(pl.BlockSpec(memory_space=pl.ANY))
        aliases = {len(args): 0}
        args.append(y_prev)
    grid_spec = pltpu.PrefetchScalarGridSpec(
        num_scalar_prefetch=2, grid=(x_part.shape[0] // tm, n_f), in_specs=in_specs,
        out_specs=pl.BlockSpec((tm, d), lambda t, f, te, tv: (tile0 + t, 0)),
        scratch_shapes=[pltpu.VMEM((tm, d), F32), pltpu.VMEM((tm, d), BF16)])
    return pl.pallas_call(
        functools.partial(_gmm_kernel, n_f=n_f, tile0=tile0), grid_spec=grid_spec,
        out_shape=jax.ShapeDtypeStruct((rows, d), F32), input_output_aliases=aliases,
        compiler_params=_cp("parallel", "arbitrary"), name="moe_experts")(*args)


def _combine_kernel(x_ref, y1_ref, y2_ref, g_ref, o_ref):
    o_ref[...] = x_ref[...] + _rms(y1_ref[...] + y2_ref[...], g_ref[...])


def _combine(x, y1, y2, g, *, tm):
    m, d = x.shape
    row = lambda: pl.BlockSpec((tm, d), lambda i: (i, 0))
    return pl.pallas_call(
        _combine_kernel, grid=(m // tm,),
        in_specs=[row(), row(), row(), pl.BlockSpec((1, d), lambda i: (0, 0))],
        out_specs=row(), out_shape=jax.ShapeDtypeStruct((m, d), F32),
        compiler_params=_cp("parallel"), name="moe_combine")(x, y1, y2, g)


def _moe_block(xs, g4, g5, w_r, b_r, w1, w3, w2):
    pad = LANES - N_EXPERTS
    w_r_p = jnp.pad(w_r, ((0, 0), (0, pad)))
    whi = w_r_p.astype(BF16)
    wlo = (w_r_p - whi.astype(F32)).astype(BF16)
    b_p = jnp.pad(b_r, (0, pad)).reshape(1, LANES)
    hn, idx, gate = [], [], []
    for x in xs:
        h, i, gt = _router(x, g4, whi, wlo, b_p, tm=min(ROW_TILE, x.shape[0]), after=(w1, w3, w2))
        hn.append(h)
        idx.append(i[:, :2])
        gate.append(gt[:, :2])
    hn = jnp.concatenate(hn, axis=0)
    e_flat = jnp.concatenate(idx, axis=0).reshape(-1)
    g_flat = jnp.concatenate(gate, axis=0).reshape(-1)
    n_assign = e_flat.shape[0]
    tm = MOE_TILE
    n_tiles = n_assign // tm + N_EXPERTS
    rows = n_tiles * tm

    onehot = (e_flat[:, None] == jnp.arange(N_EXPERTS, dtype=jnp.int32)[None, :]).astype(jnp.int32)
    rank = jnp.sum((jnp.cumsum(onehot, axis=0) - onehot) * onehot, axis=1)
    cnt = jnp.sum(onehot, axis=0)
    tiles_e = (cnt + tm - 1) // tm
    tile_end = jnp.cumsum(tiles_e)
    row_start = (tile_end - tiles_e) * tm
    dest = row_start[e_flat] + rank
    tile_ids = jnp.arange(n_tiles, dtype=jnp.int32)
    tile_expert = jnp.minimum(jnp.sum((tile_end[None, :] <= tile_ids[:, None]).astype(jnp.int32), axis=1),
                              N_EXPERTS - 1)
    tile_valid = (tile_ids < tile_end[-1]).astype(jnp.int32)
    row_assign = jnp.zeros((rows,), jnp.int32).at[dest].set(jnp.arange(n_assign, dtype=jnp.int32))
    row_ids = jnp.arange(rows, dtype=jnp.int32)
    row_expert = jnp.repeat(tile_expert, tm)
    row_valid = (row_ids - row_start[row_expert]) < cnt[row_expert]
    src_tok = row_assign // 2
    gate_sorted = jnp.where(row_valid, g_flat[row_assign], 0.0).reshape(rows, 1)

    take_rows = lambda a, idx: a.at[idx].get(mode="promise_in_bounds")
    y_sorted = None
    part_tiles = n_tiles // MOE_PARTS
    for part in range(MOE_PARTS):
        tile0 = part * part_tiles
        x_part = take_rows(hn, src_tok[tile0 * tm:(tile0 + part_tiles) * tm])
        y_sorted = _gmm(tile_expert, tile_valid, x_part, gate_sorted, w1, w3, w2, tile0=tile0, y_prev=y_sorted)

    outs = []
    off = 0
    dest2 = dest.reshape(-1, 2)
    for x in xs:
        m = x.shape[0]
        d1 = dest2[off:off + m, 0]
        d2 = dest2[off:off + m, 1]
        outs.append(_combine(x, take_rows(y_sorted, d1), take_rows(y_sorted, d2), g5,
                             tm=min(ROW_TILE, m)))
        off += m
    return outs


def _block_diag_pairs(w):
    nb, bw, _ = w.shape
    w = w.reshape(nb // 2, 2, bw, bw)
    z = jnp.zeros((nb // 2, bw, bw), w.dtype)
    top = jnp.concatenate([w[:, 0], z], axis=2)
    bot = jnp.concatenate([z, w[:, 1]], axis=2)
    return jnp.concatenate([top, bot], axis=1).astype(BF16)


def kernel(x_prompt, x_sample, cache_fox_k, cache_fox_v, cache_fox_logf, state_lru_h, state_lru_conv, state_conv_buf, state_pool_buf, cache_mem_k, cache_mem_v, page_table, mem_prompt, norm_g, w_xq, w_xk, w_xv, w_xo, w_in_e, b_f, lru_conv_w, lru_conv_b, lru_wa, lru_ba, lru_wi, lru_bi, lru_lam, w_out_e, w_ff1, w_ff3, w_ff2, w_in_o, cc_dw_w, cc_dw_b, cc_ln_g, cc_ln_b, pool_w, pool_scale, w_out_o, w_router, b_router, w_e1, w_e3, w_e2):
    bp, seq, d = x_prompt.shape
    bs = x_sample.shape[0]
    depth = norm_g.shape[0]
    page = cache_fox_k.shape[2]
    past_len = page_table.shape[1] * page
    mem_len = mem_prompt.shape[1]
    tm_p = ROW_TILE

    xp = x_prompt.reshape(bp * seq, d)
    xs = x_sample.reshape(bs, d)
    mem = mem_prompt.reshape(bp * mem_len, d)
    vec = lambda v: v.reshape(1, -1)

    fk_p, fv_p, fl_p, lh_p, lc_p, cb_p, pb_p, mk_pl, mv_pl = [], [], [], [], [], [], [], [], []
    fk_s, fv_s, fl_s, lh_s, lc_s, cb_s, pb_s = [], [], [], [], [], [], []

    for l in range(depth):
        g = [vec(norm_g[l, i]) for i in range(norm_g.shape[1])]
        w_kv = jnp.concatenate([w_xk[l], w_xv[l]], axis=1).astype(BF16)
        memkv = _norm_matmul(mem, g[6], w_kv, tm=tm_p, name="mem_kv")
        mk_pl.append(memkv[:, :MEM_W].reshape(bp, mem_len, MEM_H, MEM_HD))
        mv_pl.append(memkv[:, MEM_W:].reshape(bp, mem_len, MEM_H, MEM_HD))

        if l % 2 == 0:
            i = l // 2
            n_main = 2 * LRU_W + 3 * FOX_W
            w_main = w_in_e[i][:, :n_main].astype(BF16)
            w_fl_t = jnp.pad(w_in_e[i][:, n_main:].T, ((0, 16 - FOX_H), (0, 0))).astype(BF16)
            cw, cb = lru_conv_w[i], vec(lru_conv_b[i])
            wa, wi = _block_diag_pairs(lru_wa[i]), _block_diag_pairs(lru_wi[i])
            ba, bi, lam = vec(lru_ba[i]), vec(lru_bi[i]), vec(lru_lam[i])
            bf = b_f[i].reshape(FOX_H, 1)
            w_out = w_out_e[i].astype(BF16)

            z, fl_t = _norm_matmul(xp, g[0], w_main, tm=tm_p, wt=w_fl_t, name="in_proj_even")
            lf_t, c_t = _fox_prep(fl_t, bf, bp, seq)
            lru_out, h_last = _lru_prompt(z, bp, seq, cw, cb, wa, ba, wi, bi, lam)
            att = _fox_prompt(z, c_t, bp, seq)
            xp = _matmul_norm_res([lru_out, att], w_out, g[1], xp, tm=tm_p, name="out_proj_even")
            z3 = z.reshape(bp, seq, n_main)
            fk_p.append(z3[:, :, 2 * LRU_W + FOX_W:2 * LRU_W + 2 * FOX_W].reshape(bp, seq, FOX_H, FOX_HD))
            fv_p.append(z3[:, :, 2 * LRU_W + 2 * FOX_W:].reshape(bp, seq, FOX_H, FOX_HD))
            fl_p.append(lf_t.T.reshape(bp, seq, FOX_H))
            lh_p.append(h_last.reshape(bp, LRU_W))
            lc_p.append(z3[:, seq - (LRU_CONV - 1):, :LRU_W])

            zs, fls_t = _norm_matmul(xs, g[0], w_main, tm=bs, wt=w_fl_t, name="in_proj_even_s")
            lfs_t = _logsig(fls_t, bf)
            pre_t = jnp.swapaxes(state_lru_conv[i], 0, 1)
            lru_out_s, h_s = _lru_sample(zs, pre_t, state_lru_h[i], cw, cb, wa, ba, wi, bi, lam)
            q_s = zs[:, 2 * LRU_W:2 * LRU_W + FOX_W]
            k_s = zs[:, 2 * LRU_W + FOX_W:2 * LRU_W + 2 * FOX_W]
            v_s = zs[:, 2 * LRU_W + 2 * FOX_W:]
            att_s = _fox_sample(page_table, i, q_s.reshape(bs, FOX_H, FOX_HD), k_s.reshape(bs, FOX_H, FOX_HD),
                                v_s.reshape(bs, FOX_H, FOX_HD), lfs_t.T.reshape(bs, FOX_H, 1),
                                jnp.transpose(cache_fox_k, (0, 1, 3, 4, 2)),
                                jnp.transpose(cache_fox_v, (0, 1, 3, 4, 2)),
                                jnp.swapaxes(cache_fox_logf[i], 1, 2))
            xs = _matmul_norm_res([lru_out_s, att_s.reshape(bs, FOX_W)], w_out, g[1], xs, tm=bs,
                                  name="out_proj_even_s")
            fk_s.append(k_s.reshape(bs, 1, FOX_H, FOX_HD))
            fv_s.append(v_s.reshape(bs, 1, FOX_H, FOX_HD))
            fl_s.append(lfs_t.T.reshape(bs, 1, FOX_H))
            lh_s.append(h_s)
            lc_s.append(jnp.concatenate([state_lru_conv[i][:, 1:], zs[:, None, :LRU_W]], axis=1))
        else:
            j = l // 2
            w_in = w_in_o[j].astype(BF16)
            dww, dwb = cc_dw_w[j], vec(cc_dw_b[j])
            lng, lnb = vec(cc_ln_g[j]), vec(cc_ln_b[j])
            pw, ps = pool_w[j].astype(BF16), vec(pool_scale[j])
            w_out = w_out_o[j].astype(BF16)

            z = _norm_matmul(xp, g[0], w_in, tm=tm_p, name="in_proj_odd")
            cv, pool, cbuf, pbuf = _odd_prompt(z, bp, seq, dww, dwb, lng, lnb, pw, ps)
            xp = _matmul_norm_res([cv, pool], w_out, g[1], xp, tm=tm_p, name="out_proj_odd")
            cb_p.append(cbuf)
            pb_p.append(pbuf)

            zs = _norm_matmul(xs, g[0], w_in, tm=bs, name="in_proj_odd_s")
            cv_s, pool_s, glu_s = _odd_sample(zs, jnp.swapaxes(state_conv_buf[j], 0, 1),
                                              jnp.swapaxes(state_pool_buf[j], 0, 1),
                                              dww, dwb, lng, lnb, pw, ps, past_len)
            xs = _matmul_norm_res([cv_s, pool_s], w_out, g[1], xs, tm=bs, name="out_proj_odd_s")
            cb_s.append(jnp.concatenate([state_conv_buf[j][:, 1:], glu_s[:, None, :]], axis=1))
            pb_s.append(jnp.concatenate([state_pool_buf[j][:, 1:], zs[:, None, 2 * CONV_W:]], axis=1))

        wq, wo = w_xq[l].astype(BF16), w_xo[l].astype(BF16)
        q = _norm_matmul(xp, g[2], wq, tm=tm_p, name="xattn_q")
        o = _xattn_prompt(q, memkv, bp, seq)
        xp = _matmul_norm_res([o], wo, g[3], xp, tm=tm_p, name="xattn_o")
        q_s = _norm_matmul(xs, g[2], wq, tm=bs, name="xattn_q_s")
        o_s = _xattn_sample(q_s.reshape(bs, 1, MEM_W), l,
                            cache_mem_k.reshape(depth, bs, mem_len * MEM_H, MEM_HD),
                            cache_mem_v.reshape(depth, bs, mem_len * MEM_H, MEM_HD))
        xs = _matmul_norm_res([o_s.reshape(bs, MEM_W)], wo, g[3], xs, tm=bs, name="xattn_o_s")

        if l % 2 == 0:
            i = l // 2
            w1, w3, w2 = (_cast_bf16(w[i:i + 1])[0] for w in (w_ff1, w_ff3, w_ff2))
            xp = _ffn(xp, g[4], g[5], w1, w3, w2, tm=tm_p)
            xs = _ffn(xs, g[4], g[5], w1, w3, w2, tm=bs)
        else:
            j = l // 2
            xp, xs = _moe_block([xp, xs], g[4], g[5], w_router[j], b_router[j],
                                _cast_bf16(w_e1[j]), _cast_bf16(w_e3[j]), _cast_bf16(w_e2[j]))

    return (xp.reshape(bp, seq, d), xs.reshape(bs, 1, d),
            jnp.stack(fk_p), jnp.stack(fv_p), jnp.stack(fl_p), jnp.stack(lh_p), jnp.stack(lc_p),
            jnp.stack(cb_p), jnp.stack(pb_p), jnp.stack(mk_pl), jnp.stack(mv_pl),
            jnp.stack(fk_s), jnp.stack(fv_s), jnp.stack(fl_s), jnp.stack(lh_s), jnp.stack(lc_s),
            jnp.stack(cb_s), jnp.stack(pb_s))
```

```python
import functools

import jax
import jax.numpy as jnp
from jax import lax
from jax.experimental import pallas as pl
from jax.experimental.pallas import tpu as pltpu

F32 = jnp.float32
BF16 = jnp.bfloat16

D_MODEL = 1024
LRU_W = 512
LRU_CONV = 4
LRU_C = 8.0
FOX_H = 8
FOX_HD = 64
FOX_W = FOX_H * FOX_HD
CONV_W = 512
CONV_K = 31
POOL_W = 512
POOL_WINDOWS = (2, 4, 8, 16)
POOL_GW = POOL_W // len(POOL_WINDOWS)
POOL_BUF = max(POOL_WINDOWS) - 1
MEM_H = 4
MEM_HD = 128
MEM_W = MEM_H * MEM_HD
N_EXPERTS = 8
EPS = 1e-6
NEG = -1e30

LANES = 128
ROW_TILE = 512
SEQ_TILE = 512
SCAN_TILE = 256
ATT_ROWS = 32
CONV_ROWS = 64
ATT_HEADS = 4
PAGES_PER_STEP = 16
MOE_TILE = 512
FF_ROW_SLAB = 256
MOE_FF_TILE = 1792
FFN_FF_TILE = 1408
CAST_BLOCK_BYTES = 8 * 1024 * 1024
MOE_PARTS = 4
VMEM_LIMIT = 56 * 1024 * 1024


def _cp(*sem):
    return pltpu.CompilerParams(dimension_semantics=sem, vmem_limit_bytes=VMEM_LIMIT)


def _rms(x, g):
    return x * lax.rsqrt(jnp.mean(x * x, axis=-1, keepdims=True) + EPS) * g


def _sigmoid(x):
    return 1.0 / (1.0 + jnp.exp(-x))


def _softplus(x):
    return jnp.maximum(x, 0.0) + jnp.log1p(jnp.exp(-jnp.abs(x)))


def _gelu_tanh(x):
    return 0.5 * x * (1.0 + jnp.tanh(0.7978845608028654 * (x + 0.044715 * (x * x * x))))


def _dot(a, b):
    return jnp.dot(a, b, preferred_element_type=F32)


def _dot_nt(a, b):
    return lax.dot_general(a, b, (((1,), (1,)), ((), ())), preferred_element_type=F32)


def _cast_kernel(x_ref, o_ref):
    o_ref[...] = x_ref[...].astype(BF16)


def _cast_bf16(w):
    e, k, n = w.shape
    n_k = pl.cdiv(k * n * 4, CAST_BLOCK_BYTES)
    assert k % (16 * n_k) == 0
    return pl.pallas_call(
        _cast_kernel, grid=(e, n_k),
        in_specs=[pl.BlockSpec((None, k // n_k, n), lambda i, j: (i, j, 0))],
        out_specs=pl.BlockSpec((None, k // n_k, n), lambda i, j: (i, j, 0)),
        out_shape=jax.ShapeDtypeStruct(w.shape, BF16),
        compiler_params=_cp("parallel", "parallel"), name="cast_bf16")(w)


def _norm_matmul_kernel(x_ref, g_ref, w_ref, *rest, has_t):
    hn = _rms(x_ref[...], g_ref[...]).astype(BF16)
    if has_t:
        wt_ref, o_ref, ot_ref = rest
        ot_ref[...] = _dot_nt(wt_ref[...], hn)
    else:
        (o_ref,) = rest
    o_ref[...] = _dot(hn, w_ref[...])


def _norm_matmul(x, g, w, *, tm, wt=None, name):
    m, d = x.shape
    n = w.shape[1]
    in_specs = [pl.BlockSpec((tm, d), lambda i: (i, 0)),
                pl.BlockSpec((1, d), lambda i: (0, 0)),
                pl.BlockSpec((d, n), lambda i: (0, 0))]
    out_shape = [jax.ShapeDtypeStruct((m, n), F32)]
    out_specs = [pl.BlockSpec((tm, n), lambda i: (i, 0))]
    args = [x, g, w]
    if wt is not None:
        in_specs.append(pl.BlockSpec(wt.shape, lambda i: (0, 0)))
        out_shape.append(jax.ShapeDtypeStruct((wt.shape[0], m), F32))
        out_specs.append(pl.BlockSpec((wt.shape[0], tm), lambda i: (0, i)))
        args.append(wt)
    res = pl.pallas_call(
        functools.partial(_norm_matmul_kernel, has_t=wt is not None),
        grid=(m // tm,), in_specs=in_specs, out_specs=out_specs, out_shape=out_shape,
        compiler_params=_cp("parallel"), name=name)(*args)
    return res if wt is not None else res[0]


def _matmul_norm_res_kernel(*refs, widths):
    n_a = len(widths)
    a_refs = refs[:n_a]
    w_ref, g_ref, r_ref, o_ref = refs[n_a:]
    y = None
    off = 0
    for a_ref, k in zip(a_refs, widths):
        part = _dot(a_ref[...].astype(BF16), w_ref[off:off + k, :])
        y = part if y is None else y + part
        off += k
    o_ref[...] = r_ref[...] + _rms(y, g_ref[...])


def _matmul_norm_res(a_list, w, g, resid, *, tm, name):
    m, d = resid.shape
    widths = tuple(a.shape[1] for a in a_list)
    in_specs = [pl.BlockSpec((tm, k), lambda i: (i, 0)) for k in widths]
    in_specs += [pl.BlockSpec(w.shape, lambda i: (0, 0)),
                 pl.BlockSpec((1, d), lambda i: (0, 0)),
                 pl.BlockSpec((tm, d), lambda i: (i, 0))]
    return pl.pallas_call(
        functools.partial(_matmul_norm_res_kernel, widths=widths),
        grid=(m // tm,), in_specs=in_specs,
        out_specs=pl.BlockSpec((tm, d), lambda i: (i, 0)),
        out_shape=jax.ShapeDtypeStruct((m, d), F32),
        compiler_params=_cp("parallel"), name=name)(*a_list, w, g, resid)


def _lru_gates(xc, wa_ref, ba_ref, wi_ref, bi_ref, lam_ref):
    xb = xc.astype(BF16)
    ra, ia = [], []
    for c in range(LRU_W // LANES):
        xs = xb[:, c * LANES:(c + 1) * LANES]
        ra.append(_dot(xs, wa_ref[c]))
        ia.append(_dot(xs, wi_ref[c]))
    r = _sigmoid(jnp.concatenate(ra, axis=1) + ba_ref[...])
    ig = _sigmoid(jnp.concatenate(ia, axis=1) + bi_ref[...])
    log_a = -LRU_C * r * _softplus(-lam_ref[...])
    a = jnp.exp(log_a)
    bx = jnp.sqrt(-jnp.tanh(log_a) * (a * a + 1.0)) * (ig * xc)
    return a, bx


def _lru_prompt_kernel(xl_ref, gate_ref, cw_ref, cb_ref, wa_ref, ba_ref, wi_ref, bi_ref, lam_ref,
                       out_ref, hlast_ref, ext_ref, hc_ref, *, tc, n_t):
    t = pl.program_id(1)

    @pl.when(t == 0)
    def _():
        ext_ref[0:8, :] = jnp.zeros((8, LRU_W), F32)
        hc_ref[...] = jnp.zeros((1, LRU_W), F32)

    xl = xl_ref[...]
    ext_ref[8:8 + tc, :] = xl
    xc = cb_ref[...] + cw_ref[LRU_CONV - 1:LRU_CONV, :] * xl
    for j in range(1, LRU_CONV):
        xc = xc + cw_ref[LRU_CONV - 1 - j:LRU_CONV - j, :] * ext_ref[pl.ds(8 - j, tc), :]
    ext_ref[0:8, :] = ext_ref[tc:tc + 8, :]

    a, b = _lru_gates(xc, wa_ref, ba_ref, wi_ref, bi_ref, lam_ref)
    row = lax.broadcasted_iota(jnp.int32, (tc, 1), 0)
    d = 1
    while d < tc:
        keep = row >= d
        a_sh = jnp.where(keep, pltpu.roll(a, d, 0), 1.0)
        b_sh = jnp.where(keep, pltpu.roll(b, d, 0), 0.0)
        b = a * b_sh + b
        a = a * a_sh
        d *= 2
    h = a * hc_ref[...] + b
    hc_ref[...] = h[tc - 1:tc, :]
    out_ref[...] = (_gelu_tanh(gate_ref[...]) * h).astype(BF16)

    @pl.when(t == n_t - 1)
    def _():
        hlast_ref[...] = h[tc - 1:tc, :]


def _lru_prompt(z, n_b, seq, cw, cb, wa, ba, wi, bi, lam):
    tc = SCAN_TILE
    n_t = seq // tc
    vec = lambda: pl.BlockSpec((1, LRU_W), lambda b, t: (0, 0))
    bd = lambda: pl.BlockSpec((LRU_W // LANES, LANES, LANES), lambda b, t: (0, 0, 0))
    return pl.pallas_call(
        functools.partial(_lru_prompt_kernel, tc=tc, n_t=n_t),
        grid=(n_b, n_t),
        in_specs=[pl.BlockSpec((tc, LRU_W), lambda b, t: (b * n_t + t, 0)),
                  pl.BlockSpec((tc, LRU_W), lambda b, t: (b * n_t + t, 1)),
                  pl.BlockSpec((LRU_CONV, LRU_W), lambda b, t: (0, 0)),
                  vec(), bd(), vec(), bd(), vec(), vec()],
        out_specs=[pl.BlockSpec((tc, LRU_W), lambda b, t: (b * n_t + t, 0)),
                   pl.BlockSpec((None, 1, LRU_W), lambda b, t: (b, 0, 0))],
        out_shape=[jax.ShapeDtypeStruct((n_b * seq, LRU_W), BF16),
                   jax.ShapeDtypeStruct((n_b, 1, LRU_W), F32)],
        scratch_shapes=[pltpu.VMEM((tc + 8, LRU_W), F32), pltpu.VMEM((1, LRU_W), F32)],
        compiler_params=_cp("parallel", "arbitrary"), name="lru_prompt")(
            z, z, cw, cb, wa, ba, wi, bi, lam)


def _lru_sample_kernel(z_ref, pre_ref, h0_ref, cw_ref, cb_ref, wa_ref, ba_ref, wi_ref, bi_ref, lam_ref,
                       out_ref, h_ref):
    xl = z_ref[:, 0:LRU_W]
    gate = z_ref[:, LRU_W:2 * LRU_W]
    xc = cb_ref[...] + cw_ref[LRU_CONV - 1:LRU_CONV, :] * xl
    for k in range(LRU_CONV - 1):
        xc = xc + cw_ref[k:k + 1, :] * pre_ref[k]
    a, bx = _lru_gates(xc, wa_ref, ba_ref, wi_ref, bi_ref, lam_ref)
    h = a * h0_ref[...] + bx
    h_ref[...] = h
    out_ref[...] = (_gelu_tanh(gate) * h).astype(BF16)


def _lru_sample(z, prefix_t, h0, cw, cb, wa, ba, wi, bi, lam):
    n_b = z.shape[0]
    return pl.pallas_call(
        _lru_sample_kernel,
        out_shape=[jax.ShapeDtypeStruct((n_b, LRU_W), BF16), jax.ShapeDtypeStruct((n_b, LRU_W), F32)],
        compiler_params=pltpu.CompilerParams(vmem_limit_bytes=VMEM_LIMIT),
        name="lru_sample")(z, prefix_t, h0, cw, cb, wa, ba, wi, bi, lam)


def _log_sigmoid(x):
    return jnp.minimum(x, 0.0) - jnp.log1p(jnp.exp(-jnp.abs(x)))


def _lane_cumsum(x):
    n = x.shape[1]
    lane = lax.broadcasted_iota(jnp.int32, x.shape, 1)
    d = 1
    while d < n:
        x = x + jnp.where(lane >= d, pltpu.roll(x, d, 1), 0.0)
        d *= 2
    return x


def _fox_prep_kernel(fl_ref, bf_ref, lf_ref, c_ref):
    lf = _log_sigmoid(fl_ref[0:FOX_H, :] + bf_ref[...])
    lf_ref[...] = lf
    c_ref[...] = _lane_cumsum(lf)


def _fox_prep(fl_t, bf, n_b, seq):
    return pl.pallas_call(
        _fox_prep_kernel, grid=(n_b,),
        in_specs=[pl.BlockSpec((fl_t.shape[0], seq), lambda b: (0, b)),
                  pl.BlockSpec((FOX_H, 1), lambda b: (0, 0))],
        out_specs=[pl.BlockSpec((FOX_H, seq), lambda b: (0, b)),
                   pl.BlockSpec((FOX_H, seq), lambda b: (0, b))],
        out_shape=[jax.ShapeDtypeStruct((FOX_H, n_b * seq), F32)] * 2,
        compiler_params=_cp("parallel"), name="fox_prep")(fl_t, bf)


def _logsig_kernel(fl_ref, bf_ref, lf_ref):
    lf_ref[...] = _log_sigmoid(fl_ref[0:FOX_H, :] + bf_ref[...])


def _logsig(fl_t, bf):
    return pl.pallas_call(
        _logsig_kernel, out_shape=jax.ShapeDtypeStruct((FOX_H, fl_t.shape[1]), F32),
        name="fox_logf_sample")(fl_t, bf)


def _fox_prompt_kernel(qi_ref, ki_ref, q_ref, k_ref, v_ref, cq_ref, ck_ref, o_ref,
                       m_sc, l_sc, acc_sc, s_sc, p_sc, al_sc, *, tb, hg):
    grp = pl.program_id(1)
    t = pl.program_id(2)
    qi = qi_ref[t]
    ki = ki_ref[t]
    width = hg * FOX_HD

    @pl.when(ki == 0)
    def _():
        m_sc[...] = jnp.full(m_sc.shape, NEG, F32)
        l_sc[...] = jnp.zeros(l_sc.shape, F32)
        acc_sc[...] = jnp.zeros(acc_sc.shape, F32)

    def step(diagonal):
        q = q_ref[...] * (FOX_HD ** -0.5)
        k = k_ref[...].astype(BF16)
        v = v_ref[...].astype(BF16)
        lane_head = jnp.right_shift(lax.broadcasted_iota(jnp.int32, (tb, width), 1),
                                    FOX_HD.bit_length() - 1)
        rc = ATT_ROWS
        if diagonal:
            row_i = lax.broadcasted_iota(jnp.int32, (rc, tb), 0)
            col_i = lax.broadcasted_iota(jnp.int32, (rc, tb), 1)
        n_lt = tb // LANES
        for h in range(hg):
            s_sc[h] = _dot_nt(jnp.where(lane_head == h, q, 0.0).astype(BF16), k)
            c_q = cq_ref[pl.ds(hg * grp + h, 1), :]
            c_k = ck_ref[pl.ds(hg * grp + h, 1), :]
            bias = c_q[:, 0:1] - c_k
            for c in range(tb // rc):
                rows = slice(c * rc, (c + 1) * rc)
                s = s_sc[h, rows, :] + bias
                if diagonal:
                    s = jnp.where(col_i <= row_i + c * rc, s, NEG)
                s_sc[h, rows, :] = s
                m_prev = m_sc[h, rows, :]
                m_new = jnp.maximum(m_prev, jnp.max(s, axis=1, keepdims=True))
                m_sc[h, rows, :] = m_new
                al_sc[h, rows, :] = jnp.exp(m_prev - m_new)
            for c in range(tb // rc):
                rows = slice(c * rc, (c + 1) * rc)
                m_new = m_sc[h, rows, :]
                p_sum = None
                for j in range(n_lt):
                    cols = slice(j * LANES, (j + 1) * LANES)
                    p = jnp.exp(s_sc[h, rows, cols] - m_new)
                    p_sc[h, rows, cols] = p.astype(BF16)
                    p_sum = p if p_sum is None else p_sum + p
                l_sc[h, rows, :] = al_sc[h, rows, :] * l_sc[h, rows, :] + p_sum
            lt = (h * FOX_HD) // LANES
            pv = _dot(p_sc[h], v)[:, lt * LANES:(lt + 1) * LANES]
            acc_sc[h] = al_sc[h] * acc_sc[h] + pv

    @pl.when(ki < qi)
    def _():
        step(False)

    @pl.when(ki == qi)
    def _():
        step(True)
        lane = lax.broadcasted_iota(jnp.int32, (tb, LANES), 1)
        heads_per_tile = LANES // FOX_HD
        for lt in range(width // LANES):
            o = None
            for i in range(heads_per_tile):
                h = lt * heads_per_tile + i
                o_h = acc_sc[h] / jnp.sum(l_sc[h], axis=1, keepdims=True)
                o = o_h if o is None else jnp.where(lane < i * FOX_HD, o, o_h)
            o_ref[:, lt * LANES:(lt + 1) * LANES] = o.astype(BF16)


def _fox_prompt(z, c_t, n_b, seq):
    tb = SEQ_TILE
    hg = ATT_HEADS
    width = hg * FOX_HD
    n_q = seq // tb
    q_blk = (2 * LRU_W) // width
    k_blk = q_blk + FOX_W // width
    v_blk = k_blk + FOX_W // width
    pairs = [(qi, ki) for qi in range(n_q) for ki in range(qi + 1)]
    qi_list = jnp.asarray([p[0] for p in pairs], jnp.int32)
    ki_list = jnp.asarray([p[1] for p in pairs], jnp.int32)
    grid_spec = pltpu.PrefetchScalarGridSpec(
        num_scalar_prefetch=2, grid=(n_b, FOX_H // hg, len(pairs)),
        in_specs=[
            pl.BlockSpec((tb, width), lambda b, g, t, qi, ki: (b * n_q + qi[t], q_blk + g)),
            pl.BlockSpec((tb, width), lambda b, g, t, qi, ki: (b * n_q + ki[t], k_blk + g)),
            pl.BlockSpec((tb, width), lambda b, g, t, qi, ki: (b * n_q + ki[t], v_blk + g)),
            pl.BlockSpec((FOX_H, tb), lambda b, g, t, qi, ki: (0, b * n_q + qi[t])),
            pl.BlockSpec((FOX_H, tb), lambda b, g, t, qi, ki: (0, b * n_q + ki[t])),
        ],
        out_specs=pl.BlockSpec((tb, width), lambda b, g, t, qi, ki: (b * n_q + qi[t], g)),
        scratch_shapes=[pltpu.VMEM((hg, tb, LANES), F32), pltpu.VMEM((hg, tb, LANES), F32),
                        pltpu.VMEM((hg, tb, LANES), F32), pltpu.VMEM((hg, tb, tb), F32),
                        pltpu.VMEM((hg, tb, tb), BF16), pltpu.VMEM((hg, tb, LANES), F32)])
    return pl.pallas_call(
        functools.partial(_fox_prompt_kernel, tb=tb, hg=hg), grid_spec=grid_spec,
        out_shape=jax.ShapeDtypeStruct((n_b * seq, FOX_W), BF16),
        compiler_params=_cp("parallel", "parallel", "arbitrary"),
        name="fox_prompt")(qi_list, ki_list, z, z, z, c_t, c_t)


def _fox_sample_kernel(pt_ref, q_ref, kn_ref, vn_ref, lfn_ref, *rest, n_pg, n_g):
    k_refs = rest[0:n_pg]
    v_refs = rest[n_pg:2 * n_pg]
    lf_refs = rest[2 * n_pg:3 * n_pg]
    o_ref, m_sc, l_sc, acc_sc, cc_sc = rest[3 * n_pg:]
    g = pl.program_id(1)
    page = lf_refs[0].shape[1]

    @pl.when(g == 0)
    def _():
        m_sc[...] = jnp.full(m_sc.shape, NEG, F32)
        l_sc[...] = jnp.zeros(l_sc.shape, F32)
        acc_sc[...] = jnp.zeros(acc_sc.shape, F32)
        cc_sc[...] = jnp.zeros(cc_sc.shape, F32)


    q = q_ref[...] * (FOX_HD ** -0.5)
    q16 = jnp.concatenate([q, jnp.zeros_like(q)], axis=0)
    row_q = lax.broadcasted_iota(jnp.int32, (2 * FOX_H, FOX_HD), 0)
    q_only = [jnp.where(row_q == h, q16, 0.0).astype(BF16) for h in range(FOX_H)]

    s_parts = []
    for j in range(n_pg):
        s_j = None
        for h in range(FOX_H):
            part = _dot(q_only[h], k_refs[j][h].astype(BF16))
            s_j = part if s_j is None else s_j + part
        s_parts.append(s_j[0:FOX_H])
    s = jnp.concatenate(s_parts, axis=1)
    lf = jnp.concatenate([lf_refs[j][...] for j in range(n_pg)], axis=1)
    c = _lane_cumsum(lf) + cc_sc[...]
    cc_sc[...] = c[:, c.shape[1] - 1:]
    s = s - c
    m_prev = m_sc[...]
    m_new = jnp.maximum(m_prev, jnp.max(s, axis=1, keepdims=True))
    alpha = jnp.exp(m_prev - m_new)
    p = jnp.exp(s - m_new)
    l_sc[...] = alpha * l_sc[...] + jnp.sum(p, axis=1, keepdims=True)
    p16 = jnp.concatenate([p, jnp.zeros_like(p)], axis=0)
    row_p = lax.broadcasted_iota(jnp.int32, (2 * FOX_H, page), 0)
    pv = None
    for j in range(n_pg):
        p_j = p16[:, j * page:(j + 1) * page]
        for h in range(FOX_H):
            part = _dot_nt(jnp.where(row_p == h, p_j, 0.0).astype(BF16), v_refs[j][h].astype(BF16))
            pv = part if pv is None else pv + part
    acc_sc[...] = alpha * acc_sc[...] + pv[0:FOX_H]
    m_sc[...] = m_new

    @pl.when(g == n_g - 1)
    def _():
        s_n = jnp.sum(q * kn_ref[...], axis=1, keepdims=True)
        s_n = s_n - (cc_sc[...] + lfn_ref[...])
        m_p = m_sc[...]
        m_n = jnp.maximum(m_p, s_n)
        al = jnp.exp(m_p - m_n)
        p_n = jnp.exp(s_n - m_n)
        l_n = al * l_sc[...] + p_n
        o_ref[...] = ((al * acc_sc[...] + p_n * vn_ref[...]) / l_n).astype(BF16)


def _fox_sample(page_table, layer, q, k_new, v_new, lf_new, cache_k, cache_v, cache_lf_t):
    n_b, n_pages = page_table.shape
    n_pg = PAGES_PER_STEP
    n_g = n_pages // n_pg
    page = cache_k.shape[4]
    head_spec = lambda: pl.BlockSpec((None, FOX_H, FOX_HD), lambda b, g, pt: (b, 0, 0))

    def kv_spec(j):
        return pl.BlockSpec((None, None, FOX_H, FOX_HD, page),
                            lambda b, g, pt, j=j: (layer, pt[b * n_pages + g * n_pg + j], 0, 0, 0))

    def lf_spec(j):
        return pl.BlockSpec((None, FOX_H, page),
                            lambda b, g, pt, j=j: (pt[b * n_pages + g * n_pg + j], 0, 0))

    in_specs = [head_spec(), head_spec(), head_spec(),
                pl.BlockSpec((None, FOX_H, 1), lambda b, g, pt: (b, 0, 0))]
    in_specs += [kv_spec(j) for j in range(n_pg)]
    in_specs += [kv_spec(j) for j in range(n_pg)]
    in_specs += [lf_spec(j) for j in range(n_pg)]
    grid_spec = pltpu.PrefetchScalarGridSpec(
        num_scalar_prefetch=1, grid=(n_b, n_g), in_specs=in_specs,
        out_specs=pl.BlockSpec((None, FOX_H, FOX_HD), lambda b, g, pt: (b, 0, 0)),
        scratch_shapes=[pltpu.VMEM((FOX_H, 1), F32), pltpu.VMEM((FOX_H, 1), F32),
                        pltpu.VMEM((FOX_H, FOX_HD), F32), pltpu.VMEM((FOX_H, 1), F32)])
    return pl.pallas_call(
        functools.partial(_fox_sample_kernel, n_pg=n_pg, n_g=n_g),
        grid_spec=grid_spec,
        out_shape=jax.ShapeDtypeStruct((n_b, FOX_H, FOX_HD), BF16),
        compiler_params=_cp("parallel", "arbitrary"), name="fox_sample")(
            page_table.reshape(-1), q, k_new, v_new, lf_new,
            *([cache_k] * n_pg), *([cache_v] * n_pg), *([cache_lf_t] * n_pg))


def _xattn_prompt_kernel(q_ref, mk_ref, mv_ref, o_ref):
    outs = []
    for h in range(MEM_H):
        cs = slice(h * MEM_HD, (h + 1) * MEM_HD)
        qh = q_ref[:, cs].astype(BF16)
        s = _dot_nt(qh, mk_ref[:, cs].astype(BF16)) * (MEM_HD ** -0.5)
        p = jnp.exp(s - jnp.max(s, axis=1, keepdims=True))
        l = jnp.sum(p, axis=1, keepdims=True)
        outs.append(_dot(p.astype(BF16), mv_ref[:, cs].astype(BF16)) / l)
    o_ref[...] = jnp.concatenate(outs, axis=1).astype(BF16)


def _xattn_prompt(q, memkv, n_b, seq):
    tb = SEQ_TILE
    n_t = seq // tb
    mem_len = memkv.shape[0] // n_b
    return pl.pallas_call(
        _xattn_prompt_kernel, grid=(n_b, n_t),
        in_specs=[pl.BlockSpec((tb, MEM_W), lambda b, t: (b * n_t + t, 0)),
                  pl.BlockSpec((mem_len, MEM_W), lambda b, t: (b, 0)),
                  pl.BlockSpec((mem_len, MEM_W), lambda b, t: (b, 1))],
        out_specs=pl.BlockSpec((tb, MEM_W), lambda b, t: (b * n_t + t, 0)),
        out_shape=jax.ShapeDtypeStruct((n_b * seq, MEM_W), BF16),
        compiler_params=_cp("parallel", "parallel"), name="xattn_prompt")(q, memkv, memkv)


def _xattn_sample_kernel(q_ref, mk_ref, mv_ref, o_ref):
    q = q_ref[...]
    row = lax.broadcasted_iota(jnp.int32, (16, MEM_W), 0)
    col = lax.broadcasted_iota(jnp.int32, (16, MEM_W), 1)
    head_cols = jnp.right_shift(col, MEM_HD.bit_length() - 1) == row
    q_rows = jnp.where(head_cols, q, 0.0).astype(BF16)
    mem_len = mk_ref.shape[0] // MEM_H

    def heads_on_lanes(ref):
        return jnp.concatenate([ref[pl.ds(h, mem_len, stride=MEM_H), :] for h in range(MEM_H)],
                               axis=1).astype(BF16)

    s = _dot_nt(q_rows, heads_on_lanes(mk_ref)) * (MEM_HD ** -0.5)
    p = jnp.exp(s - jnp.max(s, axis=1, keepdims=True))
    l = jnp.sum(p, axis=1, keepdims=True)
    o = _dot(p.astype(BF16), heads_on_lanes(mv_ref)) / l
    o_ref[...] = jnp.sum(jnp.where(head_cols, o, 0.0), axis=0, keepdims=True).astype(BF16)


def _xattn_sample(q, layer, mk, mv):
    n_b, rows = mk.shape[1], mk.shape[2]
    return pl.pallas_call(
        _xattn_sample_kernel, grid=(n_b,),
        in_specs=[pl.BlockSpec((None, 1, MEM_W), lambda b: (b, 0, 0)),
                  pl.BlockSpec((None, None, rows, MEM_HD), lambda b: (layer, b, 0, 0)),
                  pl.BlockSpec((None, None, rows, MEM_HD), lambda b: (layer, b, 0, 0))],
        out_specs=pl.BlockSpec((None, 1, MEM_W), lambda b: (b, 0, 0)),
        out_shape=jax.ShapeDtypeStruct((n_b, 1, MEM_W), BF16),
        compiler_params=_cp("parallel"), name="xattn_sample")(q, mk, mv)


def _swiglu_accumulate(x_sc, w1_ref, w3_ref, w2_ref, acc_sc):
    tm = x_sc.shape[0]
    slab = min(tm, FF_ROW_SLAB)
    for r in range(tm // slab):
        rows = slice(r * slab, (r + 1) * slab)
        x = x_sc[rows, :]
        h1 = _dot(x, w1_ref[...])
        h3 = _dot(x, w3_ref[...])
        hh = (h1 * _sigmoid(h1) * h3).astype(BF16)
        acc_sc[rows, :] += _dot(hh, w2_ref[...])


def _ffn_kernel(x_ref, g4_ref, g5_ref, w1_ref, w3_ref, w2_ref, o_ref, hn_sc, acc_sc, *, n_f):
    f = pl.program_id(1)

    @pl.when(f == 0)
    def _():
        hn_sc[...] = _rms(x_ref[...], g4_ref[...]).astype(BF16)
        acc_sc[...] = jnp.zeros(acc_sc.shape, F32)

    _swiglu_accumulate(hn_sc, w1_ref, w3_ref, w2_ref, acc_sc)

    @pl.when(f == n_f - 1)
    def _():
        o_ref[...] = x_ref[...] + _rms(acc_sc[...], g5_ref[...])


def _ffn(x, g4, g5, w1, w3, w2, *, tm):
    m, d = x.shape
    ff = w1.shape[1]
    tf = FFN_FF_TILE
    n_f = ff // tf
    return pl.pallas_call(
        functools.partial(_ffn_kernel, n_f=n_f), grid=(m // tm, n_f),
        in_specs=[pl.BlockSpec((tm, d), lambda i, f: (i, 0)),
                  pl.BlockSpec((1, d), lambda i, f: (0, 0)),
                  pl.BlockSpec((1, d), lambda i, f: (0, 0)),
                  pl.BlockSpec((d, tf), lambda i, f: (0, f)),
                  pl.BlockSpec((d, tf), lambda i, f: (0, f)),
                  pl.BlockSpec((tf, d), lambda i, f: (f, 0))],
        out_specs=pl.BlockSpec((tm, d), lambda i, f: (i, 0)),
        out_shape=jax.ShapeDtypeStruct((m, d), F32),
        scratch_shapes=[pltpu.VMEM((tm, d), BF16), pltpu.VMEM((tm, d), F32)],
        compiler_params=_cp("parallel", "arbitrary"), name="ffn")(x, g4, g5, w1, w3, w2)


def _layernorm_silu(x, g, b):
    mu = jnp.mean(x, axis=-1, keepdims=True)
    xc = x - mu
    y = xc * lax.rsqrt(jnp.mean(xc * xc, axis=-1, keepdims=True) + EPS) * g + b
    return y * _sigmoid(y)


def _odd_prompt_kernel(a_ref, gt_ref, up_ref, dww_ref, dwb_ref, lng_ref, lnb_ref, pw_ref, ps_ref,
                       cv_ref, pool_ref, cbuf_ref, pbuf_ref, eg_ref, eu_ref, sh_ref, *, tc, n_t):
    t = pl.program_id(1)
    halo_g, halo_u = 32, 16

    @pl.when(t == 0)
    def _():
        eg_ref[0:halo_g, :] = jnp.zeros((halo_g, CONV_W), F32)
        eu_ref[0:halo_u, :] = jnp.zeros((halo_u, POOL_W), F32)

    glu = a_ref[...] * _sigmoid(gt_ref[...])
    up = up_ref[...]
    eg_ref[halo_g:halo_g + tc, :] = glu
    eu_ref[halo_u:halo_u + tc, :] = up

    ext = eg_ref[...]
    n_ext = tc + halo_g
    sh_ref[0] = ext
    for s in range(1, 8):
        sh_ref[s] = pltpu.roll(ext, n_ext - s, 0)
    rc = CONV_ROWS
    for c in range(tc // rc):
        parts = []
        for j in range(CONV_W // LANES):
            cols = slice(j * LANES, (j + 1) * LANES)
            acc = jnp.broadcast_to(dwb_ref[:, cols], (rc, LANES))
            for k in range(CONV_K):
                off = k + halo_g - (CONV_K - 1)
                acc = acc + dww_ref[k:k + 1, cols] * sh_ref[off % 8, pl.ds(c * rc + (off // 8) * 8, rc), cols]
            parts.append(acc)
        cv_ref[c * rc:(c + 1) * rc, :] = _layernorm_silu(
            jnp.concatenate(parts, axis=1), lng_ref[...], lnb_ref[...]).astype(BF16)

    pos = t * tc + lax.broadcasted_iota(jnp.int32, (tc, 1), 0)
    outs = []
    for gi, w in enumerate(POOL_WINDOWS):
        cs = slice(gi * POOL_GW, (gi + 1) * POOL_GW)
        u_g = up[:, cs]
        win = u_g
        for j in range(1, w):
            win = win + eu_ref[pl.ds(halo_u - j, tc), cs]
        cnt = jnp.minimum(pos + 1, w).astype(F32)
        dlt = win / cnt - u_g
        outs.append(_dot(dlt.astype(BF16), pw_ref[gi]))
    pool_ref[...] = (jnp.concatenate(outs, axis=1) * ps_ref[...]).astype(BF16)

    @pl.when(t == n_t - 1)
    def _():
        cbuf_ref[...] = eg_ref[pl.ds(halo_g + tc - (CONV_K - 1), CONV_K - 1), :]
        pbuf_ref[...] = eu_ref[pl.ds(halo_u + tc - POOL_BUF, POOL_BUF), :]

    eg_ref[0:halo_g, :] = eg_ref[tc:tc + halo_g, :]
    eu_ref[0:halo_u, :] = eu_ref[tc:tc + halo_u, :]


def _odd_prompt(z, n_b, seq, dww, dwb, lng, lnb, pw, ps):
    tc = SEQ_TILE
    n_t = seq // tc
    vec = lambda: pl.BlockSpec((1, CONV_W), lambda b, t: (0, 0))
    return pl.pallas_call(
        functools.partial(_odd_prompt_kernel, tc=tc, n_t=n_t), grid=(n_b, n_t),
        in_specs=[pl.BlockSpec((tc, CONV_W), lambda b, t: (b * n_t + t, 0)),
                  pl.BlockSpec((tc, CONV_W), lambda b, t: (b * n_t + t, 1)),
                  pl.BlockSpec((tc, POOL_W), lambda b, t: (b * n_t + t, 2)),
                  pl.BlockSpec((CONV_K, CONV_W), lambda b, t: (0, 0)),
                  vec(), vec(), vec(),
                  pl.BlockSpec((len(POOL_WINDOWS), POOL_GW, POOL_GW), lambda b, t: (0, 0, 0)),
                  vec()],
        out_specs=[pl.BlockSpec((tc, CONV_W), lambda b, t: (b * n_t + t, 0)),
                   pl.BlockSpec((tc, POOL_W), lambda b, t: (b * n_t + t, 0)),
                   pl.BlockSpec((None, CONV_K - 1, CONV_W), lambda b, t: (b, 0, 0)),
                   pl.BlockSpec((None, POOL_BUF, POOL_W), lambda b, t: (b, 0, 0))],
        out_shape=[jax.ShapeDtypeStruct((n_b * seq, CONV_W), BF16),
                   jax.ShapeDtypeStruct((n_b * seq, POOL_W), BF16),
                   jax.ShapeDtypeStruct((n_b, CONV_K - 1, CONV_W), F32),
                   jax.ShapeDtypeStruct((n_b, POOL_BUF, POOL_W), F32)],
        scratch_shapes=[pltpu.VMEM((tc + 32, CONV_W), F32), pltpu.VMEM((tc + 16, POOL_W), F32),
                        pltpu.VMEM((8, tc + 32, CONV_W), F32)],
        compiler_params=_cp("parallel", "arbitrary"), name="odd_prompt")(
            z, z, z, dww, dwb, lng, lnb, pw, ps)


def _odd_sample_kernel(z_ref, cbuf_ref, pbuf_ref, dww_ref, dwb_ref, lng_ref, lnb_ref, pw_ref, ps_ref,
                       cv_ref, pool_ref, glu_ref, *, pos0):
    glu = z_ref[:, 0:CONV_W] * _sigmoid(z_ref[:, CONV_W:2 * CONV_W])
    up = z_ref[:, 2 * CONV_W:2 * CONV_W + POOL_W]
    glu_ref[...] = glu
    acc = dwb_ref[...] + dww_ref[CONV_K - 1:CONV_K, :] * glu
    for k in range(CONV_K - 1):
        acc = acc + dww_ref[k:k + 1, :] * cbuf_ref[k]
    cv_ref[...] = _layernorm_silu(acc, lng_ref[...], lnb_ref[...]).astype(BF16)
    outs = []
    for gi, w in enumerate(POOL_WINDOWS):
        cs = slice(gi * POOL_GW, (gi + 1) * POOL_GW)
        u_g = up[:, cs]
        win = u_g
        for j in range(1, w):
            win = win + pbuf_ref[POOL_BUF - j][:, cs]
        dlt = win / float(min(pos0 + 1, w)) - u_g
        outs.append(_dot(dlt.astype(BF16), pw_ref[gi]))
    pool_ref[...] = (jnp.concatenate(outs, axis=1) * ps_ref[...]).astype(BF16)


def _odd_sample(z, cbuf_t, pbuf_t, dww, dwb, lng, lnb, pw, ps, pos0):
    n_b = z.shape[0]
    return pl.pallas_call(
        functools.partial(_odd_sample_kernel, pos0=pos0),
        out_shape=[jax.ShapeDtypeStruct((n_b, CONV_W), BF16),
                   jax.ShapeDtypeStruct((n_b, POOL_W), BF16),
                   jax.ShapeDtypeStruct((n_b, CONV_W), F32)],
        compiler_params=pltpu.CompilerParams(vmem_limit_bytes=VMEM_LIMIT),
        name="odd_sample")(z, cbuf_t, pbuf_t, dww, dwb, lng, lnb, pw, ps)


def _router_kernel(x_ref, g_ref, whi_ref, wlo_ref, b_ref, *rest):
    hn_ref, idx_ref, gate_ref = rest[-3:]
    hn = _rms(x_ref[...], g_ref[...])
    hb = hn.astype(BF16)
    hn_ref[...] = hn
    hlo = (hn - hb.astype(F32)).astype(BF16)
    logits = _dot(hb, whi_ref[...]) + (_dot(hb, wlo_ref[...]) + _dot(hlo, whi_ref[...])) + b_ref[...]
    lane = lax.broadcasted_iota(jnp.int32, logits.shape, 1)
    lane_f = lane.astype(F32)
    logits = jnp.where(lane < N_EXPERTS, logits, NEG)
    m1 = jnp.max(logits, axis=1, keepdims=True)
    i1 = jnp.min(jnp.where(logits == m1, lane_f, float(LANES)), axis=1, keepdims=True)
    rest = jnp.where(lane_f == i1, NEG, logits)
    m2 = jnp.max(rest, axis=1, keepdims=True)
    i2 = jnp.min(jnp.where(rest == m2, lane_f, float(LANES)), axis=1, keepdims=True)
    e = jnp.exp(m2 - m1)
    g1 = 1.0 / (1.0 + e)
    g2 = e / (1.0 + e)
    idx_ref[...] = jnp.where(lane == 0, i1, jnp.where(lane == 1, i2, 0.0)).astype(jnp.int32)
    gate_ref[...] = jnp.where(lane == 0, g1, jnp.where(lane == 1, g2, 0.0))


def _router(x, g, whi, wlo, b, *, tm, after=()):
    m, d = x.shape
    return pl.pallas_call(
        _router_kernel, grid=(m // tm,),
        in_specs=[pl.BlockSpec((tm, d), lambda i: (i, 0)),
                  pl.BlockSpec((1, d), lambda i: (0, 0)),
                  pl.BlockSpec((d, LANES), lambda i: (0, 0)),
                  pl.BlockSpec((d, LANES), lambda i: (0, 0)),
                  pl.BlockSpec((1, LANES), lambda i: (0, 0))]
        + [pl.BlockSpec(memory_space=pl.ANY) for _ in after],
        out_specs=[pl.BlockSpec((tm, d), lambda i: (i, 0)),
                   pl.BlockSpec((tm, LANES), lambda i: (i, 0)),
                   pl.BlockSpec((tm, LANES), lambda i: (i, 0))],
        out_shape=[jax.ShapeDtypeStruct((m, d), F32),
                   jax.ShapeDtypeStruct((m, LANES), jnp.int32),
                   jax.ShapeDtypeStruct((m, LANES), F32)],
        compiler_params=_cp("parallel"), name="router")(x, g, whi, wlo, b, *after)


def _gmm_kernel(te_ref, tv_ref, x_ref, gate_ref, w1_ref, w3_ref, w2_ref, *rest, n_f, tile0):
    o_ref, acc_sc, xb_sc = rest[-3:]
    t = tile0 + pl.program_id(0)
    f = pl.program_id(1)

    @pl.when(f == 0)
    def _():
        acc_sc[...] = jnp.zeros(acc_sc.shape, F32)
        xb_sc[...] = x_ref[...].astype(BF16)

    @pl.when(tv_ref[t] != 0)
    def _():
        _swiglu_accumulate(xb_sc, w1_ref, w3_ref, w2_ref, acc_sc)

    @pl.when(f == n_f - 1)
    def _():
        o_ref[...] = acc_sc[...] * gate_ref[...]


def _gmm(tile_expert, tile_valid, x_part, gate_sorted, w1, w3, w2, *, tile0, y_prev=None):
    rows = gate_sorted.shape[0]
    d, ff = w1.shape[1], w1.shape[2]
    tm, tf = MOE_TILE, MOE_FF_TILE
    n_f = ff // tf
    in_specs = [pl.BlockSpec((tm, d), lambda t, f, te, tv: (t, 0)),
                pl.BlockSpec((tm, 1), lambda t, f, te, tv: (tile0 + t, 0)),
                pl.BlockSpec((None, d, tf), lambda t, f, te, tv: (te[tile0 + t], 0, f)),
                pl.BlockSpec((None, d, tf), lambda t, f, te, tv: (te[tile0 + t], 0, f)),
                pl.BlockSpec((None, tf, d), lambda t, f, te, tv: (te[tile0 + t], f, 0))]
    args = [tile_expert, tile_valid, x_part, gate_sorted, w1, w3, w2]
    aliases = {}
    if y_prev is not None:
        in_specs.append(pl.BlockSpec(memory_space=pl.ANY))
        aliases = {len(args): 0}
        args.append(y_prev)
    grid_spec = pltpu.PrefetchScalarGridSpec(
        num_scalar_prefetch=2, grid=(x_part.shape[0] // tm, n_f), in_specs=in_specs,
        out_specs=pl.BlockSpec((tm, d), lambda t, f, te, tv: (tile0 + t, 0)),
        scratch_shapes=[pltpu.VMEM((tm, d), F32), pltpu.VMEM((tm, d), BF16)])
    return pl.pallas_call(
        functools.partial(_gmm_kernel, n_f=n_f, tile0=tile0), grid_spec=grid_spec,
        out_shape=jax.ShapeDtypeStruct((rows, d), F32), input_output_aliases=aliases,
        compiler_params=_cp("parallel", "arbitrary"), name="moe_experts")(*args)


def _combine_kernel(x_ref, y1_ref, y2_ref, g_ref, o_ref):
    o_ref[...] = x_ref[...] + _rms(y1_ref[...] + y2_ref[...], g_ref[...])


def _combine(x, y1, y2, g, *, tm):
    m, d = x.shape
    row = lambda: pl.BlockSpec((tm, d), lambda i: (i, 0))
    return pl.pallas_call(
        _combine_kernel, grid=(m // tm,),
        in_specs=[row(), row(), row(), pl.BlockSpec((1, d), lambda i: (0, 0))],
        out_specs=row(), out_shape=jax.ShapeDtypeStruct((m, d), F32),
        compiler_params=_cp("parallel"), name="moe_combine")(x, y1, y2, g)


def _moe_block(xs, g4, g5, w_r, b_r, w1, w3, w2):
    pad = LANES - N_EXPERTS
    w_r_p = jnp.pad(w_r, ((0, 0), (0, pad)))
    whi = w_r_p.astype(BF16)
    wlo = (w_r_p - whi.astype(F32)).astype(BF16)
    b_p = jnp.pad(b_r, (0, pad)).reshape(1, LANES)
    hn, idx, gate = [], [], []
    for x in xs:
        h, i, gt = _router(x, g4, whi, wlo, b_p, tm=min(ROW_TILE, x.shape[0]), after=(w1, w3, w2))
        hn.append(h)
        idx.append(i[:, :2])
        gate.append(gt[:, :2])
    hn = jnp.concatenate(hn, axis=0)
    e_flat = jnp.concatenate(idx, axis=0).reshape(-1)
    g_flat = jnp.concatenate(gate, axis=0).reshape(-1)
    n_assign = e_flat.shape[0]
    tm = MOE_TILE
    n_tiles = n_assign // tm + N_EXPERTS
    rows = n_tiles * tm

    onehot = (e_flat[:, None] == jnp.arange(N_EXPERTS, dtype=jnp.int32)[None, :]).astype(jnp.int32)
    rank = jnp.sum((jnp.cumsum(onehot, axis=0) - onehot) * onehot, axis=1)
    cnt = jnp.sum(onehot, axis=0)
    tiles_e = (cnt + tm - 1) // tm
    tile_end = jnp.cumsum(tiles_e)
    row_start = (tile_end - tiles_e) * tm
    dest = row_start[e_flat] + rank
    tile_ids = jnp.arange(n_tiles, dtype=jnp.int32)
    tile_expert = jnp.minimum(jnp.sum((tile_end[None, :] <= tile_ids[:, None]).astype(jnp.int32), axis=1),
                              N_EXPERTS - 1)
    tile_valid = (tile_ids < tile_end[-1]).astype(jnp.int32)
    row_assign = jnp.zeros((rows,), jnp.int32).at[dest].set(jnp.arange(n_assign, dtype=jnp.int32))
    row_ids = jnp.arange(rows, dtype=jnp.int32)
    row_expert = jnp.repeat(tile_expert, tm)
    row_valid = (row_ids - row_start[row_expert]) < cnt[row_expert]
    src_tok = row_assign // 2
    gate_sorted = jnp.where(row_valid, g_flat[row_assign], 0.0).reshape(rows, 1)

    take_rows = lambda a, idx: a.at[idx].get(mode="promise_in_bounds")
    y_sorted = jnp.zeros((rows, hn.shape[1]), F32)
    part_tiles = n_tiles // MOE_PARTS
    for part in range(MOE_PARTS):
        tile0 = part * part_tiles
        x_part = take_rows(hn, src_tok[tile0 * tm:(tile0 + part_tiles) * tm])
        y_sorted = _gmm(tile_expert, tile_valid, x_part, gate_sorted, w1, w3, w2, tile0=tile0, y_prev=y_sorted)

    outs = []
    off = 0
    dest2 = dest.reshape(-1, 2)
    for x in xs:
        m = x.shape[0]
        d1 = dest2[off:off + m, 0]
        d2 = dest2[off:off + m, 1]
        outs.append(_combine(x, take_rows(y_sorted, d1), take_rows(y_sorted, d2), g5,
                             tm=min(ROW_TILE, m)))
        off += m
    return outs


def _block_diag_pairs(w):
    nb, bw, _ = w.shape
    w = w.reshape(nb // 2, 2, bw, bw)
    z = jnp.zeros((nb // 2, bw, bw), w.dtype)
    top = jnp.concatenate([w[:, 0], z], axis=2)
    bot = jnp.concatenate([z, w[:, 1]], axis=2)
    return jnp.concatenate([top, bot], axis=1).astype(BF16)


def kernel(x_prompt, x_sample, cache_fox_k, cache_fox_v, cache_fox_logf, state_lru_h, state_lru_conv, state_conv_buf, state_pool_buf, cache_mem_k, cache_mem_v, page_table, mem_prompt, norm_g, w_xq, w_xk, w_xv, w_xo, w_in_e, b_f, lru_conv_w, lru_conv_b, lru_wa, lru_ba, lru_wi, lru_bi, lru_lam, w_out_e, w_ff1, w_ff3, w_ff2, w_in_o, cc_dw_w, cc_dw_b, cc_ln_g, cc_ln_b, pool_w, pool_scale, w_out_o, w_router, b_router, w_e1, w_e3, w_e2):
    bp, seq, d = x_prompt.shape
    bs = x_sample.shape[0]
    depth = norm_g.shape[0]
    page = cache_fox_k.shape[2]
    past_len = page_table.shape[1] * page
    mem_len = mem_prompt.shape[1]
    tm_p = ROW_TILE

    xp = x_prompt.reshape(bp * seq, d)
    xs = x_sample.reshape(bs, d)
    mem = mem_prompt.reshape(bp * mem_len, d)
    vec = lambda v: v.reshape(1, -1)

    fk_p, fv_p, fl_p, lh_p, lc_p, cb_p, pb_p, mk_pl, mv_pl = [], [], [], [], [], [], [], [], []
    fk_s, fv_s, fl_s, lh_s, lc_s, cb_s, pb_s = [], [], [], [], [], [], []

    for l in range(depth):
        g = [vec(norm_g[l, i]) for i in range(norm_g.shape[1])]
        w_kv = jnp.concatenate([w_xk[l], w_xv[l]], axis=1).astype(BF16)
        memkv = _norm_matmul(mem, g[6], w_kv, tm=tm_p, name="mem_kv")
        mk_pl.append(memkv[:, :MEM_W].reshape(bp, mem_len, MEM_H, MEM_HD))
        mv_pl.append(memkv[:, MEM_W:].reshape(bp, mem_len, MEM_H, MEM_HD))

        if l % 2 == 0:
            i = l // 2
            n_main = 2 * LRU_W + 3 * FOX_W
            w_main = w_in_e[i][:, :n_main].astype(BF16)
            w_fl_t = jnp.pad(w_in_e[i][:, n_main:].T, ((0, 16 - FOX_H), (0, 0))).astype(BF16)
            cw, cb = lru_conv_w[i], vec(lru_conv_b[i])
            wa, wi = _block_diag_pairs(lru_wa[i]), _block_diag_pairs(lru_wi[i])
            ba, bi, lam = vec(lru_ba[i]), vec(lru_bi[i]), vec(lru_lam[i])
            bf = b_f[i].reshape(FOX_H, 1)
            w_out = w_out_e[i].astype(BF16)

            z, fl_t = _norm_matmul(xp, g[0], w_main, tm=tm_p, wt=w_fl_t, name="in_proj_even")
            lf_t, c_t = _fox_prep(fl_t, bf, bp, seq)
            lru_out, h_last = _lru_prompt(z, bp, seq, cw, cb, wa, ba, wi, bi, lam)
            att = _fox_prompt(z, c_t, bp, seq)
            xp = _matmul_norm_res([lru_out, att], w_out, g[1], xp, tm=tm_p, name="out_proj_even")
            z3 = z.reshape(bp, seq, n_main)
            fk_p.append(z3[:, :, 2 * LRU_W + FOX_W:2 * LRU_W + 2 * FOX_W].reshape(bp, seq, FOX_H, FOX_HD))
            fv_p.append(z3[:, :, 2 * LRU_W + 2 * FOX_W:].reshape(bp, seq, FOX_H, FOX_HD))
            fl_p.append(lf_t.T.reshape(bp, seq, FOX_H))
            lh_p.append(h_last.reshape(bp, LRU_W))
            lc_p.append(z3[:, seq - (LRU_CONV - 1):, :LRU_W])

            zs, fls_t = _norm_matmul(xs, g[0], w_main, tm=bs, wt=w_fl_t, name="in_proj_even_s")
            lfs_t = _logsig(fls_t, bf)
            pre_t = jnp.swapaxes(state_lru_conv[i], 0, 1)
            lru_out_s, h_s = _lru_sample(zs, pre_t, state_lru_h[i], cw, cb, wa, ba, wi, bi, lam)
            q_s = zs[:, 2 * LRU_W:2 * LRU_W + FOX_W]
            k_s = zs[:, 2 * LRU_W + FOX_W:2 * LRU_W + 2 * FOX_W]
            v_s = zs[:, 2 * LRU_W + 2 * FOX_W:]
            att_s = _fox_sample(page_table, i, q_s.reshape(bs, FOX_H, FOX_HD), k_s.reshape(bs, FOX_H, FOX_HD),
                                v_s.reshape(bs, FOX_H, FOX_HD), lfs_t.T.reshape(bs, FOX_H, 1),
                                jnp.transpose(cache_fox_k, (0, 1, 3, 4, 2)),
                                jnp.transpose(cache_fox_v, (0, 1, 3, 4, 2)),
                                jnp.swapaxes(cache_fox_logf[i], 1, 2))
            xs = _matmul_norm_res([lru_out_s, att_s.reshape(bs, FOX_W)], w_out, g[1], xs, tm=bs,
                                  name="out_proj_even_s")
            fk_s.append(k_s.reshape(bs, 1, FOX_H, FOX_HD))
            fv_s.append(v_s.reshape(bs, 1, FOX_H, FOX_HD))
            fl_s.append(lfs_t.T.reshape(bs, 1, FOX_H))
            lh_s.append(h_s)
            lc_s.append(jnp.concatenate([state_lru_conv[i][:, 1:], zs[:, None, :LRU_W]], axis=1))
        else:
            j = l // 2
            w_in = w_in_o[j].astype(BF16)
            dww, dwb = cc_dw_w[j], vec(cc_dw_b[j])
            lng, lnb = vec(cc_ln_g[j]), vec(cc_ln_b[j])
            pw, ps = pool_w[j].astype(BF16), vec(pool_scale[j])
            w_out = w_out_o[j].astype(BF16)

            z = _norm_matmul(xp, g[0], w_in, tm=tm_p, name="in_proj_odd")
            cv, pool, cbuf, pbuf = _odd_prompt(z, bp, seq, dww, dwb, lng, lnb, pw, ps)
            xp = _matmul_norm_res([cv, pool], w_out, g[1], xp, tm=tm_p, name="out_proj_odd")
            cb_p.append(cbuf)
            pb_p.append(pbuf)

            zs = _norm_matmul(xs, g[0], w_in, tm=bs, name="in_proj_odd_s")
            cv_s, pool_s, glu_s = _odd_sample(zs, jnp.swapaxes(state_conv_buf[j], 0, 1),
                                              jnp.swapaxes(state_pool_buf[j], 0, 1),
                                              dww, dwb, lng, lnb, pw, ps, past_len)
            xs = _matmul_norm_res([cv_s, pool_s], w_out, g[1], xs, tm=bs, name="out_proj_odd_s")
            cb_s.append(jnp.concatenate([state_conv_buf[j][:, 1:], glu_s[:, None, :]], axis=1))
            pb_s.append(jnp.concatenate([state_pool_buf[j][:, 1:], zs[:, None, 2 * CONV_W:]], axis=1))

        wq, wo = w_xq[l].astype(BF16), w_xo[l].astype(BF16)
        q = _norm_matmul(xp, g[2], wq, tm=tm_p, name="xattn_q")
        o = _xattn_prompt(q, memkv, bp, seq)
        xp = _matmul_norm_res([o], wo, g[3], xp, tm=tm_p, name="xattn_o")
        q_s = _norm_matmul(xs, g[2], wq, tm=bs, name="xattn_q_s")
        o_s = _xattn_sample(q_s.reshape(bs, 1, MEM_W), l,
                            cache_mem_k.reshape(depth, bs, mem_len * MEM_H, MEM_HD),
                            cache_mem_v.reshape(depth, bs, mem_len * MEM_H, MEM_HD))
        xs = _matmul_norm_res([o_s.reshape(bs, MEM_W)], wo, g[3], xs, tm=bs, name="xattn_o_s")

        if l % 2 == 0:
            i = l // 2
            w1, w3, w2 = (_cast_bf16(w[i:i + 1])[0] for w in (w_ff1, w_ff3, w_ff2))
            xp = _ffn(xp, g[4], g[5], w1, w3, w2, tm=tm_p)
            xs = _ffn(xs, g[4], g[5], w1, w3, w2, tm=bs)
        else:
            j = l // 2
            xp, xs = _moe_block([xp, xs], g[4], g[5], w_router[j], b_router[j],
                                _cast_bf16(w_e1[j]), _cast_bf16(w_e3[j]), _cast_bf16(w_e2[j]))

    return (xp.reshape(bp, seq, d), xs.reshape(bs, 1, d),
            jnp.stack(fk_p), jnp.stack(fv_p), jnp.stack(fl_p), jnp.stack(lh_p), jnp.stack(lc_p),
            jnp.stack(cb_p), jnp.stack(pb_p), jnp.stack(mk_pl), jnp.stack(mv_pl),
            jnp.stack(fk_s), jnp.stack(fv_s), jnp.stack(fl_s), jnp.stack(lh_s), jnp.stack(lc_s),
            jnp.stack(cb_s), jnp.stack(pb_s))
```

```python
import functools

import jax
import jax.numpy as jnp
from jax import lax
from jax.experimental import pallas as pl
from jax.experimental.pallas import tpu as pltpu

F32 = jnp.float32
BF16 = jnp.bfloat16

D_MODEL = 1024
LRU_W = 512
LRU_CONV = 4
LRU_C = 8.0
FOX_H = 8
FOX_HD = 64
FOX_W = FOX_H * FOX_HD
CONV_W = 512
CONV_K = 31
POOL_W = 512
POOL_WINDOWS = (2, 4, 8, 16)
POOL_GW = POOL_W // len(POOL_WINDOWS)
POOL_BUF = max(POOL_WINDOWS) - 1
MEM_H = 4
MEM_HD = 128
MEM_W = MEM_H * MEM_HD
N_EXPERTS = 8
EPS = 1e-6
NEG = -1e30
LOG2E = 1.4426950408889634

LANES = 128
ROW_TILE = 512
SEQ_TILE = 512
SCAN_TILE = 256
ATT_ROWS = 32
CONV_ROWS = 64
ATT_HEADS = 4
PAGES_PER_STEP = 16
MOE_TILE = 512
FF_ROW_SLAB = 256
MOE_FF_TILE = 1792
FFN_FF_TILE = 1408
CAST_BLOCK_BYTES = 8 * 1024 * 1024
MOE_PARTS = 4
VMEM_LIMIT = 56 * 1024 * 1024


def _cp(*sem):
    return pltpu.CompilerParams(dimension_semantics=sem, vmem_limit_bytes=VMEM_LIMIT)


def _rms(x, g):
    return x * lax.rsqrt(jnp.mean(x * x, axis=-1, keepdims=True) + EPS) * g


def _sigmoid(x):
    return 1.0 / (1.0 + jnp.exp(-x))


def _softplus(x):
    return jnp.maximum(x, 0.0) + jnp.log1p(jnp.exp(-jnp.abs(x)))


def _gelu_tanh(x):
    return 0.5 * x * (1.0 + jnp.tanh(0.7978845608028654 * (x + 0.044715 * (x * x * x))))


def _dot(a, b):
    return jnp.dot(a, b, preferred_element_type=F32)


def _dot_nt(a, b):
    return lax.dot_general(a, b, (((1,), (1,)), ((), ())), preferred_element_type=F32)


def _cast_kernel(x_ref, o_ref):
    o_ref[...] = x_ref[...].astype(BF16)


def _cast_bf16(w):
    e, k, n = w.shape
    n_k = pl.cdiv(k * n * 4, CAST_BLOCK_BYTES)
    assert k % (16 * n_k) == 0
    return pl.pallas_call(
        _cast_kernel, grid=(e, n_k),
        in_specs=[pl.BlockSpec((None, k // n_k, n), lambda i, j: (i, j, 0))],
        out_specs=pl.BlockSpec((None, k // n_k, n), lambda i, j: (i, j, 0)),
        out_shape=jax.ShapeDtypeStruct(w.shape, BF16),
        compiler_params=_cp("parallel", "parallel"), name="cast_bf16")(w)


def _norm_matmul_kernel(x_ref, g_ref, w_ref, *rest, has_t):
    hn = _rms(x_ref[...], g_ref[...]).astype(BF16)
    if has_t:
        wt_ref, o_ref, ot_ref = rest
        ot_ref[...] = _dot_nt(wt_ref[...], hn)
    else:
        (o_ref,) = rest
    o_ref[...] = _dot(hn, w_ref[...])


def _norm_matmul(x, g, w, *, tm, wt=None, name):
    m, d = x.shape
    n = w.shape[1]
    in_specs = [pl.BlockSpec((tm, d), lambda i: (i, 0)),
                pl.BlockSpec((1, d), lambda i: (0, 0)),
                pl.BlockSpec((d, n), lambda i: (0, 0))]
    out_shape = [jax.ShapeDtypeStruct((m, n), F32)]
    out_specs = [pl.BlockSpec((tm, n), lambda i: (i, 0))]
    args = [x, g, w]
    if wt is not None:
        in_specs.append(pl.BlockSpec(wt.shape, lambda i: (0, 0)))
        out_shape.append(jax.ShapeDtypeStruct((wt.shape[0], m), F32))
        out_specs.append(pl.BlockSpec((wt.shape[0], tm), lambda i: (0, i)))
        args.append(wt)
    res = pl.pallas_call(
        functools.partial(_norm_matmul_kernel, has_t=wt is not None),
        grid=(m // tm,), in_specs=in_specs, out_specs=out_specs, out_shape=out_shape,
        compiler_params=_cp("parallel"), name=name)(*args)
    return res if wt is not None else res[0]


def _matmul_norm_res_kernel(*refs, widths):
    n_a = len(widths)
    a_refs = refs[:n_a]
    w_ref, g_ref, r_ref, o_ref = refs[n_a:]
    y = None
    off = 0
    for a_ref, k in zip(a_refs, widths):
        part = _dot(a_ref[...].astype(BF16), w_ref[off:off + k, :])
        y = part if y is None else y + part
        off += k
    o_ref[...] = r_ref[...] + _rms(y, g_ref[...])


def _matmul_norm_res(a_list, w, g, resid, *, tm, name):
    m, d = resid.shape
    widths = tuple(a.shape[1] for a in a_list)
    in_specs = [pl.BlockSpec((tm, k), lambda i: (i, 0)) for k in widths]
    in_specs += [pl.BlockSpec(w.shape, lambda i: (0, 0)),
                 pl.BlockSpec((1, d), lambda i: (0, 0)),
                 pl.BlockSpec((tm, d), lambda i: (i, 0))]
    return pl.pallas_call(
        functools.partial(_matmul_norm_res_kernel, widths=widths),
        grid=(m // tm,), in_specs=in_specs,
        out_specs=pl.BlockSpec((tm, d), lambda i: (i, 0)),
        out_shape=jax.ShapeDtypeStruct((m, d), F32),
        compiler_params=_cp("parallel"), name=name)(*a_list, w, g, resid)


def _lru_gates(xc, wa_ref, ba_ref, wi_ref, bi_ref, lam_ref):
    xb = xc.astype(BF16)
    ra, ia = [], []
    for c in range(LRU_W // LANES):
        xs = xb[:, c * LANES:(c + 1) * LANES]
        ra.append(_dot(xs, wa_ref[c]))
        ia.append(_dot(xs, wi_ref[c]))
    r = _sigmoid(jnp.concatenate(ra, axis=1) + ba_ref[...])
    ig = _sigmoid(jnp.concatenate(ia, axis=1) + bi_ref[...])
    log_a = -LRU_C * r * _softplus(-lam_ref[...])
    a = jnp.exp(log_a)
    bx = jnp.sqrt(-jnp.tanh(log_a) * (a * a + 1.0)) * (ig * xc)
    return a, bx


def _lru_prompt_kernel(xl_ref, gate_ref, cw_ref, cb_ref, wa_ref, ba_ref, wi_ref, bi_ref, lam_ref,
                       out_ref, hlast_ref, ext_ref, hc_ref, *, tc, n_t):
    t = pl.program_id(1)

    @pl.when(t == 0)
    def _():
        ext_ref[0:8, :] = jnp.zeros((8, LRU_W), F32)
        hc_ref[...] = jnp.zeros((1, LRU_W), F32)

    xl = xl_ref[...]
    ext_ref[8:8 + tc, :] = xl
    xc = cb_ref[...] + cw_ref[LRU_CONV - 1:LRU_CONV, :] * xl
    for j in range(1, LRU_CONV):
        xc = xc + cw_ref[LRU_CONV - 1 - j:LRU_CONV - j, :] * ext_ref[pl.ds(8 - j, tc), :]
    ext_ref[0:8, :] = ext_ref[tc:tc + 8, :]

    a, b = _lru_gates(xc, wa_ref, ba_ref, wi_ref, bi_ref, lam_ref)
    row = lax.broadcasted_iota(jnp.int32, (tc, 1), 0)
    d = 1
    while d < tc:
        keep = row >= d
        a_sh = jnp.where(keep, pltpu.roll(a, d, 0), 1.0)
        b_sh = jnp.where(keep, pltpu.roll(b, d, 0), 0.0)
        b = a * b_sh + b
        a = a * a_sh
        d *= 2
    h = a * hc_ref[...] + b
    hc_ref[...] = h[tc - 1:tc, :]
    out_ref[...] = (_gelu_tanh(gate_ref[...]) * h).astype(BF16)

    @pl.when(t == n_t - 1)
    def _():
        hlast_ref[...] = h[tc - 1:tc, :]


def _lru_prompt(z, n_b, seq, cw, cb, wa, ba, wi, bi, lam):
    tc = SCAN_TILE
    n_t = seq // tc
    vec = lambda: pl.BlockSpec((1, LRU_W), lambda b, t: (0, 0))
    bd = lambda: pl.BlockSpec((LRU_W // LANES, LANES, LANES), lambda b, t: (0, 0, 0))
    return pl.pallas_call(
        functools.partial(_lru_prompt_kernel, tc=tc, n_t=n_t),
        grid=(n_b, n_t),
        in_specs=[pl.BlockSpec((tc, LRU_W), lambda b, t: (b * n_t + t, 0)),
                  pl.BlockSpec((tc, LRU_W), lambda b, t: (b * n_t + t, 1)),
                  pl.BlockSpec((LRU_CONV, LRU_W), lambda b, t: (0, 0)),
                  vec(), bd(), vec(), bd(), vec(), vec()],
        out_specs=[pl.BlockSpec((tc, LRU_W), lambda b, t: (b * n_t + t, 0)),
                   pl.BlockSpec((None, 1, LRU_W), lambda b, t: (b, 0, 0))],
        out_shape=[jax.ShapeDtypeStruct((n_b * seq, LRU_W), BF16),
                   jax.ShapeDtypeStruct((n_b, 1, LRU_W), F32)],
        scratch_shapes=[pltpu.VMEM((tc + 8, LRU_W), F32), pltpu.VMEM((1, LRU_W), F32)],
        compiler_params=_cp("parallel", "arbitrary"), name="lru_prompt")(
            z, z, cw, cb, wa, ba, wi, bi, lam)


def _lru_sample_kernel(z_ref, pre_ref, h0_ref, cw_ref, cb_ref, wa_ref, ba_ref, wi_ref, bi_ref, lam_ref,
                       out_ref, h_ref):
    xl = z_ref[:, 0:LRU_W]
    gate = z_ref[:, LRU_W:2 * LRU_W]
    xc = cb_ref[...] + cw_ref[LRU_CONV - 1:LRU_CONV, :] * xl
    for k in range(LRU_CONV - 1):
        xc = xc + cw_ref[k:k + 1, :] * pre_ref[k]
    a, bx = _lru_gates(xc, wa_ref, ba_ref, wi_ref, bi_ref, lam_ref)
    h = a * h0_ref[...] + bx
    h_ref[...] = h
    out_ref[...] = (_gelu_tanh(gate) * h).astype(BF16)


def _lru_sample(z, prefix_t, h0, cw, cb, wa, ba, wi, bi, lam):
    n_b = z.shape[0]
    return pl.pallas_call(
        _lru_sample_kernel,
        out_shape=[jax.ShapeDtypeStruct((n_b, LRU_W), BF16), jax.ShapeDtypeStruct((n_b, LRU_W), F32)],
        compiler_params=pltpu.CompilerParams(vmem_limit_bytes=VMEM_LIMIT),
        name="lru_sample")(z, prefix_t, h0, cw, cb, wa, ba, wi, bi, lam)


def _log_sigmoid(x):
    return jnp.minimum(x, 0.0) - jnp.log1p(jnp.exp(-jnp.abs(x)))


def _lane_cumsum(x):
    n = x.shape[1]
    lane = lax.broadcasted_iota(jnp.int32, x.shape, 1)
    d = 1
    while d < n:
        x = x + jnp.where(lane >= d, pltpu.roll(x, d, 1), 0.0)
        d *= 2
    return x


def _fox_prep_kernel(fl_ref, bf_ref, lf_ref, c_ref):
    lf = _log_sigmoid(fl_ref[0:FOX_H, :] + bf_ref[...])
    lf_ref[...] = lf
    c_ref[...] = _lane_cumsum(lf)


def _fox_prep(fl_t, bf, n_b, seq):
    return pl.pallas_call(
        _fox_prep_kernel, grid=(n_b,),
        in_specs=[pl.BlockSpec((fl_t.shape[0], seq), lambda b: (0, b)),
                  pl.BlockSpec((FOX_H, 1), lambda b: (0, 0))],
        out_specs=[pl.BlockSpec((FOX_H, seq), lambda b: (0, b)),
                   pl.BlockSpec((FOX_H, seq), lambda b: (0, b))],
        out_shape=[jax.ShapeDtypeStruct((FOX_H, n_b * seq), F32)] * 2,
        compiler_params=_cp("parallel"), name="fox_prep")(fl_t, bf)


def _logsig_kernel(fl_ref, bf_ref, lf_ref):
    lf_ref[...] = _log_sigmoid(fl_ref[0:FOX_H, :] + bf_ref[...])


def _logsig(fl_t, bf):
    return pl.pallas_call(
        _logsig_kernel, out_shape=jax.ShapeDtypeStruct((FOX_H, fl_t.shape[1]), F32),
        name="fox_logf_sample")(fl_t, bf)


def _fox_prompt_kernel(qi_ref, ki_ref, q_ref, k_ref, v_ref, cq_ref, ck_ref, o_ref,
                       m_sc, l_sc, acc_sc, s_sc, p_sc, al_sc, *, tb, hg):
    grp = pl.program_id(1)
    t = pl.program_id(2)
    qi = qi_ref[t]
    ki = ki_ref[t]
    width = hg * FOX_HD

    @pl.when(ki == 0)
    def _():
        m_sc[...] = jnp.full(m_sc.shape, NEG, F32)
        l_sc[...] = jnp.zeros(l_sc.shape, F32)
        acc_sc[...] = jnp.zeros(acc_sc.shape, F32)

    def step(diagonal):
        q = q_ref[...] * (FOX_HD ** -0.5 * LOG2E)
        k = k_ref[...].astype(BF16)
        v = v_ref[...].astype(BF16)
        lane_head = jnp.right_shift(lax.broadcasted_iota(jnp.int32, (tb, width), 1),
                                    FOX_HD.bit_length() - 1)
        rc = ATT_ROWS
        if diagonal:
            row_i = lax.broadcasted_iota(jnp.int32, (rc, tb), 0)
            col_i = lax.broadcasted_iota(jnp.int32, (rc, tb), 1)
        n_lt = tb // LANES
        for h in range(hg):
            s_sc[h] = _dot_nt(jnp.where(lane_head == h, q, 0.0).astype(BF16), k)
            c_q = cq_ref[pl.ds(hg * grp + h, 1), :]
            c_k = ck_ref[pl.ds(hg * grp + h, 1), :]
            bias = (c_q[:, 0:1] - c_k) * LOG2E
            for c in range(tb // rc):
                rows = slice(c * rc, (c + 1) * rc)
                s = s_sc[h, rows, :] + bias
                if diagonal:
                    s = jnp.where(col_i <= row_i + c * rc, s, NEG)
                s_sc[h, rows, :] = s
                m_prev = m_sc[h, rows, :]
                m_new = jnp.maximum(m_prev, jnp.max(s, axis=1, keepdims=True))
                m_sc[h, rows, :] = m_new
                al_sc[h, rows, :] = jnp.exp2(m_prev - m_new)
            for c in range(tb // rc):
                rows = slice(c * rc, (c + 1) * rc)
                m_new = m_sc[h, rows, :]
                p_sum = None
                for j in range(n_lt):
                    cols = slice(j * LANES, (j + 1) * LANES)
                    p = jnp.exp2(s_sc[h, rows, cols] - m_new)
                    p_sc[h, rows, cols] = p.astype(BF16)
                    p_sum = p if p_sum is None else p_sum + p
                l_sc[h, rows, :] = al_sc[h, rows, :] * l_sc[h, rows, :] + p_sum
            lt = (h * FOX_HD) // LANES
            pv = _dot(p_sc[h], v)[:, lt * LANES:(lt + 1) * LANES]
            acc_sc[h] = al_sc[h] * acc_sc[h] + pv

    @pl.when(ki < qi)
    def _():
        step(False)

    @pl.when(ki == qi)
    def _():
        step(True)
        lane = lax.broadcasted_iota(jnp.int32, (tb, LANES), 1)
        heads_per_tile = LANES // FOX_HD
        for lt in range(width // LANES):
            o = None
            for i in range(heads_per_tile):
                h = lt * heads_per_tile + i
                o_h = acc_sc[h] / jnp.sum(l_sc[h], axis=1, keepdims=True)
                o = o_h if o is None else jnp.where(lane < i * FOX_HD, o, o_h)
            o_ref[:, lt * LANES:(lt + 1) * LANES] = o.astype(BF16)


def _fox_prompt(z, c_t, n_b, seq):
    tb = SEQ_TILE
    hg = ATT_HEADS
    width = hg * FOX_HD
    n_q = seq // tb
    q_blk = (2 * LRU_W) // width
    k_blk = q_blk + FOX_W // width
    v_blk = k_blk + FOX_W // width
    pairs = [(qi, ki) for qi in range(n_q) for ki in range(qi + 1)]
    qi_list = jnp.asarray([p[0] for p in pairs], jnp.int32)
    ki_list = jnp.asarray([p[1] for p in pairs], jnp.int32)
    grid_spec = pltpu.PrefetchScalarGridSpec(
        num_scalar_prefetch=2, grid=(n_b, FOX_H // hg, len(pairs)),
        in_specs=[
            pl.BlockSpec((tb, width), lambda b, g, t, qi, ki: (b * n_q + qi[t], q_blk + g)),
            pl.BlockSpec((tb, width), lambda b, g, t, qi, ki: (b * n_q + ki[t], k_blk + g)),
            pl.BlockSpec((tb, width), lambda b, g, t, qi, ki: (b * n_q + ki[t], v_blk + g)),
            pl.BlockSpec((FOX_H, tb), lambda b, g, t, qi, ki: (0, b * n_q + qi[t])),
            pl.BlockSpec((FOX_H, tb), lambda b, g, t, qi, ki: (0, b * n_q + ki[t])),
        ],
        out_specs=pl.BlockSpec((tb, width), lambda b, g, t, qi, ki: (b * n_q + qi[t], g)),
        scratch_shapes=[pltpu.VMEM((hg, tb, LANES), F32), pltpu.VMEM((hg, tb, LANES), F32),
                        pltpu.VMEM((hg, tb, LANES), F32), pltpu.VMEM((hg, tb, tb), F32),
                        pltpu.VMEM((hg, tb, tb), BF16), pltpu.VMEM((hg, tb, LANES), F32)])
    return pl.pallas_call(
        functools.partial(_fox_prompt_kernel, tb=tb, hg=hg), grid_spec=grid_spec,
        out_shape=jax.ShapeDtypeStruct((n_b * seq, FOX_W), BF16),
        compiler_params=_cp("parallel", "parallel", "arbitrary"),
        name="fox_prompt")(qi_list, ki_list, z, z, z, c_t, c_t)


def _fox_sample_kernel(pt_ref, q_ref, kn_ref, vn_ref, lfn_ref, *rest, n_pg, n_g):
    k_refs = rest[0:n_pg]
    v_refs = rest[n_pg:2 * n_pg]
    lf_refs = rest[2 * n_pg:3 * n_pg]
    o_ref, m_sc, l_sc, acc_sc, cc_sc = rest[3 * n_pg:]
    g = pl.program_id(1)
    page = lf_refs[0].shape[1]

    @pl.when(g == 0)
    def _():
        m_sc[...] = jnp.full(m_sc.shape, NEG, F32)
        l_sc[...] = jnp.zeros(l_sc.shape, F32)
        acc_sc[...] = jnp.zeros(acc_sc.shape, F32)
        cc_sc[...] = jnp.zeros(cc_sc.shape, F32)


    q = q_ref[...] * (FOX_HD ** -0.5)
    q16 = jnp.concatenate([q, jnp.zeros_like(q)], axis=0)
    row_q = lax.broadcasted_iota(jnp.int32, (2 * FOX_H, FOX_HD), 0)
    q_only = [jnp.where(row_q == h, q16, 0.0).astype(BF16) for h in range(FOX_H)]

    s_parts = []
    for j in range(n_pg):
        s_j = None
        for h in range(FOX_H):
            part = _dot(q_only[h], k_refs[j][h].astype(BF16))
            s_j = part if s_j is None else s_j + part
        s_parts.append(s_j[0:FOX_H])
    s = jnp.concatenate(s_parts, axis=1)
    lf = jnp.concatenate([lf_refs[j][...] for j in range(n_pg)], axis=1)
    c = _lane_cumsum(lf) + cc_sc[...]
    cc_sc[...] = c[:, c.shape[1] - 1:]
    s = s - c
    m_prev = m_sc[...]
    m_new = jnp.maximum(m_prev, jnp.max(s, axis=1, keepdims=True))
    alpha = jnp.exp(m_prev - m_new)
    p = jnp.exp(s - m_new)
    l_sc[...] = alpha * l_sc[...] + jnp.sum(p, axis=1, keepdims=True)
    p16 = jnp.concatenate([p, jnp.zeros_like(p)], axis=0)
    row_p = lax.broadcasted_iota(jnp.int32, (2 * FOX_H, page), 0)
    pv = None
    for j in range(n_pg):
        p_j = p16[:, j * page:(j + 1) * page]
        for h in range(FOX_H):
            part = _dot_nt(jnp.where(row_p == h, p_j, 0.0).astype(BF16), v_refs[j][h].astype(BF16))
            pv = part if pv is None else pv + part
    acc_sc[...] = alpha * acc_sc[...] + pv[0:FOX_H]
    m_sc[...] = m_new

    @pl.when(g == n_g - 1)
    def _():
        s_n = jnp.sum(q * kn_ref[...], axis=1, keepdims=True)
        s_n = s_n - (cc_sc[...] + lfn_ref[...])
        m_p = m_sc[...]
        m_n = jnp.maximum(m_p, s_n)
        al = jnp.exp(m_p - m_n)
        p_n = jnp.exp(s_n - m_n)
        l_n = al * l_sc[...] + p_n
        o_ref[...] = ((al * acc_sc[...] + p_n * vn_ref[...]) / l_n).astype(BF16)


def _fox_sample(page_table, layer, q, k_new, v_new, lf_new, cache_k, cache_v, cache_lf_t):
    n_b, n_pages = page_table.shape
    n_pg = PAGES_PER_STEP
    n_g = n_pages // n_pg
    page = cache_k.shape[4]
    head_spec = lambda: pl.BlockSpec((None, FOX_H, FOX_HD), lambda b, g, pt: (b, 0, 0))

    def kv_spec(j):
        return pl.BlockSpec((None, None, FOX_H, FOX_HD, page),
                            lambda b, g, pt, j=j: (layer, pt[b * n_pages + g * n_pg + j], 0, 0, 0))

    def lf_spec(j):
        return pl.BlockSpec((None, FOX_H, page),
                            lambda b, g, pt, j=j: (pt[b * n_pages + g * n_pg + j], 0, 0))

    in_specs = [head_spec(), head_spec(), head_spec(),
                pl.BlockSpec((None, FOX_H, 1), lambda b, g, pt: (b, 0, 0))]
    in_specs += [kv_spec(j) for j in range(n_pg)]
    in_specs += [kv_spec(j) for j in range(n_pg)]
    in_specs += [lf_spec(j) for j in range(n_pg)]
    grid_spec = pltpu.PrefetchScalarGridSpec(
        num_scalar_prefetch=1, grid=(n_b, n_g), in_specs=in_specs,
        out_specs=pl.BlockSpec((None, FOX_H, FOX_HD), lambda b, g, pt: (b, 0, 0)),
        scratch_shapes=[pltpu.VMEM((FOX_H, 1), F32), pltpu.VMEM((FOX_H, 1), F32),
                        pltpu.VMEM((FOX_H, FOX_HD), F32), pltpu.VMEM((FOX_H, 1), F32)])
    return pl.pallas_call(
        functools.partial(_fox_sample_kernel, n_pg=n_pg, n_g=n_g),
        grid_spec=grid_spec,
        out_shape=jax.ShapeDtypeStruct((n_b, FOX_H, FOX_HD), BF16),
        compiler_params=_cp("parallel", "arbitrary"), name="fox_sample")(
            page_table.reshape(-1), q, k_new, v_new, lf_new,
            *([cache_k] * n_pg), *([cache_v] * n_pg), *([cache_lf_t] * n_pg))


def _xattn_prompt_kernel(q_ref, mk_ref, mv_ref, o_ref):
    outs = []
    for h in range(MEM_H):
        cs = slice(h * MEM_HD, (h + 1) * MEM_HD)
        qh = q_ref[:, cs].astype(BF16)
        s = _dot_nt(qh, mk_ref[:, cs].astype(BF16)) * (MEM_HD ** -0.5)
        p = jnp.exp(s - jnp.max(s, axis=1, keepdims=True))
        l = jnp.sum(p, axis=1, keepdims=True)
        outs.append(_dot(p.astype(BF16), mv_ref[:, cs].astype(BF16)) / l)
    o_ref[...] = jnp.concatenate(outs, axis=1).astype(BF16)


def _xattn_prompt(q, memkv, n_b, seq):
    tb = SEQ_TILE
    n_t = seq // tb
    mem_len = memkv.shape[0] // n_b
    return pl.pallas_call(
        _xattn_prompt_kernel, grid=(n_b, n_t),
        in_specs=[pl.BlockSpec((tb, MEM_W), lambda b, t: (b * n_t + t, 0)),
                  pl.BlockSpec((mem_len, MEM_W), lambda b, t: (b, 0)),
                  pl.BlockSpec((mem_len, MEM_W), lambda b, t: (b, 1))],
        out_specs=pl.BlockSpec((tb, MEM_W), lambda b, t: (b * n_t + t, 0)),
        out_shape=jax.ShapeDtypeStruct((n_b * seq, MEM_W), BF16),
        compiler_params=_cp("parallel", "parallel"), name="xattn_prompt")(q, memkv, memkv)


def _xattn_sample_kernel(q_ref, mk_ref, mv_ref, o_ref):
    q = q_ref[...]
    row = lax.broadcasted_iota(jnp.int32, (16, MEM_W), 0)
    col = lax.broadcasted_iota(jnp.int32, (16, MEM_W), 1)
    head_cols = jnp.right_shift(col, MEM_HD.bit_length() - 1) == row
    q_rows = jnp.where(head_cols, q, 0.0).astype(BF16)
    mem_len = mk_ref.shape[0] // MEM_H

    def heads_on_lanes(ref):
        return jnp.concatenate([ref[pl.ds(h, mem_len, stride=MEM_H), :] for h in range(MEM_H)],
                               axis=1).astype(BF16)

    s = _dot_nt(q_rows, heads_on_lanes(mk_ref)) * (MEM_HD ** -0.5)
    p = jnp.exp(s - jnp.max(s, axis=1, keepdims=True))
    l = jnp.sum(p, axis=1, keepdims=True)
    o = _dot(p.astype(BF16), heads_on_lanes(mv_ref)) / l
    o_ref[...] = jnp.sum(jnp.where(head_cols, o, 0.0), axis=0, keepdims=True).astype(BF16)


def _xattn_sample(q, layer, mk, mv):
    n_b, rows = mk.shape[1], mk.shape[2]
    return pl.pallas_call(
        _xattn_sample_kernel, grid=(n_b,),
        in_specs=[pl.BlockSpec((None, 1, MEM_W), lambda b: (b, 0, 0)),
                  pl.BlockSpec((None, None, rows, MEM_HD), lambda b: (layer, b, 0, 0)),
                  pl.BlockSpec((None, None, rows, MEM_HD), lambda b: (layer, b, 0, 0))],
        out_specs=pl.BlockSpec((None, 1, MEM_W), lambda b: (b, 0, 0)),
        out_shape=jax.ShapeDtypeStruct((n_b, 1, MEM_W), BF16),
        compiler_params=_cp("parallel"), name="xattn_sample")(q, mk, mv)


def _swiglu_accumulate(x_sc, w1_ref, w3_ref, w2_ref, acc_sc):
    tm = x_sc.shape[0]
    slab = min(tm, FF_ROW_SLAB)
    for r in range(tm // slab):
        rows = slice(r * slab, (r + 1) * slab)
        x = x_sc[rows, :]
        h1 = _dot(x, w1_ref[...])
        h3 = _dot(x, w3_ref[...])
        hh = (h1 * _sigmoid(h1) * h3).astype(BF16)
        acc_sc[rows, :] += _dot(hh, w2_ref[...])


def _ffn_kernel(x_ref, g4_ref, g5_ref, w1_ref, w3_ref, w2_ref, o_ref, hn_sc, acc_sc, *, n_f):
    f = pl.program_id(1)

    @pl.when(f == 0)
    def _():
        hn_sc[...] = _rms(x_ref[...], g4_ref[...]).astype(BF16)
        acc_sc[...] = jnp.zeros(acc_sc.shape, F32)

    _swiglu_accumulate(hn_sc, w1_ref, w3_ref, w2_ref, acc_sc)

    @pl.when(f == n_f - 1)
    def _():
        o_ref[...] = x_ref[...] + _rms(acc_sc[...], g5_ref[...])


def _ffn(x, g4, g5, w1, w3, w2, *, tm):
    m, d = x.shape
    ff = w1.shape[1]
    tf = FFN_FF_TILE
    n_f = ff // tf
    return pl.pallas_call(
        functools.partial(_ffn_kernel, n_f=n_f), grid=(m // tm, n_f),
        in_specs=[pl.BlockSpec((tm, d), lambda i, f: (i, 0)),
                  pl.BlockSpec((1, d), lambda i, f: (0, 0)),
                  pl.BlockSpec((1, d), lambda i, f: (0, 0)),
                  pl.BlockSpec((d, tf), lambda i, f: (0, f)),
                  pl.BlockSpec((d, tf), lambda i, f: (0, f)),
                  pl.BlockSpec((tf, d), lambda i, f: (f, 0))],
        out_specs=pl.BlockSpec((tm, d), lambda i, f: (i, 0)),
        out_shape=jax.ShapeDtypeStruct((m, d), F32),
        scratch_shapes=[pltpu.VMEM((tm, d), BF16), pltpu.VMEM((tm, d), F32)],
        compiler_params=_cp("parallel", "arbitrary"), name="ffn")(x, g4, g5, w1, w3, w2)


def _layernorm_silu(x, g, b):
    mu = jnp.mean(x, axis=-1, keepdims=True)
    xc = x - mu
    y = xc * lax.rsqrt(jnp.mean(xc * xc, axis=-1, keepdims=True) + EPS) * g + b
    return y * _sigmoid(y)


def _odd_prompt_kernel(a_ref, gt_ref, up_ref, dww_ref, dwb_ref, lng_ref, lnb_ref, pw_ref, ps_ref,
                       cv_ref, pool_ref, cbuf_ref, pbuf_ref, eg_ref, eu_ref, sh_ref, *, tc, n_t):
    t = pl.program_id(1)
    halo_g, halo_u = 32, 16

    @pl.when(t == 0)
    def _():
        eg_ref[0:halo_g, :] = jnp.zeros((halo_g, CONV_W), F32)
        eu_ref[0:halo_u, :] = jnp.zeros((halo_u, POOL_W), F32)

    glu = a_ref[...] * _sigmoid(gt_ref[...])
    up = up_ref[...]
    eg_ref[halo_g:halo_g + tc, :] = glu
    eu_ref[halo_u:halo_u + tc, :] = up

    ext = eg_ref[...]
    n_ext = tc + halo_g
    sh_ref[0] = ext
    for s in range(1, 8):
        sh_ref[s] = pltpu.roll(ext, n_ext - s, 0)
    rc = CONV_ROWS
    for c in range(tc // rc):
        parts = []
        for j in range(CONV_W // LANES):
            cols = slice(j * LANES, (j + 1) * LANES)
            acc = jnp.broadcast_to(dwb_ref[:, cols], (rc, LANES))
            for k in range(CONV_K):
                off = k + halo_g - (CONV_K - 1)
                acc = acc + dww_ref[k:k + 1, cols] * sh_ref[off % 8, pl.ds(c * rc + (off // 8) * 8, rc), cols]
            parts.append(acc)
        cv_ref[c * rc:(c + 1) * rc, :] = _layernorm_silu(
            jnp.concatenate(parts, axis=1), lng_ref[...], lnb_ref[...]).astype(BF16)

    pos = t * tc + lax.broadcasted_iota(jnp.int32, (tc, 1), 0)
    outs = []
    for gi, w in enumerate(POOL_WINDOWS):
        cs = slice(gi * POOL_GW, (gi + 1) * POOL_GW)
        u_g = up[:, cs]
        win = u_g
        for j in range(1, w):
            win = win + eu_ref[pl.ds(halo_u - j, tc), cs]
        cnt = jnp.minimum(pos + 1, w).astype(F32)
        dlt = win / cnt - u_g
        outs.append(_dot(dlt.astype(BF16), pw_ref[gi]))
    pool_ref[...] = (jnp.concatenate(outs, axis=1) * ps_ref[...]).astype(BF16)

    @pl.when(t == n_t - 1)
    def _():
        cbuf_ref[...] = eg_ref[pl.ds(halo_g + tc - (CONV_K - 1), CONV_K - 1), :]
        pbuf_ref[...] = eu_ref[pl.ds(halo_u + tc - POOL_BUF, POOL_BUF), :]

    eg_ref[0:halo_g, :] = eg_ref[tc:tc + halo_g, :]
    eu_ref[0:halo_u, :] = eu_ref[tc:tc + halo_u, :]


def _odd_prompt(z, n_b, seq, dww, dwb, lng, lnb, pw, ps):
    tc = SEQ_TILE
    n_t = seq // tc
    vec = lambda: pl.BlockSpec((1, CONV_W), lambda b, t: (0, 0))
    return pl.pallas_call(
        functools.partial(_odd_prompt_kernel, tc=tc, n_t=n_t), grid=(n_b, n_t),
        in_specs=[pl.BlockSpec((tc, CONV_W), lambda b, t: (b * n_t + t, 0)),
                  pl.BlockSpec((tc, CONV_W), lambda b, t: (b * n_t + t, 1)),
                  pl.BlockSpec((tc, POOL_W), lambda b, t: (b * n_t + t, 2)),
                  pl.BlockSpec((CONV_K, CONV_W), lambda b, t: (0, 0)),
                  vec(), vec(), vec(),
                  pl.BlockSpec((len(POOL_WINDOWS), POOL_GW, POOL_GW), lambda b, t: (0, 0, 0)),
                  vec()],
        out_specs=[pl.BlockSpec((tc, CONV_W), lambda b, t: (b * n_t + t, 0)),
                   pl.BlockSpec((tc, POOL_W), lambda b, t: (b * n_t + t, 0)),
                   pl.BlockSpec((None, CONV_K - 1, CONV_W), lambda b, t: (b, 0, 0)),
                   pl.BlockSpec((None, POOL_BUF, POOL_W), lambda b, t: (b, 0, 0))],
        out_shape=[jax.ShapeDtypeStruct((n_b * seq, CONV_W), BF16),
                   jax.ShapeDtypeStruct((n_b * seq, POOL_W), BF16),
                   jax.ShapeDtypeStruct((n_b, CONV_K - 1, CONV_W), F32),
                   jax.ShapeDtypeStruct((n_b, POOL_BUF, POOL_W), F32)],
        scratch_shapes=[pltpu.VMEM((tc + 32, CONV_W), F32), pltpu.VMEM((tc + 16, POOL_W), F32),
                        pltpu.VMEM((8, tc + 32, CONV_W), F32)],
        compiler_params=_cp("parallel", "arbitrary"), name="odd_prompt")(
            z, z, z, dww, dwb, lng, lnb, pw, ps)


def _odd_sample_kernel(z_ref, cbuf_ref, pbuf_ref, dww_ref, dwb_ref, lng_ref, lnb_ref, pw_ref, ps_ref,
                       cv_ref, pool_ref, glu_ref, *, pos0):
    glu = z_ref[:, 0:CONV_W] * _sigmoid(z_ref[:, CONV_W:2 * CONV_W])
    up = z_ref[:, 2 * CONV_W:2 * CONV_W + POOL_W]
    glu_ref[...] = glu
    acc = dwb_ref[...] + dww_ref[CONV_K - 1:CONV_K, :] * glu
    for k in range(CONV_K - 1):
        acc = acc + dww_ref[k:k + 1, :] * cbuf_ref[k]
    cv_ref[...] = _layernorm_silu(acc, lng_ref[...], lnb_ref[...]).astype(BF16)
    outs = []
    for gi, w in enumerate(POOL_WINDOWS):
        cs = slice(gi * POOL_GW, (gi + 1) * POOL_GW)
        u_g = up[:, cs]
        win = u_g
        for j in range(1, w):
            win = win + pbuf_ref[POOL_BUF - j][:, cs]
        dlt = win / float(min(pos0 + 1, w)) - u_g
        outs.append(_dot(dlt.astype(BF16), pw_ref[gi]))
    pool_ref[...] = (jnp.concatenate(outs, axis=1) * ps_ref[...]).astype(BF16)


def _odd_sample(z, cbuf_t, pbuf_t, dww, dwb, lng, lnb, pw, ps, pos0):
    n_b = z.shape[0]
    return pl.pallas_call(
        functools.partial(_odd_sample_kernel, pos0=pos0),
        out_shape=[jax.ShapeDtypeStruct((n_b, CONV_W), BF16),
                   jax.ShapeDtypeStruct((n_b, POOL_W), BF16),
                   jax.ShapeDtypeStruct((n_b, CONV_W), F32)],
        compiler_params=pltpu.CompilerParams(vmem_limit_bytes=VMEM_LIMIT),
        name="odd_sample")(z, cbuf_t, pbuf_t, dww, dwb, lng, lnb, pw, ps)


def _router_kernel(x_ref, g_ref, whi_ref, wlo_ref, b_ref, *rest):
    hn_ref, idx_ref, gate_ref = rest[-3:]
    hn = _rms(x_ref[...], g_ref[...])
    hb = hn.astype(BF16)
    hn_ref[...] = hn
    hlo = (hn - hb.astype(F32)).astype(BF16)
    logits = _dot(hb, whi_ref[...]) + (_dot(hb, wlo_ref[...]) + _dot(hlo, whi_ref[...])) + b_ref[...]
    lane = lax.broadcasted_iota(jnp.int32, logits.shape, 1)
    lane_f = lane.astype(F32)
    logits = jnp.where(lane < N_EXPERTS, logits, NEG)
    m1 = jnp.max(logits, axis=1, keepdims=True)
    i1 = jnp.min(jnp.where(logits == m1, lane_f, float(LANES)), axis=1, keepdims=True)
    rest = jnp.where(lane_f == i1, NEG, logits)
    m2 = jnp.max(rest, axis=1, keepdims=True)
    i2 = jnp.min(jnp.where(rest == m2, lane_f, float(LANES)), axis=1, keepdims=True)
    e = jnp.exp(m2 - m1)
    g1 = 1.0 / (1.0 + e)
    g2 = e / (1.0 + e)
    idx_ref[...] = jnp.where(lane == 0, i1, jnp.where(lane == 1, i2, 0.0)).astype(jnp.int32)
    gate_ref[...] = jnp.where(lane == 0, g1, jnp.where(lane == 1, g2, 0.0))


def _router(x, g, whi, wlo, b, *, tm, after=()):
    m, d = x.shape
    return pl.pallas_call(
        _router_kernel, grid=(m // tm,),
        in_specs=[pl.BlockSpec((tm, d), lambda i: (i, 0)),
                  pl.BlockSpec((1, d), lambda i: (0, 0)),
                  pl.BlockSpec((d, LANES), lambda i: (0, 0)),
                  pl.BlockSpec((d, LANES), lambda i: (0, 0)),
                  pl.BlockSpec((1, LANES), lambda i: (0, 0))]
        + [pl.BlockSpec(memory_space=pl.ANY) for _ in after],
        out_specs=[pl.BlockSpec((tm, d), lambda i: (i, 0)),
                   pl.BlockSpec((tm, LANES), lambda i: (i, 0)),
                   pl.BlockSpec((tm, LANES), lambda i: (i, 0))],
        out_shape=[jax.ShapeDtypeStruct((m, d), F32),
                   jax.ShapeDtypeStruct((m, LANES), jnp.int32),
                   jax.ShapeDtypeStruct((m, LANES), F32)],
        compiler_params=_cp("parallel"), name="router")(x, g, whi, wlo, b, *after)


def _gmm_kernel(te_ref, tv_ref, x_ref, gate_ref, w1_ref, w3_ref, w2_ref, *rest, n_f, tile0):
    o_ref, acc_sc, xb_sc = rest[-3:]
    t = tile0 + pl.program_id(0)
    f = pl.program_id(1)

    @pl.when(f == 0)
    def _():
        acc_sc[...] = jnp.zeros(acc_sc.shape, F32)
        xb_sc[...] = x_ref[...].astype(BF16)

    @pl.when(tv_ref[t] != 0)
    def _():
        _swiglu_accumulate(xb_sc, w1_ref, w3_ref, w2_ref, acc_sc)

    @pl.when(f == n_f - 1)
    def _():
        o_ref[...] = acc_sc[...] * gate_ref[...]


def _gmm(tile_expert, tile_valid, x_part, gate_sorted, w1, w3, w2, *, tile0, y_prev=None):
    rows = gate_sorted.shape[0]
    d, ff = w1.shape[1], w1.shape[2]
    tm, tf = MOE_TILE, MOE_FF_TILE
    n_f = ff // tf
    in_specs = [pl.BlockSpec((tm, d), lambda t, f, te, tv: (t, 0)),
                pl.BlockSpec((tm, 1), lambda t, f, te, tv: (tile0 + t, 0)),
                pl.BlockSpec((None, d, tf), lambda t, f, te, tv: (te[tile0 + t], 0, f)),
                pl.BlockSpec((None, d, tf), lambda t, f, te, tv: (te[tile0 + t], 0, f)),
                pl.BlockSpec((None, tf, d), lambda t, f, te, tv: (te[tile0 + t], f, 0))]
    args = [tile_expert, tile_valid, x_part, gate_sorted, w1, w3, w2]
    aliases = {}
    if y_prev is not None:
        in_specs.append(pl.BlockSpec(memory_space=pl.ANY))
        aliases = {len(args): 0}
        args.append(y_prev)
    grid_spec = pltpu.PrefetchScalarGridSpec(
        num_scalar_prefetch=2, grid=(x_part.shape[0] // tm, n_f), in_specs=in_specs,
        out_specs=pl.BlockSpec((tm, d), lambda t, f, te, tv: (tile0 + t, 0)),
        scratch_shapes=[pltpu.VMEM((tm, d), F32), pltpu.VMEM((tm, d), BF16)])
    return pl.pallas_call(
        functools.partial(_gmm_kernel, n_f=n_f, tile0=tile0), grid_spec=grid_spec,
        out_shape=jax.ShapeDtypeStruct((rows, d), F32), input_output_aliases=aliases,
        compiler_params=_cp("parallel", "arbitrary"), name="moe_experts")(*args)


def _combine_kernel(x_ref, y1_ref, y2_ref, g_ref, o_ref):
    o_ref[...] = x_ref[...] + _rms(y1_ref[...] + y2_ref[...], g_ref[...])


def _combine(x, y1, y2, g, *, tm):
    m, d = x.shape
    row = lambda: pl.BlockSpec((tm, d), lambda i: (i, 0))
    return pl.pallas_call(
        _combine_kernel, grid=(m // tm,),
        in_specs=[row(), row(), row(), pl.BlockSpec((1, d), lambda i: (0, 0))],
        out_specs=row(), out_shape=jax.ShapeDtypeStruct((m, d), F32),
        compiler_params=_cp("parallel"), name="moe_combine")(x, y1, y2, g)


def _moe_block(xs, g4, g5, w_r, b_r, w1, w3, w2):
    pad = LANES - N_EXPERTS
    w_r_p = jnp.pad(w_r, ((0, 0), (0, pad)))
    whi = w_r_p.astype(BF16)
    wlo = (w_r_p - whi.astype(F32)).astype(BF16)
    b_p = jnp.pad(b_r, (0, pad)).reshape(1, LANES)
    hn, idx, gate = [], [], []
    for x in xs:
        h, i, gt = _router(x, g4, whi, wlo, b_p, tm=min(ROW_TILE, x.shape[0]), after=(w1, w3, w2))
        hn.append(h)
        idx.append(i[:, :2])
        gate.append(gt[:, :2])
    hn = jnp.concatenate(hn, axis=0)
    e_flat = jnp.concatenate(idx, axis=0).reshape(-1)
    g_flat = jnp.concatenate(gate, axis=0).reshape(-1)
    n_assign = e_flat.shape[0]
    tm = MOE_TILE
    n_tiles = n_assign // tm + N_EXPERTS
    rows = n_tiles * tm

    onehot = (e_flat[:, None] == jnp.arange(N_EXPERTS, dtype=jnp.int32)[None, :]).astype(jnp.int32)
    rank = jnp.sum((jnp.cumsum(onehot, axis=0) - onehot) * onehot, axis=1)
    cnt = jnp.sum(onehot, axis=0)
    tiles_e = (cnt + tm - 1) // tm
    tile_end = jnp.cumsum(tiles_e)
    row_start = (tile_end - tiles_e) * tm
    dest = row_start[e_flat] + rank
    tile_ids = jnp.arange(n_tiles, dtype=jnp.int32)
    tile_expert = jnp.minimum(jnp.sum((tile_end[None, :] <= tile_ids[:, None]).astype(jnp.int32), axis=1),
                              N_EXPERTS - 1)
    tile_valid = (tile_ids < tile_end[-1]).astype(jnp.int32)
    row_assign = jnp.zeros((rows,), jnp.int32).at[dest].set(jnp.arange(n_assign, dtype=jnp.int32))
    row_ids = jnp.arange(rows, dtype=jnp.int32)
    row_expert = jnp.repeat(tile_expert, tm)
    row_valid = (row_ids - row_start[row_expert]) < cnt[row_expert]
    src_tok = row_assign // 2
    gate_sorted = jnp.where(row_valid, g_flat[row_assign], 0.0).reshape(rows, 1)

    take_rows = lambda a, idx: a.at[idx].get(mode="promise_in_bounds")
    y_sorted = jnp.zeros((rows, hn.shape[1]), F32)
    part_tiles = n_tiles // MOE_PARTS
    for part in range(MOE_PARTS):
        tile0 = part * part_tiles
        x_part = take_rows(hn, src_tok[tile0 * tm:(tile0 + part_tiles) * tm])
        y_sorted = _gmm(tile_expert, tile_valid, x_part, gate_sorted, w1, w3, w2, tile0=tile0, y_prev=y_sorted)

    outs = []
    off = 0
    dest2 = dest.reshape(-1, 2)
    for x in xs:
        m = x.shape[0]
        d1 = dest2[off:off + m, 0]
        d2 = dest2[off:off + m, 1]
        outs.append(_combine(x, take_rows(y_sorted, d1), take_rows(y_sorted, d2), g5,
                             tm=min(ROW_TILE, m)))
        off += m
    return outs


def _block_diag_pairs(w):
    nb, bw, _ = w.shape
    w = w.reshape(nb // 2, 2, bw, bw)
    z = jnp.zeros((nb // 2, bw, bw), w.dtype)
    top = jnp.concatenate([w[:, 0], z], axis=2)
    bot = jnp.concatenate([z, w[:, 1]], axis=2)
    return jnp.concatenate([top, bot], axis=1).astype(BF16)


def kernel(x_prompt, x_sample, cache_fox_k, cache_fox_v, cache_fox_logf, state_lru_h, state_lru_conv, state_conv_buf, state_pool_buf, cache_mem_k, cache_mem_v, page_table, mem_prompt, norm_g, w_xq, w_xk, w_xv, w_xo, w_in_e, b_f, lru_conv_w, lru_conv_b, lru_wa, lru_ba, lru_wi, lru_bi, lru_lam, w_out_e, w_ff1, w_ff3, w_ff2, w_in_o, cc_dw_w, cc_dw_b, cc_ln_g, cc_ln_b, pool_w, pool_scale, w_out_o, w_router, b_router, w_e1, w_e3, w_e2):
    bp, seq, d = x_prompt.shape
    bs = x_sample.shape[0]
    depth = norm_g.shape[0]
    page = cache_fox_k.shape[2]
    past_len = page_table.shape[1] * page
    mem_len = mem_prompt.shape[1]
    tm_p = ROW_TILE

    xp = x_prompt.reshape(bp * seq, d)
    xs = x_sample.reshape(bs, d)
    mem = mem_prompt.reshape(bp * mem_len, d)
    vec = lambda v: v.reshape(1, -1)

    fk_p, fv_p, fl_p, lh_p, lc_p, cb_p, pb_p, mk_pl, mv_pl = [], [], [], [], [], [], [], [], []
    fk_s, fv_s, fl_s, lh_s, lc_s, cb_s, pb_s = [], [], [], [], [], [], []

    for l in range(depth):
        g = [vec(norm_g[l, i]) for i in range(norm_g.shape[1])]
        w_kv = jnp.concatenate([w_xk[l], w_xv[l]], axis=1).astype(BF16)
        memkv = _norm_matmul(mem, g[6], w_kv, tm=tm_p, name="mem_kv")
        mk_pl.append(memkv[:, :MEM_W].reshape(bp, mem_len, MEM_H, MEM_HD))
        mv_pl.append(memkv[:, MEM_W:].reshape(bp, mem_len, MEM_H, MEM_HD))

        if l % 2 == 0:
            i = l // 2
            n_main = 2 * LRU_W + 3 * FOX_W
            w_main = w_in_e[i][:, :n_main].astype(BF16)
            w_fl_t = jnp.pad(w_in_e[i][:, n_main:].T, ((0, 16 - FOX_H), (0, 0))).astype(BF16)
            cw, cb = lru_conv_w[i], vec(lru_conv_b[i])
            wa, wi = _block_diag_pairs(lru_wa[i]), _block_diag_pairs(lru_wi[i])
            ba, bi, lam = vec(lru_ba[i]), vec(lru_bi[i]), vec(lru_lam[i])
            bf = b_f[i].reshape(FOX_H, 1)
            w_out = w_out_e[i].astype(BF16)

            z, fl_t = _norm_matmul(xp, g[0], w_main, tm=tm_p, wt=w_fl_t, name="in_proj_even")
            lf_t, c_t = _fox_prep(fl_t, bf, bp, seq)
            lru_out, h_last = _lru_prompt(z, bp, seq, cw, cb, wa, ba, wi, bi, lam)
            att = _fox_prompt(z, c_t, bp, seq)
            xp = _matmul_norm_res([lru_out, att], w_out, g[1], xp, tm=tm_p, name="out_proj_even")
            z3 = z.reshape(bp, seq, n_main)
            fk_p.append(z3[:, :, 2 * LRU_W + FOX_W:2 * LRU_W + 2 * FOX_W].reshape(bp, seq, FOX_H, FOX_HD))
            fv_p.append(z3[:, :, 2 * LRU_W + 2 * FOX_W:].reshape(bp, seq, FOX_H, FOX_HD))
            fl_p.append(lf_t.T.reshape(bp, seq, FOX_H))
            lh_p.append(h_last.reshape(bp, LRU_W))
            lc_p.append(z3[:, seq - (LRU_CONV - 1):, :LRU_W])

            zs, fls_t = _norm_matmul(xs, g[0], w_main, tm=bs, wt=w_fl_t, name="in_proj_even_s")
            lfs_t = _logsig(fls_t, bf)
            pre_t = jnp.swapaxes(state_lru_conv[i], 0, 1)
            lru_out_s, h_s = _lru_sample(zs, pre_t, state_lru_h[i], cw, cb, wa, ba, wi, bi, lam)
            q_s = zs[:, 2 * LRU_W:2 * LRU_W + FOX_W]
            k_s = zs[:, 2 * LRU_W + FOX_W:2 * LRU_W + 2 * FOX_W]
            v_s = zs[:, 2 * LRU_W + 2 * FOX_W:]
            att_s = _fox_sample(page_table, i, q_s.reshape(bs, FOX_H, FOX_HD), k_s.reshape(bs, FOX_H, FOX_HD),
                                v_s.reshape(bs, FOX_H, FOX_HD), lfs_t.T.reshape(bs, FOX_H, 1),
                                jnp.transpose(cache_fox_k, (0, 1, 3, 4, 2)),
                                jnp.transpose(cache_fox_v, (0, 1, 3, 4, 2)),
                                jnp.swapaxes(cache_fox_logf[i], 1, 2))
            xs = _matmul_norm_res([lru_out_s, att_s.reshape(bs, FOX_W)], w_out, g[1], xs, tm=bs,
                                  name="out_proj_even_s")
            fk_s.append(k_s.reshape(bs, 1, FOX_H, FOX_HD))
            fv_s.append(v_s.reshape(bs, 1, FOX_H, FOX_HD))
            fl_s.append(lfs_t.T.reshape(bs, 1, FOX_H))
            lh_s.append(h_s)
            lc_s.append(jnp.concatenate([state_lru_conv[i][:, 1:], zs[:, None, :LRU_W]], axis=1))
        else:
            j = l // 2
            w_in = w_in_o[j].astype(BF16)
            dww, dwb = cc_dw_w[j], vec(cc_dw_b[j])
            lng, lnb = vec(cc_ln_g[j]), vec(cc_ln_b[j])
            pw, ps = pool_w[j].astype(BF16), vec(pool_scale[j])
            w_out = w_out_o[j].astype(BF16)

            z = _norm_matmul(xp, g[0], w_in, tm=tm_p, name="in_proj_odd")
            cv, pool, cbuf, pbuf = _odd_prompt(z, bp, seq, dww, dwb, lng, lnb, pw, ps)
            xp = _matmul_norm_res([cv, pool], w_out, g[1], xp, tm=tm_p, name="out_proj_odd")
            cb_p.append(cbuf)
            pb_p.append(pbuf)

            zs = _norm_matmul(xs, g[0], w_in, tm=bs, name="in_proj_odd_s")
            cv_s, pool_s, glu_s = _odd_sample(zs, jnp.swapaxes(state_conv_buf[j], 0, 1),
                                              jnp.swapaxes(state_pool_buf[j], 0, 1),
                                              dww, dwb, lng, lnb, pw, ps, past_len)
            xs = _matmul_norm_res([cv_s, pool_s], w_out, g[1], xs, tm=bs, name="out_proj_odd_s")
            cb_s.append(jnp.concatenate([state_conv_buf[j][:, 1:], glu_s[:, None, :]], axis=1))
            pb_s.append(jnp.concatenate([state_pool_buf[j][:, 1:], zs[:, None, 2 * CONV_W:]], axis=1))

        wq, wo = w_xq[l].astype(BF16), w_xo[l].astype(BF16)
        q = _norm_matmul(xp, g[2], wq, tm=tm_p, name="xattn_q")
        o = _xattn_prompt(q, memkv, bp, seq)
        xp = _matmul_norm_res([o], wo, g[3], xp, tm=tm_p, name="xattn_o")
        q_s = _norm_matmul(xs, g[2], wq, tm=bs, name="xattn_q_s")
        o_s = _xattn_sample(q_s.reshape(bs, 1, MEM_W), l,
                            cache_mem_k.reshape(depth, bs, mem_len * MEM_H, MEM_HD),
                            cache_mem_v.reshape(depth, bs, mem_len * MEM_H, MEM_HD))
        xs = _matmul_norm_res([o_s.reshape(bs, MEM_W)], wo, g[3], xs, tm=bs, name="xattn_o_s")

        if l % 2 == 0:
            i = l // 2
            w1, w3, w2 = (_cast_bf16(w[i:i + 1])[0] for w in (w_ff1, w_ff3, w_ff2))
            xp = _ffn(xp, g[4], g[5], w1, w3, w2, tm=tm_p)
            xs = _ffn(xs, g[4], g[5], w1, w3, w2, tm=bs)
        else:
            j = l // 2
            xp, xs = _moe_block([xp, xs], g[4], g[5], w_router[j], b_router[j],
                                _cast_bf16(w_e1[j]), _cast_bf16(w_e3[j]), _cast_bf16(w_e2[j]))

    return (xp.reshape(bp, seq, d), xs.reshape(bs, 1, d),
            jnp.stack(fk_p), jnp.stack(fv_p), jnp.stack(fl_p), jnp.stack(lh_p), jnp.stack(lc_p),
            jnp.stack(cb_p), jnp.stack(pb_p), jnp.stack(mk_pl), jnp.stack(mv_pl),
            jnp.stack(fk_s), jnp.stack(fv_s), jnp.stack(fl_s), jnp.stack(lh_s), jnp.stack(lc_s),
            jnp.stack(cb_s), jnp.stack(pb_s))
```
